```python
import jax, jax.numpy as jnp
from jax import lax
import numpy as np

D_MODEL = 1024
BATCH = 8
SEQ = 2048
DEPTH = 2
DEC_BATCH = 32
DEC_SEQ = 4
PAST_LEN = 16384
PAGE_SIZE = 128

N_A_LAYERS = DEPTH // 2
N_B_LAYERS = DEPTH - N_A_LAYERS
H_A = 16
D_NOPE = 64
D_ROPE = 32
D_V = 64
D_QC = 384
D_C = 256
D_CKV = D_C + D_ROPE
SCALE_A = (D_NOPE + D_ROPE) ** -0.5
H_B = 16
N_KV_B = 4
HD_B = 64
G_B = H_B // N_KV_B
WINDOW = 128
ROT_B = HD_B // 4
SCALE_B = HD_B ** -0.5
D_FF_RAW = -(-8 * D_MODEL // 3)
D_FF = ((D_FF_RAW + 255) // 256) * 256
ROPE_THETA = 500000.0
Q_BLOCK = 128
EPS = 1e-6
NEG = -1e30

kernel_name = 'yoco_mla_swa_sink_decoder_step'


def rms_norm(x, g):
    xf = x.astype(jnp.float32)
    y = xf * lax.rsqrt(jnp.mean(xf * xf, axis=-1, keepdims=True) + EPS)
    return (y * g.astype(jnp.float32)).astype(x.dtype)


def rope_cos_sin(pos, n_rot):
    inv = ROPE_THETA ** (-jnp.arange(0, n_rot, 2, dtype=jnp.float32) / n_rot)
    ang = pos.astype(jnp.float32)[:, None] * inv[None, :]
    return jnp.cos(ang), jnp.sin(ang)


def apply_rope(x, cos, sin):
    r = cos.shape[-1]
    x1 = x[..., :r].astype(jnp.float32)
    x2 = x[..., r:2 * r].astype(jnp.float32)
    c = cos[:, None, :]
    s = sin[:, None, :]
    rot = jnp.concatenate([x1 * c - x2 * s, x2 * c + x1 * s], axis=-1).astype(x.dtype)
    return jnp.concatenate([rot, x[..., 2 * r:]], axis=-1)


def swiglu(hn, w_in, w_out):
    a = hn @ w_in
    return (jax.nn.silu(a[..., :D_FF]) * a[..., D_FF:]) @ w_out


def sink_softmax(s, sink):
    sk = jnp.broadcast_to(sink, s.shape[:-1] + (1,))
    p = jax.nn.softmax(jnp.concatenate([s, sk], axis=-1), axis=-1)
    return p[..., :-1]


def mla_project(hn, pos, w_in, g_qc, w_uq, g_ckv, g_qn, g_qr, g_kr):
    b, s, _ = hn.shape
    a = hn @ w_in
    c_q = rms_norm(a[..., :D_QC], g_qc)
    c_kv = rms_norm(a[..., D_QC:D_QC + D_C], g_ckv)
    k_pe = a[..., D_QC + D_C:][:, :, None, :]
    q = (c_q @ w_uq).reshape(b, s, H_A, D_NOPE + D_ROPE)
    cos, sin = rope_cos_sin(pos, D_ROPE)
    q_nope = rms_norm(q[..., :D_NOPE], g_qn)
    q_pe = apply_rope(rms_norm(q[..., D_NOPE:], g_qr), cos, sin)
    k_pe = apply_rope(rms_norm(k_pe, g_kr), cos, sin)[:, :, 0]
    rows = jnp.concatenate([c_kv, k_pe], axis=-1)
    return q_nope, q_pe, rows


def mla_keys(rows, w_uk, g_kn):
    c = rows[..., :D_C]
    k_nope = jnp.einsum('bsc,chd->bshd', c, w_uk.reshape(D_C, H_A, D_NOPE))
    return rms_norm(k_nope, g_kn), rows[..., D_C:]


def mla_scores(q_nope, q_pe, k_nope, k_pe):
    s = jnp.einsum('bqhd,bshd->bhqs', q_nope, k_nope) + jnp.einsum('bqhr,bsr->bhqs', q_pe, k_pe)
    return s.astype(jnp.float32) * SCALE_A


def mla_prompt_attn(q_nope, q_pe, rows, w_uk, w_uv, g_kn):
    b, s_len = q_nope.shape[:2]
    nb = s_len // Q_BLOCK
    k_nope, k_pe = mla_keys(rows, w_uk, g_kn)
    v = jnp.einsum('bsc,chd->bshd', rows[..., :D_C], w_uv.reshape(D_C, H_A, D_V))
    qn = q_nope.reshape(b, nb, Q_BLOCK, H_A, D_NOPE).transpose(1, 0, 2, 3, 4)
    qp = q_pe.reshape(b, nb, Q_BLOCK, H_A, D_ROPE).transpose(1, 0, 2, 3, 4)
    k_pos = jnp.arange(s_len)

    def block(args):
        i, qn_b, qp_b = args
        sc = mla_scores(qn_b, qp_b, k_nope, k_pe)
        q_pos = i * Q_BLOCK + jnp.arange(Q_BLOCK)
        sc = jnp.where(k_pos[None, :] <= q_pos[:, None], sc, NEG)
        p = jax.nn.softmax(sc, axis=-1).astype(v.dtype)
        return jnp.einsum('bhqs,bshd->bqhd', p, v)

    o = lax.map(block, (jnp.arange(nb), qn, qp))
    return o.transpose(1, 0, 2, 3, 4).reshape(b, s_len, H_A * D_V)


def online_softmax_update(carry, s, vals):
    m, l, acc = carry
    m_new = jnp.maximum(m, jnp.max(s, axis=-1))
    corr = jnp.exp(m - m_new)
    p = jnp.exp(s - m_new[..., None])
    return (m_new, l * corr + jnp.sum(p, axis=-1),
            acc * corr[..., None] + jnp.einsum('bhts,bsc->bhtc', p, vals))


def mla_paged_attn(q_nope, q_pe, new_rows, cache_l, page_table, w_uk, w_uv, g_kn):
    bd, t = q_nope.shape[:2]
    f32 = jnp.float32
    init = (jnp.full((bd, H_A, t), NEG, f32), jnp.zeros((bd, H_A, t), f32),
            jnp.zeros((bd, H_A, t, D_C), f32))

    def page_step(carry, page_ids):
        rows = cache_l[page_ids]
        k_nope, k_pe = mla_keys(rows, w_uk, g_kn)
        sc = mla_scores(q_nope, q_pe, k_nope, k_pe)
        return online_softmax_update(carry, sc, rows[..., :D_C].astype(f32)), None

    carry, _ = lax.scan(page_step, init, page_table.T)
    k_nope, k_pe = mla_keys(new_rows, w_uk, g_kn)
    sc = mla_scores(q_nope, q_pe, k_nope, k_pe)
    sc = jnp.where(jnp.tril(jnp.ones((t, t), dtype=bool)), sc, NEG)
    _, l, acc = online_softmax_update(carry, sc, new_rows[..., :D_C].astype(f32))
    o_lat = acc / l[..., None]
    o = jnp.einsum('bhtc,chd->bthd', o_lat, w_uv.reshape(D_C, H_A, D_V).astype(f32))
    return o.astype(q_nope.dtype).reshape(bd, t, H_A * D_V)


def shared_kv(h, pos, g_kv, w_kv, g_k):
    b, s, _ = h.shape
    kv = rms_norm(h, g_kv) @ w_kv
    k = kv[..., :N_KV_B * HD_B].reshape(b, s, N_KV_B, HD_B)
    v = kv[..., N_KV_B * HD_B:].reshape(b, s, N_KV_B, HD_B)
    cos, sin = rope_cos_sin(pos, ROT_B)
    return apply_rope(rms_norm(k, g_k), cos, sin), v


def swa_query(hn, pos, w_q, g_q):
    b, s, _ = hn.shape
    q = rms_norm((hn @ w_q).reshape(b, s, H_B, HD_B), g_q)
    cos, sin = rope_cos_sin(pos, ROT_B)
    return apply_rope(q, cos, sin)


def swa_banded_attn(q, k, v, sink):
    b, s_len = q.shape[:2]
    nb = s_len // WINDOW
    qb = q.reshape(b, nb, WINDOW, N_KV_B, G_B, HD_B)

    def band(z):
        zp = jnp.concatenate([jnp.zeros_like(z[:, :WINDOW]), z[:, :s_len - WINDOW]], axis=1)
        return jnp.concatenate([zp.reshape(b, nb, WINDOW, N_KV_B, HD_B),
                                z.reshape(b, nb, WINDOW, N_KV_B, HD_B)], axis=2)

    kb, vb = band(k), band(v)
    sc = jnp.einsum('bnqkgd,bnskd->bnkgqs', qb, kb).astype(jnp.float32) * SCALE_B
    blk = jnp.arange(nb)[:, None, None]
    q_rel = jnp.arange(WINDOW)[None, :, None] + WINDOW
    k_rel = jnp.arange(2 * WINDOW)[None, None, :]
    diff = q_rel - k_rel
    valid = (diff >= 0) & (diff < WINDOW) & (blk * WINDOW - WINDOW + k_rel >= 0)
    sc = jnp.where(valid[None, :, None, None], sc, NEG)
    p = sink_softmax(sc, sink.astype(jnp.float32).reshape(N_KV_B, G_B)[None, None, :, :, None, None])
    o = jnp.einsum('bnkgqs,bnskd->bnqkgd', p.astype(v.dtype), vb)
    return o.reshape(b, s_len, H_B * HD_B)


def swa_explicit_attn(q, q_pos, k, v, k_pos, sink):
    bd, t = q.shape[:2]
    qg = q.reshape(bd, t, N_KV_B, G_B, HD_B)
    sc = jnp.einsum('btkgd,bskd->bkgts', qg, k).astype(jnp.float32) * SCALE_B
    diff = q_pos[:, None] - k_pos[None, :]
    sc = jnp.where((diff >= 0) & (diff < WINDOW), sc, NEG)
    p = sink_softmax(sc, sink.astype(jnp.float32).reshape(N_KV_B, G_B)[None, :, :, None, None])
    o = jnp.einsum('bkgts,bskd->btkgd', p.astype(v.dtype), v)
    return o.reshape(bd, t, H_B * HD_B)


def setup_inputs(seed: int = 0) -> dict:
    key = jax.random.key(seed)
    ks = iter(jax.random.split(key, 48))
    f32 = jnp.float32

    def w(shape, fan_in):
        return jax.random.normal(next(ks), shape, f32) * fan_in ** -0.5

    def gain(shape):
        return 1.0 + 0.02 * jax.random.normal(next(ks), shape, f32)

    def act(shape):
        return jax.random.normal(next(ks), shape, f32)

    n_pages = PAST_LEN // PAGE_SIZE
    n_used = DEC_BATCH * n_pages
    n_pool = n_used + (n_used + 3) // 4
    w_buf = min(WINDOW, PAST_LEN)
    perm = jax.random.permutation(next(ks), n_pool)
    page_table = perm[:n_used].reshape(DEC_BATCH, n_pages).astype(jnp.int32)
    return {
        'x_prompt': act((BATCH, SEQ, D_MODEL)),
        'x_sample': act((DEC_BATCH, DEC_SEQ, D_MODEL)),
        'cache_mla': act((N_A_LAYERS, n_pool, PAGE_SIZE, D_CKV)),
        'state_win_k': act((DEC_BATCH, w_buf, N_KV_B, HD_B)),
        'state_win_v': act((DEC_BATCH, w_buf, N_KV_B, HD_B)),
        'page_table': page_table,
        'norm_attn': gain((DEPTH, D_MODEL)),
        'norm_ffn': gain((DEPTH, D_MODEL)),
        'w_a_in': w((N_A_LAYERS, D_MODEL, D_QC + D_C + D_ROPE), D_MODEL),
        'g_qc': gain((N_A_LAYERS, D_QC)),
        'w_uq': w((N_A_LAYERS, D_QC, H_A * (D_NOPE + D_ROPE)), D_QC),
        'g_ckv': gain((N_A_LAYERS, D_C)),
        'w_uk': w((N_A_LAYERS, D_C, H_A * D_NOPE), D_C),
        'w_uv': w((N_A_LAYERS, D_C, H_A * D_V), D_C),
        'g_qn_a': gain((N_A_LAYERS, D_NOPE)),
        'g_qr_a': gain((N_A_LAYERS, D_ROPE)),
        'g_kn_a': gain((N_A_LAYERS, D_NOPE)),
        'g_kr_a': gain((N_A_LAYERS, D_ROPE)),
        'w_a_out': w((N_A_LAYERS, H_A * D_V, D_MODEL), H_A * D_V),
        'g_kv_shared': gain((D_MODEL,)),
        'w_kv_shared': w((D_MODEL, 2 * N_KV_B * HD_B), D_MODEL),
        'g_k_b': gain((HD_B,)),
        'w_q_b': w((N_B_LAYERS, D_MODEL, H_B * HD_B), D_MODEL),
        'g_q_b': gain((N_B_LAYERS, HD_B)),
        'sinks': 0.5 * act((N_B_LAYERS, H_B)),
        'w_b_out': w((N_B_LAYERS, H_B * HD_B, D_MODEL), H_B * HD_B),
        'w_ffn_in': w((DEPTH, D_MODEL, 2 * D_FF), D_MODEL),
        'w_ffn_out': w((DEPTH, D_FF, D_MODEL), D_FF),
    }


def reference(x_prompt, x_sample, cache_mla, state_win_k, state_win_v, page_table,
              norm_attn, norm_ffn, w_a_in, g_qc, w_uq, g_ckv, w_uk, w_uv,
              g_qn_a, g_qr_a, g_kn_a, g_kr_a, w_a_out,
              g_kv_shared, w_kv_shared, g_k_b, w_q_b, g_q_b, sinks, w_b_out,
              w_ffn_in, w_ffn_out):
    seq = x_prompt.shape[1]
    dec_seq = x_sample.shape[1]
    past_len = page_table.shape[1] * PAGE_SIZE
    w_buf = state_win_k.shape[1]
    pos_p = jnp.arange(seq)
    pos_s = past_len + jnp.arange(dec_seq)

    def trunk(x, pos, mla_attend, swa_attend):
        h = x
        rows_all = []
        k_sh = None
        v_sh = None
        for l in range(DEPTH):
            hn = rms_norm(h, norm_attn[l])
            if l < N_A_LAYERS:
                q_nope, q_pe, rows = mla_project(hn, pos, w_a_in[l], g_qc[l], w_uq[l], g_ckv[l],
                                                 g_qn_a[l], g_qr_a[l], g_kr_a[l])
                h = h + mla_attend(l, q_nope, q_pe, rows) @ w_a_out[l]
                rows_all.append(rows)
            else:
                if l == N_A_LAYERS:
                    k_sh, v_sh = shared_kv(h, pos, g_kv_shared, w_kv_shared, g_k_b)
                lb = l - N_A_LAYERS
                q = swa_query(hn, pos, w_q_b[lb], g_q_b[lb])
                h = h + swa_attend(q, k_sh, v_sh, sinks[lb]) @ w_b_out[lb]
            h = h + swiglu(rms_norm(h, norm_ffn[l]), w_ffn_in[l], w_ffn_out[l])
        return h, jnp.stack(rows_all), k_sh, v_sh

    def mla_attend_prompt(la, q_nope, q_pe, rows):
        return mla_prompt_attn(q_nope, q_pe, rows, w_uk[la], w_uv[la], g_kn_a[la])

    def mla_attend_sample(la, q_nope, q_pe, rows):
        return mla_paged_attn(q_nope, q_pe, rows, cache_mla[la], page_table,
                              w_uk[la], w_uv[la], g_kn_a[la])

    def swa_attend_sample(q, k, v, sink):
        k_all = jnp.concatenate([state_win_k, k], axis=1)
        v_all = jnp.concatenate([state_win_v, v], axis=1)
        k_pos = jnp.concatenate([past_len - w_buf + jnp.arange(w_buf), pos_s])
        return swa_explicit_attn(q, pos_s, k_all, v_all, k_pos, sink)

    y_prompt, rows_p, k_p, v_p = trunk(x_prompt, pos_p, mla_attend_prompt, swa_banded_attn)
    y_sample, rows_s, k_s, v_s = trunk(x_sample, pos_s, mla_attend_sample, swa_attend_sample)

    w_p = min(WINDOW, seq)
    win_k_p = k_p[:, seq - w_p:]
    win_v_p = v_p[:, seq - w_p:]
    win_k_s = jnp.concatenate([state_win_k, k_s], axis=1)[:, -w_buf:]
    win_v_s = jnp.concatenate([state_win_v, v_s], axis=1)[:, -w_buf:]
    return (y_prompt, y_sample, rows_p, rows_s, win_k_p, win_v_p, win_k_s, win_v_s)
```

```python
import functools

import numpy as np
import jax
import jax.numpy as jnp
from jax import lax
from jax.experimental import pallas as pl
from jax.experimental.pallas import tpu as pltpu

F32 = jnp.float32
BF16 = jnp.bfloat16

D_MODEL = 1024
PAGE_SIZE = 128
H_A = 16
D_NOPE = 64
D_ROPE = 32
D_V = 64
D_QC = 384
D_C = 256
D_CKV = D_C + D_ROPE
SCALE_A = (D_NOPE + D_ROPE) ** -0.5
H_B = 16
N_KV_B = 4
HD_B = 64
G_B = H_B // N_KV_B
WINDOW = 128
ROT_B = HD_B // 4
SCALE_B = HD_B ** -0.5
D_FF = 2816
ROPE_THETA = 500000.0
EPS = 1e-6
NEG = -1e30

LANES = 128
SLAB = 128
A_IN_COLS = 768
KPE_LANE = 64
VMEM_LIMIT = 56 * 1024 * 1024
FF_CHUNK = 256
PAGES_PER_STEP = 8

_NT = (((1,), (1,)), ((), ()))


def _dot(a, b):
    return jnp.dot(a, b, preferred_element_type=F32)


def _dot_nt(a, b):
    return lax.dot_general(a, b, _NT, preferred_element_type=F32)


def _rms(x, g):
    ms = jnp.mean(x * x, axis=-1, keepdims=True)
    return x * lax.rsqrt(ms + EPS) * g


def _rope_slab(x, c, s1, s2, half):
    return x * c + pltpu.roll(x, LANES - half, 1) * s1 + pltpu.roll(x, half, 1) * s2


def _segment_scale(raw, seg_ref, inv_cnt_ref, expand_ref):
    ss = _dot((raw * raw).astype(BF16), seg_ref[...])
    rs = lax.rsqrt(ss * inv_cnt_ref[...] + EPS)
    hi = rs.astype(BF16)
    lo = (rs - hi.astype(F32)).astype(BF16)
    return _dot(jnp.concatenate([hi, lo], axis=1), expand_ref[...])


def _proj_a_kernel(x_ref, tab_ref, g_attn_ref, w_in_ref, g_qc_ref, g_ckv_ref, g_kpe_ref,
                   w_uq_ref, segq_ref, cntq_ref, expq_ref, gq_ref,
                   w_uk_ref, segk_ref, cntk_ref, expk_ref, gk_ref, w_uv_ref,
                   q_ref, k_ref, v_ref, rows_ref):
    c, s1, s2 = tab_ref[0], tab_ref[1], tab_ref[2]
    hn = _rms(x_ref[...], g_attn_ref[...]).astype(BF16)
    a = _dot(hn, w_in_ref[...])
    cq = _rms(a[:, :D_QC], g_qc_ref[...]).astype(BF16)
    ckv = _rms(a[:, D_QC:D_QC + D_C], g_ckv_ref[...])
    kpe = a[:, D_QC + D_C:]
    ms = jnp.sum(kpe * kpe, axis=-1, keepdims=True) * (1.0 / D_ROPE)
    kpe = _rope_slab(kpe * lax.rsqrt(ms + EPS) * g_kpe_ref[...], c, s1, s2, D_ROPE // 2)

    q_raw = _dot(cq, w_uq_ref[...])
    qn = q_raw * _segment_scale(q_raw, segq_ref, cntq_ref, expq_ref) * gq_ref[...]
    ckv_b = ckv.astype(BF16)
    k_raw = _dot(ckv_b, w_uk_ref[...])
    kn = k_raw * _segment_scale(k_raw, segk_ref, cntk_ref, expk_ref) * gk_ref[...]
    for h in range(H_A):
        sl = slice(SLAB * h, SLAB * (h + 1))
        q_ref[:, sl] = _rope_slab(qn[:, sl], c, s1, s2, D_ROPE // 2).astype(BF16)
        k_ref[:, sl] = (kn[:, sl] + kpe).astype(BF16)
    v_ref[...] = _dot(ckv_b, w_uv_ref[...]).astype(BF16)
    rows_ref[:, :D_C] = ckv
    rows_ref[:, D_C:] = kpe[:, KPE_LANE:KPE_LANE + D_ROPE]


def _const_spec(shape):
    zeros = (0,) * len(shape)
    return pl.BlockSpec(shape, lambda *_: zeros, pipeline_mode=pl.Buffered(1))


def _proj_a(x, tab, wa, tm):
    n = x.shape[0]
    n_tab = tab.shape[1] // tm
    weights = [wa["g_attn"], wa["w_in"], wa["g_qc"], wa["g_ckv"], wa["g_kpe"],
               wa["w_uq"], wa["segq"], wa["cntq"], wa["expq"], wa["gq"],
               wa["w_uk"], wa["segk"], wa["cntk"], wa["expk"], wa["gk"], wa["w_uv"]]
    wide = H_A * SLAB
    return pl.pallas_call(
        _proj_a_kernel,
        grid=(n // tm,),
        in_specs=[pl.BlockSpec((tm, D_MODEL), lambda i: (i, 0)),
                  pl.BlockSpec((3, tm, LANES), lambda i: (0, i % n_tab, 0))]
                 + [_const_spec(w.shape) for w in weights],
        out_specs=[pl.BlockSpec((tm, wide), lambda i: (i, 0)),
                   pl.BlockSpec((tm, wide), lambda i: (i, 0)),
                   pl.BlockSpec((tm, wide), lambda i: (i, 0)),
                   pl.BlockSpec((tm, D_CKV), lambda i: (i, 0))],
        out_shape=[jax.ShapeDtypeStruct((n, wide), BF16),
                   jax.ShapeDtypeStruct((n, wide), BF16),
                   jax.ShapeDtypeStruct((n, wide), BF16),
                   jax.ShapeDtypeStruct((n, D_CKV), F32)],
        compiler_params=pltpu.CompilerParams(dimension_semantics=("arbitrary",),
                                             vmem_limit_bytes=VMEM_LIMIT),
        name="mla_proj",
    )(x, tab, *weights)


def _attn_a_kernel(q_ref, k_ref, v_ref, o_ref, *, tq):
    i = pl.program_id(2)
    row = lax.broadcasted_iota(jnp.int32, (tq, tq), 0)
    col = lax.broadcasted_iota(jnp.int32, (tq, tq), 1)
    out = None
    for e in range(2):
        ls = slice(SLAB * e, SLAB * (e + 1))
        q = q_ref[:, ls]

        def step(t, carry, diagonal):
            m, l, acc = carry
            start = pl.multiple_of(t * tq, tq)
            s = _dot_nt(q, k_ref[pl.ds(start, tq), ls])
            if diagonal:
                s = jnp.where(col <= row, s, NEG)
            m_new = jnp.maximum(m, jnp.max(s, axis=-1, keepdims=True))
            alpha = jnp.exp(m - m_new)
            p = jnp.exp(s - m_new)
            l = alpha * l + jnp.sum(p, axis=-1, keepdims=True)
            acc = alpha * acc + _dot(p.astype(BF16), v_ref[pl.ds(start, tq), ls])
            return m_new, l, acc

        init = (jnp.full((tq, 1), NEG, F32), jnp.zeros((tq, 1), F32), jnp.zeros((tq, SLAB), F32))
        carry = lax.fori_loop(0, i, lambda t, cr: step(t, cr, False), init)
        _, l, acc = step(i, carry, True)
        o_e = acc / l
        out = o_e if out is None else out + o_e
    o_ref[...] = out.astype(BF16)


def _attn_a(q, k, v, batch, seq, tq):
    nq = seq // tq
    pairs = H_A // 2
    return pl.pallas_call(
        functools.partial(_attn_a_kernel, tq=tq),
        grid=(batch, pairs, nq),
        in_specs=[pl.BlockSpec((tq, 2 * SLAB), lambda b, j, i: (b * nq + i, j)),
                  pl.BlockSpec((seq, 2 * SLAB), lambda b, j, i: (b, j)),
                  pl.BlockSpec((seq, 2 * SLAB), lambda b, j, i: (b, j))],
        out_specs=pl.BlockSpec((tq, SLAB), lambda b, j, i: (b * nq + i, j)),
        out_shape=jax.ShapeDtypeStruct((batch * seq, H_A * D_V), BF16),
        compiler_params=pltpu.CompilerParams(
            dimension_semantics=("arbitrary", "arbitrary", "arbitrary"),
            vmem_limit_bytes=VMEM_LIMIT),
        name="mla_prompt_attn",
    )(q, k, v)


def _qabs_kernel(q_ref, gk_ref, wukt_ref, o_ref):
    for h in range(H_A):
        qs = (q_ref[:, SLAB * h:SLAB * (h + 1)].astype(F32) * gk_ref[...]).astype(BF16)
        o_ref[:, D_C * h:D_C * (h + 1)] = _dot(qs, wukt_ref[h]).astype(BF16)


def _qabs(q, gk_slab, wukt):
    n = q.shape[0]
    return pl.pallas_call(
        _qabs_kernel,
        out_shape=jax.ShapeDtypeStruct((n, H_A * D_C), BF16),
        compiler_params=pltpu.CompilerParams(vmem_limit_bytes=VMEM_LIMIT),
        name="mla_absorb_q",
    )(q, gk_slab, wukt)


def _paged_kernel(pt_ref, *refs, pp, t_new):
    pages = refs[:pp]
    wukt_ref, qabs_ref, qpe_ref, new_ref, o_ref, lhs_sc, m_sc, l_sc, acc_sc = refs[pp:]
    step = pl.program_id(1)
    rows_q = qabs_ref.shape[0]

    @pl.when(step == 0)
    def _():
        lhs_sc[:H_A * D_NOPE, :] = wukt_ref[...]
        lhs_sc[H_A * D_NOPE:, :] = qabs_ref[...]
        m_sc[...] = jnp.full(m_sc.shape, NEG, F32)
        l_sc[...] = jnp.zeros(l_sc.shape, F32)
        acc_sc[...] = jnp.zeros(acc_sc.shape, F32)

    def process(rows2, mask):
        keys = rows2.shape[0]
        c2 = rows2[:, :D_C].astype(BF16)
        kpe = rows2[:, D_C:].astype(BF16)
        big = _dot_nt(lhs_sc[...], c2)
        kt = big[:H_A * D_NOPE]
        ssq = jnp.sum((kt * kt).reshape(H_A, D_NOPE, keys), axis=1)
        rs = lax.rsqrt(ssq * (1.0 / D_NOPE) + EPS)
        rs_q = jnp.concatenate([rs] * (rows_q // H_A), axis=0)
        s = big[H_A * D_NOPE:] * rs_q + _dot_nt(qpe_ref[...], kpe)
        if mask is not None:
            s = jnp.where(mask, s, NEG)
        m_old = m_sc[...]
        m_new = jnp.maximum(m_old, jnp.max(s, axis=-1, keepdims=True))
        corr = jnp.exp(m_old - m_new)
        p = jnp.exp(s - m_new)
        l_sc[...] = l_sc[...] * corr + jnp.sum(p, axis=-1, keepdims=True)
        acc_sc[...] = acc_sc[...] * corr + _dot(p.astype(BF16), c2)
        m_sc[...] = m_new

    for u in range(0, pp, 2):
        process(jnp.concatenate([pages[u][...], pages[u + 1][...]], axis=0), None)

    @pl.when(step == pl.num_programs(1) - 1)
    def _():
        keys = new_ref.shape[0]
        t_row = lax.shift_right_logical(lax.broadcasted_iota(jnp.int32, (rows_q, keys), 0), H_A.bit_length() - 1)
        s_col = lax.broadcasted_iota(jnp.int32, (rows_q, keys), 1)
        process(new_ref[...], (s_col <= t_row) & (s_col < t_new))
        o_ref[...] = acc_sc[...] / l_sc[...]


def _paged_attn(page_table, cache, wukt, qabs, qpe, new_pad, t_new):
    bd, n_pages = page_table.shape
    pp = PAGES_PER_STEP
    rows_q = qabs.shape[1]

    def page_spec(u):
        return pl.BlockSpec((None, None, PAGE_SIZE, D_CKV),
                            lambda b, s, pt: (0, pt[b, s * pp + u], 0, 0))

    grid_spec = pltpu.PrefetchScalarGridSpec(
        num_scalar_prefetch=1,
        grid=(bd, n_pages // pp),
        in_specs=[page_spec(u) for u in range(pp)] + [
            pl.BlockSpec(wukt.shape, lambda b, s, pt: (0, 0)),
            pl.BlockSpec((None, rows_q, D_C), lambda b, s, pt: (b, 0, 0)),
            pl.BlockSpec((None, rows_q, D_ROPE), lambda b, s, pt: (b, 0, 0)),
            pl.BlockSpec((None,) + new_pad.shape[1:], lambda b, s, pt: (b, 0, 0))],
        out_specs=pl.BlockSpec((None, rows_q, D_C), lambda b, s, pt: (b, 0, 0)),
        scratch_shapes=[pltpu.VMEM((H_A * D_NOPE + rows_q, D_C), BF16),
                        pltpu.VMEM((rows_q, 1), F32),
                        pltpu.VMEM((rows_q, 1), F32),
                        pltpu.VMEM((rows_q, D_C), F32)])
    return pl.pallas_call(
        functools.partial(_paged_kernel, pp=pp, t_new=t_new),
        grid_spec=grid_spec,
        out_shape=jax.ShapeDtypeStruct((bd, rows_q, D_C), F32),
        compiler_params=pltpu.CompilerParams(dimension_semantics=("arbitrary", "arbitrary"),
                                             vmem_limit_bytes=VMEM_LIMIT),
        name="mla_paged_attn",
    )(page_table, *([cache] * pp), wukt, qabs, qpe, new_pad)


def _latent_out_kernel(olat_ref, w_uv_ref, o_ref):
    for j in range(H_A // 2):
        acc = None
        for e in range(2):
            h = 2 * j + e
            part = _dot(olat_ref[:, D_C * h:D_C * (h + 1)].astype(BF16),
                        w_uv_ref[:, SLAB * h:SLAB * (h + 1)])
            acc = part if acc is None else acc + part
        o_ref[:, SLAB * j:SLAB * (j + 1)] = acc.astype(BF16)


def _latent_out(olat, w_uv_pad):
    n = olat.shape[0]
    return pl.pallas_call(
        _latent_out_kernel,
        out_shape=jax.ShapeDtypeStruct((n, H_A * D_V), BF16),
        compiler_params=pltpu.CompilerParams(vmem_limit_bytes=VMEM_LIMIT),
        name="mla_latent_out",
    )(olat, w_uv_pad)


def _post_kernel(x_ref, o_ref, w_o_ref, g_ref, w_in_ref, w_out_ref, y_ref):
    h1 = x_ref[...] + _dot(o_ref[...], w_o_ref[...])
    hn = _rms(h1, g_ref[...]).astype(BF16)
    acc = h1
    for c in range(D_FF // FF_CHUNK):
        lo = c * FF_CHUNK
        a1 = _dot(hn, w_in_ref[:, lo:lo + FF_CHUNK])
        a2 = _dot(hn, w_in_ref[:, D_FF + lo:D_FF + lo + FF_CHUNK])
        gate = (a1 * jax.nn.sigmoid(a1)) * a2
        acc = acc + _dot(gate.astype(BF16), w_out_ref[lo:lo + FF_CHUNK, :])
    y_ref[...] = acc


def _post(x, o, w_o, g, w_in, w_out, tm):
    n = x.shape[0]
    return pl.pallas_call(
        _post_kernel,
        grid=(n // tm,),
        in_specs=[pl.BlockSpec((tm, D_MODEL), lambda i: (i, 0)),
                  pl.BlockSpec((tm, o.shape[1]), lambda i: (i, 0)),
                  _const_spec(w_o.shape), _const_spec(g.shape),
                  _const_spec(w_in.shape), _const_spec(w_out.shape)],
        out_specs=pl.BlockSpec((tm, D_MODEL), lambda i: (i, 0)),
        out_shape=jax.ShapeDtypeStruct((n, D_MODEL), F32),
        compiler_params=pltpu.CompilerParams(dimension_semantics=("arbitrary",),
                                             vmem_limit_bytes=VMEM_LIMIT),
        name="outproj_swiglu",
    )(x, o, w_o, g, w_in, w_out)


def _proj_b_kernel(h_ref, tab_ref, g_kv_ref, w_kv_ref, segk_ref, cntk_ref, expk_ref, gk_ref,
                   g_attn_ref, w_q_ref, segq_ref, cntq_ref, expq_ref, gq_ref,
                   k_ref, v_ref, q_ref):
    c, s1, s2 = tab_ref[0], tab_ref[1], tab_ref[2]
    h = h_ref[...]
    hr = h * lax.rsqrt(jnp.mean(h * h, axis=-1, keepdims=True) + EPS)
    kv = _dot((hr * g_kv_ref[...]).astype(BF16), w_kv_ref[...])
    kw = N_KV_B * HD_B
    k_raw = kv[:, :kw]
    kn = k_raw * _segment_scale(k_raw, segk_ref, cntk_ref, expk_ref) * gk_ref[...]
    for j in range(kw // LANES):
        sl = slice(LANES * j, LANES * (j + 1))
        k_ref[:, sl] = _rope_slab(kn[:, sl], c, s1, s2, ROT_B // 2)
    v_ref[...] = kv[:, kw:]
    q_raw = _dot((hr * g_attn_ref[...]).astype(BF16), w_q_ref[...])
    qn = q_raw * _segment_scale(q_raw, segq_ref, cntq_ref, expq_ref) * gq_ref[...]
    for j in range(H_B * HD_B // LANES):
        sl = slice(LANES * j, LANES * (j + 1))
        q_ref[:, sl] = _rope_slab(qn[:, sl], c, s1, s2, ROT_B // 2).astype(BF16)


def _proj_b(h, tab, wb, tm):
    n = h.shape[0]
    n_tab = tab.shape[1] // tm
    weights = [wb["g_kv"], wb["w_kv"], wb["segk"], wb["cntk"], wb["expk"], wb["gk"],
               wb["g_attn"], wb["w_q"], wb["segq"], wb["cntq"], wb["expq"], wb["gq"]]
    kw = N_KV_B * HD_B
    return pl.pallas_call(
        _proj_b_kernel,
        grid=(n // tm,),
        in_specs=[pl.BlockSpec((tm, D_MODEL), lambda i: (i, 0)),
                  pl.BlockSpec((3, tm, LANES), lambda i: (0, i % n_tab, 0))]
                 + [_const_spec(w.shape) for w in weights],
        out_specs=[pl.BlockSpec((tm, kw), lambda i: (i, 0)),
                   pl.BlockSpec((tm, kw), lambda i: (i, 0)),
                   pl.BlockSpec((tm, H_B * HD_B), lambda i: (i, 0))],
        out_shape=[jax.ShapeDtypeStruct((n, kw), F32),
                   jax.ShapeDtypeStruct((n, kw), F32),
                   jax.ShapeDtypeStruct((n, H_B * HD_B), BF16)],
        compiler_params=pltpu.CompilerParams(dimension_semantics=("arbitrary",),
                                             vmem_limit_bytes=VMEM_LIMIT),
        name="swa_proj",
    )(h, tab, *weights)


def _swa_group(q_lhs, kslab, vslab, kv, valid, sink_of):
    lane = lax.broadcasted_iota(jnp.int32, kslab.shape, 1)
    own = (lane >= HD_B) if kv % 2 else (lane < HD_B)
    k_half = [None, None]
    v_half = [None, None]
    k_half[kv % 2] = jnp.where(own, kslab, 0.0)
    v_half[kv % 2] = jnp.where(own, vslab, 0.0)
    k_half[1 - kv % 2] = pltpu.roll(k_half[kv % 2], HD_B, 1)
    v_half[1 - kv % 2] = pltpu.roll(v_half[kv % 2], HD_B, 1)
    m2 = q_lhs.shape[0]
    top = lax.broadcasted_iota(jnp.int32, (m2, 1), 0) < (m2 // 2)
    out = None
    for par in range(2):
        s = _dot_nt(q_lhs, k_half[par].astype(BF16))
        s = jnp.where(valid, s, NEG)
        sink = jnp.where(top, sink_of(G_B * kv + par), sink_of(G_B * kv + par + 2))
        m = jnp.maximum(jnp.max(s, axis=-1, keepdims=True), sink)
        p = jnp.exp(s - m)
        den = jnp.sum(p, axis=-1, keepdims=True) + jnp.exp(sink - m)
        o = _dot(p.astype(BF16), v_half[par].astype(BF16)) / den
        out = o if out is None else out + o
    return out


def _attn_b_kernel(sink_ref, q_ref, kp_ref, kc_ref, vp_ref, vc_ref, o_ref):
    n = pl.program_id(1)
    kcat = jnp.concatenate([kp_ref[...], kc_ref[...]], axis=0)
    vcat = jnp.concatenate([vp_ref[...], vc_ref[...]], axis=0)
    shape = (2 * WINDOW, 2 * WINDOW)
    qi = lax.broadcasted_iota(jnp.int32, shape, 0) & (WINDOW - 1)
    col = lax.broadcasted_iota(jnp.int32, shape, 1)
    valid = (col > qi) & (col <= qi + WINDOW) & ((col >= WINDOW) | (n > 0))
    for kv in range(N_KV_B):
        base = G_B * HD_B * kv
        q_lhs = jnp.concatenate([q_ref[:, base:base + LANES], q_ref[:, base + LANES:base + 2 * LANES]], axis=0)
        ks = slice(LANES * (kv // 2), LANES * (kv // 2 + 1))
        o = _swa_group(q_lhs, kcat[:, ks], vcat[:, ks], kv, valid, lambda hh: sink_ref[hh])
        o_ref[:, base:base + LANES] = o[:WINDOW].astype(BF16)
        o_ref[:, base + LANES:base + 2 * LANES] = o[WINDOW:].astype(BF16)


def _attn_b(sinks, q, k, v, batch, seq):
    nb = seq // WINDOW
    kw = N_KV_B * HD_B
    prev = lambda b, n: (b * nb + jnp.maximum(n - 1, 0), 0)
    cur = lambda b, n: (b * nb + n, 0)
    return pl.pallas_call(
        _attn_b_kernel,
        grid=(batch, nb),
        in_specs=[pl.BlockSpec(memory_space=pltpu.SMEM),
                  pl.BlockSpec((WINDOW, H_B * HD_B), cur),
                  pl.BlockSpec((WINDOW, kw), prev), pl.BlockSpec((WINDOW, kw), cur),
                  pl.BlockSpec((WINDOW, kw), prev), pl.BlockSpec((WINDOW, kw), cur)],
        out_specs=pl.BlockSpec((WINDOW, H_B * HD_B), cur),
        out_shape=jax.ShapeDtypeStruct((batch * seq, H_B * HD_B), BF16),
        compiler_params=pltpu.CompilerParams(dimension_semantics=("arbitrary", "arbitrary"),
                                             vmem_limit_bytes=VMEM_LIMIT),
        name="swa_prompt_attn",
    )(sinks, q, k, k, v, v)


def _attn_b_sample_kernel(sink_ref, q_ref, wk_ref, wv_ref, nk_ref, nv_ref, o_ref, k_sc, v_sc, q_sc, *, t):
    w_buf = wk_ref.shape[0]
    keys = k_sc.shape[0]
    k_sc[...] = jnp.zeros(k_sc.shape, F32)
    v_sc[...] = jnp.zeros(v_sc.shape, F32)
    q_sc[...] = jnp.zeros(q_sc.shape, F32)
    k_sc[:w_buf, :] = wk_ref[...]
    v_sc[:w_buf, :] = wv_ref[...]
    k_sc[w_buf:w_buf + t, :] = nk_ref[...]
    v_sc[w_buf:w_buf + t, :] = nv_ref[...]
    kall = k_sc[...]
    vall = v_sc[...]
    half_rows = 8
    rows = 2 * half_rows
    ti = lax.broadcasted_iota(jnp.int32, (rows, keys), 0) & (half_rows - 1)
    col = lax.broadcasted_iota(jnp.int32, (rows, keys), 1)
    diff = jnp.where(col < w_buf, ti + w_buf - col, ti - (col - w_buf))
    valid = (diff >= 0) & (diff < WINDOW) & (col < w_buf + t) & (ti < t)
    for kv in range(N_KV_B):
        base = G_B * HD_B * kv
        q_sc[kv, :t, :] = q_ref[:, base:base + LANES]
        q_sc[kv, half_rows:half_rows + t, :] = q_ref[:, base + LANES:base + 2 * LANES]
        q_lhs = q_sc[kv]
        ks = slice(LANES * (kv // 2), LANES * (kv // 2 + 1))
        o = _swa_group(q_lhs.astype(BF16), kall[:, ks], vall[:, ks], kv, valid, lambda hh: sink_ref[hh])
        o_ref[:, base:base + LANES] = o[:t]
        o_ref[:, base + LANES:base + 2 * LANES] = o[half_rows:half_rows + t]


def _attn_b_sample(sinks, q, win_k, win_v, new_k, new_v):
    bd, t, _ = q.shape
    w_buf = win_k.shape[1]
    kw = N_KV_B * HD_B
    keys = w_buf + 16
    blk = lambda shape: pl.BlockSpec((None,) + shape, lambda b: (b, 0, 0))
    return pl.pallas_call(
        functools.partial(_attn_b_sample_kernel, t=t),
        grid=(bd,),
        in_specs=[pl.BlockSpec(memory_space=pltpu.SMEM),
                  blk((t, H_B * HD_B)), blk((w_buf, kw)), blk((w_buf, kw)), blk((t, kw)), blk((t, kw))],
        out_specs=blk((t, H_B * HD_B)),
        out_shape=jax.ShapeDtypeStruct((bd, t, H_B * HD_B), F32),
        scratch_shapes=[pltpu.VMEM((keys, kw), F32), pltpu.VMEM((keys, kw), F32),
                        pltpu.VMEM((N_KV_B, 16, LANES), F32)],
        compiler_params=pltpu.CompilerParams(dimension_semantics=("arbitrary",),
                                             vmem_limit_bytes=VMEM_LIMIT),
        name="swa_sample_attn",
    )(sinks, q, win_k, win_v, new_k, new_v)


def _rope_tables(pos, n_rot, period, lane_lo):
    half = n_rot // 2
    inv = ROPE_THETA ** (-jnp.arange(0, n_rot, 2, dtype=F32) / n_rot)
    ang = pos.astype(F32)[:, None] * inv[None, :]
    cos, sin = jnp.cos(ang), jnp.sin(ang)
    rel = np.arange(LANES) % period - lane_lo
    in1 = (rel >= 0) & (rel < half)
    in2 = (rel >= half) & (rel < 2 * half)
    idx = np.where(in1, rel, np.where(in2, rel - half, 0))
    cg, sg = cos[:, idx], sin[:, idx]
    return jnp.stack([jnp.where(in1 | in2, cg, 1.0), jnp.where(in1, -sg, 0.0), jnp.where(in2, sg, 0.0)])


def _segments(width, seg_lanes):
    seg = np.zeros((width, LANES), np.float32)
    cnt = np.zeros((1, LANES), np.float32)
    for s, (lo, hi) in enumerate(seg_lanes):
        seg[lo:hi, s] = 1.0
        cnt[0, s] = 1.0 / (hi - lo)
    expand = np.concatenate([seg.T, seg.T], axis=0)
    return jnp.asarray(seg, BF16), jnp.asarray(cnt, F32), jnp.asarray(expand, BF16)


def _slab_gain(parts):
    g = jnp.zeros((SLAB,), F32)
    for lo, vals in parts:
        g = g.at[lo:lo + vals.shape[0]].set(vals)
    return g[None, :]


def _prep_a(norm_attn, w_a_in, g_qc, w_uq, g_ckv, w_uk, w_uv, g_qn, g_qr, g_kn, g_kr):
    w_in = jnp.concatenate([w_a_in[:, :D_QC + D_C], jnp.zeros((D_MODEL, KPE_LANE), F32),
                            w_a_in[:, D_QC + D_C:], jnp.zeros((D_MODEL, SLAB - KPE_LANE - D_ROPE), F32)], axis=1)
    dqk = D_NOPE + D_ROPE
    w_uq_pad = jnp.pad(w_uq.reshape(D_QC, H_A, dqk), ((0, 0), (0, 0), (0, SLAB - dqk))).reshape(D_QC, H_A * SLAB)
    w_uk3 = w_uk.reshape(D_C, H_A, D_NOPE)
    w_uk_pad = jnp.pad(w_uk3, ((0, 0), (0, 0), (0, SLAB - D_NOPE))).reshape(D_C, H_A * SLAB)
    w_uv3 = w_uv.reshape(D_C, H_A // 2, 2, D_V)
    even = jnp.pad(w_uv3[:, :, 0], ((0, 0), (0, 0), (0, SLAB - D_V)))
    odd = jnp.pad(w_uv3[:, :, 1], ((0, 0), (0, 0), (SLAB - D_V, 0)))
    w_uv_pad = jnp.stack([even, odd], axis=2).reshape(D_C, H_A * SLAB)
    q_segs = []
    k_segs = []
    for h in range(H_A):
        q_segs += [(SLAB * h, SLAB * h + D_NOPE), (SLAB * h + KPE_LANE, SLAB * h + KPE_LANE + D_ROPE)]
        k_segs += [(SLAB * h, SLAB * h + D_NOPE)]
    segq, cntq, expq = _segments(H_A * SLAB, q_segs)
    segk, cntk, expk = _segments(H_A * SLAB, k_segs)
    gq = jnp.tile(_slab_gain([(0, g_qn * SCALE_A), (KPE_LANE, g_qr * SCALE_A)]), (1, H_A))
    gk_slab = _slab_gain([(0, g_kn)])
    wukt = jnp.pad(jnp.transpose(w_uk3, (1, 2, 0)), ((0, 0), (0, SLAB - D_NOPE), (0, 0)))
    return dict(
        g_attn=norm_attn[None, :], w_in=w_in.astype(BF16), g_qc=g_qc[None, :], g_ckv=g_ckv[None, :],
        g_kpe=_slab_gain([(KPE_LANE, g_kr)]),
        w_uq=w_uq_pad.astype(BF16), segq=segq, cntq=cntq, expq=expq, gq=gq,
        w_uk=w_uk_pad.astype(BF16), segk=segk, cntk=cntk, expk=expk, gk=jnp.tile(gk_slab, (1, H_A)),
        w_uv=w_uv_pad.astype(BF16), gk_slab=gk_slab, wukt_pad=wukt.astype(BF16),
        wukt=jnp.transpose(w_uk).astype(BF16))


def _prep_b(g_kv, w_kv, g_k, norm_attn, w_q, g_q):
    kw = N_KV_B * HD_B
    segk, cntk, expk = _segments(kw, [(HD_B * h, HD_B * (h + 1)) for h in range(N_KV_B)])
    segq, cntq, expq = _segments(H_B * HD_B, [(HD_B * h, HD_B * (h + 1)) for h in range(H_B)])
    return dict(g_kv=g_kv[None, :], w_kv=w_kv.astype(BF16), segk=segk, cntk=cntk, expk=expk,
                gk=jnp.tile(g_k, N_KV_B)[None, :], g_attn=norm_attn[None, :], w_q=w_q.astype(BF16),
                segq=segq, cntq=cntq, expq=expq, gq=jnp.tile(g_q * SCALE_B, H_B)[None, :])


def kernel(x_prompt, x_sample, cache_mla, state_win_k, state_win_v, page_table, norm_attn, norm_ffn, w_a_in, g_qc, w_uq, g_ckv, w_uk, w_uv, g_qn_a, g_qr_a, g_kn_a, g_kr_a, w_a_out, g_kv_shared, w_kv_shared, g_k_b, w_q_b, g_q_b, sinks, w_b_out, w_ffn_in, w_ffn_out):
    batch, seq, _ = x_prompt.shape
    bd, t_dec, _ = x_sample.shape
    past_len = page_table.shape[1] * PAGE_SIZE
    w_buf = state_win_k.shape[1]
    kw = N_KV_B * HD_B
    assert w_a_in.shape[0] == 1 and w_q_b.shape[0] == 1, "one MLA layer followed by one SWA layer"
    assert w_buf == WINDOW and seq % 256 == 0 and (bd * t_dec) % 8 == 0

    wa = _prep_a(norm_attn[0], w_a_in[0], g_qc[0], w_uq[0], g_ckv[0], w_uk[0], w_uv[0],
                 g_qn_a[0], g_qr_a[0], g_kn_a[0], g_kr_a[0])
    wb = _prep_b(g_kv_shared, w_kv_shared, g_k_b, norm_attn[1], w_q_b[0], g_q_b[0])
    w_a_out_b = w_a_out[0].astype(BF16)
    w_b_out_b = w_b_out[0].astype(BF16)
    ffn_in = w_ffn_in.astype(BF16)
    ffn_out = w_ffn_out.astype(BF16)
    g_ffn = norm_ffn[:, None, :]
    sink_b = sinks[0]

    pos_p = jnp.arange(seq)
    n_s = bd * t_dec
    pos_s = past_len + jnp.arange(n_s) % t_dec
    tm_p = 256

    xp = x_prompt.reshape(batch * seq, D_MODEL)
    q, k, v, rows_p = _proj_a(xp, _rope_tables(pos_p, D_ROPE, SLAB, KPE_LANE), wa, tm_p)
    o = _attn_a(q, k, v, batch, seq, 256)
    h = _post(xp, o, w_a_out_b, g_ffn[0], ffn_in[0], ffn_out[0], tm_p)
    k_p, v_p, q_b = _proj_b(h, _rope_tables(pos_p, ROT_B, HD_B, 0), wb, tm_p)
    o = _attn_b(sink_b, q_b, k_p, v_p, batch, seq)
    y_prompt = _post(h, o, w_b_out_b, g_ffn[1], ffn_in[1], ffn_out[1], tm_p)

    xs = x_sample.reshape(n_s, D_MODEL)
    q, _, _, rows_s = _proj_a(xs, _rope_tables(pos_s, D_ROPE, SLAB, KPE_LANE), wa, n_s)
    qabs = _qabs(q, wa["gk_slab"], wa["wukt_pad"]).reshape(bd, t_dec * H_A, D_C)
    qpe = q.reshape(n_s, H_A, SLAB)[:, :, KPE_LANE:KPE_LANE + D_ROPE].reshape(bd, t_dec * H_A, D_ROPE)
    new_pad = jnp.pad(rows_s.reshape(bd, t_dec, D_CKV), ((0, 0), (0, 2 * PAGE_SIZE - t_dec), (0, 0)))
    olat = _paged_attn(page_table, cache_mla, wa["wukt"], qabs, qpe, new_pad, t_dec)
    o = _latent_out(olat.reshape(n_s, H_A * D_C), wa["w_uv"])
    h = _post(xs, o, w_a_out_b, g_ffn[0], ffn_in[0], ffn_out[0], n_s)
    k_s, v_s, q_b = _proj_b(h, _rope_tables(pos_s, ROT_B, HD_B, 0), wb, n_s)
    o = _attn_b_sample(sink_b, q_b.astype(F32).reshape(bd, t_dec, H_B * HD_B), state_win_k.reshape(bd, w_buf, kw),
                       state_win_v.reshape(bd, w_buf, kw), k_s.reshape(bd, t_dec, kw), v_s.reshape(bd, t_dec, kw))
    y_sample = _post(h, o.reshape(n_s, H_B * HD_B).astype(BF16), w_b_out_b, g_ffn[1], ffn_in[1], ffn_out[1], n_s)

    w_p = min(WINDOW, seq)
    k_p4 = k_p.reshape(batch, seq, N_KV_B, HD_B)
    v_p4 = v_p.reshape(batch, seq, N_KV_B, HD_B)
    win_k_s = jnp.concatenate([state_win_k, k_s.reshape(bd, t_dec, N_KV_B, HD_B)], axis=1)[:, -w_buf:]
    win_v_s = jnp.concatenate([state_win_v, v_s.reshape(bd, t_dec, N_KV_B, HD_B)], axis=1)[:, -w_buf:]
    return (y_prompt.reshape(batch, seq, D_MODEL), y_sample.reshape(bd, t_dec, D_MODEL),
            rows_p.reshape(1, batch, seq, D_CKV), rows_s.reshape(1, bd, t_dec, D_CKV),
            k_p4[:, seq - w_p:], v_p4[:, seq - w_p:], win_k_s, win_v_s)
```

```python
import functools

import numpy as np
import jax
import jax.numpy as jnp
from jax import lax
from jax.experimental import pallas as pl
from jax.experimental.pallas import tpu as pltpu

F32 = jnp.float32
BF16 = jnp.bfloat16

D_MODEL = 1024
PAGE_SIZE = 128
H_A = 16
D_NOPE = 64
D_ROPE = 32
D_V = 64
D_QC = 384
D_C = 256
D_CKV = D_C + D_ROPE
SCALE_A = (D_NOPE + D_ROPE) ** -0.5
H_B = 16
N_KV_B = 4
HD_B = 64
G_B = H_B // N_KV_B
WINDOW = 128
ROT_B = HD_B // 4
SCALE_B = HD_B ** -0.5
D_FF = 2816
ROPE_THETA = 500000.0
EPS = 1e-6
NEG = -1e30

LANES = 128
SLAB = 128
A_IN_COLS = 768
KPE_LANE = 64
VMEM_LIMIT = 56 * 1024 * 1024
FF_CHUNK = 256
PAGES_PER_STEP = 16

_NT = (((1,), (1,)), ((), ()))


def _dot(a, b):
    return jnp.dot(a, b, preferred_element_type=F32)


def _dot_nt(a, b):
    return lax.dot_general(a, b, _NT, preferred_element_type=F32)


def _rms(x, g):
    ms = jnp.mean(x * x, axis=-1, keepdims=True)
    return x * lax.rsqrt(ms + EPS) * g


def _rope_slab(x, c, s1, s2, half):
    return x * c + pltpu.roll(x, LANES - half, 1) * s1 + pltpu.roll(x, half, 1) * s2


def _segment_scale(raw, seg_ref, inv_cnt_ref, expand_ref):
    ss = _dot((raw * raw).astype(BF16), seg_ref[...])
    rs = lax.rsqrt(ss * inv_cnt_ref[...] + EPS)
    hi = rs.astype(BF16)
    lo = (rs - hi.astype(F32)).astype(BF16)
    return _dot(jnp.concatenate([hi, lo], axis=1), expand_ref[...])


def _proj_a_kernel(x_ref, tab_ref, g_attn_ref, w_in_ref, g_qc_ref, g_ckv_ref, g_kpe_ref,
                   w_uq_ref, segq_ref, cntq_ref, expq_ref, gq_ref,
                   w_uk_ref, segk_ref, cntk_ref, expk_ref, gk_ref, w_uv_ref,
                   q_ref, k_ref, v_ref, rows_ref):
    c, s1, s2 = tab_ref[0], tab_ref[1], tab_ref[2]
    hn = _rms(x_ref[...], g_attn_ref[...]).astype(BF16)
    a = _dot(hn, w_in_ref[...])
    cq = _rms(a[:, :D_QC], g_qc_ref[...]).astype(BF16)
    ckv = _rms(a[:, D_QC:D_QC + D_C], g_ckv_ref[...])
    kpe = a[:, D_QC + D_C:]
    ms = jnp.sum(kpe * kpe, axis=-1, keepdims=True) * (1.0 / D_ROPE)
    kpe = _rope_slab(kpe * lax.rsqrt(ms + EPS) * g_kpe_ref[...], c, s1, s2, D_ROPE // 2)

    q_raw = _dot(cq, w_uq_ref[...])
    qn = q_raw * _segment_scale(q_raw, segq_ref, cntq_ref, expq_ref) * gq_ref[...]
    ckv_b = ckv.astype(BF16)
    k_raw = _dot(ckv_b, w_uk_ref[...])
    kn = k_raw * _segment_scale(k_raw, segk_ref, cntk_ref, expk_ref) * gk_ref[...]
    for h in range(H_A):
        sl = slice(SLAB * h, SLAB * (h + 1))
        q_ref[:, sl] = _rope_slab(qn[:, sl], c, s1, s2, D_ROPE // 2).astype(BF16)
        k_ref[:, sl] = (kn[:, sl] + kpe).astype(BF16)
    v_ref[...] = _dot(ckv_b, w_uv_ref[...]).astype(BF16)
    rows_ref[:, :D_C] = ckv
    rows_ref[:, D_C:] = kpe[:, KPE_LANE:KPE_LANE + D_ROPE]


def _const_spec(shape):
    zeros = (0,) * len(shape)
    return pl.BlockSpec(shape, lambda *_: zeros, pipeline_mode=pl.Buffered(1))


def _proj_a(x, tab, wa, tm):
    n = x.shape[0]
    n_tab = tab.shape[1] // tm
    weights = [wa["g_attn"], wa["w_in"], wa["g_qc"], wa["g_ckv"], wa["g_kpe"],
               wa["w_uq"], wa["segq"], wa["cntq"], wa["expq"], wa["gq"],
               wa["w_uk"], wa["segk"], wa["cntk"], wa["expk"], wa["gk"], wa["w_uv"]]
    wide = H_A * SLAB
    return pl.pallas_call(
        _proj_a_kernel,
        grid=(n // tm,),
        in_specs=[pl.BlockSpec((tm, D_MODEL), lambda i: (i, 0)),
                  pl.BlockSpec((3, tm, LANES), lambda i: (0, i % n_tab, 0))]
                 + [_const_spec(w.shape) for w in weights],
        out_specs=[pl.BlockSpec((tm, wide), lambda i: (i, 0)),
                   pl.BlockSpec((tm, wide), lambda i: (i, 0)),
                   pl.BlockSpec((tm, wide), lambda i: (i, 0)),
                   pl.BlockSpec((tm, D_CKV), lambda i: (i, 0))],
        out_shape=[jax.ShapeDtypeStruct((n, wide), BF16),
                   jax.ShapeDtypeStruct((n, wide), BF16),
                   jax.ShapeDtypeStruct((n, wide), BF16),
                   jax.ShapeDtypeStruct((n, D_CKV), F32)],
        compiler_params=pltpu.CompilerParams(dimension_semantics=("arbitrary",),
                                             vmem_limit_bytes=VMEM_LIMIT),
        name="mla_proj",
    )(x, tab, *weights)


def _flash_update(q, k, v, carry, mask):
    m, l, acc = carry
    s = _dot_nt(q, k)
    if mask is not None:
        s = jnp.where(mask, s, NEG)
    m_new = jnp.maximum(m, jnp.max(s, axis=-1, keepdims=True))
    alpha = jnp.exp(m - m_new)
    p = jnp.exp(s - m_new)
    l = alpha * l + jnp.sum(p, axis=-1, keepdims=True)
    acc = alpha * acc + _dot(p.astype(BF16), v)
    return m_new, l, acc


def _attn_a_kernel(q_ref, k_ref, v_ref, o_ref, *, tk):
    i = pl.program_id(2)
    tq = 2 * tk
    causal = (lax.broadcasted_iota(jnp.int32, (tk, tk), 1) <= lax.broadcasted_iota(jnp.int32, (tk, tk), 0))
    slabs = [slice(SLAB * e, SLAB * (e + 1)) for e in range(2)]
    qs = [q_ref[:, ls] for ls in slabs]

    def kv_tile(t, ls):
        start = pl.multiple_of(t * tk, tk)
        return k_ref[pl.ds(start, tk), ls], v_ref[pl.ds(start, tk), ls]

    def body(t, carries):
        return tuple(_flash_update(qs[e], *kv_tile(t, slabs[e]), carries[e], None) for e in range(2))

    def init(rows):
        return (jnp.full((rows, 1), NEG, F32), jnp.zeros((rows, 1), F32), jnp.zeros((rows, SLAB), F32))

    carries = lax.fori_loop(0, 2 * i, body, (init(tq), init(tq)))
    out = None
    for e in range(2):
        top = tuple(c[:tk] for c in carries[e])
        bot = tuple(c[tk:] for c in carries[e])
        k0, v0 = kv_tile(2 * i, slabs[e])
        k1, v1 = kv_tile(2 * i + 1, slabs[e])
        top = _flash_update(qs[e][:tk], k0, v0, top, causal)
        bot = _flash_update(qs[e][tk:], k0, v0, bot, None)
        bot = _flash_update(qs[e][tk:], k1, v1, bot, causal)
        o_e = jnp.concatenate([top[2] / top[1], bot[2] / bot[1]], axis=0)
        out = o_e if out is None else out + o_e
    o_ref[...] = out.astype(BF16)


def _attn_a(q, k, v, batch, seq, tq):
    nq = seq // tq
    pairs = H_A // 2
    return pl.pallas_call(
        functools.partial(_attn_a_kernel, tk=tq // 2),
        grid=(batch, pairs, nq),
        in_specs=[pl.BlockSpec((tq, 2 * SLAB), lambda b, j, i: (b * nq + i, j)),
                  pl.BlockSpec((seq, 2 * SLAB), lambda b, j, i: (b, j)),
                  pl.BlockSpec((seq, 2 * SLAB), lambda b, j, i: (b, j))],
        out_specs=pl.BlockSpec((tq, SLAB), lambda b, j, i: (b * nq + i, j)),
        out_shape=jax.ShapeDtypeStruct((batch * seq, H_A * D_V), BF16),
        compiler_params=pltpu.CompilerParams(
            dimension_semantics=("arbitrary", "arbitrary", "arbitrary"),
            vmem_limit_bytes=VMEM_LIMIT),
        name="mla_prompt_attn",
    )(q, k, v)


def _qabs_kernel(q_ref, gk_ref, wukt_ref, o_ref):
    for h in range(H_A):
        qs = (q_ref[:, SLAB * h:SLAB * (h + 1)].astype(F32) * gk_ref[...]).astype(BF16)
        o_ref[:, D_C * h:D_C * (h + 1)] = _dot(qs, wukt_ref[h]).astype(BF16)


def _qabs(q, gk_slab, wukt):
    n = q.shape[0]
    return pl.pallas_call(
        _qabs_kernel,
        out_shape=jax.ShapeDtypeStruct((n, H_A * D_C), BF16),
        compiler_params=pltpu.CompilerParams(vmem_limit_bytes=VMEM_LIMIT),
        name="mla_absorb_q",
    )(q, gk_slab, wukt)


def _paged_kernel(pt_ref, *refs, pp, t_new):
    pages = refs[:pp]
    wukt_ref, qabs_ref, qpe_ref, new_ref, o_ref, lhs_sc, m_sc, l_sc, acc_sc, s_sc, ct_sc = refs[pp:]
    step = pl.program_id(1)
    rows_q = qabs_ref.shape[0]

    @pl.when(step == 0)
    def _():
        lhs_sc[:H_A * D_NOPE, :] = wukt_ref[...]
        lhs_sc[H_A * D_NOPE:, :] = qabs_ref[...]
        m_sc[...] = jnp.full(m_sc.shape, NEG, F32)
        l_sc[...] = jnp.zeros(l_sc.shape, F32)
        acc_sc[...] = jnp.zeros(acc_sc.shape, F32)
        s_sc[...] = jnp.full(s_sc.shape, -jnp.inf, F32)
        ct_sc[...] = jnp.zeros(ct_sc.shape, BF16)

    def scores(ct, kpet):
        keys = ct.shape[1]
        big = _dot(lhs_sc[...], ct)
        kt = big[:H_A * D_NOPE]
        ssq = jnp.sum((kt * kt).reshape(H_A, D_NOPE, keys), axis=1)
        rs = lax.rsqrt(ssq * (1.0 / D_NOPE) + EPS)
        rs_q = jnp.concatenate([rs] * (rows_q // H_A), axis=0)
        return big[H_A * D_NOPE:] * rs_q + _dot(qpe_ref[...], kpet)

    def accumulate(s, ct):
        m_old = m_sc[...]
        m_new = jnp.maximum(m_old, jnp.max(s, axis=-1, keepdims=True))
        corr = jnp.exp(m_old - m_new)
        p = jnp.exp(s - m_new)
        l_sc[...] = l_sc[...] * corr + jnp.sum(p, axis=-1, keepdims=True)
        acc_sc[...] = acc_sc[...] * corr + _dot_nt(p.astype(BF16), ct)
        m_sc[...] = m_new

    s_prev, ct_prev = s_sc[...], ct_sc[...]
    cts, ss = [], []
    for u in range(0, pp, 2):
        ct = jnp.concatenate([pages[u][:D_C, :], pages[u + 1][:D_C, :]], axis=1).astype(BF16)
        kpet = jnp.concatenate([pages[u][D_C:, :], pages[u + 1][D_C:, :]], axis=1).astype(BF16)
        cts.append(ct)
        ss.append(scores(ct, kpet))
    accumulate(s_prev, ct_prev)
    s_sc[...] = jnp.concatenate(ss, axis=1)
    ct_sc[...] = jnp.concatenate(cts, axis=1)

    @pl.when(step == pl.num_programs(1) - 1)
    def _():
        accumulate(s_sc[...], ct_sc[...])
        keys = new_ref.shape[1]
        t_row = lax.shift_right_logical(lax.broadcasted_iota(jnp.int32, (rows_q, keys), 0), H_A.bit_length() - 1)
        s_col = lax.broadcasted_iota(jnp.int32, (rows_q, keys), 1)
        ct = new_ref[:D_C, :].astype(BF16)
        s = scores(ct, new_ref[D_C:, :].astype(BF16))
        accumulate(jnp.where((s_col <= t_row) & (s_col < t_new), s, NEG), ct)
        o_ref[...] = acc_sc[...] / l_sc[...]


def _paged_attn(page_table, cache, wukt, qabs, qpe, new_pad, t_new):
    bd, n_pages = page_table.shape
    pp = PAGES_PER_STEP
    rows_q = qabs.shape[1]

    def page_spec(u):
        return pl.BlockSpec((None, None, D_CKV, PAGE_SIZE),
                            lambda b, s, pt: (0, pt[b, s * pp + u], 0, 0))

    grid_spec = pltpu.PrefetchScalarGridSpec(
        num_scalar_prefetch=1,
        grid=(bd, n_pages // pp),
        in_specs=[page_spec(u) for u in range(pp)] + [
            pl.BlockSpec(wukt.shape, lambda b, s, pt: (0, 0)),
            pl.BlockSpec((None, rows_q, D_C), lambda b, s, pt: (b, 0, 0)),
            pl.BlockSpec((None, rows_q, D_ROPE), lambda b, s, pt: (b, 0, 0)),
            pl.BlockSpec((None,) + new_pad.shape[1:], lambda b, s, pt: (b, 0, 0))],
        out_specs=pl.BlockSpec((None, rows_q, D_C), lambda b, s, pt: (b, 0, 0)),
        scratch_shapes=[pltpu.VMEM((H_A * D_NOPE + rows_q, D_C), BF16),
                        pltpu.VMEM((rows_q, 1), F32),
                        pltpu.VMEM((rows_q, 1), F32),
                        pltpu.VMEM((rows_q, D_C), F32),
                        pltpu.VMEM((rows_q, pp * PAGE_SIZE), F32),
                        pltpu.VMEM((D_C, pp * PAGE_SIZE), BF16)])
    return pl.pallas_call(
        functools.partial(_paged_kernel, pp=pp, t_new=t_new),
        grid_spec=grid_spec,
        out_shape=jax.ShapeDtypeStruct((bd, rows_q, D_C), F32),
        compiler_params=pltpu.CompilerParams(dimension_semantics=("arbitrary", "arbitrary"),
                                             vmem_limit_bytes=VMEM_LIMIT),
        name="mla_paged_attn",
    )(page_table, *([cache] * pp), wukt, qabs, qpe, new_pad)


def _latent_out_kernel(olat_ref, w_uv_ref, o_ref):
    for j in range(H_A // 2):
        acc = None
        for e in range(2):
            h = 2 * j + e
            part = _dot(olat_ref[:, D_C * h:D_C * (h + 1)].astype(BF16),
                        w_uv_ref[:, SLAB * h:SLAB * (h + 1)])
            acc = part if acc is None else acc + part
        o_ref[:, SLAB * j:SLAB * (j + 1)] = acc.astype(BF16)


def _latent_out(olat, w_uv_pad):
    n = olat.shape[0]
    return pl.pallas_call(
        _latent_out_kernel,
        out_shape=jax.ShapeDtypeStruct((n, H_A * D_V), BF16),
        compiler_params=pltpu.CompilerParams(vmem_limit_bytes=VMEM_LIMIT),
        name="mla_latent_out",
    )(olat, w_uv_pad)


def _post_kernel(x_ref, o_ref, w_o_ref, g_ref, w_in_ref, w_out_ref, y_ref):
    h1 = x_ref[...] + _dot(o_ref[...], w_o_ref[...])
    hn = _rms(h1, g_ref[...]).astype(BF16)
    acc = h1
    for c in range(D_FF // FF_CHUNK):
        lo = c * FF_CHUNK
        a1 = _dot(hn, w_in_ref[:, lo:lo + FF_CHUNK])
        a2 = _dot(hn, w_in_ref[:, D_FF + lo:D_FF + lo + FF_CHUNK])
        gate = (a1 * jax.nn.sigmoid(a1)) * a2
        acc = acc + _dot(gate.astype(BF16), w_out_ref[lo:lo + FF_CHUNK, :])
    y_ref[...] = acc


def _post(x, o, w_o, g, w_in, w_out, tm):
    n = x.shape[0]
    return pl.pallas_call(
        _post_kernel,
        grid=(n // tm,),
        in_specs=[pl.BlockSpec((tm, D_MODEL), lambda i: (i, 0)),
                  pl.BlockSpec((tm, o.shape[1]), lambda i: (i, 0)),
                  _const_spec(w_o.shape), _const_spec(g.shape),
                  _const_spec(w_in.shape), _const_spec(w_out.shape)],
        out_specs=pl.BlockSpec((tm, D_MODEL), lambda i: (i, 0)),
        out_shape=jax.ShapeDtypeStruct((n, D_MODEL), F32),
        compiler_params=pltpu.CompilerParams(dimension_semantics=("arbitrary",),
                                             vmem_limit_bytes=VMEM_LIMIT),
        name="outproj_swiglu",
    )(x, o, w_o, g, w_in, w_out)


def _proj_b_kernel(h_ref, tab_ref, g_kv_ref, w_kv_ref, segk_ref, cntk_ref, expk_ref, gk_ref,
                   g_attn_ref, w_q_ref, segq_ref, cntq_ref, expq_ref, gq_ref,
                   k_ref, v_ref, q_ref):
    c, s1, s2 = tab_ref[0], tab_ref[1], tab_ref[2]
    h = h_ref[...]
    hr = h * lax.rsqrt(jnp.mean(h * h, axis=-1, keepdims=True) + EPS)
    kv = _dot((hr * g_kv_ref[...]).astype(BF16), w_kv_ref[...])
    kw = N_KV_B * HD_B
    k_raw = kv[:, :kw]
    kn = k_raw * _segment_scale(k_raw, segk_ref, cntk_ref, expk_ref) * gk_ref[...]
    for j in range(kw // LANES):
        sl = slice(LANES * j, LANES * (j + 1))
        k_ref[:, sl] = _rope_slab(kn[:, sl], c, s1, s2, ROT_B // 2)
    v_ref[...] = kv[:, kw:]
    q_raw = _dot((hr * g_attn_ref[...]).astype(BF16), w_q_ref[...])
    qn = q_raw * _segment_scale(q_raw, segq_ref, cntq_ref, expq_ref) * gq_ref[...]
    for j in range(H_B * HD_B // LANES):
        sl = slice(LANES * j, LANES * (j + 1))
        q_ref[:, sl] = _rope_slab(qn[:, sl], c, s1, s2, ROT_B // 2).astype(BF16)


def _proj_b(h, tab, wb, tm):
    n = h.shape[0]
    n_tab = tab.shape[1] // tm
    weights = [wb["g_kv"], wb["w_kv"], wb["segk"], wb["cntk"], wb["expk"], wb["gk"],
               wb["g_attn"], wb["w_q"], wb["segq"], wb["cntq"], wb["expq"], wb["gq"]]
    kw = N_KV_B * HD_B
    return pl.pallas_call(
        _proj_b_kernel,
        grid=(n // tm,),
        in_specs=[pl.BlockSpec((tm, D_MODEL), lambda i: (i, 0)),
                  pl.BlockSpec((3, tm, LANES), lambda i: (0, i % n_tab, 0))]
                 + [_const_spec(w.shape) for w in weights],
        out_specs=[pl.BlockSpec((tm, kw), lambda i: (i, 0)),
                   pl.BlockSpec((tm, kw), lambda i: (i, 0)),
                   pl.BlockSpec((tm, H_B * HD_B), lambda i: (i, 0))],
        out_shape=[jax.ShapeDtypeStruct((n, kw), F32),
                   jax.ShapeDtypeStruct((n, kw), F32),
                   jax.ShapeDtypeStruct((n, H_B * HD_B), BF16)],
        compiler_params=pltpu.CompilerParams(dimension_semantics=("arbitrary",),
                                             vmem_limit_bytes=VMEM_LIMIT),
        name="swa_proj",
    )(h, tab, *weights)


def _swa_group(q_lhs, kslab, vslab, kv, valid, sink_of):
    lane = lax.broadcasted_iota(jnp.int32, kslab.shape, 1)
    own = (lane >= HD_B) if kv % 2 else (lane < HD_B)
    k_half = [None, None]
    v_half = [None, None]
    k_half[kv % 2] = jnp.where(own, kslab, 0.0)
    v_half[kv % 2] = jnp.where(own, vslab, 0.0)
    k_half[1 - kv % 2] = pltpu.roll(k_half[kv % 2], HD_B, 1)
    v_half[1 - kv % 2] = pltpu.roll(v_half[kv % 2], HD_B, 1)
    m2 = q_lhs.shape[0]
    top = lax.broadcasted_iota(jnp.int32, (m2, 1), 0) < (m2 // 2)
    out = None
    for par in range(2):
        s = _dot_nt(q_lhs, k_half[par].astype(BF16))
        s = jnp.where(valid, s, NEG)
        sink = jnp.where(top, sink_of(G_B * kv + par), sink_of(G_B * kv + par + 2))
        m = jnp.maximum(jnp.max(s, axis=-1, keepdims=True), sink)
        p = jnp.exp(s - m)
        den = jnp.sum(p, axis=-1, keepdims=True) + jnp.exp(sink - m)
        o = _dot(p.astype(BF16), v_half[par].astype(BF16)) / den
        out = o if out is None else out + o
    return out


def _attn_b_kernel(sink_ref, q_ref, kp_ref, kc_ref, vp_ref, vc_ref, o_ref):
    n = pl.program_id(1)
    kcat = jnp.concatenate([kp_ref[...], kc_ref[...]], axis=0)
    vcat = jnp.concatenate([vp_ref[...], vc_ref[...]], axis=0)
    shape = (2 * WINDOW, 2 * WINDOW)
    qi = lax.broadcasted_iota(jnp.int32, shape, 0) & (WINDOW - 1)
    col = lax.broadcasted_iota(jnp.int32, shape, 1)
    valid = (col > qi) & (col <= qi + WINDOW) & ((col >= WINDOW) | (n > 0))
    for kv in range(N_KV_B):
        base = G_B * HD_B * kv
        q_lhs = jnp.concatenate([q_ref[:, base:base + LANES], q_ref[:, base + LANES:base + 2 * LANES]], axis=0)
        ks = slice(LANES * (kv // 2), LANES * (kv // 2 + 1))
        o = _swa_group(q_lhs, kcat[:, ks], vcat[:, ks], kv, valid, lambda hh: sink_ref[hh])
        o_ref[:, base:base + LANES] = o[:WINDOW].astype(BF16)
        o_ref[:, base + LANES:base + 2 * LANES] = o[WINDOW:].astype(BF16)


def _attn_b(sinks, q, k, v, batch, seq):
    nb = seq // WINDOW
    kw = N_KV_B * HD_B
    prev = lambda b, n: (b * nb + jnp.maximum(n - 1, 0), 0)
    cur = lambda b, n: (b * nb + n, 0)
    return pl.pallas_call(
        _attn_b_kernel,
        grid=(batch, nb),
        in_specs=[pl.BlockSpec(memory_space=pltpu.SMEM),
                  pl.BlockSpec((WINDOW, H_B * HD_B), cur),
                  pl.BlockSpec((WINDOW, kw), prev), pl.BlockSpec((WINDOW, kw), cur),
                  pl.BlockSpec((WINDOW, kw), prev), pl.BlockSpec((WINDOW, kw), cur)],
        out_specs=pl.BlockSpec((WINDOW, H_B * HD_B), cur),
        out_shape=jax.ShapeDtypeStruct((batch * seq, H_B * HD_B), BF16),
        compiler_params=pltpu.CompilerParams(dimension_semantics=("arbitrary", "arbitrary"),
                                             vmem_limit_bytes=VMEM_LIMIT),
        name="swa_prompt_attn",
    )(sinks, q, k, k, v, v)


def _attn_b_sample_kernel(sink_ref, q_ref, wk_ref, wv_ref, nk_ref, nv_ref, o_ref, k_sc, v_sc, q_sc, *, t):
    w_buf = wk_ref.shape[0]
    keys = k_sc.shape[0]
    k_sc[...] = jnp.zeros(k_sc.shape, F32)
    v_sc[...] = jnp.zeros(v_sc.shape, F32)
    q_sc[...] = jnp.zeros(q_sc.shape, F32)
    k_sc[:w_buf, :] = wk_ref[...]
    v_sc[:w_buf, :] = wv_ref[...]
    k_sc[w_buf:w_buf + t, :] = nk_ref[...]
    v_sc[w_buf:w_buf + t, :] = nv_ref[...]
    kall = k_sc[...]
    vall = v_sc[...]
    half_rows = 8
    rows = 2 * half_rows
    ti = lax.broadcasted_iota(jnp.int32, (rows, keys), 0) & (half_rows - 1)
    col = lax.broadcasted_iota(jnp.int32, (rows, keys), 1)
    diff = jnp.where(col < w_buf, ti + w_buf - col, ti - (col - w_buf))
    valid = (diff >= 0) & (diff < WINDOW) & (col < w_buf + t) & (ti < t)
    for kv in range(N_KV_B):
        base = G_B * HD_B * kv
        q_sc[kv, :t, :] = q_ref[:, base:base + LANES]
        q_sc[kv, half_rows:half_rows + t, :] = q_ref[:, base + LANES:base + 2 * LANES]
        q_lhs = q_sc[kv]
        ks = slice(LANES * (kv // 2), LANES * (kv // 2 + 1))
        o = _swa_group(q_lhs.astype(BF16), kall[:, ks], vall[:, ks], kv, valid, lambda hh: sink_ref[hh])
        o_ref[:, base:base + LANES] = o[:t]
        o_ref[:, base + LANES:base + 2 * LANES] = o[half_rows:half_rows + t]


def _attn_b_sample(sinks, q, win_k, win_v, new_k, new_v):
    bd, t, _ = q.shape
    w_buf = win_k.shape[1]
    kw = N_KV_B * HD_B
    keys = w_buf + 16
    blk = lambda shape: pl.BlockSpec((None,) + shape, lambda b: (b, 0, 0))
    return pl.pallas_call(
        functools.partial(_attn_b_sample_kernel, t=t),
        grid=(bd,),
        in_specs=[pl.BlockSpec(memory_space=pltpu.SMEM),
                  blk((t, H_B * HD_B)), blk((w_buf, kw)), blk((w_buf, kw)), blk((t, kw)), blk((t, kw))],
        out_specs=blk((t, H_B * HD_B)),
        out_shape=jax.ShapeDtypeStruct((bd, t, H_B * HD_B), F32),
        scratch_shapes=[pltpu.VMEM((keys, kw), F32), pltpu.VMEM((keys, kw), F32),
                        pltpu.VMEM((N_KV_B, 16, LANES), F32)],
        compiler_params=pltpu.CompilerParams(dimension_semantics=("arbitrary",),
                                             vmem_limit_bytes=VMEM_LIMIT),
        name="swa_sample_attn",
    )(sinks, q, win_k, win_v, new_k, new_v)


def _rope_tables(pos, n_rot, period, lane_lo):
    half = n_rot // 2
    inv = ROPE_THETA ** (-jnp.arange(0, n_rot, 2, dtype=F32) / n_rot)
    ang = pos.astype(F32)[:, None] * inv[None, :]
    cos, sin = jnp.cos(ang), jnp.sin(ang)
    rel = np.arange(LANES) % period - lane_lo
    in1 = (rel >= 0) & (rel < half)
    in2 = (rel >= half) & (rel < 2 * half)
    idx = np.where(in1, rel, np.where(in2, rel - half, 0))
    cg, sg = cos[:, idx], sin[:, idx]
    return jnp.stack([jnp.where(in1 | in2, cg, 1.0), jnp.where(in1, -sg, 0.0), jnp.where(in2, sg, 0.0)])


def _segments(width, seg_lanes):
    seg = np.zeros((width, LANES), np.float32)
    cnt = np.zeros((1, LANES), np.float32)
    for s, (lo, hi) in enumerate(seg_lanes):
        seg[lo:hi, s] = 1.0
        cnt[0, s] = 1.0 / (hi - lo)
    expand = np.concatenate([seg.T, seg.T], axis=0)
    return jnp.asarray(seg, BF16), jnp.asarray(cnt, F32), jnp.asarray(expand, BF16)


def _slab_gain(parts):
    g = jnp.zeros((SLAB,), F32)
    for lo, vals in parts:
        g = g.at[lo:lo + vals.shape[0]].set(vals)
    return g[None, :]


def _prep_a(norm_attn, w_a_in, g_qc, w_uq, g_ckv, w_uk, w_uv, g_qn, g_qr, g_kn, g_kr):
    w_in = jnp.concatenate([w_a_in[:, :D_QC + D_C], jnp.zeros((D_MODEL, KPE_LANE), F32),
                            w_a_in[:, D_QC + D_C:], jnp.zeros((D_MODEL, SLAB - KPE_LANE - D_ROPE), F32)], axis=1)
    dqk = D_NOPE + D_ROPE
    w_uq_pad = jnp.pad(w_uq.reshape(D_QC, H_A, dqk), ((0, 0), (0, 0), (0, SLAB - dqk))).reshape(D_QC, H_A * SLAB)
    w_uk3 = w_uk.reshape(D_C, H_A, D_NOPE)
    w_uk_pad = jnp.pad(w_uk3, ((0, 0), (0, 0), (0, SLAB - D_NOPE))).reshape(D_C, H_A * SLAB)
    w_uv3 = w_uv.reshape(D_C, H_A // 2, 2, D_V)
    even = jnp.pad(w_uv3[:, :, 0], ((0, 0), (0, 0), (0, SLAB - D_V)))
    odd = jnp.pad(w_uv3[:, :, 1], ((0, 0), (0, 0), (SLAB - D_V, 0)))
    w_uv_pad = jnp.stack([even, odd], axis=2).reshape(D_C, H_A * SLAB)
    q_segs = []
    k_segs = []
    for h in range(H_A):
        q_segs += [(SLAB * h, SLAB * h + D_NOPE), (SLAB * h + KPE_LANE, SLAB * h + KPE_LANE + D_ROPE)]
        k_segs += [(SLAB * h, SLAB * h + D_NOPE)]
    segq, cntq, expq = _segments(H_A * SLAB, q_segs)
    segk, cntk, expk = _segments(H_A * SLAB, k_segs)
    gq = jnp.tile(_slab_gain([(0, g_qn * SCALE_A), (KPE_LANE, g_qr * SCALE_A)]), (1, H_A))
    gk_slab = _slab_gain([(0, g_kn)])
    wukt = jnp.pad(jnp.transpose(w_uk3, (1, 2, 0)), ((0, 0), (0, SLAB - D_NOPE), (0, 0)))
    return dict(
        g_attn=norm_attn[None, :], w_in=w_in.astype(BF16), g_qc=g_qc[None, :], g_ckv=g_ckv[None, :],
        g_kpe=_slab_gain([(KPE_LANE, g_kr)]),
        w_uq=w_uq_pad.astype(BF16), segq=segq, cntq=cntq, expq=expq, gq=gq,
        w_uk=w_uk_pad.astype(BF16), segk=segk, cntk=cntk, expk=expk, gk=jnp.tile(gk_slab, (1, H_A)),
        w_uv=w_uv_pad.astype(BF16), gk_slab=gk_slab, wukt_pad=wukt.astype(BF16),
        wukt=jnp.transpose(w_uk).astype(BF16))


def _prep_b(g_kv, w_kv, g_k, norm_attn, w_q, g_q):
    kw = N_KV_B * HD_B
    segk, cntk, expk = _segments(kw, [(HD_B * h, HD_B * (h + 1)) for h in range(N_KV_B)])
    segq, cntq, expq = _segments(H_B * HD_B, [(HD_B * h, HD_B * (h + 1)) for h in range(H_B)])
    return dict(g_kv=g_kv[None, :], w_kv=w_kv.astype(BF16), segk=segk, cntk=cntk, expk=expk,
                gk=jnp.tile(g_k, N_KV_B)[None, :], g_attn=norm_attn[None, :], w_q=w_q.astype(BF16),
                segq=segq, cntq=cntq, expq=expq, gq=jnp.tile(g_q * SCALE_B, H_B)[None, :])


def kernel(x_prompt, x_sample, cache_mla, state_win_k, state_win_v, page_table, norm_attn, norm_ffn, w_a_in, g_qc, w_uq, g_ckv, w_uk, w_uv, g_qn_a, g_qr_a, g_kn_a, g_kr_a, w_a_out, g_kv_shared, w_kv_shared, g_k_b, w_q_b, g_q_b, sinks, w_b_out, w_ffn_in, w_ffn_out):
    batch, seq, _ = x_prompt.shape
    bd, t_dec, _ = x_sample.shape
    past_len = page_table.shape[1] * PAGE_SIZE
    w_buf = state_win_k.shape[1]
    kw = N_KV_B * HD_B
    assert w_a_in.shape[0] == 1 and w_q_b.shape[0] == 1, "one MLA layer followed by one SWA layer"
    assert w_buf == WINDOW and seq % 256 == 0 and (bd * t_dec) % 8 == 0

    wa = _prep_a(norm_attn[0], w_a_in[0], g_qc[0], w_uq[0], g_ckv[0], w_uk[0], w_uv[0],
                 g_qn_a[0], g_qr_a[0], g_kn_a[0], g_kr_a[0])
    wb = _prep_b(g_kv_shared, w_kv_shared, g_k_b, norm_attn[1], w_q_b[0], g_q_b[0])
    w_a_out_b = w_a_out[0].astype(BF16)
    w_b_out_b = w_b_out[0].astype(BF16)
    ffn_in = w_ffn_in.astype(BF16)
    ffn_out = w_ffn_out.astype(BF16)
    g_ffn = norm_ffn[:, None, :]
    sink_b = sinks[0]

    pos_p = jnp.arange(seq)
    n_s = bd * t_dec
    pos_s = past_len + jnp.arange(n_s) % t_dec
    tm_p = 256

    xp = x_prompt.reshape(batch * seq, D_MODEL)
    q, k, v, rows_p = _proj_a(xp, _rope_tables(pos_p, D_ROPE, SLAB, KPE_LANE), wa, tm_p)
    o = _attn_a(q, k, v, batch, seq, 512)
    h = _post(xp, o, w_a_out_b, g_ffn[0], ffn_in[0], ffn_out[0], tm_p)
    k_p, v_p, q_b = _proj_b(h, _rope_tables(pos_p, ROT_B, HD_B, 0), wb, tm_p)
    o = _attn_b(sink_b, q_b, k_p, v_p, batch, seq)
    y_prompt = _post(h, o, w_b_out_b, g_ffn[1], ffn_in[1], ffn_out[1], tm_p)

    xs = x_sample.reshape(n_s, D_MODEL)
    q, _, _, rows_s = _proj_a(xs, _rope_tables(pos_s, D_ROPE, SLAB, KPE_LANE), wa, n_s)
    qabs = _qabs(q, wa["gk_slab"], wa["wukt_pad"]).reshape(bd, t_dec * H_A, D_C)
    qpe = q.reshape(n_s, H_A, SLAB)[:, :, KPE_LANE:KPE_LANE + D_ROPE].reshape(bd, t_dec * H_A, D_ROPE)
    new_pad = jnp.pad(jnp.swapaxes(rows_s.reshape(bd, t_dec, D_CKV), 1, 2), ((0, 0), (0, 0), (0, PAGE_SIZE - t_dec)))
    olat = _paged_attn(page_table, jnp.swapaxes(cache_mla, 2, 3), wa["wukt"], qabs, qpe, new_pad, t_dec)
    o = _latent_out(olat.reshape(n_s, H_A * D_C), wa["w_uv"])
    h = _post(xs, o, w_a_out_b, g_ffn[0], ffn_in[0], ffn_out[0], n_s)
    k_s, v_s, q_b = _proj_b(h, _rope_tables(pos_s, ROT_B, HD_B, 0), wb, n_s)
    o = _attn_b_sample(sink_b, q_b.astype(F32).reshape(bd, t_dec, H_B * HD_B), state_win_k.reshape(bd, w_buf, kw),
                       state_win_v.reshape(bd, w_buf, kw), k_s.reshape(bd, t_dec, kw), v_s.reshape(bd, t_dec, kw))
    y_sample = _post(h, o.reshape(n_s, H_B * HD_B).astype(BF16), w_b_out_b, g_ffn[1], ffn_in[1], ffn_out[1], n_s)

    w_p = min(WINDOW, seq)
    k_p4 = k_p.reshape(batch, seq, N_KV_B, HD_B)
    v_p4 = v_p.reshape(batch, seq, N_KV_B, HD_B)
    win_k_s = jnp.concatenate([state_win_k, k_s.reshape(bd, t_dec, N_KV_B, HD_B)], axis=1)[:, -w_buf:]
    win_v_s = jnp.concatenate([state_win_v, v_s.reshape(bd, t_dec, N_KV_B, HD_B)], axis=1)[:, -w_buf:]
    return (y_prompt.reshape(batch, seq, D_MODEL), y_sample.reshape(bd, t_dec, D_MODEL),
            rows_p.reshape(1, batch, seq, D_CKV), rows_s.reshape(1, bd, t_dec, D_CKV),
            k_p4[:, seq - w_p:], v_p4[:, seq - w_p:], win_k_s, win_v_s)
```

```python
import functools

import numpy as np
import jax
import jax.numpy as jnp
from jax import lax
from jax.experimental import pallas as pl
from jax.experimental.pallas import tpu as pltpu

F32 = jnp.float32
BF16 = jnp.bfloat16

D_MODEL = 1024
PAGE_SIZE = 128
H_A = 16
D_NOPE = 64
D_ROPE = 32
D_V = 64
D_QC = 384
D_C = 256
D_CKV = D_C + D_ROPE
SCALE_A = (D_NOPE + D_ROPE) ** -0.5
H_B = 16
N_KV_B = 4
HD_B = 64
G_B = H_B // N_KV_B
WINDOW = 128
ROT_B = HD_B // 4
SCALE_B = HD_B ** -0.5
D_FF = 2816
ROPE_THETA = 500000.0
EPS = 1e-6
NEG = -1e30

LANES = 128
SLAB = 128
A_IN_COLS = 768
KPE_LANE = 64
VMEM_LIMIT = 56 * 1024 * 1024
FF_CHUNK = 256
PAGES_PER_STEP = 16
TM_MLA_PROJ = 256
TM_SWA_PROJ = 512
TM_POST = 512
TQ_MLA = 256
ATTN_LOOKAHEAD = 2
SWA_SAMPLE_SEQS = 8
SWA_BLOCKS = 4

_NT = (((1,), (1,)), ((), ()))


def _dot(a, b):
    return jnp.dot(a, b, preferred_element_type=F32)


def _dot_nt(a, b):
    return lax.dot_general(a, b, _NT, preferred_element_type=F32)


def _rms(x, g):
    ms = jnp.mean(x * x, axis=-1, keepdims=True)
    return x * lax.rsqrt(ms + EPS) * g


def _rope_slab(x, c, s1, s2, half):
    return x * c + pltpu.roll(x, LANES - half, 1) * s1 + pltpu.roll(x, half, 1) * s2


def _segment_scales(items):
    sums = [_dot((raw * raw).astype(BF16), seg_ref[...]) for raw, seg_ref, _, _ in items]
    scales = []
    for ss, (_, _, inv_cnt_ref, expand_ref) in zip(sums, items):
        rs = lax.rsqrt(ss * inv_cnt_ref[...] + EPS)
        hi = rs.astype(BF16)
        lo = (rs - hi.astype(F32)).astype(BF16)
        scales.append(_dot(jnp.concatenate([hi, lo], axis=1), expand_ref[...]))
    return scales


def _proj_a_kernel(x_ref, tab_ref, g_attn_ref, w_in_ref, g_qc_ref, g_ckv_ref, g_kpe_ref,
                   w_uq_ref, segq_ref, cntq_ref, expq_ref, gq_ref,
                   w_uk_ref, segk_ref, cntk_ref, expk_ref, gk_ref, w_uv_ref,
                   q_ref, k_ref, v_ref, rows_ref):
    c, s1, s2 = tab_ref[0], tab_ref[1], tab_ref[2]
    hn = _rms(x_ref[...], g_attn_ref[...]).astype(BF16)
    a = _dot(hn, w_in_ref[...])
    cq = _rms(a[:, :D_QC], g_qc_ref[...]).astype(BF16)
    ckv = _rms(a[:, D_QC:D_QC + D_C], g_ckv_ref[...])
    kpe = a[:, D_QC + D_C:]
    ms = jnp.sum(kpe * kpe, axis=-1, keepdims=True) * (1.0 / D_ROPE)
    kpe = _rope_slab(kpe * lax.rsqrt(ms + EPS) * g_kpe_ref[...], c, s1, s2, D_ROPE // 2)

    ckv_b = ckv.astype(BF16)
    q_raw = _dot(cq, w_uq_ref[...])
    qn = q_raw * _segment_scales([(q_raw, segq_ref, cntq_ref, expq_ref)])[0] * gq_ref[...]
    k_raw = _dot(ckv_b, w_uk_ref[...])
    kn = k_raw * _segment_scales([(k_raw, segk_ref, cntk_ref, expk_ref)])[0] * gk_ref[...]
    for h in range(H_A):
        sl = slice(SLAB * h, SLAB * (h + 1))
        q_ref[:, sl] = _rope_slab(qn[:, sl], c, s1, s2, D_ROPE // 2).astype(BF16)
        k_ref[:, sl] = (kn[:, sl] + kpe).astype(BF16)
    v_ref[...] = _dot(ckv_b, w_uv_ref[...]).astype(BF16)
    rows_ref[:, :D_C] = ckv
    rows_ref[:, D_C:] = kpe[:, KPE_LANE:KPE_LANE + D_ROPE]


def _const_spec(shape):
    zeros = (0,) * len(shape)
    return pl.BlockSpec(shape, lambda *_: zeros, pipeline_mode=pl.Buffered(1))


def _proj_a(x, tab, wa, tm):
    n = x.shape[0]
    n_tab = tab.shape[1] // tm
    weights = [wa["g_attn"], wa["w_in"], wa["g_qc"], wa["g_ckv"], wa["g_kpe"],
               wa["w_uq"], wa["segq"], wa["cntq"], wa["expq"], wa["gq"],
               wa["w_uk"], wa["segk"], wa["cntk"], wa["expk"], wa["gk"], wa["w_uv"]]
    wide = H_A * SLAB
    return pl.pallas_call(
        _proj_a_kernel,
        grid=(n // tm,),
        in_specs=[pl.BlockSpec((tm, D_MODEL), lambda i: (i, 0)),
                  pl.BlockSpec((3, tm, LANES), lambda i: (0, i % n_tab, 0))]
                 + [_const_spec(w.shape) for w in weights],
        out_specs=[pl.BlockSpec((tm, wide), lambda i: (i, 0)),
                   pl.BlockSpec((tm, wide), lambda i: (i, 0)),
                   pl.BlockSpec((tm, wide), lambda i: (i, 0)),
                   pl.BlockSpec((tm, D_CKV), lambda i: (i, 0))],
        out_shape=[jax.ShapeDtypeStruct((n, wide), BF16),
                   jax.ShapeDtypeStruct((n, wide), BF16),
                   jax.ShapeDtypeStruct((n, wide), BF16),
                   jax.ShapeDtypeStruct((n, D_CKV), F32)],
        compiler_params=pltpu.CompilerParams(dimension_semantics=("arbitrary",),
                                             vmem_limit_bytes=VMEM_LIMIT),
        name="mla_proj",
    )(x, tab, *weights)


def _attn_a_kernel(q_ref, k_ref, v_ref, o_ref, *, tq):
    seq = q_ref.shape[0]
    causal = (lax.broadcasted_iota(jnp.int32, (tq, tq), 1) <= lax.broadcasted_iota(jnp.int32, (tq, tq), 0))
    jobs = [(c, e) for c in range(seq // tq) for e in range(2)]

    def windows(c, e):
        return slice(c * tq, (c + 1) * tq), slice(0, c * tq), slice(SLAB * e, SLAB * (e + 1))

    def score(c, e):
        rows, past, ls = windows(c, e)
        q = q_ref[rows, ls]
        s_d = _dot_nt(q, k_ref[rows, ls])
        return s_d, (_dot_nt(q, k_ref[past, ls]) if c else None)

    def attend(c, e, s_d, s_p):
        rows, past, ls = windows(c, e)
        s_d = jnp.where(causal, s_d, NEG)
        m = jnp.max(s_d, axis=-1, keepdims=True)
        if c:
            m = jnp.maximum(m, jnp.max(s_p, axis=-1, keepdims=True))
        p_d = jnp.exp(s_d - m)
        l = jnp.sum(p_d, axis=-1, keepdims=True)
        acc = _dot(p_d.astype(BF16), v_ref[rows, ls])
        if c:
            p_p = jnp.exp(s_p - m)
            l = l + jnp.sum(p_p, axis=-1, keepdims=True)
            acc = acc + _dot(p_p.astype(BF16), v_ref[past, ls])
        return acc / l

    ahead = [score(*jobs[j]) for j in range(min(ATTN_LOOKAHEAD, len(jobs)))]
    out = None
    for j, (c, e) in enumerate(jobs):
        if j + ATTN_LOOKAHEAD < len(jobs):
            ahead.append(score(*jobs[j + ATTN_LOOKAHEAD]))
        o_e = attend(c, e, *ahead[j])
        ahead[j] = None
        out = o_e if e == 0 else out + o_e
        if e == 1:
            o_ref[c * tq:(c + 1) * tq, :] = out.astype(BF16)


def _attn_a(q, k, v, batch, seq, tq):
    pairs = H_A // 2
    return pl.pallas_call(
        functools.partial(_attn_a_kernel, tq=tq),
        grid=(batch, pairs),
        in_specs=[pl.BlockSpec((seq, 2 * SLAB), lambda b, j: (b, j)),
                  pl.BlockSpec((seq, 2 * SLAB), lambda b, j: (b, j)),
                  pl.BlockSpec((seq, 2 * SLAB), lambda b, j: (b, j))],
        out_specs=pl.BlockSpec((seq, SLAB), lambda b, j: (b, j)),
        out_shape=jax.ShapeDtypeStruct((batch * seq, H_A * D_V), BF16),
        compiler_params=pltpu.CompilerParams(
            dimension_semantics=("arbitrary", "arbitrary"),
            vmem_limit_bytes=VMEM_LIMIT),
        name="mla_prompt_attn",
    )(q, k, v)


def _qabs_kernel(q_ref, gk_ref, wukt_ref, o_ref):
    for h in range(H_A):
        qs = (q_ref[:, SLAB * h:SLAB * (h + 1)].astype(F32) * gk_ref[...]).astype(BF16)
        o_ref[:, D_C * h:D_C * (h + 1)] = _dot(qs, wukt_ref[h]).astype(BF16)


def _qabs(q, gk_slab, wukt):
    n = q.shape[0]
    return pl.pallas_call(
        _qabs_kernel,
        out_shape=jax.ShapeDtypeStruct((n, H_A * D_C), BF16),
        compiler_params=pltpu.CompilerParams(vmem_limit_bytes=VMEM_LIMIT),
        name="mla_absorb_q",
    )(q, gk_slab, wukt)


def _paged_kernel(pt_ref, *refs, pp, t_new):
    pages = refs[:pp]
    wukt_ref, qabs_ref, qpe_ref, new_ref, o_ref, lhs_sc, m_sc, l_sc, acc_sc, s_sc, ct_sc = refs[pp:]
    step = pl.program_id(1)
    rows_q = qabs_ref.shape[0]

    @pl.when(step == 0)
    def _():
        lhs_sc[:H_A * D_NOPE, :] = wukt_ref[...]
        lhs_sc[H_A * D_NOPE:, :] = qabs_ref[...]
        m_sc[...] = jnp.full(m_sc.shape, NEG, F32)
        l_sc[...] = jnp.zeros(l_sc.shape, F32)
        acc_sc[...] = jnp.zeros(acc_sc.shape, F32)
        s_sc[...] = jnp.full(s_sc.shape, -jnp.inf, F32)
        ct_sc[...] = jnp.zeros(ct_sc.shape, BF16)

    def scores(ct, kpet):
        keys = ct.shape[1]
        big = _dot(lhs_sc[...], ct)
        kt = big[:H_A * D_NOPE]
        ssq = jnp.sum((kt * kt).reshape(H_A, D_NOPE, keys), axis=1)
        rs = lax.rsqrt(ssq * (1.0 / D_NOPE) + EPS)
        rs_q = jnp.concatenate([rs] * (rows_q // H_A), axis=0)
        return big[H_A * D_NOPE:] * rs_q + _dot(qpe_ref[...], kpet)

    def accumulate(s, ct):
        m_old = m_sc[...]
        m_new = jnp.maximum(m_old, jnp.max(s, axis=-1, keepdims=True))
        corr = jnp.exp(m_old - m_new)
        p = jnp.exp(s - m_new)
        l_sc[...] = l_sc[...] * corr + jnp.sum(p, axis=-1, keepdims=True)
        acc_sc[...] = acc_sc[...] * corr + _dot_nt(p.astype(BF16), ct)
        m_sc[...] = m_new

    s_prev, ct_prev = s_sc[...], ct_sc[...]
    cts, ss = [], []
    for u in range(0, pp, 2):
        ct = jnp.concatenate([pages[u][:D_C, :], pages[u + 1][:D_C, :]], axis=1).astype(BF16)
        kpet = jnp.concatenate([pages[u][D_C:, :], pages[u + 1][D_C:, :]], axis=1).astype(BF16)
        cts.append(ct)
        ss.append(scores(ct, kpet))
    accumulate(s_prev, ct_prev)
    s_sc[...] = jnp.concatenate(ss, axis=1)
    ct_sc[...] = jnp.concatenate(cts, axis=1)

    @pl.when(step == pl.num_programs(1) - 1)
    def _():
        accumulate(s_sc[...], ct_sc[...])
        keys = new_ref.shape[1]
        t_row = lax.shift_right_logical(lax.broadcasted_iota(jnp.int32, (rows_q, keys), 0), H_A.bit_length() - 1)
        s_col = lax.broadcasted_iota(jnp.int32, (rows_q, keys), 1)
        ct = new_ref[:D_C, :].astype(BF16)
        s = scores(ct, new_ref[D_C:, :].astype(BF16))
        accumulate(jnp.where((s_col <= t_row) & (s_col < t_new), s, NEG), ct)
        o_ref[...] = acc_sc[...] / l_sc[...]


def _paged_attn(page_table, cache, wukt, qabs, qpe, new_pad, t_new):
    bd, n_pages = page_table.shape
    pp = PAGES_PER_STEP
    rows_q = qabs.shape[1]

    def page_spec(u):
        return pl.BlockSpec((None, None, D_CKV, PAGE_SIZE),
                            lambda b, s, pt: (0, pt[b, s * pp + u], 0, 0))

    grid_spec = pltpu.PrefetchScalarGridSpec(
        num_scalar_prefetch=1,
        grid=(bd, n_pages // pp),
        in_specs=[page_spec(u) for u in range(pp)] + [
            pl.BlockSpec(wukt.shape, lambda b, s, pt: (0, 0)),
            pl.BlockSpec((None, rows_q, D_C), lambda b, s, pt: (b, 0, 0)),
            pl.BlockSpec((None, rows_q, D_ROPE), lambda b, s, pt: (b, 0, 0)),
            pl.BlockSpec((None,) + new_pad.shape[1:], lambda b, s, pt: (b, 0, 0))],
        out_specs=pl.BlockSpec((None, rows_q, D_C), lambda b, s, pt: (b, 0, 0)),
        scratch_shapes=[pltpu.VMEM((H_A * D_NOPE + rows_q, D_C), BF16),
                        pltpu.VMEM((rows_q, 1), F32),
                        pltpu.VMEM((rows_q, 1), F32),
                        pltpu.VMEM((rows_q, D_C), F32),
                        pltpu.VMEM((rows_q, pp * PAGE_SIZE), F32),
                        pltpu.VMEM((D_C, pp * PAGE_SIZE), BF16)])
    return pl.pallas_call(
        functools.partial(_paged_kernel, pp=pp, t_new=t_new),
        grid_spec=grid_spec,
        out_shape=jax.ShapeDtypeStruct((bd, rows_q, D_C), F32),
        compiler_params=pltpu.CompilerParams(dimension_semantics=("arbitrary", "arbitrary"),
                                             vmem_limit_bytes=VMEM_LIMIT),
        name="mla_paged_attn",
    )(page_table, *([cache] * pp), wukt, qabs, qpe, new_pad)


def _latent_out_kernel(olat_ref, w_uv_ref, o_ref):
    for j in range(H_A // 2):
        acc = None
        for e in range(2):
            h = 2 * j + e
            part = _dot(olat_ref[:, D_C * h:D_C * (h + 1)].astype(BF16),
                        w_uv_ref[:, SLAB * h:SLAB * (h + 1)])
            acc = part if acc is None else acc + part
        o_ref[:, SLAB * j:SLAB * (j + 1)] = acc.astype(BF16)


def _latent_out(olat, w_uv_pad):
    n = olat.shape[0]
    return pl.pallas_call(
        _latent_out_kernel,
        out_shape=jax.ShapeDtypeStruct((n, H_A * D_V), BF16),
        compiler_params=pltpu.CompilerParams(vmem_limit_bytes=VMEM_LIMIT),
        name="mla_latent_out",
    )(olat, w_uv_pad)


def _post_kernel(x_ref, o_ref, w_o_ref, g_ref, w_in_ref, w_out_ref, y_ref):
    h1 = x_ref[...] + _dot(o_ref[...], w_o_ref[...])
    hn = _rms(h1, g_ref[...]).astype(BF16)
    acc = h1
    for c in range(D_FF // FF_CHUNK):
        lo = c * FF_CHUNK
        a1 = _dot(hn, w_in_ref[:, lo:lo + FF_CHUNK])
        a2 = _dot(hn, w_in_ref[:, D_FF + lo:D_FF + lo + FF_CHUNK])
        gate = (a1 * jax.nn.sigmoid(a1)) * a2
        acc = acc + _dot(gate.astype(BF16), w_out_ref[lo:lo + FF_CHUNK, :])
    y_ref[...] = acc


def _post(x, o, w_o, g, w_in, w_out, tm):
    n = x.shape[0]
    return pl.pallas_call(
        _post_kernel,
        grid=(n // tm,),
        in_specs=[pl.BlockSpec((tm, D_MODEL), lambda i: (i, 0)),
                  pl.BlockSpec((tm, o.shape[1]), lambda i: (i, 0)),
                  _const_spec(w_o.shape), _const_spec(g.shape),
                  _const_spec(w_in.shape), _const_spec(w_out.shape)],
        out_specs=pl.BlockSpec((tm, D_MODEL), lambda i: (i, 0)),
        out_shape=jax.ShapeDtypeStruct((n, D_MODEL), F32),
        compiler_params=pltpu.CompilerParams(dimension_semantics=("arbitrary",),
                                             vmem_limit_bytes=VMEM_LIMIT),
        name="outproj_swiglu",
    )(x, o, w_o, g, w_in, w_out)


def _proj_b_kernel(h_ref, tab_ref, g_kv_ref, w_kv_ref, segk_ref, cntk_ref, expk_ref, gk_ref,
                   g_attn_ref, w_q_ref, segq_ref, cntq_ref, expq_ref, gq_ref,
                   k_ref, v_ref, q_ref):
    c, s1, s2 = tab_ref[0], tab_ref[1], tab_ref[2]
    h = h_ref[...]
    hr = h * lax.rsqrt(jnp.mean(h * h, axis=-1, keepdims=True) + EPS)
    kv = _dot((hr * g_kv_ref[...]).astype(BF16), w_kv_ref[...])
    q_raw = _dot((hr * g_attn_ref[...]).astype(BF16), w_q_ref[...])
    kw = N_KV_B * HD_B
    k_raw = kv[:, :kw]
    v_ref[...] = kv[:, kw:]
    k_scale, q_scale = _segment_scales([(k_raw, segk_ref, cntk_ref, expk_ref),
                                        (q_raw, segq_ref, cntq_ref, expq_ref)])
    kn = k_raw * k_scale * gk_ref[...]
    for j in range(kw // LANES):
        sl = slice(LANES * j, LANES * (j + 1))
        k_ref[:, sl] = _rope_slab(kn[:, sl], c, s1, s2, ROT_B // 2)
    qn = q_raw * q_scale * gq_ref[...]
    for j in range(H_B * HD_B // LANES):
        sl = slice(LANES * j, LANES * (j + 1))
        q_ref[:, sl] = _rope_slab(qn[:, sl], c, s1, s2, ROT_B // 2).astype(BF16)


def _proj_b(h, tab, wb, tm):
    n = h.shape[0]
    n_tab = tab.shape[1] // tm
    weights = [wb["g_kv"], wb["w_kv"], wb["segk"], wb["cntk"], wb["expk"], wb["gk"],
               wb["g_attn"], wb["w_q"], wb["segq"], wb["cntq"], wb["expq"], wb["gq"]]
    kw = N_KV_B * HD_B
    return pl.pallas_call(
        _proj_b_kernel,
        grid=(n // tm,),
        in_specs=[pl.BlockSpec((tm, D_MODEL), lambda i: (i, 0)),
                  pl.BlockSpec((3, tm, LANES), lambda i: (0, i % n_tab, 0))]
                 + [_const_spec(w.shape) for w in weights],
        out_specs=[pl.BlockSpec((tm, kw), lambda i: (i, 0)),
                   pl.BlockSpec((tm, kw), lambda i: (i, 0)),
                   pl.BlockSpec((tm, H_B * HD_B), lambda i: (i, 0))],
        out_shape=[jax.ShapeDtypeStruct((n, kw), F32),
                   jax.ShapeDtypeStruct((n, kw), F32),
                   jax.ShapeDtypeStruct((n, H_B * HD_B), BF16)],
        compiler_params=pltpu.CompilerParams(dimension_semantics=("arbitrary",),
                                             vmem_limit_bytes=VMEM_LIMIT),
        name="swa_proj",
    )(h, tab, *weights)


def _swa_halves(slab, kv):
    lane = lax.broadcasted_iota(jnp.int32, slab.shape, 1)
    own = (lane >= HD_B) if kv % 2 else (lane < HD_B)
    halves = [None, None]
    halves[kv % 2] = jnp.where(own, slab, 0.0)
    halves[1 - kv % 2] = pltpu.roll(halves[kv % 2], HD_B, 1)
    return [h.astype(BF16) for h in halves]


def _swa_attend(jobs, valid_of, sink_of):
    scores = [[_dot_nt(q, kh[par]) for par in range(2)] for q, kh, _, _ in jobs]
    probs = []
    for j, (q, _, _, kv) in enumerate(jobs):
        m2 = q.shape[0]
        top = lax.broadcasted_iota(jnp.int32, (m2, 1), 0) < (m2 // 2)
        row = []
        for par in range(2):
            s = jnp.where(valid_of(j), scores[j][par], NEG)
            sink = jnp.where(top, sink_of(G_B * kv + par), sink_of(G_B * kv + par + 2))
            m = jnp.maximum(jnp.max(s, axis=-1, keepdims=True), sink)
            p = jnp.exp(s - m)
            den = jnp.sum(p, axis=-1, keepdims=True) + jnp.exp(sink - m)
            row.append((p.astype(BF16), den))
        probs.append(row)
    outs = []
    for j, (_, _, vh, _) in enumerate(jobs):
        o = [_dot(probs[j][par][0], vh[par]) / probs[j][par][1] for par in range(2)]
        outs.append(o[0] + o[1])
    return outs


def _attn_b_kernel(sink_ref, q_ref, kp_ref, kc_ref, vp_ref, vc_ref, o_ref):
    g = pl.program_id(1)
    kcat = jnp.concatenate([kp_ref[...], kc_ref[...]], axis=0)
    vcat = jnp.concatenate([vp_ref[...], vc_ref[...]], axis=0)
    shape = (2 * WINDOW, 2 * WINDOW)
    qi = lax.broadcasted_iota(jnp.int32, shape, 0) & (WINDOW - 1)
    col = lax.broadcasted_iota(jnp.int32, shape, 1)
    band = (col > qi) & (col <= qi + WINDOW)
    band_first = band & ((col >= WINDOW) | (g > 0))
    n_blocks = q_ref.shape[0] // WINDOW
    for kv in range(N_KV_B):
        base = G_B * HD_B * kv
        ks = slice(LANES * (kv // 2), LANES * (kv // 2 + 1))
        k_half = _swa_halves(kcat[:, ks], kv)
        v_half = _swa_halves(vcat[:, ks], kv)
        jobs = []
        for r in range(n_blocks):
            rows = slice(WINDOW * r, WINDOW * (r + 1))
            win = slice(WINDOW * r, WINDOW * (r + 2))
            q_lhs = jnp.concatenate([q_ref[rows, base:base + LANES], q_ref[rows, base + LANES:base + 2 * LANES]], axis=0)
            jobs.append((q_lhs, [h[win] for h in k_half], [h[win] for h in v_half], kv))
        outs = _swa_attend(jobs, lambda r: band if r else band_first, lambda hh: sink_ref[hh])
        for r, o in enumerate(outs):
            rows = slice(WINDOW * r, WINDOW * (r + 1))
            o_ref[rows, base:base + LANES] = o[:WINDOW].astype(BF16)
            o_ref[rows, base + LANES:base + 2 * LANES] = o[WINDOW:].astype(BF16)


def _attn_b(sinks, q, k, v, batch, seq):
    nb = seq // WINDOW
    ng = nb // SWA_BLOCKS
    kw = N_KV_B * HD_B
    prev = lambda b, g: (b * nb + jnp.maximum(SWA_BLOCKS * g - 1, 0), 0)
    cur = lambda b, g: (b * ng + g, 0)
    return pl.pallas_call(
        _attn_b_kernel,
        grid=(batch, ng),
        in_specs=[pl.BlockSpec(memory_space=pltpu.SMEM),
                  pl.BlockSpec((SWA_BLOCKS * WINDOW, H_B * HD_B), cur),
                  pl.BlockSpec((WINDOW, kw), prev), pl.BlockSpec((SWA_BLOCKS * WINDOW, kw), cur),
                  pl.BlockSpec((WINDOW, kw), prev), pl.BlockSpec((SWA_BLOCKS * WINDOW, kw), cur)],
        out_specs=pl.BlockSpec((SWA_BLOCKS * WINDOW, H_B * HD_B), cur),
        out_shape=jax.ShapeDtypeStruct((batch * seq, H_B * HD_B), BF16),
        compiler_params=pltpu.CompilerParams(dimension_semantics=("arbitrary", "arbitrary"),
                                             vmem_limit_bytes=VMEM_LIMIT),
        name="swa_prompt_attn",
    )(sinks, q, k, k, v, v)


SWA_Q_ROWS = 8


def _attn_b_sample_kernel(sink_ref, q_ref, k_ref, v_ref, o_ref, *, t, w_buf):
    keys = k_ref.shape[1]
    rows = 2 * SWA_Q_ROWS
    ti = lax.broadcasted_iota(jnp.int32, (rows, keys), 0) & (SWA_Q_ROWS - 1)
    col = lax.broadcasted_iota(jnp.int32, (rows, keys), 1)
    diff = jnp.where(col < w_buf, ti + w_buf - col, ti - (col - w_buf))
    valid = (diff >= 0) & (diff < WINDOW) & (col < w_buf + t) & (ti < t)
    jobs = []
    for b in range(q_ref.shape[0]):
        for kv in range(N_KV_B):
            ks = slice(LANES * (kv // 2), LANES * (kv // 2 + 1))
            jobs.append((q_ref[b, kv], _swa_halves(k_ref[b, :, ks], kv), _swa_halves(v_ref[b, :, ks], kv), kv))
    outs = _swa_attend(jobs, lambda j: valid, lambda hh: sink_ref[hh])
    for j, o in enumerate(outs):
        o_ref[j // N_KV_B, j % N_KV_B] = o


def _attn_b_sample(sinks, q, k_all, v_all, t, w_buf):
    bd = q.shape[0]
    bs = SWA_SAMPLE_SEQS
    blk = lambda a: pl.BlockSpec((bs,) + a.shape[1:], lambda b: (b,) + (0,) * (a.ndim - 1))
    return pl.pallas_call(
        functools.partial(_attn_b_sample_kernel, t=t, w_buf=w_buf),
        grid=(bd // bs,),
        in_specs=[pl.BlockSpec(memory_space=pltpu.SMEM), blk(q), blk(k_all), blk(v_all)],
        out_specs=blk(q),
        out_shape=jax.ShapeDtypeStruct(q.shape, F32),
        compiler_params=pltpu.CompilerParams(dimension_semantics=("arbitrary",),
                                             vmem_limit_bytes=VMEM_LIMIT),
        name="swa_sample_attn",
    )(sinks, q, k_all, v_all)


def _rope_tables(pos, n_rot, period, lane_lo):
    half = n_rot // 2
    inv = ROPE_THETA ** (-jnp.arange(0, n_rot, 2, dtype=F32) / n_rot)
    ang = pos.astype(F32)[:, None] * inv[None, :]
    cos, sin = jnp.cos(ang), jnp.sin(ang)
    rel = np.arange(LANES) % period - lane_lo
    in1 = (rel >= 0) & (rel < half)
    in2 = (rel >= half) & (rel < 2 * half)
    idx = np.where(in1, rel, np.where(in2, rel - half, 0))
    cg, sg = cos[:, idx], sin[:, idx]
    return jnp.stack([jnp.where(in1 | in2, cg, 1.0), jnp.where(in1, -sg, 0.0), jnp.where(in2, sg, 0.0)])


def _segments(width, seg_lanes):
    seg = np.zeros((width, LANES), np.float32)
    cnt = np.zeros((1, LANES), np.float32)
    for s, (lo, hi) in enumerate(seg_lanes):
        seg[lo:hi, s] = 1.0
        cnt[0, s] = 1.0 / (hi - lo)
    expand = np.concatenate([seg.T, seg.T], axis=0)
    return jnp.asarray(seg, BF16), jnp.asarray(cnt, F32), jnp.asarray(expand, BF16)


def _slab_gain(parts):
    g = jnp.zeros((SLAB,), F32)
    for lo, vals in parts:
        g = g.at[lo:lo + vals.shape[0]].set(vals)
    return g[None, :]


def _prep_a(norm_attn, w_a_in, g_qc, w_uq, g_ckv, w_uk, w_uv, g_qn, g_qr, g_kn, g_kr):
    w_in = jnp.concatenate([w_a_in[:, :D_QC + D_C], jnp.zeros((D_MODEL, KPE_LANE), F32),
                            w_a_in[:, D_QC + D_C:], jnp.zeros((D_MODEL, SLAB - KPE_LANE - D_ROPE), F32)], axis=1)
    dqk = D_NOPE + D_ROPE
    w_uq_pad = jnp.pad(w_uq.reshape(D_QC, H_A, dqk), ((0, 0), (0, 0), (0, SLAB - dqk))).reshape(D_QC, H_A * SLAB)
    w_uk3 = w_uk.reshape(D_C, H_A, D_NOPE)
    w_uk_pad = jnp.pad(w_uk3, ((0, 0), (0, 0), (0, SLAB - D_NOPE))).reshape(D_C, H_A * SLAB)
    w_uv3 = w_uv.reshape(D_C, H_A // 2, 2, D_V)
    even = jnp.pad(w_uv3[:, :, 0], ((0, 0), (0, 0), (0, SLAB - D_V)))
    odd = jnp.pad(w_uv3[:, :, 1], ((0, 0), (0, 0), (SLAB - D_V, 0)))
    w_uv_pad = jnp.stack([even, odd], axis=2).reshape(D_C, H_A * SLAB)
    q_segs = []
    k_segs = []
    for h in range(H_A):
        q_segs += [(SLAB * h, SLAB * h + D_NOPE), (SLAB * h + KPE_LANE, SLAB * h + KPE_LANE + D_ROPE)]
        k_segs += [(SLAB * h, SLAB * h + D_NOPE)]
    segq, cntq, expq = _segments(H_A * SLAB, q_segs)
    segk, cntk, expk = _segments(H_A * SLAB, k_segs)
    gq = jnp.tile(_slab_gain([(0, g_qn * SCALE_A), (KPE_LANE, g_qr * SCALE_A)]), (1, H_A))
    gk_slab = _slab_gain([(0, g_kn)])
    wukt = jnp.pad(jnp.transpose(w_uk3, (1, 2, 0)), ((0, 0), (0, SLAB - D_NOPE), (0, 0)))
    return dict(
        g_attn=norm_attn[None, :], w_in=w_in.astype(BF16), g_qc=g_qc[None, :], g_ckv=g_ckv[None, :],
        g_kpe=_slab_gain([(KPE_LANE, g_kr)]),
        w_uq=w_uq_pad.astype(BF16), segq=segq, cntq=cntq, expq=expq, gq=gq,
        w_uk=w_uk_pad.astype(BF16), segk=segk, cntk=cntk, expk=expk, gk=jnp.tile(gk_slab, (1, H_A)),
        w_uv=w_uv_pad.astype(BF16), gk_slab=gk_slab, wukt_pad=wukt.astype(BF16),
        wukt=jnp.transpose(w_uk).astype(BF16))


def _prep_b(g_kv, w_kv, g_k, norm_attn, w_q, g_q):
    kw = N_KV_B * HD_B
    segk, cntk, expk = _segments(kw, [(HD_B * h, HD_B * (h + 1)) for h in range(N_KV_B)])
    segq, cntq, expq = _segments(H_B * HD_B, [(HD_B * h, HD_B * (h + 1)) for h in range(H_B)])
    return dict(g_kv=g_kv[None, :], w_kv=w_kv.astype(BF16), segk=segk, cntk=cntk, expk=expk,
                gk=jnp.tile(g_k, N_KV_B)[None, :], g_attn=norm_attn[None, :], w_q=w_q.astype(BF16),
                segq=segq, cntq=cntq, expq=expq, gq=jnp.tile(g_q * SCALE_B, H_B)[None, :])


def kernel(x_prompt, x_sample, cache_mla, state_win_k, state_win_v, page_table, norm_attn, norm_ffn, w_a_in, g_qc, w_uq, g_ckv, w_uk, w_uv, g_qn_a, g_qr_a, g_kn_a, g_kr_a, w_a_out, g_kv_shared, w_kv_shared, g_k_b, w_q_b, g_q_b, sinks, w_b_out, w_ffn_in, w_ffn_out):
    batch, seq, _ = x_prompt.shape
    bd, t_dec, _ = x_sample.shape
    past_len = page_table.shape[1] * PAGE_SIZE
    w_buf = state_win_k.shape[1]
    kw = N_KV_B * HD_B
    assert w_a_in.shape[0] == 1 and w_q_b.shape[0] == 1, "one MLA layer followed by one SWA layer"
    assert w_buf == WINDOW and seq % TM_POST == 0 and (bd * t_dec) % 8 == 0
    assert t_dec <= SWA_Q_ROWS and bd % SWA_SAMPLE_SEQS == 0 and page_table.shape[1] % PAGES_PER_STEP == 0

    wa = _prep_a(norm_attn[0], w_a_in[0], g_qc[0], w_uq[0], g_ckv[0], w_uk[0], w_uv[0],
                 g_qn_a[0], g_qr_a[0], g_kn_a[0], g_kr_a[0])
    wb = _prep_b(g_kv_shared, w_kv_shared, g_k_b, norm_attn[1], w_q_b[0], g_q_b[0])
    w_a_out_b = w_a_out[0].astype(BF16)
    w_b_out_b = w_b_out[0].astype(BF16)
    ffn_in = w_ffn_in.astype(BF16)
    ffn_out = w_ffn_out.astype(BF16)
    g_ffn = norm_ffn[:, None, :]
    sink_b = sinks[0]

    pos_p = jnp.arange(seq)
    n_s = bd * t_dec
    pos_s = past_len + jnp.arange(n_s) % t_dec

    xp = x_prompt.reshape(batch * seq, D_MODEL)
    q, k, v, rows_p = _proj_a(xp, _rope_tables(pos_p, D_ROPE, SLAB, KPE_LANE), wa, TM_MLA_PROJ)
    o = _attn_a(q, k, v, batch, seq, TQ_MLA)
    h = _post(xp, o, w_a_out_b, g_ffn[0], ffn_in[0], ffn_out[0], TM_POST)
    k_p, v_p, q_b = _proj_b(h, _rope_tables(pos_p, ROT_B, HD_B, 0), wb, TM_SWA_PROJ)
    o = _attn_b(sink_b, q_b, k_p, v_p, batch, seq)
    y_prompt = _post(h, o, w_b_out_b, g_ffn[1], ffn_in[1], ffn_out[1], TM_POST)

    xs = x_sample.reshape(n_s, D_MODEL)
    q, _, _, rows_s = _proj_a(xs, _rope_tables(pos_s, D_ROPE, SLAB, KPE_LANE), wa, n_s)
    qabs = _qabs(q, wa["gk_slab"], wa["wukt_pad"]).reshape(bd, t_dec * H_A, D_C)
    qpe = q.reshape(n_s, H_A, SLAB)[:, :, KPE_LANE:KPE_LANE + D_ROPE].reshape(bd, t_dec * H_A, D_ROPE)
    new_pad = jnp.pad(jnp.swapaxes(rows_s.reshape(bd, t_dec, D_CKV), 1, 2), ((0, 0), (0, 0), (0, PAGE_SIZE - t_dec)))
    olat = _paged_attn(page_table, jnp.swapaxes(cache_mla, 2, 3), wa["wukt"], qabs, qpe, new_pad, t_dec)
    o = _latent_out(olat.reshape(n_s, H_A * D_C), wa["w_uv"])
    h = _post(xs, o, w_a_out_b, g_ffn[0], ffn_in[0], ffn_out[0], n_s)
    k_s, v_s, q_b = _proj_b(h, _rope_tables(pos_s, ROT_B, HD_B, 0), wb, n_s)
    q4 = jnp.transpose(q_b.reshape(bd, t_dec, N_KV_B, 2, LANES), (0, 2, 3, 1, 4))
    q4 = jnp.pad(q4, ((0, 0), (0, 0), (0, 0), (0, SWA_Q_ROWS - t_dec), (0, 0))).reshape(bd, N_KV_B, 2 * SWA_Q_ROWS, LANES)
    key_pad = jnp.zeros((bd, 16 - t_dec, kw), F32)
    k_all = jnp.concatenate([state_win_k.reshape(bd, w_buf, kw), k_s.reshape(bd, t_dec, kw), key_pad], axis=1)
    v_all = jnp.concatenate([state_win_v.reshape(bd, w_buf, kw), v_s.reshape(bd, t_dec, kw), key_pad], axis=1)
    o4 = _attn_b_sample(sink_b, q4, k_all, v_all, t_dec, w_buf)
    o = jnp.transpose(o4.reshape(bd, N_KV_B, 2, SWA_Q_ROWS, LANES)[:, :, :, :t_dec], (0, 3, 1, 2, 4))
    y_sample = _post(h, o.reshape(n_s, H_B * HD_B).astype(BF16), w_b_out_b, g_ffn[1], ffn_in[1], ffn_out[1], n_s)

    w_p = min(WINDOW, seq)
    k_p4 = k_p.reshape(batch, seq, N_KV_B, HD_B)
    v_p4 = v_p.reshape(batch, seq, N_KV_B, HD_B)
    win_k_s = jnp.concatenate([state_win_k, k_s.reshape(bd, t_dec, N_KV_B, HD_B)], axis=1)[:, -w_buf:]
    win_v_s = jnp.concatenate([state_win_v, v_s.reshape(bd, t_dec, N_KV_B, HD_B)], axis=1)[:, -w_buf:]
    return (y_prompt.reshape(batch, seq, D_MODEL), y_sample.reshape(bd, t_dec, D_MODEL),
            rows_p.reshape(1, batch, seq, D_CKV), rows_s.reshape(1, bd, t_dec, D_CKV),
            k_p4[:, seq - w_p:], v_p4[:, seq - w_p:], win_k_s, win_v_s)
```

```python
import functools

import numpy as np
import jax
import jax.numpy as jnp
from jax import lax
from jax.experimental import pallas as pl
from jax.experimental.pallas import tpu as pltpu

F32 = jnp.float32
BF16 = jnp.bfloat16

D_MODEL = 1024
PAGE_SIZE = 128
H_A = 16
D_NOPE = 64
D_ROPE = 32
D_V = 64
D_QC = 384
D_C = 256
D_CKV = D_C + D_ROPE
SCALE_A = (D_NOPE + D_ROPE) ** -0.5
H_B = 16
N_KV_B = 4
HD_B = 64
G_B = H_B // N_KV_B
WINDOW = 128
ROT_B = HD_B // 4
SCALE_B = HD_B ** -0.5
D_FF = 2816
ROPE_THETA = 500000.0
EPS = 1e-6
NEG = -1e30

LANES = 128
SLAB = 128
A_IN_COLS = 768
KPE_LANE = 64
VMEM_LIMIT = 56 * 1024 * 1024
FF_CHUNK = 256
PAGES_PER_GROUP = 16
TM_MLA_PROJ = 256
TM_SWA_PROJ = 512
TM_POST = 512
TQ_MLA = 256
ATTN_LOOKAHEAD = 2
SWA_SAMPLE_SEQS = 8
SWA_BLOCKS = 4

_NT = (((1,), (1,)), ((), ()))


def _dot(a, b):
    return jnp.dot(a, b, preferred_element_type=F32)


def _dot_nt(a, b):
    return lax.dot_general(a, b, _NT, preferred_element_type=F32)


def _rms(x, g):
    ms = jnp.mean(x * x, axis=-1, keepdims=True)
    return x * lax.rsqrt(ms + EPS) * g


def _rope_slab(x, c, s1, s2, half):
    return x * c + pltpu.roll(x, LANES - half, 1) * s1 + pltpu.roll(x, half, 1) * s2


def _segment_scales(items):
    sums = [_dot((raw * raw).astype(BF16), seg_ref[...]) for raw, seg_ref, _, _ in items]
    scales = []
    for ss, (_, _, inv_cnt_ref, expand_ref) in zip(sums, items):
        rs = lax.rsqrt(ss * inv_cnt_ref[...] + EPS)
        hi = rs.astype(BF16)
        lo = (rs - hi.astype(F32)).astype(BF16)
        scales.append(_dot(jnp.concatenate([hi, lo], axis=1), expand_ref[...]))
    return scales


def _proj_a_kernel(x_ref, tab_ref, g_attn_ref, w_in_ref, g_qc_ref, g_ckv_ref, g_kpe_ref,
                   w_uq_ref, segq_ref, cntq_ref, expq_ref, gq_ref,
                   w_uk_ref, segk_ref, cntk_ref, expk_ref, gk_ref, w_uv_ref,
                   q_ref, k_ref, v_ref, rows_ref):
    c, s1, s2 = tab_ref[0], tab_ref[1], tab_ref[2]
    hn = _rms(x_ref[...], g_attn_ref[...]).astype(BF16)
    a = _dot(hn, w_in_ref[...])
    cq = _rms(a[:, :D_QC], g_qc_ref[...]).astype(BF16)
    ckv = _rms(a[:, D_QC:D_QC + D_C], g_ckv_ref[...])
    kpe = a[:, D_QC + D_C:]
    ms = jnp.sum(kpe * kpe, axis=-1, keepdims=True) * (1.0 / D_ROPE)
    kpe = _rope_slab(kpe * lax.rsqrt(ms + EPS) * g_kpe_ref[...], c, s1, s2, D_ROPE // 2)

    ckv_b = ckv.astype(BF16)
    q_raw = _dot(cq, w_uq_ref[...])
    qn = q_raw * _segment_scales([(q_raw, segq_ref, cntq_ref, expq_ref)])[0] * gq_ref[...]
    k_raw = _dot(ckv_b, w_uk_ref[...])
    kn = k_raw * _segment_scales([(k_raw, segk_ref, cntk_ref, expk_ref)])[0] * gk_ref[...]
    for h in range(H_A):
        sl = slice(SLAB * h, SLAB * (h + 1))
        q_ref[:, sl] = _rope_slab(qn[:, sl], c, s1, s2, D_ROPE // 2).astype(BF16)
        k_ref[:, sl] = (kn[:, sl] + kpe).astype(BF16)
    v_ref[...] = _dot(ckv_b, w_uv_ref[...]).astype(BF16)
    rows_ref[:, :D_C] = ckv
    rows_ref[:, D_C:] = kpe[:, KPE_LANE:KPE_LANE + D_ROPE]


def _const_spec(shape):
    zeros = (0,) * len(shape)
    return pl.BlockSpec(shape, lambda *_: zeros, pipeline_mode=pl.Buffered(1))


def _proj_a(x, tab, wa, tm):
    n = x.shape[0]
    n_tab = tab.shape[1] // tm
    weights = [wa["g_attn"], wa["w_in"], wa["g_qc"], wa["g_ckv"], wa["g_kpe"],
               wa["w_uq"], wa["segq"], wa["cntq"], wa["expq"], wa["gq"],
               wa["w_uk"], wa["segk"], wa["cntk"], wa["expk"], wa["gk"], wa["w_uv"]]
    wide = H_A * SLAB
    return pl.pallas_call(
        _proj_a_kernel,
        grid=(n // tm,),
        in_specs=[pl.BlockSpec((tm, D_MODEL), lambda i: (i, 0)),
                  pl.BlockSpec((3, tm, LANES), lambda i: (0, i % n_tab, 0))]
                 + [_const_spec(w.shape) for w in weights],
        out_specs=[pl.BlockSpec((tm, wide), lambda i: (i, 0)),
                   pl.BlockSpec((tm, wide), lambda i: (i, 0)),
                   pl.BlockSpec((tm, wide), lambda i: (i, 0)),
                   pl.BlockSpec((tm, D_CKV), lambda i: (i, 0))],
        out_shape=[jax.ShapeDtypeStruct((n, wide), BF16),
                   jax.ShapeDtypeStruct((n, wide), BF16),
                   jax.ShapeDtypeStruct((n, wide), BF16),
                   jax.ShapeDtypeStruct((n, D_CKV), F32)],
        compiler_params=pltpu.CompilerParams(dimension_semantics=("arbitrary",),
                                             vmem_limit_bytes=VMEM_LIMIT),
        name="mla_proj",
    )(x, tab, *weights)


def _attn_a_kernel(q_ref, k_ref, v_ref, o_ref, *, tq):
    seq = q_ref.shape[0]
    causal = (lax.broadcasted_iota(jnp.int32, (tq, tq), 1) <= lax.broadcasted_iota(jnp.int32, (tq, tq), 0))
    jobs = [(c, e) for c in range(seq // tq) for e in range(2)]

    def windows(c, e):
        return slice(c * tq, (c + 1) * tq), slice(0, c * tq), slice(SLAB * e, SLAB * (e + 1))

    def score(c, e):
        rows, past, ls = windows(c, e)
        q = q_ref[rows, ls]
        s_d = _dot_nt(q, k_ref[rows, ls])
        return s_d, (_dot_nt(q, k_ref[past, ls]) if c else None)

    def attend(c, e, s_d, s_p):
        rows, past, ls = windows(c, e)
        s_d = jnp.where(causal, s_d, NEG)
        m = jnp.max(s_d, axis=-1, keepdims=True)
        if c:
            m = jnp.maximum(m, jnp.max(s_p, axis=-1, keepdims=True))
        p_d = jnp.exp(s_d - m)
        l = jnp.sum(p_d, axis=-1, keepdims=True)
        acc = _dot(p_d.astype(BF16), v_ref[rows, ls])
        if c:
            p_p = jnp.exp(s_p - m)
            l = l + jnp.sum(p_p, axis=-1, keepdims=True)
            acc = acc + _dot(p_p.astype(BF16), v_ref[past, ls])
        return acc / l

    ahead = [score(*jobs[j]) for j in range(min(ATTN_LOOKAHEAD, len(jobs)))]
    out = None
    for j, (c, e) in enumerate(jobs):
        if j + ATTN_LOOKAHEAD < len(jobs):
            ahead.append(score(*jobs[j + ATTN_LOOKAHEAD]))
        o_e = attend(c, e, *ahead[j])
        ahead[j] = None
        out = o_e if e == 0 else out + o_e
        if e == 1:
            o_ref[c * tq:(c + 1) * tq, :] = out.astype(BF16)


def _attn_a(q, k, v, batch, seq, tq):
    pairs = H_A // 2
    return pl.pallas_call(
        functools.partial(_attn_a_kernel, tq=tq),
        grid=(batch, pairs),
        in_specs=[pl.BlockSpec((seq, 2 * SLAB), lambda b, j: (b, j)),
                  pl.BlockSpec((seq, 2 * SLAB), lambda b, j: (b, j)),
                  pl.BlockSpec((seq, 2 * SLAB), lambda b, j: (b, j))],
        out_specs=pl.BlockSpec((seq, SLAB), lambda b, j: (b, j)),
        out_shape=jax.ShapeDtypeStruct((batch * seq, H_A * D_V), BF16),
        compiler_params=pltpu.CompilerParams(
            dimension_semantics=("arbitrary", "arbitrary"),
            vmem_limit_bytes=VMEM_LIMIT),
        name="mla_prompt_attn",
    )(q, k, v)


def _qabs_kernel(q_ref, gk_ref, wukt_ref, o_ref):
    for h in range(H_A):
        qs = (q_ref[:, SLAB * h:SLAB * (h + 1)].astype(F32) * gk_ref[...]).astype(BF16)
        o_ref[:, D_C * h:D_C * (h + 1)] = _dot(qs, wukt_ref[h]).astype(BF16)


def _qabs(q, gk_slab, wukt):
    n = q.shape[0]
    return pl.pallas_call(
        _qabs_kernel,
        out_shape=jax.ShapeDtypeStruct((n, H_A * D_C), BF16),
        compiler_params=pltpu.CompilerParams(vmem_limit_bytes=VMEM_LIMIT),
        name="mla_absorb_q",
    )(q, gk_slab, wukt)


def _paged_kernel(pt_ref, cache_ref, wukt_ref, qabs_ref, qpe_ref, new_ref, o_ref, lhs_sc, pg_sc, sem, *, n_pages, t_new):
    seq = pl.program_id(0)
    group = pg_sc.shape[1]
    n_groups = n_pages // group
    rows_q = qabs_ref.shape[0]

    def page_copy(sq, g, u):
        slot = g % 2
        return pltpu.make_async_copy(cache_ref.at[0, pt_ref[sq, g * group + u]], pg_sc.at[slot, u], sem.at[slot])

    def start_group(sq, g):
        for u in range(group):
            page_copy(sq, g, u).start()

    def wait_group(sq, g):
        for u in range(group):
            page_copy(sq, g, u).wait()

    @pl.when(seq == 0)
    def _():
        start_group(seq, 0)

    lhs_sc[:H_A * D_NOPE, :] = wukt_ref[...]
    lhs_sc[H_A * D_NOPE:, :] = qabs_ref[...]

    def scores(ct, kpet):
        keys = ct.shape[1]
        big = _dot(lhs_sc[...], ct)
        kt = big[:H_A * D_NOPE]
        ssq = jnp.sum((kt * kt).reshape(D_NOPE, H_A, keys), axis=0)
        rs = lax.rsqrt(ssq * (1.0 / D_NOPE) + EPS)
        rs_q = jnp.concatenate([rs] * (rows_q // H_A), axis=0)
        return big[H_A * D_NOPE:] * rs_q + _dot(qpe_ref[...], kpet)

    def accumulate(state, s, ct):
        m_old, l, acc = state
        m_new = jnp.maximum(m_old, jnp.max(s, axis=-1, keepdims=True))
        corr = jnp.exp(m_old - m_new)
        p = jnp.exp(s - m_new)
        return (m_new, l * corr + jnp.sum(p, axis=-1, keepdims=True),
                acc * corr + _dot_nt(p.astype(BF16), ct))

    state = (jnp.full((rows_q, 1), NEG, F32), jnp.zeros((rows_q, 1), F32), jnp.zeros((rows_q, D_C), F32))
    pending = None
    for g in range(n_groups):
        wait_group(seq, g)
        if g + 1 < n_groups:
            start_group(seq, g + 1)
        else:
            @pl.when(seq + 1 < pl.num_programs(0))
            def _():
                start_group(seq + 1, 0)
        slot = g % 2
        cts, ss = [], []
        for u in range(0, group, 2):
            ct = jnp.concatenate([pg_sc[slot, u, :D_C, :], pg_sc[slot, u + 1, :D_C, :]], axis=1).astype(BF16)
            kpet = jnp.concatenate([pg_sc[slot, u, D_C:, :], pg_sc[slot, u + 1, D_C:, :]], axis=1).astype(BF16)
            cts.append(ct)
            ss.append(scores(ct, kpet))
        if pending is not None:
            state = accumulate(state, *pending)
        pending = (jnp.concatenate(ss, axis=1), jnp.concatenate(cts, axis=1))
    state = accumulate(state, *pending)

    keys = new_ref.shape[1]
    t_row = lax.shift_right_logical(lax.broadcasted_iota(jnp.int32, (rows_q, keys), 0), H_A.bit_length() - 1)
    s_col = lax.broadcasted_iota(jnp.int32, (rows_q, keys), 1)
    ct = new_ref[:D_C, :].astype(BF16)
    s = scores(ct, new_ref[D_C:, :].astype(BF16))
    _, l, acc = accumulate(state, jnp.where((s_col <= t_row) & (s_col < t_new), s, NEG), ct)
    o_ref[...] = acc / l


def _paged_attn(page_table, cache, wukt, qabs, qpe, new_pad, t_new):
    bd, n_pages = page_table.shape
    rows_q = qabs.shape[1]
    assert (n_pages // PAGES_PER_GROUP) % 2 == 0, "the slot of a page group must not depend on the sequence"
    grid_spec = pltpu.PrefetchScalarGridSpec(
        num_scalar_prefetch=1,
        grid=(bd,),
        in_specs=[pl.BlockSpec(memory_space=pl.ANY),
                  pl.BlockSpec(wukt.shape, lambda b, pt: (0, 0)),
                  pl.BlockSpec((None, rows_q, D_C), lambda b, pt: (b, 0, 0)),
                  pl.BlockSpec((None, rows_q, D_ROPE), lambda b, pt: (b, 0, 0)),
                  pl.BlockSpec((None,) + new_pad.shape[1:], lambda b, pt: (b, 0, 0))],
        out_specs=pl.BlockSpec((None, rows_q, D_C), lambda b, pt: (b, 0, 0)),
        scratch_shapes=[pltpu.VMEM((H_A * D_NOPE + rows_q, D_C), BF16),
                        pltpu.VMEM((2, PAGES_PER_GROUP, D_CKV, PAGE_SIZE), F32),
                        pltpu.SemaphoreType.DMA((2,))])
    return pl.pallas_call(
        functools.partial(_paged_kernel, n_pages=n_pages, t_new=t_new),
        grid_spec=grid_spec,
        out_shape=jax.ShapeDtypeStruct((bd, rows_q, D_C), F32),
        compiler_params=pltpu.CompilerParams(dimension_semantics=("arbitrary",),
                                             vmem_limit_bytes=VMEM_LIMIT),
        name="mla_paged_attn",
    )(page_table, cache, wukt, qabs, qpe, new_pad)


def _latent_out_kernel(olat_ref, w_uv_ref, o_ref):
    for j in range(H_A // 2):
        acc = None
        for e in range(2):
            h = 2 * j + e
            part = _dot(olat_ref[:, D_C * h:D_C * (h + 1)].astype(BF16),
                        w_uv_ref[:, SLAB * h:SLAB * (h + 1)])
            acc = part if acc is None else acc + part
        o_ref[:, SLAB * j:SLAB * (j + 1)] = acc.astype(BF16)


def _latent_out(olat, w_uv_pad):
    n = olat.shape[0]
    return pl.pallas_call(
        _latent_out_kernel,
        out_shape=jax.ShapeDtypeStruct((n, H_A * D_V), BF16),
        compiler_params=pltpu.CompilerParams(vmem_limit_bytes=VMEM_LIMIT),
        name="mla_latent_out",
    )(olat, w_uv_pad)


def _post_kernel(x_ref, o_ref, w_o_ref, g_ref, w_in_ref, w_out_ref, y_ref):
    h1 = x_ref[...] + _dot(o_ref[...], w_o_ref[...])
    hn = _rms(h1, g_ref[...]).astype(BF16)
    acc = h1
    for c in range(D_FF // FF_CHUNK):
        lo = c * FF_CHUNK
        a1 = _dot(hn, w_in_ref[:, lo:lo + FF_CHUNK])
        a2 = _dot(hn, w_in_ref[:, D_FF + lo:D_FF + lo + FF_CHUNK])
        gate = (a1 * jax.nn.sigmoid(a1)) * a2
        acc = acc + _dot(gate.astype(BF16), w_out_ref[lo:lo + FF_CHUNK, :])
    y_ref[...] = acc


def _post(x, o, w_o, g, w_in, w_out, tm):
    n = x.shape[0]
    return pl.pallas_call(
        _post_kernel,
        grid=(n // tm,),
        in_specs=[pl.BlockSpec((tm, D_MODEL), lambda i: (i, 0)),
                  pl.BlockSpec((tm, o.shape[1]), lambda i: (i, 0)),
                  _const_spec(w_o.shape), _const_spec(g.shape),
                  _const_spec(w_in.shape), _const_spec(w_out.shape)],
        out_specs=pl.BlockSpec((tm, D_MODEL), lambda i: (i, 0)),
        out_shape=jax.ShapeDtypeStruct((n, D_MODEL), F32),
        compiler_params=pltpu.CompilerParams(dimension_semantics=("arbitrary",),
                                             vmem_limit_bytes=VMEM_LIMIT),
        name="outproj_swiglu",
    )(x, o, w_o, g, w_in, w_out)


def _proj_b_kernel(h_ref, tab_ref, g_kv_ref, w_kv_ref, segk_ref, cntk_ref, expk_ref, gk_ref,
                   g_attn_ref, w_q_ref, segq_ref, cntq_ref, expq_ref, gq_ref,
                   k_ref, v_ref, q_ref):
    c, s1, s2 = tab_ref[0], tab_ref[1], tab_ref[2]
    h = h_ref[...]
    hr = h * lax.rsqrt(jnp.mean(h * h, axis=-1, keepdims=True) + EPS)
    kv = _dot((hr * g_kv_ref[...]).astype(BF16), w_kv_ref[...])
    q_raw = _dot((hr * g_attn_ref[...]).astype(BF16), w_q_ref[...])
    kw = N_KV_B * HD_B
    k_raw = kv[:, :kw]
    v_ref[...] = kv[:, kw:]
    k_scale, q_scale = _segment_scales([(k_raw, segk_ref, cntk_ref, expk_ref),
                                        (q_raw, segq_ref, cntq_ref, expq_ref)])
    kn = k_raw * k_scale * gk_ref[...]
    for j in range(kw // LANES):
        sl = slice(LANES * j, LANES * (j + 1))
        k_ref[:, sl] = _rope_slab(kn[:, sl], c, s1, s2, ROT_B // 2)
    qn = q_raw * q_scale * gq_ref[...]
    for j in range(H_B * HD_B // LANES):
        sl = slice(LANES * j, LANES * (j + 1))
        q_ref[:, sl] = _rope_slab(qn[:, sl], c, s1, s2, ROT_B // 2).astype(BF16)


def _proj_b(h, tab, wb, tm):
    n = h.shape[0]
    n_tab = tab.shape[1] // tm
    weights = [wb["g_kv"], wb["w_kv"], wb["segk"], wb["cntk"], wb["expk"], wb["gk"],
               wb["g_attn"], wb["w_q"], wb["segq"], wb["cntq"], wb["expq"], wb["gq"]]
    kw = N_KV_B * HD_B
    return pl.pallas_call(
        _proj_b_kernel,
        grid=(n // tm,),
        in_specs=[pl.BlockSpec((tm, D_MODEL), lambda i: (i, 0)),
                  pl.BlockSpec((3, tm, LANES), lambda i: (0, i % n_tab, 0))]
                 + [_const_spec(w.shape) for w in weights],
        out_specs=[pl.BlockSpec((tm, kw), lambda i: (i, 0)),
                   pl.BlockSpec((tm, kw), lambda i: (i, 0)),
                   pl.BlockSpec((tm, H_B * HD_B), lambda i: (i, 0))],
        out_shape=[jax.ShapeDtypeStruct((n, kw), F32),
                   jax.ShapeDtypeStruct((n, kw), F32),
                   jax.ShapeDtypeStruct((n, H_B * HD_B), BF16)],
        compiler_params=pltpu.CompilerParams(dimension_semantics=("arbitrary",),
                                             vmem_limit_bytes=VMEM_LIMIT),
        name="swa_proj",
    )(h, tab, *weights)


def _swa_halves(slab, kv):
    lane = lax.broadcasted_iota(jnp.int32, slab.shape, 1)
    own = (lane >= HD_B) if kv % 2 else (lane < HD_B)
    halves = [None, None]
    halves[kv % 2] = jnp.where(own, slab, 0.0)
    halves[1 - kv % 2] = pltpu.roll(halves[kv % 2], HD_B, 1)
    return [h.astype(BF16) for h in halves]


def _swa_attend(jobs, valid_of, sink_of):
    scores = [[_dot_nt(q, kh[par]) for par in range(2)] for q, kh, _, _ in jobs]
    probs = []
    for j, (q, _, _, kv) in enumerate(jobs):
        m2 = q.shape[0]
        top = lax.broadcasted_iota(jnp.int32, (m2, 1), 0) < (m2 // 2)
        row = []
        for par in range(2):
            s = jnp.where(valid_of(j), scores[j][par], NEG)
            sink = jnp.where(top, sink_of(G_B * kv + par), sink_of(G_B * kv + par + 2))
            m = jnp.maximum(jnp.max(s, axis=-1, keepdims=True), sink)
            p = jnp.exp(s - m)
            den = jnp.sum(p, axis=-1, keepdims=True) + jnp.exp(sink - m)
            row.append((p.astype(BF16), den))
        probs.append(row)
    outs = []
    for j, (_, _, vh, _) in enumerate(jobs):
        o = [_dot(probs[j][par][0], vh[par]) / probs[j][par][1] for par in range(2)]
        outs.append(o[0] + o[1])
    return outs


def _attn_b_kernel(sink_ref, q_ref, kp_ref, kc_ref, vp_ref, vc_ref, o_ref):
    g = pl.program_id(1)
    kcat = jnp.concatenate([kp_ref[...], kc_ref[...]], axis=0)
    vcat = jnp.concatenate([vp_ref[...], vc_ref[...]], axis=0)
    shape = (2 * WINDOW, 2 * WINDOW)
    qi = lax.broadcasted_iota(jnp.int32, shape, 0) & (WINDOW - 1)
    col = lax.broadcasted_iota(jnp.int32, shape, 1)
    band = (col > qi) & (col <= qi + WINDOW)
    band_first = band & ((col >= WINDOW) | (g > 0))
    n_blocks = q_ref.shape[0] // WINDOW
    for kv in range(N_KV_B):
        base = G_B * HD_B * kv
        ks = slice(LANES * (kv // 2), LANES * (kv // 2 + 1))
        k_half = _swa_halves(kcat[:, ks], kv)
        v_half = _swa_halves(vcat[:, ks], kv)
        jobs = []
        for r in range(n_blocks):
            rows = slice(WINDOW * r, WINDOW * (r + 1))
            win = slice(WINDOW * r, WINDOW * (r + 2))
            q_lhs = jnp.concatenate([q_ref[rows, base:base + LANES], q_ref[rows, base + LANES:base + 2 * LANES]], axis=0)
            jobs.append((q_lhs, [h[win] for h in k_half], [h[win] for h in v_half], kv))
        outs = _swa_attend(jobs, lambda r: band if r else band_first, lambda hh: sink_ref[hh])
        for r, o in enumerate(outs):
            rows = slice(WINDOW * r, WINDOW * (r + 1))
            o_ref[rows, base:base + LANES] = o[:WINDOW].astype(BF16)
            o_ref[rows, base + LANES:base + 2 * LANES] = o[WINDOW:].astype(BF16)


def _attn_b(sinks, q, k, v, batch, seq):
    nb = seq // WINDOW
    ng = nb // SWA_BLOCKS
    kw = N_KV_B * HD_B
    prev = lambda b, g: (b * nb + jnp.maximum(SWA_BLOCKS * g - 1, 0), 0)
    cur = lambda b, g: (b * ng + g, 0)
    return pl.pallas_call(
        _attn_b_kernel,
        grid=(batch, ng),
        in_specs=[pl.BlockSpec(memory_space=pltpu.SMEM),
                  pl.BlockSpec((SWA_BLOCKS * WINDOW, H_B * HD_B), cur),
                  pl.BlockSpec((WINDOW, kw), prev), pl.BlockSpec((SWA_BLOCKS * WINDOW, kw), cur),
                  pl.BlockSpec((WINDOW, kw), prev), pl.BlockSpec((SWA_BLOCKS * WINDOW, kw), cur)],
        out_specs=pl.BlockSpec((SWA_BLOCKS * WINDOW, H_B * HD_B), cur),
        out_shape=jax.ShapeDtypeStruct((batch * seq, H_B * HD_B), BF16),
        compiler_params=pltpu.CompilerParams(dimension_semantics=("arbitrary", "arbitrary"),
                                             vmem_limit_bytes=VMEM_LIMIT),
        name="swa_prompt_attn",
    )(sinks, q, k, k, v, v)


SWA_Q_ROWS = 8


def _attn_b_sample_kernel(sink_ref, q_ref, k_ref, v_ref, o_ref, *, t, w_buf):
    keys = k_ref.shape[1]
    rows = 2 * SWA_Q_ROWS
    ti = lax.broadcasted_iota(jnp.int32, (rows, keys), 0) & (SWA_Q_ROWS - 1)
    col = lax.broadcasted_iota(jnp.int32, (rows, keys), 1)
    diff = jnp.where(col < w_buf, ti + w_buf - col, ti - (col - w_buf))
    valid = (diff >= 0) & (diff < WINDOW) & (col < w_buf + t) & (ti < t)
    jobs = []
    for b in range(q_ref.shape[0]):
        for kv in range(N_KV_B):
            ks = slice(LANES * (kv // 2), LANES * (kv // 2 + 1))
            jobs.append((q_ref[b, kv], _swa_halves(k_ref[b, :, ks], kv), _swa_halves(v_ref[b, :, ks], kv), kv))
    outs = _swa_attend(jobs, lambda j: valid, lambda hh: sink_ref[hh])
    for j, o in enumerate(outs):
        o_ref[j // N_KV_B, j % N_KV_B] = o


def _attn_b_sample(sinks, q, k_all, v_all, t, w_buf):
    bd = q.shape[0]
    bs = SWA_SAMPLE_SEQS
    blk = lambda a: pl.BlockSpec((bs,) + a.shape[1:], lambda b: (b,) + (0,) * (a.ndim - 1))
    return pl.pallas_call(
        functools.partial(_attn_b_sample_kernel, t=t, w_buf=w_buf),
        grid=(bd // bs,),
        in_specs=[pl.BlockSpec(memory_space=pltpu.SMEM), blk(q), blk(k_all), blk(v_all)],
        out_specs=blk(q),
        out_shape=jax.ShapeDtypeStruct(q.shape, F32),
        compiler_params=pltpu.CompilerParams(dimension_semantics=("arbitrary",),
                                             vmem_limit_bytes=VMEM_LIMIT),
        name="swa_sample_attn",
    )(sinks, q, k_all, v_all)


def _rope_tables(pos, n_rot, period, lane_lo):
    half = n_rot // 2
    inv = ROPE_THETA ** (-jnp.arange(0, n_rot, 2, dtype=F32) / n_rot)
    ang = pos.astype(F32)[:, None] * inv[None, :]
    cos, sin = jnp.cos(ang), jnp.sin(ang)
    rel = np.arange(LANES) % period - lane_lo
    in1 = (rel >= 0) & (rel < half)
    in2 = (rel >= half) & (rel < 2 * half)
    idx = np.where(in1, rel, np.where(in2, rel - half, 0))
    cg, sg = cos[:, idx], sin[:, idx]
    return jnp.stack([jnp.where(in1 | in2, cg, 1.0), jnp.where(in1, -sg, 0.0), jnp.where(in2, sg, 0.0)])


def _segments(width, seg_lanes):
    seg = np.zeros((width, LANES), np.float32)
    cnt = np.zeros((1, LANES), np.float32)
    for s, (lo, hi) in enumerate(seg_lanes):
        seg[lo:hi, s] = 1.0
        cnt[0, s] = 1.0 / (hi - lo)
    expand = np.concatenate([seg.T, seg.T], axis=0)
    return jnp.asarray(seg, BF16), jnp.asarray(cnt, F32), jnp.asarray(expand, BF16)


def _slab_gain(parts):
    g = jnp.zeros((SLAB,), F32)
    for lo, vals in parts:
        g = g.at[lo:lo + vals.shape[0]].set(vals)
    return g[None, :]


def _prep_a(norm_attn, w_a_in, g_qc, w_uq, g_ckv, w_uk, w_uv, g_qn, g_qr, g_kn, g_kr):
    w_in = jnp.concatenate([w_a_in[:, :D_QC + D_C], jnp.zeros((D_MODEL, KPE_LANE), F32),
                            w_a_in[:, D_QC + D_C:], jnp.zeros((D_MODEL, SLAB - KPE_LANE - D_ROPE), F32)], axis=1)
    dqk = D_NOPE + D_ROPE
    w_uq_pad = jnp.pad(w_uq.reshape(D_QC, H_A, dqk), ((0, 0), (0, 0), (0, SLAB - dqk))).reshape(D_QC, H_A * SLAB)
    w_uk3 = w_uk.reshape(D_C, H_A, D_NOPE)
    w_uk_pad = jnp.pad(w_uk3, ((0, 0), (0, 0), (0, SLAB - D_NOPE))).reshape(D_C, H_A * SLAB)
    w_uv3 = w_uv.reshape(D_C, H_A // 2, 2, D_V)
    even = jnp.pad(w_uv3[:, :, 0], ((0, 0), (0, 0), (0, SLAB - D_V)))
    odd = jnp.pad(w_uv3[:, :, 1], ((0, 0), (0, 0), (SLAB - D_V, 0)))
    w_uv_pad = jnp.stack([even, odd], axis=2).reshape(D_C, H_A * SLAB)
    q_segs = []
    k_segs = []
    for h in range(H_A):
        q_segs += [(SLAB * h, SLAB * h + D_NOPE), (SLAB * h + KPE_LANE, SLAB * h + KPE_LANE + D_ROPE)]
        k_segs += [(SLAB * h, SLAB * h + D_NOPE)]
    segq, cntq, expq = _segments(H_A * SLAB, q_segs)
    segk, cntk, expk = _segments(H_A * SLAB, k_segs)
    gq = jnp.tile(_slab_gain([(0, g_qn * SCALE_A), (KPE_LANE, g_qr * SCALE_A)]), (1, H_A))
    gk_slab = _slab_gain([(0, g_kn)])
    wukt = jnp.pad(jnp.transpose(w_uk3, (1, 2, 0)), ((0, 0), (0, SLAB - D_NOPE), (0, 0)))
    return dict(
        g_attn=norm_attn[None, :], w_in=w_in.astype(BF16), g_qc=g_qc[None, :], g_ckv=g_ckv[None, :],
        g_kpe=_slab_gain([(KPE_LANE, g_kr)]),
        w_uq=w_uq_pad.astype(BF16), segq=segq, cntq=cntq, expq=expq, gq=gq,
        w_uk=w_uk_pad.astype(BF16), segk=segk, cntk=cntk, expk=expk, gk=jnp.tile(gk_slab, (1, H_A)),
        w_uv=w_uv_pad.astype(BF16), gk_slab=gk_slab, wukt_pad=wukt.astype(BF16),
        wukt=jnp.transpose(w_uk3, (2, 1, 0)).reshape(H_A * D_NOPE, D_C).astype(BF16))


def _prep_b(g_kv, w_kv, g_k, norm_attn, w_q, g_q):
    kw = N_KV_B * HD_B
    segk, cntk, expk = _segments(kw, [(HD_B * h, HD_B * (h + 1)) for h in range(N_KV_B)])
    segq, cntq, expq = _segments(H_B * HD_B, [(HD_B * h, HD_B * (h + 1)) for h in range(H_B)])
    return dict(g_kv=g_kv[None, :], w_kv=w_kv.astype(BF16), segk=segk, cntk=cntk, expk=expk,
                gk=jnp.tile(g_k, N_KV_B)[None, :], g_attn=norm_attn[None, :], w_q=w_q.astype(BF16),
                segq=segq, cntq=cntq, expq=expq, gq=jnp.tile(g_q * SCALE_B, H_B)[None, :])


def kernel(x_prompt, x_sample, cache_mla, state_win_k, state_win_v, page_table, norm_attn, norm_ffn, w_a_in, g_qc, w_uq, g_ckv, w_uk, w_uv, g_qn_a, g_qr_a, g_kn_a, g_kr_a, w_a_out, g_kv_shared, w_kv_shared, g_k_b, w_q_b, g_q_b, sinks, w_b_out, w_ffn_in, w_ffn_out):
    batch, seq, _ = x_prompt.shape
    bd, t_dec, _ = x_sample.shape
    past_len = page_table.shape[1] * PAGE_SIZE
    w_buf = state_win_k.shape[1]
    kw = N_KV_B * HD_B
    assert w_a_in.shape[0] == 1 and w_q_b.shape[0] == 1, "one MLA layer followed by one SWA layer"
    assert w_buf == WINDOW and seq % TM_POST == 0 and (bd * t_dec) % 8 == 0
    assert t_dec <= SWA_Q_ROWS and bd % SWA_SAMPLE_SEQS == 0 and page_table.shape[1] % (2 * PAGES_PER_GROUP) == 0

    wa = _prep_a(norm_attn[0], w_a_in[0], g_qc[0], w_uq[0], g_ckv[0], w_uk[0], w_uv[0],
                 g_qn_a[0], g_qr_a[0], g_kn_a[0], g_kr_a[0])
    wb = _prep_b(g_kv_shared, w_kv_shared, g_k_b, norm_attn[1], w_q_b[0], g_q_b[0])
    w_a_out_b = w_a_out[0].astype(BF16)
    w_b_out_b = w_b_out[0].astype(BF16)
    ffn_in = w_ffn_in.astype(BF16)
    ffn_out = w_ffn_out.astype(BF16)
    g_ffn = norm_ffn[:, None, :]
    sink_b = sinks[0]

    pos_p = jnp.arange(seq)
    n_s = bd * t_dec
    pos_s = past_len + jnp.arange(n_s) % t_dec

    xp = x_prompt.reshape(batch * seq, D_MODEL)
    q, k, v, rows_p = _proj_a(xp, _rope_tables(pos_p, D_ROPE, SLAB, KPE_LANE), wa, TM_MLA_PROJ)
    o = _attn_a(q, k, v, batch, seq, TQ_MLA)
    h = _post(xp, o, w_a_out_b, g_ffn[0], ffn_in[0], ffn_out[0], TM_POST)
    k_p, v_p, q_b = _proj_b(h, _rope_tables(pos_p, ROT_B, HD_B, 0), wb, TM_SWA_PROJ)
    o = _attn_b(sink_b, q_b, k_p, v_p, batch, seq)
    y_prompt = _post(h, o, w_b_out_b, g_ffn[1], ffn_in[1], ffn_out[1], TM_POST)

    xs = x_sample.reshape(n_s, D_MODEL)
    q, _, _, rows_s = _proj_a(xs, _rope_tables(pos_s, D_ROPE, SLAB, KPE_LANE), wa, n_s)
    qabs = _qabs(q, wa["gk_slab"], wa["wukt_pad"]).reshape(bd, t_dec * H_A, D_C)
    qpe = q.reshape(n_s, H_A, SLAB)[:, :, KPE_LANE:KPE_LANE + D_ROPE].reshape(bd, t_dec * H_A, D_ROPE)
    new_pad = jnp.pad(jnp.swapaxes(rows_s.reshape(bd, t_dec, D_CKV), 1, 2), ((0, 0), (0, 0), (0, PAGE_SIZE - t_dec)))
    olat = _paged_attn(page_table, jnp.swapaxes(cache_mla, 2, 3), wa["wukt"], qabs, qpe, new_pad, t_dec)
    o = _latent_out(olat.reshape(n_s, H_A * D_C), wa["w_uv"])
    h = _post(xs, o, w_a_out_b, g_ffn[0], ffn_in[0], ffn_out[0], n_s)
    k_s, v_s, q_b = _proj_b(h, _rope_tables(pos_s, ROT_B, HD_B, 0), wb, n_s)
    q4 = jnp.transpose(q_b.reshape(bd, t_dec, N_KV_B, 2, LANES), (0, 2, 3, 1, 4))
    q4 = jnp.pad(q4, ((0, 0), (0, 0), (0, 0), (0, SWA_Q_ROWS - t_dec), (0, 0))).reshape(bd, N_KV_B, 2 * SWA_Q_ROWS, LANES)
    key_pad = jnp.zeros((bd, 16 - t_dec, kw), F32)
    k_all = jnp.concatenate([state_win_k.reshape(bd, w_buf, kw), k_s.reshape(bd, t_dec, kw), key_pad], axis=1)
    v_all = jnp.concatenate([state_win_v.reshape(bd, w_buf, kw), v_s.reshape(bd, t_dec, kw), key_pad], axis=1)
    o4 = _attn_b_sample(sink_b, q4, k_all, v_all, t_dec, w_buf)
    o = jnp.transpose(o4.reshape(bd, N_KV_B, 2, SWA_Q_ROWS, LANES)[:, :, :, :t_dec], (0, 3, 1, 2, 4))
    y_sample = _post(h, o.reshape(n_s, H_B * HD_B).astype(BF16), w_b_out_b, g_ffn[1], ffn_in[1], ffn_out[1], n_s)

    w_p = min(WINDOW, seq)
    k_p4 = k_p.reshape(batch, seq, N_KV_B, HD_B)
    v_p4 = v_p.reshape(batch, seq, N_KV_B, HD_B)
    win_k_s = jnp.concatenate([state_win_k, k_s.reshape(bd, t_dec, N_KV_B, HD_B)], axis=1)[:, -w_buf:]
    win_v_s = jnp.concatenate([state_win_v, v_s.reshape(bd, t_dec, N_KV_B, HD_B)], axis=1)[:, -w_buf:]
    return (y_prompt.reshape(batch, seq, D_MODEL), y_sample.reshape(bd, t_dec, D_MODEL),
            rows_p.reshape(1, batch, seq, D_CKV), rows_s.reshape(1, bd, t_dec, D_CKV),
            k_p4[:, seq - w_p:], v_p4[:, seq - w_p:], win_k_s, win_v_s)
```

```python
import functools

import numpy as np
import jax
import jax.numpy as jnp
from jax import lax
from jax.experimental import pallas as pl
from jax.experimental.pallas import tpu as pltpu

F32 = jnp.float32
BF16 = jnp.bfloat16

D_MODEL = 1024
PAGE_SIZE = 128
H_A = 16
D_NOPE = 64
D_ROPE = 32
D_V = 64
D_QC = 384
D_C = 256
D_CKV = D_C + D_ROPE
SCALE_A = (D_NOPE + D_ROPE) ** -0.5
H_B = 16
N_KV_B = 4
HD_B = 64
G_B = H_B // N_KV_B
WINDOW = 128
ROT_B = HD_B // 4
SCALE_B = HD_B ** -0.5
D_FF = 2816
ROPE_THETA = 500000.0
EPS = 1e-6
NEG = -1e30
LOG2E = 1.4426950408889634

LANES = 128
SLAB = 128
A_IN_COLS = 768
KPE_LANE = 64
VMEM_LIMIT = 56 * 1024 * 1024
FF_CHUNK = 256
PAGES_PER_GROUP = 16
TM_MLA_PROJ = 256
TM_SWA_PROJ = 512
TM_POST = 512
TQ_MLA = 256
ATTN_LOOKAHEAD = 2
SWA_SAMPLE_SEQS = 8
SWA_BLOCKS = 4

_NT = (((1,), (1,)), ((), ()))


def _dot(a, b):
    return jnp.dot(a, b, preferred_element_type=F32)


def _dot_nt(a, b):
    return lax.dot_general(a, b, _NT, preferred_element_type=F32)


def _rms(x, g):
    ms = jnp.mean(x * x, axis=-1, keepdims=True)
    return x * lax.rsqrt(ms + EPS) * g


def _rope_slab(x, c, s1, s2, half):
    return x * c + pltpu.roll(x, LANES - half, 1) * s1 + pltpu.roll(x, half, 1) * s2


def _segment_scales(items):
    sums = [_dot((raw * raw).astype(BF16), seg_ref[...]) for raw, seg_ref, _, _ in items]
    scales = []
    for ss, (_, _, inv_cnt_ref, expand_ref) in zip(sums, items):
        rs = lax.rsqrt(ss * inv_cnt_ref[...] + EPS)
        hi = rs.astype(BF16)
        lo = (rs - hi.astype(F32)).astype(BF16)
        scales.append(_dot(jnp.concatenate([hi, lo], axis=1), expand_ref[...]))
    return scales


def _proj_a_kernel(x_ref, tab_ref, g_attn_ref, w_in_ref, g_qc_ref, g_ckv_ref, g_kpe_ref,
                   w_uq_ref, segq_ref, cntq_ref, expq_ref, gq_ref,
                   w_uk_ref, segk_ref, cntk_ref, expk_ref, gk_ref, w_uv_ref, v_ones_ref,
                   q_ref, k_ref, v_ref, rows_ref):
    c, s1, s2 = tab_ref[0], tab_ref[1], tab_ref[2]
    hn = _rms(x_ref[...], g_attn_ref[...]).astype(BF16)
    a = _dot(hn, w_in_ref[...])
    cq = _rms(a[:, :D_QC], g_qc_ref[...]).astype(BF16)
    ckv = _rms(a[:, D_QC:D_QC + D_C], g_ckv_ref[...])
    kpe = a[:, D_QC + D_C:]
    ms = jnp.sum(kpe * kpe, axis=-1, keepdims=True) * (1.0 / D_ROPE)
    kpe = _rope_slab(kpe * lax.rsqrt(ms + EPS) * g_kpe_ref[...], c, s1, s2, D_ROPE // 2)

    ckv_b = ckv.astype(BF16)
    q_raw = _dot(cq, w_uq_ref[...])
    qn = q_raw * _segment_scales([(q_raw, segq_ref, cntq_ref, expq_ref)])[0] * gq_ref[...]
    k_raw = _dot(ckv_b, w_uk_ref[...])
    kn = k_raw * _segment_scales([(k_raw, segk_ref, cntk_ref, expk_ref)])[0] * gk_ref[...]
    for h in range(H_A):
        sl = slice(SLAB * h, SLAB * (h + 1))
        q_ref[:, sl] = _rope_slab(qn[:, sl], c, s1, s2, D_ROPE // 2).astype(BF16)
        k_ref[:, sl] = (kn[:, sl] + kpe).astype(BF16)
    v_ref[...] = (_dot(ckv_b, w_uv_ref[...]) + v_ones_ref[...]).astype(BF16)
    rows_ref[:, :D_C] = ckv
    rows_ref[:, D_C:] = kpe[:, KPE_LANE:KPE_LANE + D_ROPE]


def _const_spec(shape):
    zeros = (0,) * len(shape)
    return pl.BlockSpec(shape, lambda *_: zeros, pipeline_mode=pl.Buffered(1))


def _proj_a(x, tab, wa, tm):
    n = x.shape[0]
    n_tab = tab.shape[1] // tm
    weights = [wa["g_attn"], wa["w_in"], wa["g_qc"], wa["g_ckv"], wa["g_kpe"],
               wa["w_uq"], wa["segq"], wa["cntq"], wa["expq"], wa["gq"],
               wa["w_uk"], wa["segk"], wa["cntk"], wa["expk"], wa["gk"], wa["w_uv"], wa["v_ones"]]
    wide = H_A * SLAB
    return pl.pallas_call(
        _proj_a_kernel,
        grid=(n // tm,),
        in_specs=[pl.BlockSpec((tm, D_MODEL), lambda i: (i, 0)),
                  pl.BlockSpec((3, tm, LANES), lambda i: (0, i % n_tab, 0))]
                 + [_const_spec(w.shape) for w in weights],
        out_specs=[pl.BlockSpec((tm, wide), lambda i: (i, 0)),
                   pl.BlockSpec((tm, wide), lambda i: (i, 0)),
                   pl.BlockSpec((tm, wide), lambda i: (i, 0)),
                   pl.BlockSpec((tm, D_CKV), lambda i: (i, 0))],
        out_shape=[jax.ShapeDtypeStruct((n, wide), BF16),
                   jax.ShapeDtypeStruct((n, wide), BF16),
                   jax.ShapeDtypeStruct((n, wide), BF16),
                   jax.ShapeDtypeStruct((n, D_CKV), F32)],
        compiler_params=pltpu.CompilerParams(dimension_semantics=("arbitrary",),
                                             vmem_limit_bytes=VMEM_LIMIT),
        name="mla_proj",
    )(x, tab, *weights)


def _ones_lane(parity):
    return D_V if parity == 0 else 0


def _attn_a_kernel(q_ref, k_ref, v_ref, o_ref, *, tq):
    seq = q_ref.shape[0]
    causal = (lax.broadcasted_iota(jnp.int32, (tq, tq), 1) <= lax.broadcasted_iota(jnp.int32, (tq, tq), 0))
    low_half = lax.broadcasted_iota(jnp.int32, (tq, SLAB), 1) < D_V
    jobs = [(c, e) for c in range(seq // tq) for e in range(2)]

    def windows(c, e):
        return slice(c * tq, (c + 1) * tq), slice(0, c * tq), slice(SLAB * e, SLAB * (e + 1))

    def score(c, e):
        rows, past, ls = windows(c, e)
        q = q_ref[rows, ls]
        s_d = _dot_nt(q, k_ref[rows, ls])
        return s_d, (_dot_nt(q, k_ref[past, ls]) if c else None)

    def attend(c, e, s_d, s_p):
        rows, past, ls = windows(c, e)
        s_d = jnp.where(causal, s_d, NEG)
        m = jnp.max(s_d, axis=-1, keepdims=True)
        if c:
            m = jnp.maximum(m, jnp.max(s_p, axis=-1, keepdims=True))
        acc = _dot(jnp.exp2(s_d - m).astype(BF16), v_ref[rows, ls])
        if c:
            acc = acc + _dot(jnp.exp2(s_p - m).astype(BF16), v_ref[past, ls])
        ones = _ones_lane(e)
        return acc / acc[:, ones:ones + 1]

    ahead = [score(*jobs[j]) for j in range(min(ATTN_LOOKAHEAD, len(jobs)))]
    out = None
    for j, (c, e) in enumerate(jobs):
        if j + ATTN_LOOKAHEAD < len(jobs):
            ahead.append(score(*jobs[j + ATTN_LOOKAHEAD]))
        o_e = attend(c, e, *ahead[j])
        ahead[j] = None
        if e == 0:
            out = o_e
        else:
            o_ref[c * tq:(c + 1) * tq, :] = jnp.where(low_half, out, o_e).astype(BF16)


def _attn_a(q, k, v, batch, seq, tq):
    pairs = H_A // 2
    return pl.pallas_call(
        functools.partial(_attn_a_kernel, tq=tq),
        grid=(batch, pairs),
        in_specs=[pl.BlockSpec((seq, 2 * SLAB), lambda b, j: (b, j)),
                  pl.BlockSpec((seq, 2 * SLAB), lambda b, j: (b, j)),
                  pl.BlockSpec((seq, 2 * SLAB), lambda b, j: (b, j))],
        out_specs=pl.BlockSpec((seq, SLAB), lambda b, j: (b, j)),
        out_shape=jax.ShapeDtypeStruct((batch * seq, H_A * D_V), BF16),
        compiler_params=pltpu.CompilerParams(
            dimension_semantics=("arbitrary", "arbitrary"),
            vmem_limit_bytes=VMEM_LIMIT),
        name="mla_prompt_attn",
    )(q, k, v)


def _qabs_kernel(q_ref, gk_ref, wukt_ref, o_ref):
    for h in range(H_A):
        qs = (q_ref[:, SLAB * h:SLAB * (h + 1)].astype(F32) * gk_ref[...]).astype(BF16)
        o_ref[:, D_C * h:D_C * (h + 1)] = _dot(qs, wukt_ref[h]).astype(BF16)


def _qabs(q, gk_slab, wukt):
    n = q.shape[0]
    return pl.pallas_call(
        _qabs_kernel,
        out_shape=jax.ShapeDtypeStruct((n, H_A * D_C), BF16),
        compiler_params=pltpu.CompilerParams(vmem_limit_bytes=VMEM_LIMIT),
        name="mla_absorb_q",
    )(q, gk_slab, wukt)


def _paged_kernel(pt_ref, cache_ref, wukt_ref, qabs_ref, qpe_ref, new_ref, o_ref, lhs_sc, pg_sc, sem, *, n_pages, t_new):
    seq = pl.program_id(0)
    group = pg_sc.shape[1]
    n_groups = n_pages // group
    rows_q = qabs_ref.shape[0]

    def page_copy(sq, g, u):
        slot = g % 2
        return pltpu.make_async_copy(cache_ref.at[0, pt_ref[sq, g * group + u]], pg_sc.at[slot, u], sem.at[slot])

    def start_group(sq, g):
        for u in range(group):
            page_copy(sq, g, u).start()

    def wait_group(sq, g):
        for u in range(group):
            page_copy(sq, g, u).wait()

    @pl.when(seq == 0)
    def _():
        start_group(seq, 0)

    lhs_sc[:H_A * D_NOPE, :] = wukt_ref[...]
    lhs_sc[H_A * D_NOPE:, :] = qabs_ref[...]

    def scores(ct, kpet):
        keys = ct.shape[1]
        big = _dot(lhs_sc[...], ct)
        kt = big[:H_A * D_NOPE]
        ssq = jnp.sum((kt * kt).reshape(D_NOPE, H_A, keys), axis=0)
        rs = lax.rsqrt(ssq * (1.0 / D_NOPE) + EPS)
        rs_q = jnp.concatenate([rs] * (rows_q // H_A), axis=0)
        return big[H_A * D_NOPE:] * rs_q + _dot(qpe_ref[...], kpet)

    def accumulate(state, s, ct):
        m_old, l, acc = state
        m_new = jnp.maximum(m_old, jnp.max(s, axis=-1, keepdims=True))
        corr = jnp.exp2(m_old - m_new)
        p = jnp.exp2(s - m_new)
        return (m_new, l * corr + jnp.sum(p, axis=-1, keepdims=True),
                acc * corr + _dot_nt(p.astype(BF16), ct))

    state = (jnp.full((rows_q, 1), NEG, F32), jnp.zeros((rows_q, 1), F32), jnp.zeros((rows_q, D_C), F32))
    pending = None
    for g in range(n_groups):
        wait_group(seq, g)
        if g + 1 < n_groups:
            start_group(seq, g + 1)
        else:
            @pl.when(seq + 1 < pl.num_programs(0))
            def _():
                start_group(seq + 1, 0)
        slot = g % 2
        cts, ss = [], []
        for u in range(0, group, 2):
            ct = jnp.concatenate([pg_sc[slot, u, :D_C, :], pg_sc[slot, u + 1, :D_C, :]], axis=1).astype(BF16)
            kpet = jnp.concatenate([pg_sc[slot, u, D_C:, :], pg_sc[slot, u + 1, D_C:, :]], axis=1).astype(BF16)
            cts.append(ct)
            ss.append(scores(ct, kpet))
        if pending is not None:
            state = accumulate(state, *pending)
        pending = (jnp.concatenate(ss, axis=1), jnp.concatenate(cts, axis=1))
    state = accumulate(state, *pending)

    keys = new_ref.shape[1]
    t_row = lax.shift_right_logical(lax.broadcasted_iota(jnp.int32, (rows_q, keys), 0), H_A.bit_length() - 1)
    s_col = lax.broadcasted_iota(jnp.int32, (rows_q, keys), 1)
    ct = new_ref[:D_C, :].astype(BF16)
    s = scores(ct, new_ref[D_C:, :].astype(BF16))
    _, l, acc = accumulate(state, jnp.where((s_col <= t_row) & (s_col < t_new), s, NEG), ct)
    o_ref[...] = acc / l


def _paged_attn(page_table, cache, wukt, qabs, qpe, new_pad, t_new):
    bd, n_pages = page_table.shape
    rows_q = qabs.shape[1]
    assert (n_pages // PAGES_PER_GROUP) % 2 == 0, "the slot of a page group must not depend on the sequence"
    grid_spec = pltpu.PrefetchScalarGridSpec(
        num_scalar_prefetch=1,
        grid=(bd,),
        in_specs=[pl.BlockSpec(memory_space=pl.ANY),
                  pl.BlockSpec(wukt.shape, lambda b, pt: (0, 0)),
                  pl.BlockSpec((None, rows_q, D_C), lambda b, pt: (b, 0, 0)),
                  pl.BlockSpec((None, rows_q, D_ROPE), lambda b, pt: (b, 0, 0)),
                  pl.BlockSpec((None,) + new_pad.shape[1:], lambda b, pt: (b, 0, 0))],
        out_specs=pl.BlockSpec((None, rows_q, D_C), lambda b, pt: (b, 0, 0)),
        scratch_shapes=[pltpu.VMEM((H_A * D_NOPE + rows_q, D_C), BF16),
                        pltpu.VMEM((2, PAGES_PER_GROUP, D_CKV, PAGE_SIZE), F32),
                        pltpu.SemaphoreType.DMA((2,))])
    return pl.pallas_call(
        functools.partial(_paged_kernel, n_pages=n_pages, t_new=t_new),
        grid_spec=grid_spec,
        out_shape=jax.ShapeDtypeStruct((bd, rows_q, D_C), F32),
        compiler_params=pltpu.CompilerParams(dimension_semantics=("arbitrary",),
                                             vmem_limit_bytes=VMEM_LIMIT),
        name="mla_paged_attn",
    )(page_table, cache, wukt, qabs, qpe, new_pad)


def _latent_out_kernel(olat_ref, w_uv_ref, o_ref):
    for j in range(H_A // 2):
        acc = None
        for e in range(2):
            h = 2 * j + e
            part = _dot(olat_ref[:, D_C * h:D_C * (h + 1)].astype(BF16),
                        w_uv_ref[:, SLAB * h:SLAB * (h + 1)])
            acc = part if acc is None else acc + part
        o_ref[:, SLAB * j:SLAB * (j + 1)] = acc.astype(BF16)


def _latent_out(olat, w_uv_pad):
    n = olat.shape[0]
    return pl.pallas_call(
        _latent_out_kernel,
        out_shape=jax.ShapeDtypeStruct((n, H_A * D_V), BF16),
        compiler_params=pltpu.CompilerParams(vmem_limit_bytes=VMEM_LIMIT),
        name="mla_latent_out",
    )(olat, w_uv_pad)


def _post_kernel(x_ref, o_ref, w_o_ref, g_ref, w_in_ref, w_out_ref, y_ref):
    h1 = x_ref[...] + _dot(o_ref[...], w_o_ref[...])
    hn = _rms(h1, g_ref[...]).astype(BF16)
    acc = h1
    for c in range(D_FF // FF_CHUNK):
        lo = c * FF_CHUNK
        a1 = _dot(hn, w_in_ref[:, lo:lo + FF_CHUNK])
        a2 = _dot(hn, w_in_ref[:, D_FF + lo:D_FF + lo + FF_CHUNK])
        gate = (a1 * jax.nn.sigmoid(a1)) * a2
        acc = acc + _dot(gate.astype(BF16), w_out_ref[lo:lo + FF_CHUNK, :])
    y_ref[...] = acc


def _post(x, o, w_o, g, w_in, w_out, tm):
    n = x.shape[0]
    return pl.pallas_call(
        _post_kernel,
        grid=(n // tm,),
        in_specs=[pl.BlockSpec((tm, D_MODEL), lambda i: (i, 0)),
                  pl.BlockSpec((tm, o.shape[1]), lambda i: (i, 0)),
                  _const_spec(w_o.shape), _const_spec(g.shape),
                  _const_spec(w_in.shape), _const_spec(w_out.shape)],
        out_specs=pl.BlockSpec((tm, D_MODEL), lambda i: (i, 0)),
        out_shape=jax.ShapeDtypeStruct((n, D_MODEL), F32),
        compiler_params=pltpu.CompilerParams(dimension_semantics=("arbitrary",),
                                             vmem_limit_bytes=VMEM_LIMIT),
        name="outproj_swiglu",
    )(x, o, w_o, g, w_in, w_out)


def _proj_b_kernel(h_ref, tab_ref, g_kv_ref, w_kv_ref, segk_ref, cntk_ref, expk_ref, gk_ref,
                   g_attn_ref, w_q_ref, segq_ref, cntq_ref, expq_ref, gq_ref,
                   k_ref, v_ref, q_ref):
    c, s1, s2 = tab_ref[0], tab_ref[1], tab_ref[2]
    h = h_ref[...]
    hr = h * lax.rsqrt(jnp.mean(h * h, axis=-1, keepdims=True) + EPS)
    kv = _dot((hr * g_kv_ref[...]).astype(BF16), w_kv_ref[...])
    q_raw = _dot((hr * g_attn_ref[...]).astype(BF16), w_q_ref[...])
    kw = N_KV_B * HD_B
    k_raw = kv[:, :kw]
    v_ref[...] = kv[:, kw:]
    k_scale, q_scale = _segment_scales([(k_raw, segk_ref, cntk_ref, expk_ref),
                                        (q_raw, segq_ref, cntq_ref, expq_ref)])
    kn = k_raw * k_scale * gk_ref[...]
    for j in range(kw // LANES):
        sl = slice(LANES * j, LANES * (j + 1))
        k_ref[:, sl] = _rope_slab(kn[:, sl], c, s1, s2, ROT_B // 2)
    qn = q_raw * q_scale * gq_ref[...]
    for j in range(H_B * HD_B // LANES):
        sl = slice(LANES * j, LANES * (j + 1))
        q_ref[:, sl] = _rope_slab(qn[:, sl], c, s1, s2, ROT_B // 2).astype(BF16)


def _proj_b(h, tab, wb, tm):
    n = h.shape[0]
    n_tab = tab.shape[1] // tm
    weights = [wb["g_kv"], wb["w_kv"], wb["segk"], wb["cntk"], wb["expk"], wb["gk"],
               wb["g_attn"], wb["w_q"], wb["segq"], wb["cntq"], wb["expq"], wb["gq"]]
    kw = N_KV_B * HD_B
    return pl.pallas_call(
        _proj_b_kernel,
        grid=(n // tm,),
        in_specs=[pl.BlockSpec((tm, D_MODEL), lambda i: (i, 0)),
                  pl.BlockSpec((3, tm, LANES), lambda i: (0, i % n_tab, 0))]
                 + [_const_spec(w.shape) for w in weights],
        out_specs=[pl.BlockSpec((tm, kw), lambda i: (i, 0)),
                   pl.BlockSpec((tm, kw), lambda i: (i, 0)),
                   pl.BlockSpec((tm, H_B * HD_B), lambda i: (i, 0))],
        out_shape=[jax.ShapeDtypeStruct((n, kw), F32),
                   jax.ShapeDtypeStruct((n, kw), F32),
                   jax.ShapeDtypeStruct((n, H_B * HD_B), BF16)],
        compiler_params=pltpu.CompilerParams(dimension_semantics=("arbitrary",),
                                             vmem_limit_bytes=VMEM_LIMIT),
        name="swa_proj",
    )(h, tab, *weights)


def _swa_halves(slab, kv, ones_lane=False):
    lane = lax.broadcasted_iota(jnp.int32, slab.shape, 1)
    own = (lane >= HD_B) if kv % 2 else (lane < HD_B)
    halves = [None, None]
    halves[kv % 2] = jnp.where(own, slab, 0.0)
    halves[1 - kv % 2] = pltpu.roll(halves[kv % 2], HD_B, 1)
    if ones_lane:
        halves = [jnp.where(lane == _ones_lane(par), 1.0, h) for par, h in enumerate(halves)]
    return [h.astype(BF16) for h in halves]


def _swa_attend(jobs, valid_of, sink_of):
    scores = [[_dot_nt(q, kh[par]) for par in range(2)] for q, kh, _, _ in jobs]
    probs = []
    for j, (q, _, _, kv) in enumerate(jobs):
        m2 = q.shape[0]
        top = lax.broadcasted_iota(jnp.int32, (m2, 1), 0) < (m2 // 2)
        row = []
        for par in range(2):
            s = jnp.where(valid_of(j), scores[j][par], NEG)
            sink = jnp.where(top, sink_of(G_B * kv + par), sink_of(G_B * kv + par + 2)) * LOG2E
            m = jnp.maximum(jnp.max(s, axis=-1, keepdims=True), sink)
            row.append((jnp.exp2(s - m).astype(BF16), jnp.exp2(sink - m)))
        probs.append(row)
    outs = []
    for j, (q, _, vh, _) in enumerate(jobs):
        o = []
        for par in range(2):
            pv = _dot(probs[j][par][0], vh[par])
            ones = _ones_lane(par)
            o.append(pv / (pv[:, ones:ones + 1] + probs[j][par][1]))
        low_half = lax.broadcasted_iota(jnp.int32, o[0].shape, 1) < HD_B
        outs.append(jnp.where(low_half, o[0], o[1]))
    return outs


def _attn_b_kernel(sink_ref, q_ref, kp_ref, kc_ref, vp_ref, vc_ref, o_ref):
    g = pl.program_id(1)
    kcat = jnp.concatenate([kp_ref[...], kc_ref[...]], axis=0)
    vcat = jnp.concatenate([vp_ref[...], vc_ref[...]], axis=0)
    shape = (2 * WINDOW, 2 * WINDOW)
    qi = lax.broadcasted_iota(jnp.int32, shape, 0) & (WINDOW - 1)
    col = lax.broadcasted_iota(jnp.int32, shape, 1)
    band = (col > qi) & (col <= qi + WINDOW)
    band_first = band & ((col >= WINDOW) | (g > 0))
    n_blocks = q_ref.shape[0] // WINDOW
    for kv in range(N_KV_B):
        base = G_B * HD_B * kv
        ks = slice(LANES * (kv // 2), LANES * (kv // 2 + 1))
        k_half = _swa_halves(kcat[:, ks], kv)
        v_half = _swa_halves(vcat[:, ks], kv, ones_lane=True)
        jobs = []
        for r in range(n_blocks):
            rows = slice(WINDOW * r, WINDOW * (r + 1))
            win = slice(WINDOW * r, WINDOW * (r + 2))
            q_lhs = jnp.concatenate([q_ref[rows, base:base + LANES], q_ref[rows, base + LANES:base + 2 * LANES]], axis=0)
            jobs.append((q_lhs, [h[win] for h in k_half], [h[win] for h in v_half], kv))
        outs = _swa_attend(jobs, lambda r: band if r else band_first, lambda hh: sink_ref[hh])
        for r, o in enumerate(outs):
            rows = slice(WINDOW * r, WINDOW * (r + 1))
            o_ref[rows, base:base + LANES] = o[:WINDOW].astype(BF16)
            o_ref[rows, base + LANES:base + 2 * LANES] = o[WINDOW:].astype(BF16)


def _attn_b(sinks, q, k, v, batch, seq):
    nb = seq // WINDOW
    ng = nb // SWA_BLOCKS
    kw = N_KV_B * HD_B
    prev = lambda b, g: (b * nb + jnp.maximum(SWA_BLOCKS * g - 1, 0), 0)
    cur = lambda b, g: (b * ng + g, 0)
    return pl.pallas_call(
        _attn_b_kernel,
        grid=(batch, ng),
        in_specs=[pl.BlockSpec(memory_space=pltpu.SMEM),
                  pl.BlockSpec((SWA_BLOCKS * WINDOW, H_B * HD_B), cur),
                  pl.BlockSpec((WINDOW, kw), prev), pl.BlockSpec((SWA_BLOCKS * WINDOW, kw), cur),
                  pl.BlockSpec((WINDOW, kw), prev), pl.BlockSpec((SWA_BLOCKS * WINDOW, kw), cur)],
        out_specs=pl.BlockSpec((SWA_BLOCKS * WINDOW, H_B * HD_B), cur),
        out_shape=jax.ShapeDtypeStruct((batch * seq, H_B * HD_B), BF16),
        compiler_params=pltpu.CompilerParams(dimension_semantics=("arbitrary", "arbitrary"),
                                             vmem_limit_bytes=VMEM_LIMIT),
        name="swa_prompt_attn",
    )(sinks, q, k, k, v, v)


SWA_Q_ROWS = 8


def _attn_b_sample_kernel(sink_ref, q_ref, k_ref, v_ref, o_ref, *, t, w_buf):
    keys = k_ref.shape[1]
    rows = 2 * SWA_Q_ROWS
    ti = lax.broadcasted_iota(jnp.int32, (rows, keys), 0) & (SWA_Q_ROWS - 1)
    col = lax.broadcasted_iota(jnp.int32, (rows, keys), 1)
    diff = jnp.where(col < w_buf, ti + w_buf - col, ti - (col - w_buf))
    valid = (diff >= 0) & (diff < WINDOW) & (col < w_buf + t) & (ti < t)
    jobs = []
    for b in range(q_ref.shape[0]):
        for kv in range(N_KV_B):
            ks = slice(LANES * (kv // 2), LANES * (kv // 2 + 1))
            jobs.append((q_ref[b, kv], _swa_halves(k_ref[b, :, ks], kv),
                         _swa_halves(v_ref[b, :, ks], kv, ones_lane=True), kv))
    outs = _swa_attend(jobs, lambda j: valid, lambda hh: sink_ref[hh])
    for j, o in enumerate(outs):
        o_ref[j // N_KV_B, j % N_KV_B] = o


def _attn_b_sample(sinks, q, k_all, v_all, t, w_buf):
    bd = q.shape[0]
    bs = SWA_SAMPLE_SEQS
    blk = lambda a: pl.BlockSpec((bs,) + a.shape[1:], lambda b: (b,) + (0,) * (a.ndim - 1))
    return pl.pallas_call(
        functools.partial(_attn_b_sample_kernel, t=t, w_buf=w_buf),
        grid=(bd // bs,),
        in_specs=[pl.BlockSpec(memory_space=pltpu.SMEM), blk(q), blk(k_all), blk(v_all)],
        out_specs=blk(q),
        out_shape=jax.ShapeDtypeStruct(q.shape, F32),
        compiler_params=pltpu.CompilerParams(dimension_semantics=("arbitrary",),
                                             vmem_limit_bytes=VMEM_LIMIT),
        name="swa_sample_attn",
    )(sinks, q, k_all, v_all)


def _rope_tables(pos, n_rot, period, lane_lo):
    half = n_rot // 2
    inv = ROPE_THETA ** (-np.arange(0, n_rot, 2, dtype=np.float64) / n_rot)
    ang = np.asarray(pos, np.float64)[:, None] * inv[None, :]
    cos, sin = np.cos(ang), np.sin(ang)
    rel = np.arange(LANES) % period - lane_lo
    in1 = (rel >= 0) & (rel < half)
    in2 = (rel >= half) & (rel < 2 * half)
    idx = np.where(in1, rel, np.where(in2, rel - half, 0))
    cg, sg = cos[:, idx], sin[:, idx]
    tables = np.stack([np.where(in1 | in2, cg, 1.0), np.where(in1, -sg, 0.0), np.where(in2, sg, 0.0)])
    return jnp.asarray(tables, F32)


def _segments(width, seg_lanes):
    seg = np.zeros((width, LANES), np.float32)
    cnt = np.zeros((1, LANES), np.float32)
    for s, (lo, hi) in enumerate(seg_lanes):
        seg[lo:hi, s] = 1.0
        cnt[0, s] = 1.0 / (hi - lo)
    expand = np.concatenate([seg.T, seg.T], axis=0)
    return jnp.asarray(seg, BF16), jnp.asarray(cnt, F32), jnp.asarray(expand, BF16)


def _slab_gain(parts):
    pieces, pos = [], 0
    for lo, vals in parts:
        pieces += [jnp.zeros((lo - pos,), F32), vals.astype(F32)]
        pos = lo + vals.shape[0]
    pieces.append(jnp.zeros((SLAB - pos,), F32))
    return jnp.concatenate(pieces)[None, :]


def _prep_a(norm_attn, w_a_in, g_qc, w_uq, g_ckv, w_uk, w_uv, g_qn, g_qr, g_kn, g_kr):
    w_in = jnp.concatenate([w_a_in[:, :D_QC + D_C], jnp.zeros((D_MODEL, KPE_LANE), F32),
                            w_a_in[:, D_QC + D_C:], jnp.zeros((D_MODEL, SLAB - KPE_LANE - D_ROPE), F32)], axis=1)
    dqk = D_NOPE + D_ROPE
    w_uq_pad = jnp.pad(w_uq.reshape(D_QC, H_A, dqk), ((0, 0), (0, 0), (0, SLAB - dqk))).reshape(D_QC, H_A * SLAB)
    w_uk3 = w_uk.reshape(D_C, H_A, D_NOPE)
    w_uk_pad = jnp.pad(w_uk3, ((0, 0), (0, 0), (0, SLAB - D_NOPE))).reshape(D_C, H_A * SLAB)
    w_uv3 = w_uv.reshape(D_C, H_A // 2, 2, D_V)
    even = jnp.pad(w_uv3[:, :, 0], ((0, 0), (0, 0), (0, SLAB - D_V)))
    odd = jnp.pad(w_uv3[:, :, 1], ((0, 0), (0, 0), (SLAB - D_V, 0)))
    w_uv_pad = jnp.stack([even, odd], axis=2).reshape(D_C, H_A * SLAB)
    v_ones = np.zeros((1, H_A * SLAB), np.float32)
    for h in range(H_A):
        v_ones[0, SLAB * h + _ones_lane(h % 2)] = 1.0
    q_segs = []
    k_segs = []
    for h in range(H_A):
        q_segs += [(SLAB * h, SLAB * h + D_NOPE), (SLAB * h + KPE_LANE, SLAB * h + KPE_LANE + D_ROPE)]
        k_segs += [(SLAB * h, SLAB * h + D_NOPE)]
    segq, cntq, expq = _segments(H_A * SLAB, q_segs)
    segk, cntk, expk = _segments(H_A * SLAB, k_segs)
    gq = jnp.tile(_slab_gain([(0, g_qn * SCALE_A * LOG2E), (KPE_LANE, g_qr * SCALE_A * LOG2E)]), (1, H_A))
    gk_slab = _slab_gain([(0, g_kn)])
    wukt = jnp.pad(jnp.transpose(w_uk3, (1, 2, 0)), ((0, 0), (0, SLAB - D_NOPE), (0, 0)))
    return dict(
        g_attn=norm_attn[None, :], w_in=w_in.astype(BF16), g_qc=g_qc[None, :], g_ckv=g_ckv[None, :],
        g_kpe=_slab_gain([(KPE_LANE, g_kr)]),
        w_uq=w_uq_pad.astype(BF16), segq=segq, cntq=cntq, expq=expq, gq=gq,
        w_uk=w_uk_pad.astype(BF16), segk=segk, cntk=cntk, expk=expk, gk=jnp.tile(gk_slab, (1, H_A)),
        w_uv=w_uv_pad.astype(BF16), v_ones=jnp.asarray(v_ones), gk_slab=gk_slab, wukt_pad=wukt.astype(BF16),
        wukt=jnp.transpose(w_uk3, (2, 1, 0)).reshape(H_A * D_NOPE, D_C).astype(BF16))


def _prep_b(g_kv, w_kv, g_k, norm_attn, w_q, g_q):
    kw = N_KV_B * HD_B
    segk, cntk, expk = _segments(kw, [(HD_B * h, HD_B * (h + 1)) for h in range(N_KV_B)])
    segq, cntq, expq = _segments(H_B * HD_B, [(HD_B * h, HD_B * (h + 1)) for h in range(H_B)])
    return dict(g_kv=g_kv[None, :], w_kv=w_kv.astype(BF16), segk=segk, cntk=cntk, expk=expk,
                gk=jnp.tile(g_k, N_KV_B)[None, :], g_attn=norm_attn[None, :], w_q=w_q.astype(BF16),
                segq=segq, cntq=cntq, expq=expq, gq=jnp.tile(g_q * (SCALE_B * LOG2E), H_B)[None, :])


def kernel(x_prompt, x_sample, cache_mla, state_win_k, state_win_v, page_table, norm_attn, norm_ffn, w_a_in, g_qc, w_uq, g_ckv, w_uk, w_uv, g_qn_a, g_qr_a, g_kn_a, g_kr_a, w_a_out, g_kv_shared, w_kv_shared, g_k_b, w_q_b, g_q_b, sinks, w_b_out, w_ffn_in, w_ffn_out):
    batch, seq, _ = x_prompt.shape
    bd, t_dec, _ = x_sample.shape
    past_len = page_table.shape[1] * PAGE_SIZE
    w_buf = state_win_k.shape[1]
    kw = N_KV_B * HD_B
    assert w_a_in.shape[0] == 1 and w_q_b.shape[0] == 1, "one MLA layer followed by one SWA layer"
    assert w_buf == WINDOW and seq % TM_POST == 0 and (bd * t_dec) % 8 == 0
    assert t_dec <= SWA_Q_ROWS and bd % SWA_SAMPLE_SEQS == 0 and page_table.shape[1] % (2 * PAGES_PER_GROUP) == 0

    wa = _prep_a(norm_attn[0], w_a_in[0], g_qc[0], w_uq[0], g_ckv[0], w_uk[0], w_uv[0],
                 g_qn_a[0], g_qr_a[0], g_kn_a[0], g_kr_a[0])
    wb = _prep_b(g_kv_shared, w_kv_shared, g_k_b, norm_attn[1], w_q_b[0], g_q_b[0])
    w_a_out_b = w_a_out[0].astype(BF16)
    w_b_out_b = w_b_out[0].astype(BF16)
    ffn_in = w_ffn_in.astype(BF16)
    ffn_out = w_ffn_out.astype(BF16)
    g_ffn = norm_ffn[:, None, :]
    sink_b = sinks[0]

    pos_p = np.arange(seq)
    n_s = bd * t_dec
    pos_s = past_len + np.arange(n_s) % t_dec

    xp = x_prompt.reshape(batch * seq, D_MODEL)
    q, k, v, rows_p = _proj_a(xp, _rope_tables(pos_p, D_ROPE, SLAB, KPE_LANE), wa, TM_MLA_PROJ)
    o = _attn_a(q, k, v, batch, seq, TQ_MLA)
    h = _post(xp, o, w_a_out_b, g_ffn[0], ffn_in[0], ffn_out[0], TM_POST)
    k_p, v_p, q_b = _proj_b(h, _rope_tables(pos_p, ROT_B, HD_B, 0), wb, TM_SWA_PROJ)
    o = _attn_b(sink_b, q_b, k_p, v_p, batch, seq)
    y_prompt = _post(h, o, w_b_out_b, g_ffn[1], ffn_in[1], ffn_out[1], TM_POST)

    xs = x_sample.reshape(n_s, D_MODEL)
    q, _, _, rows_s = _proj_a(xs, _rope_tables(pos_s, D_ROPE, SLAB, KPE_LANE), wa, n_s)
    qabs = _qabs(q, wa["gk_slab"], wa["wukt_pad"]).reshape(bd, t_dec * H_A, D_C)
    qpe = q.reshape(n_s, H_A, SLAB)[:, :, KPE_LANE:KPE_LANE + D_ROPE].reshape(bd, t_dec * H_A, D_ROPE)
    new_pad = jnp.pad(jnp.swapaxes(rows_s.reshape(bd, t_dec, D_CKV), 1, 2), ((0, 0), (0, 0), (0, PAGE_SIZE - t_dec)))
    olat = _paged_attn(page_table, jnp.swapaxes(cache_mla, 2, 3), wa["wukt"], qabs, qpe, new_pad, t_dec)
    o = _latent_out(olat.reshape(n_s, H_A * D_C), wa["w_uv"])
    h = _post(xs, o, w_a_out_b, g_ffn[0], ffn_in[0], ffn_out[0], n_s)
    k_s, v_s, q_b = _proj_b(h, _rope_tables(pos_s, ROT_B, HD_B, 0), wb, n_s)
    q4 = jnp.transpose(q_b.reshape(bd, t_dec, N_KV_B, 2, LANES), (0, 2, 3, 1, 4))
    q4 = jnp.pad(q4, ((0, 0), (0, 0), (0, 0), (0, SWA_Q_ROWS - t_dec), (0, 0))).reshape(bd, N_KV_B, 2 * SWA_Q_ROWS, LANES)
    key_pad = jnp.zeros((bd, 16 - t_dec, kw), F32)
    k_all = jnp.concatenate([state_win_k.reshape(bd, w_buf, kw), k_s.reshape(bd, t_dec, kw), key_pad], axis=1)
    v_all = jnp.concatenate([state_win_v.reshape(bd, w_buf, kw), v_s.reshape(bd, t_dec, kw), key_pad], axis=1)
    o4 = _attn_b_sample(sink_b, q4, k_all, v_all, t_dec, w_buf)
    o = jnp.transpose(o4.reshape(bd, N_KV_B, 2, SWA_Q_ROWS, LANES)[:, :, :, :t_dec], (0, 3, 1, 2, 4))
    y_sample = _post(h, o.reshape(n_s, H_B * HD_B).astype(BF16), w_b_out_b, g_ffn[1], ffn_in[1], ffn_out[1], n_s)

    w_p = min(WINDOW, seq)
    k_p4 = k_p.reshape(batch, seq, N_KV_B, HD_B)
    v_p4 = v_p.reshape(batch, seq, N_KV_B, HD_B)
    win_k_s = jnp.concatenate([state_win_k, k_s.reshape(bd, t_dec, N_KV_B, HD_B)], axis=1)[:, -w_buf:]
    win_v_s = jnp.concatenate([state_win_v, v_s.reshape(bd, t_dec, N_KV_B, HD_B)], axis=1)[:, -w_buf:]
    return (y_prompt.reshape(batch, seq, D_MODEL), y_sample.reshape(bd, t_dec, D_MODEL),
            rows_p.reshape(1, batch, seq, D_CKV), rows_s.reshape(1, bd, t_dec, D_CKV),
            k_p4[:, seq - w_p:], v_p4[:, seq - w_p:], win_k_s, win_v_s)
```

```python
import functools

import numpy as np
import jax
import jax.numpy as jnp
from jax import lax
from jax.experimental import pallas as pl
from jax.experimental.pallas import tpu as pltpu

F32 = jnp.float32
BF16 = jnp.bfloat16

D_MODEL = 1024
PAGE_SIZE = 128
H_A = 16
D_NOPE = 64
D_ROPE = 32
D_V = 64
D_QC = 384
D_C = 256
D_CKV = D_C + D_ROPE
SCALE_A = (D_NOPE + D_ROPE) ** -0.5
H_B = 16
N_KV_B = 4
HD_B = 64
G_B = H_B // N_KV_B
WINDOW = 128
ROT_B = HD_B // 4
SCALE_B = HD_B ** -0.5
D_FF = 2816
ROPE_THETA = 500000.0
EPS = 1e-6
NEG = -1e30
LOG2E = 1.4426950408889634

LANES = 128
SLAB = 128
A_IN_COLS = 768
KPE_LANE = 64
VMEM_LIMIT = 56 * 1024 * 1024
FF_CHUNK = 256
PAGES_PER_GROUP = 16
PAGES_PER_DOT = 2
TM_MLA_PROJ = 256
TM_SWA_PROJ = 512
TM_POST = 512
TQ_MLA = 256
ATTN_LOOKAHEAD = 2
SWA_SAMPLE_SEQS = 8
SWA_BLOCKS = 4

_NT = (((1,), (1,)), ((), ()))


def _dot(a, b):
    return jnp.dot(a, b, preferred_element_type=F32)


def _dot_nt(a, b):
    return lax.dot_general(a, b, _NT, preferred_element_type=F32)


def _rms(x, g):
    ms = jnp.mean(x * x, axis=-1, keepdims=True)
    return x * lax.rsqrt(ms + EPS) * g


def _rope_slab(x, c, s1, s2, half):
    return x * c + pltpu.roll(x, LANES - half, 1) * s1 + pltpu.roll(x, half, 1) * s2


def _rope_slab_dup(x, c, s, half):
    return x * c + pltpu.roll(x, LANES - half, 1) * s


def _segment_scales(items):
    sums = [_dot((raw * raw).astype(BF16), seg_ref[...]) for raw, seg_ref, _, _ in items]
    scales = []
    for ss, (_, _, inv_cnt_ref, expand_ref) in zip(sums, items):
        rs = lax.rsqrt(ss * inv_cnt_ref[...] + EPS)
        hi = rs.astype(BF16)
        lo = (rs - hi.astype(F32)).astype(BF16)
        scales.append(_dot(jnp.concatenate([hi, lo], axis=1), expand_ref[...]))
    return scales


def _proj_a_kernel(x_ref, tab_ref, g_attn_ref, w_in_ref, g_qc_ref, g_ckv_ref, g_kpe_ref, pe_mask_ref,
                   w_uq_ref, segq_ref, cntq_ref, expq_ref, gq_ref,
                   w_uk_ref, gk_ref, w_uv_ref, v_ones_ref,
                   q_ref, k_ref, v_ref, rows_ref):
    c, s = tab_ref[0], tab_ref[1]
    hn = _rms(x_ref[...], g_attn_ref[...]).astype(BF16)
    a = _dot(hn, w_in_ref[...])
    cq = _rms(a[:, :D_QC], g_qc_ref[...]).astype(BF16)
    ckv = _rms(a[:, D_QC:D_QC + D_C], g_ckv_ref[...])
    kpe = a[:, D_QC + D_C:]
    ms = jnp.sum(kpe * kpe * pe_mask_ref[...], axis=-1, keepdims=True) * (1.0 / D_ROPE)
    kpe = _rope_slab_dup(kpe * lax.rsqrt(ms + EPS) * g_kpe_ref[...], c, s, D_ROPE // 2)

    ckv_b = ckv.astype(BF16)
    q_raw = _dot(cq, w_uq_ref[...])
    qn = q_raw * _segment_scales([(q_raw, segq_ref, cntq_ref, expq_ref)])[0] * gq_ref[...]
    k_raw = _dot(ckv_b, w_uk_ref[...])
    for h in range(H_A):
        sl = slice(SLAB * h, SLAB * (h + 1))
        q_ref[:, sl] = _rope_slab_dup(qn[:, sl], c, s, D_ROPE // 2).astype(BF16)
        k_h = k_raw[:, sl]
        ms_h = jnp.sum(k_h * k_h, axis=-1, keepdims=True) * (1.0 / D_NOPE)
        k_ref[:, sl] = (k_h * lax.rsqrt(ms_h + EPS) * gk_ref[...] + kpe).astype(BF16)
    v_ref[...] = (_dot(ckv_b, w_uv_ref[...]) + v_ones_ref[...]).astype(BF16)
    rows_ref[:, :D_C] = ckv
    rows_ref[:, D_C:] = kpe[:, KPE_LANE:KPE_LANE + D_ROPE]


def _const_spec(shape):
    zeros = (0,) * len(shape)
    return pl.BlockSpec(shape, lambda *_: zeros, pipeline_mode=pl.Buffered(1))


def _proj_a(x, tab, wa, tm):
    n = x.shape[0]
    n_tab = tab.shape[1] // tm
    weights = [wa["g_attn"], wa["w_in"], wa["g_qc"], wa["g_ckv"], wa["g_kpe"], wa["pe_mask"],
               wa["w_uq"], wa["segq"], wa["cntq"], wa["expq"], wa["gq"],
               wa["w_uk"], wa["gk_slab"], wa["w_uv"], wa["v_ones"]]
    wide = H_A * SLAB
    return pl.pallas_call(
        _proj_a_kernel,
        grid=(n // tm,),
        in_specs=[pl.BlockSpec((tm, D_MODEL), lambda i: (i, 0)),
                  pl.BlockSpec((tab.shape[0], tm, LANES), lambda i: (0, i % n_tab, 0))]
                 + [_const_spec(w.shape) for w in weights],
        out_specs=[pl.BlockSpec((tm, wide), lambda i: (i, 0)),
                   pl.BlockSpec((tm, wide), lambda i: (i, 0)),
                   pl.BlockSpec((tm, wide), lambda i: (i, 0)),
                   pl.BlockSpec((tm, D_CKV), lambda i: (i, 0))],
        out_shape=[jax.ShapeDtypeStruct((n, wide), BF16),
                   jax.ShapeDtypeStruct((n, wide), BF16),
                   jax.ShapeDtypeStruct((n, wide), BF16),
                   jax.ShapeDtypeStruct((n, D_CKV), F32)],
        compiler_params=pltpu.CompilerParams(dimension_semantics=("arbitrary",),
                                             vmem_limit_bytes=VMEM_LIMIT),
        name="mla_proj",
    )(x, tab, *weights)


def _ones_lane(parity):
    return D_V if parity == 0 else 0


def _attn_a_kernel(q_ref, k_ref, v_ref, o_ref, *, tq):
    seq = q_ref.shape[0]
    causal = (lax.broadcasted_iota(jnp.int32, (tq, tq), 1) <= lax.broadcasted_iota(jnp.int32, (tq, tq), 0))
    low_half = lax.broadcasted_iota(jnp.int32, (tq, SLAB), 1) < D_V
    jobs = [(c, e) for c in range(seq // tq) for e in range(2)]

    def windows(c, e):
        return slice(c * tq, (c + 1) * tq), slice(0, c * tq), slice(SLAB * e, SLAB * (e + 1))

    def score(c, e):
        rows, past, ls = windows(c, e)
        q = q_ref[rows, ls]
        s_d = _dot_nt(q, k_ref[rows, ls])
        return s_d, (_dot_nt(q, k_ref[past, ls]) if c else None)

    def attend(c, e, s_d, s_p):
        rows, past, ls = windows(c, e)
        s_d = jnp.where(causal, s_d, NEG)
        m = jnp.max(s_d, axis=-1, keepdims=True)
        if c:
            m = jnp.maximum(m, jnp.max(s_p, axis=-1, keepdims=True))
        acc = _dot(jnp.exp2(s_d - m).astype(BF16), v_ref[rows, ls])
        if c:
            acc = acc + _dot(jnp.exp2(s_p - m).astype(BF16), v_ref[past, ls])
        ones = _ones_lane(e)
        return acc / acc[:, ones:ones + 1]

    ahead = [score(*jobs[j]) for j in range(min(ATTN_LOOKAHEAD, len(jobs)))]
    out = None
    for j, (c, e) in enumerate(jobs):
        if j + ATTN_LOOKAHEAD < len(jobs):
            ahead.append(score(*jobs[j + ATTN_LOOKAHEAD]))
        o_e = attend(c, e, *ahead[j])
        ahead[j] = None
        if e == 0:
            out = o_e
        else:
            o_ref[c * tq:(c + 1) * tq, :] = jnp.where(low_half, out, o_e).astype(BF16)


def _attn_a(q, k, v, batch, seq, tq):
    pairs = H_A // 2
    return pl.pallas_call(
        functools.partial(_attn_a_kernel, tq=tq),
        grid=(batch, pairs),
        in_specs=[pl.BlockSpec((seq, 2 * SLAB), lambda b, j: (b, j)),
                  pl.BlockSpec((seq, 2 * SLAB), lambda b, j: (b, j)),
                  pl.BlockSpec((seq, 2 * SLAB), lambda b, j: (b, j))],
        out_specs=pl.BlockSpec((seq, SLAB), lambda b, j: (b, j)),
        out_shape=jax.ShapeDtypeStruct((batch * seq, H_A * D_V), BF16),
        compiler_params=pltpu.CompilerParams(
            dimension_semantics=("arbitrary", "arbitrary"),
            vmem_limit_bytes=VMEM_LIMIT),
        name="mla_prompt_attn",
    )(q, k, v)


def _qabs_kernel(q_ref, gk_ref, wukt_ref, o_ref):
    for h in range(H_A):
        qs = (q_ref[:, SLAB * h:SLAB * (h + 1)].astype(F32) * gk_ref[...]).astype(BF16)
        o_ref[:, D_C * h:D_C * (h + 1)] = _dot(qs, wukt_ref[h]).astype(BF16)


def _qabs(q, gk_slab, wukt):
    n = q.shape[0]
    return pl.pallas_call(
        _qabs_kernel,
        out_shape=jax.ShapeDtypeStruct((n, H_A * D_C), BF16),
        compiler_params=pltpu.CompilerParams(vmem_limit_bytes=VMEM_LIMIT),
        name="mla_absorb_q",
    )(q, gk_slab, wukt)


def _paged_kernel(pt_ref, cache_ref, wukt_ref, qabs_ref, qpe_ref, new_ref, o_ref, lhs_sc, pg_sc, sem, *, n_pages, t_new):
    seq = pl.program_id(0)
    group = pg_sc.shape[1]
    n_groups = n_pages // group
    rows_q = qabs_ref.shape[0]

    def page_copy(sq, g, u):
        slot = g % 2
        return pltpu.make_async_copy(cache_ref.at[0, pt_ref[sq, g * group + u]], pg_sc.at[slot, u], sem.at[slot])

    def start_group(sq, g):
        for u in range(group):
            page_copy(sq, g, u).start()

    def wait_group(sq, g):
        for u in range(group):
            page_copy(sq, g, u).wait()

    @pl.when(seq == 0)
    def _():
        start_group(seq, 0)

    lhs_sc[:H_A * D_NOPE, :] = wukt_ref[...]
    lhs_sc[H_A * D_NOPE:, :] = qabs_ref[...]

    def nope_scores(ct):
        keys = ct.shape[1]
        big = _dot(lhs_sc[...], ct)
        kt = big[:H_A * D_NOPE]
        ssq = jnp.sum((kt * kt).reshape(D_NOPE, H_A, keys), axis=0)
        rs = lax.rsqrt(ssq * (1.0 / D_NOPE) + EPS)
        rs_q = jnp.concatenate([rs] * (rows_q // H_A), axis=0)
        return big[H_A * D_NOPE:] * rs_q

    def scores(cts, kpets):
        return jnp.concatenate([nope_scores(ct) for ct in cts], axis=1) + _dot(qpe_ref[...], jnp.concatenate(kpets, axis=1))

    def accumulate(state, s, ct):
        m_old, l, acc = state
        m_new = jnp.maximum(m_old, jnp.max(s, axis=-1, keepdims=True))
        corr = jnp.exp2(m_old - m_new)
        p = jnp.exp2(s - m_new)
        return (m_new, l * corr + jnp.sum(p, axis=-1, keepdims=True),
                acc * corr + _dot_nt(p.astype(BF16), ct))

    state = (jnp.full((rows_q, 1), NEG, F32), jnp.zeros((rows_q, 1), F32), jnp.zeros((rows_q, D_C), F32))
    pending = None
    for g in range(n_groups):
        wait_group(seq, g)
        if g + 1 < n_groups:
            start_group(seq, g + 1)
        else:
            @pl.when(seq + 1 < pl.num_programs(0))
            def _():
                start_group(seq + 1, 0)
        slot = g % 2
        cts, kpets = [], []
        for u in range(0, group, PAGES_PER_DOT):
            pages = range(u, u + PAGES_PER_DOT)
            cts.append(jnp.concatenate([pg_sc[slot, v, :D_C, :] for v in pages], axis=1).astype(BF16))
            kpets.append(jnp.concatenate([pg_sc[slot, v, D_C:, :] for v in pages], axis=1).astype(BF16))
        s_group = scores(cts, kpets)
        if pending is not None:
            state = accumulate(state, *pending)
        pending = (s_group, jnp.concatenate(cts, axis=1))
    state = accumulate(state, *pending)

    keys = new_ref.shape[1]
    t_row = lax.shift_right_logical(lax.broadcasted_iota(jnp.int32, (rows_q, keys), 0), H_A.bit_length() - 1)
    s_col = lax.broadcasted_iota(jnp.int32, (rows_q, keys), 1)
    ct = new_ref[:D_C, :].astype(BF16)
    s = scores([ct], [new_ref[D_C:, :].astype(BF16)])
    _, l, acc = accumulate(state, jnp.where((s_col <= t_row) & (s_col < t_new), s, NEG), ct)
    o_ref[...] = acc / l


def _paged_attn(page_table, cache, wukt, qabs, qpe, new_pad, t_new):
    bd, n_pages = page_table.shape
    rows_q = qabs.shape[1]
    assert (n_pages // PAGES_PER_GROUP) % 2 == 0, "the slot of a page group must not depend on the sequence"
    grid_spec = pltpu.PrefetchScalarGridSpec(
        num_scalar_prefetch=1,
        grid=(bd,),
        in_specs=[pl.BlockSpec(memory_space=pl.ANY),
                  pl.BlockSpec(wukt.shape, lambda b, pt: (0, 0)),
                  pl.BlockSpec((None, rows_q, D_C), lambda b, pt: (b, 0, 0)),
                  pl.BlockSpec((None, rows_q, D_ROPE), lambda b, pt: (b, 0, 0)),
                  pl.BlockSpec((None,) + new_pad.shape[1:], lambda b, pt: (b, 0, 0))],
        out_specs=pl.BlockSpec((None, rows_q, D_C), lambda b, pt: (b, 0, 0)),
        scratch_shapes=[pltpu.VMEM((H_A * D_NOPE + rows_q, D_C), BF16),
                        pltpu.VMEM((2, PAGES_PER_GROUP, D_CKV, PAGE_SIZE), F32),
                        pltpu.SemaphoreType.DMA((2,))])
    return pl.pallas_call(
        functools.partial(_paged_kernel, n_pages=n_pages, t_new=t_new),
        grid_spec=grid_spec,
        out_shape=jax.ShapeDtypeStruct((bd, rows_q, D_C), F32),
        compiler_params=pltpu.CompilerParams(dimension_semantics=("arbitrary",),
                                             vmem_limit_bytes=VMEM_LIMIT),
        name="mla_paged_attn",
    )(page_table, cache, wukt, qabs, qpe, new_pad)


def _latent_out_kernel(olat_ref, w_uv_ref, o_ref):
    for j in range(H_A // 2):
        acc = None
        for e in range(2):
            h = 2 * j + e
            part = _dot(olat_ref[:, D_C * h:D_C * (h + 1)].astype(BF16),
                        w_uv_ref[:, SLAB * h:SLAB * (h + 1)])
            acc = part if acc is None else acc + part
        o_ref[:, SLAB * j:SLAB * (j + 1)] = acc.astype(BF16)


def _latent_out(olat, w_uv_pad):
    n = olat.shape[0]
    return pl.pallas_call(
        _latent_out_kernel,
        out_shape=jax.ShapeDtypeStruct((n, H_A * D_V), BF16),
        compiler_params=pltpu.CompilerParams(vmem_limit_bytes=VMEM_LIMIT),
        name="mla_latent_out",
    )(olat, w_uv_pad)


def _post_kernel(x_ref, o_ref, w_o_ref, g_ref, w_in_ref, w_out_ref, y_ref):
    h1 = x_ref[...] + _dot(o_ref[...], w_o_ref[...])
    hn = _rms(h1, g_ref[...]).astype(BF16)
    acc = h1
    for c in range(D_FF // FF_CHUNK):
        lo = c * FF_CHUNK
        a1 = _dot(hn, w_in_ref[:, lo:lo + FF_CHUNK])
        a2 = _dot(hn, w_in_ref[:, D_FF + lo:D_FF + lo + FF_CHUNK])
        gate = (a1 * jax.nn.sigmoid(a1)) * a2
        acc = acc + _dot(gate.astype(BF16), w_out_ref[lo:lo + FF_CHUNK, :])
    y_ref[...] = acc


def _post(x, o, w_o, g, w_in, w_out, tm):
    n = x.shape[0]
    return pl.pallas_call(
        _post_kernel,
        grid=(n // tm,),
        in_specs=[pl.BlockSpec((tm, D_MODEL), lambda i: (i, 0)),
                  pl.BlockSpec((tm, o.shape[1]), lambda i: (i, 0)),
                  _const_spec(w_o.shape), _const_spec(g.shape),
                  _const_spec(w_in.shape), _const_spec(w_out.shape)],
        out_specs=pl.BlockSpec((tm, D_MODEL), lambda i: (i, 0)),
        out_shape=jax.ShapeDtypeStruct((n, D_MODEL), F32),
        compiler_params=pltpu.CompilerParams(dimension_semantics=("arbitrary",),
                                             vmem_limit_bytes=VMEM_LIMIT),
        name="outproj_swiglu",
    )(x, o, w_o, g, w_in, w_out)


def _proj_b_kernel(h_ref, tab_ref, g_kv_ref, w_kv_ref, segk_ref, cntk_ref, expk_ref, gk_ref,
                   g_attn_ref, w_q_ref, segq_ref, cntq_ref, expq_ref, gq_ref,
                   k_ref, v_ref, q_ref, kt_ref, vt_ref):
    c, s1, s2 = tab_ref[0], tab_ref[1], tab_ref[2]
    h = h_ref[...]
    hr = h * lax.rsqrt(jnp.mean(h * h, axis=-1, keepdims=True) + EPS)
    kv = _dot((hr * g_kv_ref[...]).astype(BF16), w_kv_ref[...])
    q_raw = _dot((hr * g_attn_ref[...]).astype(BF16), w_q_ref[...])
    kw = N_KV_B * HD_B
    tail = slice(h.shape[0] - WINDOW, h.shape[0])
    k_raw = kv[:, :kw]
    v_ref[...] = kv[:, kw:]
    k_scale, q_scale = _segment_scales([(k_raw, segk_ref, cntk_ref, expk_ref),
                                        (q_raw, segq_ref, cntq_ref, expq_ref)])
    kn = k_raw * k_scale * gk_ref[...]
    for j in range(kw // LANES):
        sl = slice(LANES * j, LANES * (j + 1))
        k_slab = _rope_slab(kn[:, sl], c, s1, s2, ROT_B // 2)
        k_ref[:, sl] = k_slab
        kt_ref[sl, :] = k_slab[tail].T
        vt_ref[sl, :] = kv[tail, kw + LANES * j:kw + LANES * (j + 1)].T
    qn = q_raw * q_scale * gq_ref[...]
    for j in range(H_B * HD_B // LANES):
        sl = slice(LANES * j, LANES * (j + 1))
        q_ref[:, sl] = _rope_slab(qn[:, sl], c, s1, s2, ROT_B // 2).astype(BF16)


def _proj_b(h, tab, wb, tm):
    n = h.shape[0]
    n_tab = tab.shape[1] // tm
    weights = [wb["g_kv"], wb["w_kv"], wb["segk"], wb["cntk"], wb["expk"], wb["gk"],
               wb["g_attn"], wb["w_q"], wb["segq"], wb["cntq"], wb["expq"], wb["gq"]]
    kw = N_KV_B * HD_B
    return pl.pallas_call(
        _proj_b_kernel,
        grid=(n // tm,),
        in_specs=[pl.BlockSpec((tm, D_MODEL), lambda i: (i, 0)),
                  pl.BlockSpec((3, tm, LANES), lambda i: (0, i % n_tab, 0))]
                 + [_const_spec(w.shape) for w in weights],
        out_specs=[pl.BlockSpec((tm, kw), lambda i: (i, 0)),
                   pl.BlockSpec((tm, kw), lambda i: (i, 0)),
                   pl.BlockSpec((tm, H_B * HD_B), lambda i: (i, 0)),
                   pl.BlockSpec((None, kw, WINDOW), lambda i: (i // n_tab, 0, 0)),
                   pl.BlockSpec((None, kw, WINDOW), lambda i: (i // n_tab, 0, 0))],
        out_shape=[jax.ShapeDtypeStruct((n, kw), F32),
                   jax.ShapeDtypeStruct((n, kw), F32),
                   jax.ShapeDtypeStruct((n, H_B * HD_B), BF16),
                   jax.ShapeDtypeStruct((n // (n_tab * tm), kw, WINDOW), F32),
                   jax.ShapeDtypeStruct((n // (n_tab * tm), kw, WINDOW), F32)],
        compiler_params=pltpu.CompilerParams(dimension_semantics=("arbitrary",),
                                             vmem_limit_bytes=VMEM_LIMIT),
        name="swa_proj",
    )(h, tab, *weights)


def _swa_halves(slab, kv, ones_lane=False):
    lane = lax.broadcasted_iota(jnp.int32, slab.shape, 1)
    own = (lane >= HD_B) if kv % 2 else (lane < HD_B)
    halves = [None, None]
    halves[kv % 2] = jnp.where(own, slab, 0.0)
    halves[1 - kv % 2] = pltpu.roll(halves[kv % 2], HD_B, 1)
    if ones_lane:
        halves = [jnp.where(lane == _ones_lane(par), 1.0, h) for par, h in enumerate(halves)]
    return [h.astype(BF16) for h in halves]


def _swa_attend(jobs, valid_of, sink_of):
    scores = [[_dot_nt(q, kh[par]) for par in range(2)] for q, kh, _, _ in jobs]
    probs = []
    for j, (q, _, _, kv) in enumerate(jobs):
        m2 = q.shape[0]
        top = lax.broadcasted_iota(jnp.int32, (m2, 1), 0) < (m2 // 2)
        row = []
        for par in range(2):
            s = jnp.where(valid_of(j), scores[j][par], NEG)
            sink = jnp.where(top, sink_of(G_B * kv + par), sink_of(G_B * kv + par + 2)) * LOG2E
            m = jnp.maximum(jnp.max(s, axis=-1, keepdims=True), sink)
            row.append((jnp.exp2(s - m).astype(BF16), jnp.exp2(sink - m)))
        probs.append(row)
    outs = []
    for j, (q, _, vh, _) in enumerate(jobs):
        o = []
        for par in range(2):
            pv = _dot(probs[j][par][0], vh[par])
            ones = _ones_lane(par)
            o.append(pv / (pv[:, ones:ones + 1] + probs[j][par][1]))
        low_half = lax.broadcasted_iota(jnp.int32, o[0].shape, 1) < HD_B
        outs.append(jnp.where(low_half, o[0], o[1]))
    return outs


def _attn_b_kernel(sink_ref, q_ref, kp_ref, kc_ref, vp_ref, vc_ref, o_ref):
    g = pl.program_id(1)
    kcat = jnp.concatenate([kp_ref[...], kc_ref[...]], axis=0)
    vcat = jnp.concatenate([vp_ref[...], vc_ref[...]], axis=0)
    shape = (2 * WINDOW, 2 * WINDOW)
    qi = lax.broadcasted_iota(jnp.int32, shape, 0) & (WINDOW - 1)
    col = lax.broadcasted_iota(jnp.int32, shape, 1)
    band = (col > qi) & (col <= qi + WINDOW)
    band_first = band & ((col >= WINDOW) | (g > 0))
    n_blocks = q_ref.shape[0] // WINDOW
    for kv in range(N_KV_B):
        base = G_B * HD_B * kv
        ks = slice(LANES * (kv // 2), LANES * (kv // 2 + 1))
        k_half = _swa_halves(kcat[:, ks], kv)
        v_half = _swa_halves(vcat[:, ks], kv, ones_lane=True)
        jobs = []
        for r in range(n_blocks):
            rows = slice(WINDOW * r, WINDOW * (r + 1))
            win = slice(WINDOW * r, WINDOW * (r + 2))
            q_lhs = jnp.concatenate([q_ref[rows, base:base + LANES], q_ref[rows, base + LANES:base + 2 * LANES]], axis=0)
            jobs.append((q_lhs, [h[win] for h in k_half], [h[win] for h in v_half], kv))
        outs = _swa_attend(jobs, lambda r: band if r else band_first, lambda hh: sink_ref[hh])
        for r, o in enumerate(outs):
            rows = slice(WINDOW * r, WINDOW * (r + 1))
            o_ref[rows, base:base + LANES] = o[:WINDOW].astype(BF16)
            o_ref[rows, base + LANES:base + 2 * LANES] = o[WINDOW:].astype(BF16)


def _attn_b(sinks, q, k, v, batch, seq):
    nb = seq // WINDOW
    ng = nb // SWA_BLOCKS
    kw = N_KV_B * HD_B
    prev = lambda b, g: (b * nb + jnp.maximum(SWA_BLOCKS * g - 1, 0), 0)
    cur = lambda b, g: (b * ng + g, 0)
    return pl.pallas_call(
        _attn_b_kernel,
        grid=(batch, ng),
        in_specs=[pl.BlockSpec(memory_space=pltpu.SMEM),
                  pl.BlockSpec((SWA_BLOCKS * WINDOW, H_B * HD_B), cur),
                  pl.BlockSpec((WINDOW, kw), prev), pl.BlockSpec((SWA_BLOCKS * WINDOW, kw), cur),
                  pl.BlockSpec((WINDOW, kw), prev), pl.BlockSpec((SWA_BLOCKS * WINDOW, kw), cur)],
        out_specs=pl.BlockSpec((SWA_BLOCKS * WINDOW, H_B * HD_B), cur),
        out_shape=jax.ShapeDtypeStruct((batch * seq, H_B * HD_B), BF16),
        compiler_params=pltpu.CompilerParams(dimension_semantics=("arbitrary", "arbitrary"),
                                             vmem_limit_bytes=VMEM_LIMIT),
        name="swa_prompt_attn",
    )(sinks, q, k, k, v, v)


SWA_Q_ROWS = 8


def _attn_b_sample_kernel(sink_ref, q_ref, k_ref, v_ref, o_ref, *, t, w_buf):
    keys = k_ref.shape[1]
    rows = 2 * SWA_Q_ROWS
    ti = lax.broadcasted_iota(jnp.int32, (rows, keys), 0) & (SWA_Q_ROWS - 1)
    col = lax.broadcasted_iota(jnp.int32, (rows, keys), 1)
    diff = jnp.where(col < w_buf, ti + w_buf - col, ti - (col - w_buf))
    valid = (diff >= 0) & (diff < WINDOW) & (col < w_buf + t) & (ti < t)
    jobs = []
    for b in range(q_ref.shape[0]):
        for kv in range(N_KV_B):
            ks = slice(LANES * (kv // 2), LANES * (kv // 2 + 1))
            jobs.append((q_ref[b, kv], _swa_halves(k_ref[b, :, ks], kv),
                         _swa_halves(v_ref[b, :, ks], kv, ones_lane=True), kv))
    outs = _swa_attend(jobs, lambda j: valid, lambda hh: sink_ref[hh])
    for j, o in enumerate(outs):
        o_ref[j // N_KV_B, j % N_KV_B] = o


def _attn_b_sample(sinks, q, k_all, v_all, t, w_buf):
    bd = q.shape[0]
    bs = SWA_SAMPLE_SEQS
    blk = lambda a: pl.BlockSpec((bs,) + a.shape[1:], lambda b: (b,) + (0,) * (a.ndim - 1))
    return pl.pallas_call(
        functools.partial(_attn_b_sample_kernel, t=t, w_buf=w_buf),
        grid=(bd // bs,),
        in_specs=[pl.BlockSpec(memory_space=pltpu.SMEM), blk(q), blk(k_all), blk(v_all)],
        out_specs=blk(q),
        out_shape=jax.ShapeDtypeStruct(q.shape, F32),
        compiler_params=pltpu.CompilerParams(dimension_semantics=("arbitrary",),
                                             vmem_limit_bytes=VMEM_LIMIT),
        name="swa_sample_attn",
    )(sinks, q, k_all, v_all)


def _rope_tables(pos, n_rot, period, lane_lo):
    half = n_rot // 2
    inv = ROPE_THETA ** (-np.arange(0, n_rot, 2, dtype=np.float64) / n_rot)
    ang = np.asarray(pos, np.float64)[:, None] * inv[None, :]
    cos, sin = np.cos(ang), np.sin(ang)
    rel = np.arange(LANES) % period - lane_lo
    in1 = (rel >= 0) & (rel < half)
    in2 = (rel >= half) & (rel < 2 * half)
    idx = np.where(in1, rel, np.where(in2, rel - half, 0))
    cg, sg = cos[:, idx], sin[:, idx]
    tables = np.stack([np.where(in1 | in2, cg, 1.0), np.where(in1, -sg, 0.0), np.where(in2, sg, 0.0)])
    return jnp.asarray(tables, F32)


def _rope_tables_dup(pos, n_rot, lane_lo):
    half = n_rot // 2
    inv = ROPE_THETA ** (-np.arange(0, n_rot, 2, dtype=np.float64) / n_rot)
    ang = np.asarray(pos, np.float64)[:, None] * inv[None, :]
    cos, sin = np.cos(ang), np.sin(ang)
    rel = np.arange(LANES) - lane_lo
    in1 = (rel >= 0) & (rel < half)
    in2 = (rel >= half) & (rel < 2 * half)
    idx = np.where(in1, rel, np.where(in2, rel - half, 0))
    cg, sg = cos[:, idx], sin[:, idx]
    tables = np.stack([np.where(in1 | in2, cg, np.where(rel < 0, 1.0, 0.0)),
                       np.where(in1, -sg, np.where(in2, sg, 0.0))])
    return jnp.asarray(tables, F32)


def _segments(width, seg_lanes, extra=None):
    seg = np.zeros((width, LANES), np.float32)
    spread = np.zeros((LANES, width), np.float32)
    cnt = np.zeros((1, LANES), np.float32)
    for s, (lo, hi) in enumerate(seg_lanes):
        seg[lo:hi, s] = 1.0
        spread[s, lo:hi + (extra[s] if extra else 0)] = 1.0
        cnt[0, s] = 1.0 / (hi - lo)
    expand = np.concatenate([spread, spread], axis=0)
    return jnp.asarray(seg, BF16), jnp.asarray(cnt, F32), jnp.asarray(expand, BF16)


def _slab_gain(parts):
    pieces, pos = [], 0
    for lo, vals in parts:
        pieces += [jnp.zeros((lo - pos,), F32), vals.astype(F32)]
        pos = lo + vals.shape[0]
    pieces.append(jnp.zeros((SLAB - pos,), F32))
    return jnp.concatenate(pieces)[None, :]


def _prep_a(norm_attn, w_a_in, g_qc, w_uq, g_ckv, w_uk, w_uv, g_qn, g_qr, g_kn, g_kr):
    half = D_ROPE // 2
    tail = SLAB - KPE_LANE - D_ROPE - half
    w_kpe = w_a_in[:, D_QC + D_C:]
    w_in = jnp.concatenate([w_a_in[:, :D_QC + D_C], jnp.zeros((D_MODEL, KPE_LANE), F32),
                            w_kpe, w_kpe[:, :half], jnp.zeros((D_MODEL, tail), F32)], axis=1)
    dqk = D_NOPE + D_ROPE
    w_uq3 = w_uq.reshape(D_QC, H_A, dqk)
    w_uq_pad = jnp.concatenate([w_uq3, w_uq3[:, :, D_NOPE:D_NOPE + half], jnp.zeros((D_QC, H_A, tail), F32)],
                               axis=2).reshape(D_QC, H_A * SLAB)
    w_uk3 = w_uk.reshape(D_C, H_A, D_NOPE)
    w_uk_pad = jnp.pad(w_uk3, ((0, 0), (0, 0), (0, SLAB - D_NOPE))).reshape(D_C, H_A * SLAB)
    w_uv3 = w_uv.reshape(D_C, H_A // 2, 2, D_V)
    even = jnp.pad(w_uv3[:, :, 0], ((0, 0), (0, 0), (0, SLAB - D_V)))
    odd = jnp.pad(w_uv3[:, :, 1], ((0, 0), (0, 0), (SLAB - D_V, 0)))
    w_uv_pad = jnp.stack([even, odd], axis=2).reshape(D_C, H_A * SLAB)
    v_ones = np.zeros((1, H_A * SLAB), np.float32)
    for h in range(H_A):
        v_ones[0, SLAB * h + _ones_lane(h % 2)] = 1.0
    q_segs = []
    for h in range(H_A):
        q_segs += [(SLAB * h, SLAB * h + D_NOPE), (SLAB * h + KPE_LANE, SLAB * h + KPE_LANE + D_ROPE)]
    segq, cntq, expq = _segments(H_A * SLAB, q_segs, extra=[0, half] * H_A)
    q_scale = SCALE_A * LOG2E
    gq = jnp.tile(_slab_gain([(0, g_qn * q_scale), (KPE_LANE, g_qr * q_scale),
                              (KPE_LANE + D_ROPE, g_qr[:half] * q_scale)]), (1, H_A))
    gk_slab = _slab_gain([(0, g_kn)])
    pe_mask = np.zeros((1, SLAB), np.float32)
    pe_mask[0, KPE_LANE:KPE_LANE + D_ROPE] = 1.0
    wukt = jnp.pad(jnp.transpose(w_uk3, (1, 2, 0)), ((0, 0), (0, SLAB - D_NOPE), (0, 0)))
    return dict(
        g_attn=norm_attn[None, :], w_in=w_in.astype(BF16), g_qc=g_qc[None, :], g_ckv=g_ckv[None, :],
        g_kpe=_slab_gain([(KPE_LANE, g_kr), (KPE_LANE + D_ROPE, g_kr[:half])]), pe_mask=jnp.asarray(pe_mask),
        w_uq=w_uq_pad.astype(BF16), segq=segq, cntq=cntq, expq=expq, gq=gq,
        w_uk=w_uk_pad.astype(BF16),
        w_uv=w_uv_pad.astype(BF16), v_ones=jnp.asarray(v_ones), gk_slab=gk_slab, wukt_pad=wukt.astype(BF16),
        wukt=jnp.transpose(w_uk3, (2, 1, 0)).reshape(H_A * D_NOPE, D_C).astype(BF16))


def _prep_b(g_kv, w_kv, g_k, norm_attn, w_q, g_q):
    kw = N_KV_B * HD_B
    segk, cntk, expk = _segments(kw, [(HD_B * h, HD_B * (h + 1)) for h in range(N_KV_B)])
    segq, cntq, expq = _segments(H_B * HD_B, [(HD_B * h, HD_B * (h + 1)) for h in range(H_B)])
    return dict(g_kv=g_kv[None, :], w_kv=w_kv.astype(BF16), segk=segk, cntk=cntk, expk=expk,
                gk=jnp.tile(g_k, N_KV_B)[None, :], g_attn=norm_attn[None, :], w_q=w_q.astype(BF16),
                segq=segq, cntq=cntq, expq=expq, gq=jnp.tile(g_q * (SCALE_B * LOG2E), H_B)[None, :])


def kernel(x_prompt, x_sample, cache_mla, state_win_k, state_win_v, page_table, norm_attn, norm_ffn, w_a_in, g_qc, w_uq, g_ckv, w_uk, w_uv, g_qn_a, g_qr_a, g_kn_a, g_kr_a, w_a_out, g_kv_shared, w_kv_shared, g_k_b, w_q_b, g_q_b, sinks, w_b_out, w_ffn_in, w_ffn_out):
    batch, seq, _ = x_prompt.shape
    bd, t_dec, _ = x_sample.shape
    past_len = page_table.shape[1] * PAGE_SIZE
    w_buf = state_win_k.shape[1]
    kw = N_KV_B * HD_B
    assert w_a_in.shape[0] == 1 and w_q_b.shape[0] == 1, "one MLA layer followed by one SWA layer"
    assert w_buf == WINDOW and seq % TM_POST == 0 and seq % TM_SWA_PROJ == 0 and (bd * t_dec) % 8 == 0
    assert seq >= WINDOW and bd * t_dec >= WINDOW, "the window tail is taken from the last 128 rows of a tile"
    assert t_dec <= SWA_Q_ROWS and bd % SWA_SAMPLE_SEQS == 0 and page_table.shape[1] % (2 * PAGES_PER_GROUP) == 0

    wa = _prep_a(norm_attn[0], w_a_in[0], g_qc[0], w_uq[0], g_ckv[0], w_uk[0], w_uv[0],
                 g_qn_a[0], g_qr_a[0], g_kn_a[0], g_kr_a[0])
    wb = _prep_b(g_kv_shared, w_kv_shared, g_k_b, norm_attn[1], w_q_b[0], g_q_b[0])
    w_a_out_b = w_a_out[0].astype(BF16)
    w_b_out_b = w_b_out[0].astype(BF16)
    ffn_in = w_ffn_in.astype(BF16)
    ffn_out = w_ffn_out.astype(BF16)
    g_ffn = norm_ffn[:, None, :]
    sink_b = sinks[0]

    pos_p = np.arange(seq)
    n_s = bd * t_dec
    pos_s = past_len + np.arange(n_s) % t_dec

    xp = x_prompt.reshape(batch * seq, D_MODEL)
    q, k, v, rows_p = _proj_a(xp, _rope_tables_dup(pos_p, D_ROPE, KPE_LANE), wa, TM_MLA_PROJ)
    o = _attn_a(q, k, v, batch, seq, TQ_MLA)
    h = _post(xp, o, w_a_out_b, g_ffn[0], ffn_in[0], ffn_out[0], TM_POST)
    k_p, v_p, q_b, kt_p, vt_p = _proj_b(h, _rope_tables(pos_p, ROT_B, HD_B, 0), wb, TM_SWA_PROJ)
    o = _attn_b(sink_b, q_b, k_p, v_p, batch, seq)
    y_prompt = _post(h, o, w_b_out_b, g_ffn[1], ffn_in[1], ffn_out[1], TM_POST)

    xs = x_sample.reshape(n_s, D_MODEL)
    q, _, _, rows_s = _proj_a(xs, _rope_tables_dup(pos_s, D_ROPE, KPE_LANE), wa, n_s)
    qabs = _qabs(q, wa["gk_slab"], wa["wukt_pad"]).reshape(bd, t_dec * H_A, D_C)
    qpe = q.reshape(n_s, H_A, SLAB)[:, :, KPE_LANE:KPE_LANE + D_ROPE].reshape(bd, t_dec * H_A, D_ROPE)
    new_pad = jnp.pad(jnp.swapaxes(rows_s.reshape(bd, t_dec, D_CKV), 1, 2), ((0, 0), (0, 0), (0, PAGE_SIZE - t_dec)))
    olat = _paged_attn(page_table, jnp.swapaxes(cache_mla, 2, 3), wa["wukt"], qabs, qpe, new_pad, t_dec)
    o = _latent_out(olat.reshape(n_s, H_A * D_C), wa["w_uv"])
    h = _post(xs, o, w_a_out_b, g_ffn[0], ffn_in[0], ffn_out[0], n_s)
    k_s, v_s, q_b, _, _ = _proj_b(h, _rope_tables(pos_s, ROT_B, HD_B, 0), wb, n_s)
    q4 = jnp.transpose(q_b.reshape(bd, t_dec, N_KV_B, 2, LANES), (0, 2, 3, 1, 4))
    q4 = jnp.pad(q4, ((0, 0), (0, 0), (0, 0), (0, SWA_Q_ROWS - t_dec), (0, 0))).reshape(bd, N_KV_B, 2 * SWA_Q_ROWS, LANES)
    key_pad = jnp.zeros((bd, 16 - t_dec, kw), F32)
    k_all = jnp.concatenate([state_win_k.reshape(bd, w_buf, kw), k_s.reshape(bd, t_dec, kw), key_pad], axis=1)
    v_all = jnp.concatenate([state_win_v.reshape(bd, w_buf, kw), v_s.reshape(bd, t_dec, kw), key_pad], axis=1)
    o4 = _attn_b_sample(sink_b, q4, k_all, v_all, t_dec, w_buf)
    o = jnp.transpose(o4.reshape(bd, N_KV_B, 2, SWA_Q_ROWS, LANES)[:, :, :, :t_dec], (0, 3, 1, 2, 4))
    y_sample = _post(h, o.reshape(n_s, H_B * HD_B).astype(BF16), w_b_out_b, g_ffn[1], ffn_in[1], ffn_out[1], n_s)

    win_k_p = jnp.transpose(kt_p.reshape(batch, N_KV_B, HD_B, WINDOW), (0, 3, 1, 2))
    win_v_p = jnp.transpose(vt_p.reshape(batch, N_KV_B, HD_B, WINDOW), (0, 3, 1, 2))
    win_k_s = jnp.concatenate([state_win_k, k_s.reshape(bd, t_dec, N_KV_B, HD_B)], axis=1)[:, -w_buf:]
    win_v_s = jnp.concatenate([state_win_v, v_s.reshape(bd, t_dec, N_KV_B, HD_B)], axis=1)[:, -w_buf:]
    return (y_prompt.reshape(batch, seq, D_MODEL), y_sample.reshape(bd, t_dec, D_MODEL),
            rows_p.reshape(1, batch, seq, D_CKV), rows_s.reshape(1, bd, t_dec, D_CKV),
            win_k_p, win_v_p, win_k_s, win_v_s)
```

```python
import functools

import numpy as np
import jax
import jax.numpy as jnp
from jax import lax
from jax.experimental import pallas as pl
from jax.experimental.pallas import tpu as pltpu

F32 = jnp.float32
BF16 = jnp.bfloat16

D_MODEL = 1024
PAGE_SIZE = 128
H_A = 16
D_NOPE = 64
D_ROPE = 32
D_V = 64
D_QC = 384
D_C = 256
D_CKV = D_C + D_ROPE
SCALE_A = (D_NOPE + D_ROPE) ** -0.5
H_B = 16
N_KV_B = 4
HD_B = 64
G_B = H_B // N_KV_B
WINDOW = 128
ROT_B = HD_B // 4
SCALE_B = HD_B ** -0.5
D_FF = 2816
ROPE_THETA = 500000.0
EPS = 1e-6
NEG = -1e30
LOG2E = 1.4426950408889634

LANES = 128
SLAB = 128
A_IN_COLS = 768
KPE_LANE = 64
VMEM_LIMIT = 56 * 1024 * 1024
FF_CHUNK = 256
PAGES_PER_GROUP = 16
PAGES_PER_DOT = 2
ACC_AFTER_DOTS = 4
TM_MLA_PROJ = 256
TM_SWA_PROJ = 512
TM_POST = 512
TQ_MLA = 256
ATTN_LOOKAHEAD = 2
SWA_SAMPLE_SEQS = 8
SWA_BLOCKS = 4

_NT = (((1,), (1,)), ((), ()))


def _dot(a, b):
    return jnp.dot(a, b, preferred_element_type=F32)


def _dot_nt(a, b):
    return lax.dot_general(a, b, _NT, preferred_element_type=F32)


def _rms(x, g):
    ms = jnp.mean(x * x, axis=-1, keepdims=True)
    return x * lax.rsqrt(ms + EPS) * g


def _rope_slab(x, c, s1, s2, half):
    return x * c + pltpu.roll(x, LANES - half, 1) * s1 + pltpu.roll(x, half, 1) * s2


def _rope_slab_dup(x, c, s, half):
    return x * c + pltpu.roll(x, LANES - half, 1) * s


def _segment_scales(items):
    sums = [_dot((raw * raw).astype(BF16), seg_ref[...]) for raw, seg_ref, _, _ in items]
    scales = []
    for ss, (_, _, inv_cnt_ref, expand_ref) in zip(sums, items):
        rs = lax.rsqrt(ss * inv_cnt_ref[...] + EPS)
        hi = rs.astype(BF16)
        lo = (rs - hi.astype(F32)).astype(BF16)
        scales.append(_dot(jnp.concatenate([hi, lo], axis=1), expand_ref[...]))
    return scales


def _proj_a_kernel(x_ref, tab_ref, g_attn_ref, w_in_ref, g_qc_ref, g_ckv_ref, g_kpe_ref, pe_mask_ref,
                   w_uq_ref, segq_ref, cntq_ref, expq_ref, gq_ref,
                   w_uk_ref, gk_ref, w_uv_ref, v_ones_ref,
                   q_ref, k_ref, v_ref, rows_ref):
    c, s = tab_ref[0], tab_ref[1]
    hn = _rms(x_ref[...], g_attn_ref[...]).astype(BF16)
    a = _dot(hn, w_in_ref[...])
    cq = _rms(a[:, :D_QC], g_qc_ref[...]).astype(BF16)
    ckv = _rms(a[:, D_QC:D_QC + D_C], g_ckv_ref[...])
    kpe = a[:, D_QC + D_C:]
    ms = jnp.sum(kpe * kpe * pe_mask_ref[...], axis=-1, keepdims=True) * (1.0 / D_ROPE)
    kpe = _rope_slab_dup(kpe * lax.rsqrt(ms + EPS) * g_kpe_ref[...], c, s, D_ROPE // 2)

    ckv_b = ckv.astype(BF16)
    q_raw = _dot(cq, w_uq_ref[...])
    qn = q_raw * _segment_scales([(q_raw, segq_ref, cntq_ref, expq_ref)])[0] * gq_ref[...]
    k_raw = _dot(ckv_b, w_uk_ref[...])
    for h in range(H_A):
        sl = slice(SLAB * h, SLAB * (h + 1))
        q_ref[:, sl] = _rope_slab_dup(qn[:, sl], c, s, D_ROPE // 2).astype(BF16)
        k_h = k_raw[:, sl]
        ms_h = jnp.sum(k_h * k_h, axis=-1, keepdims=True) * (1.0 / D_NOPE)
        k_ref[:, sl] = (k_h * lax.rsqrt(ms_h + EPS) * gk_ref[...] + kpe).astype(BF16)
    v_ref[...] = (_dot(ckv_b, w_uv_ref[...]) + v_ones_ref[...]).astype(BF16)
    rows_ref[:, :D_C] = ckv
    rows_ref[:, D_C:] = kpe[:, KPE_LANE:KPE_LANE + D_ROPE]


def _const_spec(shape):
    zeros = (0,) * len(shape)
    return pl.BlockSpec(shape, lambda *_: zeros, pipeline_mode=pl.Buffered(1))


def _proj_a(x, tab, wa, tm):
    n = x.shape[0]
    n_tab = tab.shape[1] // tm
    weights = [wa["g_attn"], wa["w_in"], wa["g_qc"], wa["g_ckv"], wa["g_kpe"], wa["pe_mask"],
               wa["w_uq"], wa["segq"], wa["cntq"], wa["expq"], wa["gq"],
               wa["w_uk"], wa["gk_slab"], wa["w_uv"], wa["v_ones"]]
    wide = H_A * SLAB
    return pl.pallas_call(
        _proj_a_kernel,
        grid=(n // tm,),
        in_specs=[pl.BlockSpec((tm, D_MODEL), lambda i: (i, 0)),
                  pl.BlockSpec((tab.shape[0], tm, LANES), lambda i: (0, i % n_tab, 0))]
                 + [_const_spec(w.shape) for w in weights],
        out_specs=[pl.BlockSpec((tm, wide), lambda i: (i, 0)),
                   pl.BlockSpec((tm, wide), lambda i: (i, 0)),
                   pl.BlockSpec((tm, wide), lambda i: (i, 0)),
                   pl.BlockSpec((tm, D_CKV), lambda i: (i, 0))],
        out_shape=[jax.ShapeDtypeStruct((n, wide), BF16),
                   jax.ShapeDtypeStruct((n, wide), BF16),
                   jax.ShapeDtypeStruct((n, wide), BF16),
                   jax.ShapeDtypeStruct((n, D_CKV), F32)],
        compiler_params=pltpu.CompilerParams(dimension_semantics=("arbitrary",),
                                             vmem_limit_bytes=VMEM_LIMIT),
        name="mla_proj",
    )(x, tab, *weights)


def _ones_lane(parity):
    return D_V if parity == 0 else 0


def _attn_a_kernel(q_ref, k_ref, v_ref, o_ref, *, tq):
    seq = q_ref.shape[0]
    causal = (lax.broadcasted_iota(jnp.int32, (tq, tq), 1) <= lax.broadcasted_iota(jnp.int32, (tq, tq), 0))
    low_half = lax.broadcasted_iota(jnp.int32, (tq, SLAB), 1) < D_V
    jobs = [(c, e) for c in range(seq // tq) for e in range(2)]

    def windows(c, e):
        return slice(c * tq, (c + 1) * tq), slice(0, c * tq), slice(SLAB * e, SLAB * (e + 1))

    def score(c, e):
        rows, past, ls = windows(c, e)
        q = q_ref[rows, ls]
        s_d = _dot_nt(q, k_ref[rows, ls])
        return s_d, (_dot_nt(q, k_ref[past, ls]) if c else None)

    def attend(c, e, s_d, s_p):
        rows, past, ls = windows(c, e)
        s_d = jnp.where(causal, s_d, NEG)
        m = jnp.max(s_d, axis=-1, keepdims=True)
        if c:
            m = jnp.maximum(m, jnp.max(s_p, axis=-1, keepdims=True))
        acc = _dot(jnp.exp2(s_d - m).astype(BF16), v_ref[rows, ls])
        if c:
            acc = acc + _dot(jnp.exp2(s_p - m).astype(BF16), v_ref[past, ls])
        ones = _ones_lane(e)
        return acc / acc[:, ones:ones + 1]

    ahead = [score(*jobs[j]) for j in range(min(ATTN_LOOKAHEAD, len(jobs)))]
    out = None
    for j, (c, e) in enumerate(jobs):
        if j + ATTN_LOOKAHEAD < len(jobs):
            ahead.append(score(*jobs[j + ATTN_LOOKAHEAD]))
        o_e = attend(c, e, *ahead[j])
        ahead[j] = None
        if e == 0:
            out = o_e
        else:
            o_ref[c * tq:(c + 1) * tq, :] = jnp.where(low_half, out, o_e).astype(BF16)


def _attn_a(q, k, v, batch, seq, tq):
    pairs = H_A // 2
    return pl.pallas_call(
        functools.partial(_attn_a_kernel, tq=tq),
        grid=(batch, pairs),
        in_specs=[pl.BlockSpec((seq, 2 * SLAB), lambda b, j: (b, j)),
                  pl.BlockSpec((seq, 2 * SLAB), lambda b, j: (b, j)),
                  pl.BlockSpec((seq, 2 * SLAB), lambda b, j: (b, j))],
        out_specs=pl.BlockSpec((seq, SLAB), lambda b, j: (b, j)),
        out_shape=jax.ShapeDtypeStruct((batch * seq, H_A * D_V), BF16),
        compiler_params=pltpu.CompilerParams(
            dimension_semantics=("arbitrary", "arbitrary"),
            vmem_limit_bytes=VMEM_LIMIT),
        name="mla_prompt_attn",
    )(q, k, v)


def _qabs_kernel(q_ref, gk_ref, wukt_ref, o_ref):
    for h in range(H_A):
        qs = (q_ref[:, SLAB * h:SLAB * (h + 1)].astype(F32) * gk_ref[...]).astype(BF16)
        o_ref[:, D_C * h:D_C * (h + 1)] = _dot(qs, wukt_ref[h]).astype(BF16)


def _qabs(q, gk_slab, wukt):
    n = q.shape[0]
    return pl.pallas_call(
        _qabs_kernel,
        out_shape=jax.ShapeDtypeStruct((n, H_A * D_C), BF16),
        compiler_params=pltpu.CompilerParams(vmem_limit_bytes=VMEM_LIMIT),
        name="mla_absorb_q",
    )(q, gk_slab, wukt)


def _paged_kernel(pt_ref, cache_ref, wukt_ref, qabs_ref, qpe_ref, new_ref, o_ref, lhs_sc, pg_sc, sem, *, n_pages, t_new):
    seq = pl.program_id(0)
    group = pg_sc.shape[1]
    n_groups = n_pages // group
    rows_q = qabs_ref.shape[0]

    def page_copy(sq, g, u):
        slot = g % 2
        return pltpu.make_async_copy(cache_ref.at[0, pt_ref[sq, g * group + u]], pg_sc.at[slot, u], sem.at[slot])

    def start_group(sq, g):
        for u in range(group):
            page_copy(sq, g, u).start()

    def wait_group(sq, g):
        for u in range(group):
            page_copy(sq, g, u).wait()

    @pl.when(seq == 0)
    def _():
        start_group(seq, 0)

    lhs_sc[:H_A * D_NOPE, :] = wukt_ref[...]
    lhs_sc[H_A * D_NOPE:, :] = qabs_ref[...]

    def nope_scores(ct):
        keys = ct.shape[1]
        big = _dot(lhs_sc[...], ct)
        kt = big[:H_A * D_NOPE]
        ssq = jnp.sum((kt * kt).reshape(D_NOPE, H_A, keys), axis=0)
        rs = lax.rsqrt(ssq * (1.0 / D_NOPE) + EPS)
        rs_q = jnp.concatenate([rs] * (rows_q // H_A), axis=0)
        return big[H_A * D_NOPE:] * rs_q

    def rope_scores(kpets):
        return _dot(qpe_ref[...], jnp.concatenate(kpets, axis=1))

    def accumulate(state, s, ct):
        m_old, l, acc = state
        m_new = jnp.maximum(m_old, jnp.max(s, axis=-1, keepdims=True))
        corr = jnp.exp2(m_old - m_new)
        p = jnp.exp2(s - m_new)
        return (m_new, l * corr + jnp.sum(p, axis=-1, keepdims=True),
                acc * corr + _dot_nt(p.astype(BF16), ct))

    state = (jnp.full((rows_q, 1), NEG, F32), jnp.zeros((rows_q, 1), F32), jnp.zeros((rows_q, D_C), F32))
    pending = None
    for g in range(n_groups):
        wait_group(seq, g)
        if g + 1 < n_groups:
            start_group(seq, g + 1)
        else:
            @pl.when(seq + 1 < pl.num_programs(0))
            def _():
                start_group(seq + 1, 0)
        slot = g % 2
        cts, kpets = [], []
        for u in range(0, group, PAGES_PER_DOT):
            pages = range(u, u + PAGES_PER_DOT)
            cts.append(jnp.concatenate([pg_sc[slot, v, :D_C, :] for v in pages], axis=1).astype(BF16))
            kpets.append(jnp.concatenate([pg_sc[slot, v, D_C:, :] for v in pages], axis=1).astype(BF16))
        nope = []
        for i, ct in enumerate(cts):
            if i == ACC_AFTER_DOTS and pending is not None:
                state = accumulate(state, *pending)
            nope.append(nope_scores(ct))
        pending = (jnp.concatenate(nope, axis=1) + rope_scores(kpets), jnp.concatenate(cts, axis=1))
    state = accumulate(state, *pending)

    keys = new_ref.shape[1]
    t_row = lax.shift_right_logical(lax.broadcasted_iota(jnp.int32, (rows_q, keys), 0), H_A.bit_length() - 1)
    s_col = lax.broadcasted_iota(jnp.int32, (rows_q, keys), 1)
    ct = new_ref[:D_C, :].astype(BF16)
    s = nope_scores(ct) + rope_scores([new_ref[D_C:, :].astype(BF16)])
    _, l, acc = accumulate(state, jnp.where((s_col <= t_row) & (s_col < t_new), s, NEG), ct)
    o_ref[...] = acc / l


def _paged_attn(page_table, cache, wukt, qabs, qpe, new_pad, t_new):
    bd, n_pages = page_table.shape
    rows_q = qabs.shape[1]
    assert (n_pages // PAGES_PER_GROUP) % 2 == 0, "the slot of a page group must not depend on the sequence"
    grid_spec = pltpu.PrefetchScalarGridSpec(
        num_scalar_prefetch=1,
        grid=(bd,),
        in_specs=[pl.BlockSpec(memory_space=pl.ANY),
                  pl.BlockSpec(wukt.shape, lambda b, pt: (0, 0)),
                  pl.BlockSpec((None, rows_q, D_C), lambda b, pt: (b, 0, 0)),
                  pl.BlockSpec((None, rows_q, D_ROPE), lambda b, pt: (b, 0, 0)),
                  pl.BlockSpec((None,) + new_pad.shape[1:], lambda b, pt: (b, 0, 0))],
        out_specs=pl.BlockSpec((None, rows_q, D_C), lambda b, pt: (b, 0, 0)),
        scratch_shapes=[pltpu.VMEM((H_A * D_NOPE + rows_q, D_C), BF16),
                        pltpu.VMEM((2, PAGES_PER_GROUP, D_CKV, PAGE_SIZE), F32),
                        pltpu.SemaphoreType.DMA((2,))])
    return pl.pallas_call(
        functools.partial(_paged_kernel, n_pages=n_pages, t_new=t_new),
        grid_spec=grid_spec,
        out_shape=jax.ShapeDtypeStruct((bd, rows_q, D_C), F32),
        compiler_params=pltpu.CompilerParams(dimension_semantics=("arbitrary",),
                                             vmem_limit_bytes=VMEM_LIMIT),
        name="mla_paged_attn",
    )(page_table, cache, wukt, qabs, qpe, new_pad)


def _latent_out_kernel(olat_ref, w_uv_ref, o_ref):
    for j in range(H_A // 2):
        acc = None
        for e in range(2):
            h = 2 * j + e
            part = _dot(olat_ref[:, D_C * h:D_C * (h + 1)].astype(BF16),
                        w_uv_ref[:, SLAB * h:SLAB * (h + 1)])
            acc = part if acc is None else acc + part
        o_ref[:, SLAB * j:SLAB * (j + 1)] = acc.astype(BF16)


def _latent_out(olat, w_uv_pad):
    n = olat.shape[0]
    return pl.pallas_call(
        _latent_out_kernel,
        out_shape=jax.ShapeDtypeStruct((n, H_A * D_V), BF16),
        compiler_params=pltpu.CompilerParams(vmem_limit_bytes=VMEM_LIMIT),
        name="mla_latent_out",
    )(olat, w_uv_pad)


def _post_kernel(x_ref, o_ref, w_o_ref, g_ref, w_in_ref, w_out_ref, y_ref):
    h1 = x_ref[...] + _dot(o_ref[...], w_o_ref[...])
    hn = _rms(h1, g_ref[...]).astype(BF16)
    acc = h1
    for c in range(D_FF // FF_CHUNK):
        lo = c * FF_CHUNK
        a1 = _dot(hn, w_in_ref[:, lo:lo + FF_CHUNK])
        a2 = _dot(hn, w_in_ref[:, D_FF + lo:D_FF + lo + FF_CHUNK])
        gate = (a1 * jax.nn.sigmoid(a1)) * a2
        acc = acc + _dot(gate.astype(BF16), w_out_ref[lo:lo + FF_CHUNK, :])
    y_ref[...] = acc


def _post(x, o, w_o, g, w_in, w_out, tm):
    n = x.shape[0]
    return pl.pallas_call(
        _post_kernel,
        grid=(n // tm,),
        in_specs=[pl.BlockSpec((tm, D_MODEL), lambda i: (i, 0)),
                  pl.BlockSpec((tm, o.shape[1]), lambda i: (i, 0)),
                  _const_spec(w_o.shape), _const_spec(g.shape),
                  _const_spec(w_in.shape), _const_spec(w_out.shape)],
        out_specs=pl.BlockSpec((tm, D_MODEL), lambda i: (i, 0)),
        out_shape=jax.ShapeDtypeStruct((n, D_MODEL), F32),
        compiler_params=pltpu.CompilerParams(dimension_semantics=("arbitrary",),
                                             vmem_limit_bytes=VMEM_LIMIT),
        name="outproj_swiglu",
    )(x, o, w_o, g, w_in, w_out)


def _proj_b_kernel(h_ref, tab_ref, g_kv_ref, w_kv_ref, segk_ref, cntk_ref, expk_ref, gk_ref,
                   g_attn_ref, w_q_ref, segq_ref, cntq_ref, expq_ref, gq_ref,
                   k_ref, v_ref, q_ref, kt_ref, vt_ref):
    c, s1, s2 = tab_ref[0], tab_ref[1], tab_ref[2]
    h = h_ref[...]
    hr = h * lax.rsqrt(jnp.mean(h * h, axis=-1, keepdims=True) + EPS)
    kv = _dot((hr * g_kv_ref[...]).astype(BF16), w_kv_ref[...])
    q_raw = _dot((hr * g_attn_ref[...]).astype(BF16), w_q_ref[...])
    kw = N_KV_B * HD_B
    tail = slice(h.shape[0] - WINDOW, h.shape[0])
    k_raw = kv[:, :kw]
    v_ref[...] = kv[:, kw:]
    k_scale, q_scale = _segment_scales([(k_raw, segk_ref, cntk_ref, expk_ref),
                                        (q_raw, segq_ref, cntq_ref, expq_ref)])
    kn = k_raw * k_scale * gk_ref[...]
    for j in range(kw // LANES):
        sl = slice(LANES * j, LANES * (j + 1))
        k_slab = _rope_slab(kn[:, sl], c, s1, s2, ROT_B // 2)
        k_ref[:, sl] = k_slab
        kt_ref[sl, :] = k_slab[tail].T
        vt_ref[sl, :] = kv[tail, kw + LANES * j:kw + LANES * (j + 1)].T
    qn = q_raw * q_scale * gq_ref[...]
    for j in range(H_B * HD_B // LANES):
        sl = slice(LANES * j, LANES * (j + 1))
        q_ref[:, sl] = _rope_slab(qn[:, sl], c, s1, s2, ROT_B // 2).astype(BF16)


def _proj_b(h, tab, wb, tm):
    n = h.shape[0]
    n_tab = tab.shape[1] // tm
    weights = [wb["g_kv"], wb["w_kv"], wb["segk"], wb["cntk"], wb["expk"], wb["gk"],
               wb["g_attn"], wb["w_q"], wb["segq"], wb["cntq"], wb["expq"], wb["gq"]]
    kw = N_KV_B * HD_B
    return pl.pallas_call(
        _proj_b_kernel,
        grid=(n // tm,),
        in_specs=[pl.BlockSpec((tm, D_MODEL), lambda i: (i, 0)),
                  pl.BlockSpec((3, tm, LANES), lambda i: (0, i % n_tab, 0))]
                 + [_const_spec(w.shape) for w in weights],
        out_specs=[pl.BlockSpec((tm, kw), lambda i: (i, 0)),
                   pl.BlockSpec((tm, kw), lambda i: (i, 0)),
                   pl.BlockSpec((tm, H_B * HD_B), lambda i: (i, 0)),
                   pl.BlockSpec((None, kw, WINDOW), lambda i: (i // n_tab, 0, 0)),
                   pl.BlockSpec((None, kw, WINDOW), lambda i: (i // n_tab, 0, 0))],
        out_shape=[jax.ShapeDtypeStruct((n, kw), F32),
                   jax.ShapeDtypeStruct((n, kw), F32),
                   jax.ShapeDtypeStruct((n, H_B * HD_B), BF16),
                   jax.ShapeDtypeStruct((n // (n_tab * tm), kw, WINDOW), F32),
                   jax.ShapeDtypeStruct((n // (n_tab * tm), kw, WINDOW), F32)],
        compiler_params=pltpu.CompilerParams(dimension_semantics=("arbitrary",),
                                             vmem_limit_bytes=VMEM_LIMIT),
        name="swa_proj",
    )(h, tab, *weights)


def _swa_halves(slab, kv, ones_lane=False):
    lane = lax.broadcasted_iota(jnp.int32, slab.shape, 1)
    own = (lane >= HD_B) if kv % 2 else (lane < HD_B)
    halves = [None, None]
    halves[kv % 2] = jnp.where(own, slab, 0.0)
    halves[1 - kv % 2] = pltpu.roll(halves[kv % 2], HD_B, 1)
    if ones_lane:
        halves = [jnp.where(lane == _ones_lane(par), 1.0, h) for par, h in enumerate(halves)]
    return [h.astype(BF16) for h in halves]


def _swa_attend(jobs, valid_of, sink_of):
    scores = [[_dot_nt(q, kh[par]) for par in range(2)] for q, kh, _, _ in jobs]
    probs = []
    for j, (q, _, _, kv) in enumerate(jobs):
        m2 = q.shape[0]
        top = lax.broadcasted_iota(jnp.int32, (m2, 1), 0) < (m2 // 2)
        row = []
        for par in range(2):
            s = jnp.where(valid_of(j), scores[j][par], NEG)
            sink = jnp.where(top, sink_of(G_B * kv + par), sink_of(G_B * kv + par + 2)) * LOG2E
            m = jnp.maximum(jnp.max(s, axis=-1, keepdims=True), sink)
            row.append((jnp.exp2(s - m).astype(BF16), jnp.exp2(sink - m)))
        probs.append(row)
    outs = []
    for j, (q, _, vh, _) in enumerate(jobs):
        o = []
        for par in range(2):
            pv = _dot(probs[j][par][0], vh[par])
            ones = _ones_lane(par)
            o.append(pv / (pv[:, ones:ones + 1] + probs[j][par][1]))
        low_half = lax.broadcasted_iota(jnp.int32, o[0].shape, 1) < HD_B
        outs.append(jnp.where(low_half, o[0], o[1]))
    return outs


def _attn_b_kernel(sink_ref, q_ref, kp_ref, kc_ref, vp_ref, vc_ref, o_ref):
    g = pl.program_id(1)
    kcat = jnp.concatenate([kp_ref[...], kc_ref[...]], axis=0)
    vcat = jnp.concatenate([vp_ref[...], vc_ref[...]], axis=0)
    shape = (2 * WINDOW, 2 * WINDOW)
    qi = lax.broadcasted_iota(jnp.int32, shape, 0) & (WINDOW - 1)
    col = lax.broadcasted_iota(jnp.int32, shape, 1)
    band = (col > qi) & (col <= qi + WINDOW)
    band_first = band & ((col >= WINDOW) | (g > 0))
    n_blocks = q_ref.shape[0] // WINDOW
    for kv in range(N_KV_B):
        base = G_B * HD_B * kv
        ks = slice(LANES * (kv // 2), LANES * (kv // 2 + 1))
        k_half = _swa_halves(kcat[:, ks], kv)
        v_half = _swa_halves(vcat[:, ks], kv, ones_lane=True)
        jobs = []
        for r in range(n_blocks):
            rows = slice(WINDOW * r, WINDOW * (r + 1))
            win = slice(WINDOW * r, WINDOW * (r + 2))
            q_lhs = jnp.concatenate([q_ref[rows, base:base + LANES], q_ref[rows, base + LANES:base + 2 * LANES]], axis=0)
            jobs.append((q_lhs, [h[win] for h in k_half], [h[win] for h in v_half], kv))
        outs = _swa_attend(jobs, lambda r: band if r else band_first, lambda hh: sink_ref[hh])
        for r, o in enumerate(outs):
            rows = slice(WINDOW * r, WINDOW * (r + 1))
            o_ref[rows, base:base + LANES] = o[:WINDOW].astype(BF16)
            o_ref[rows, base + LANES:base + 2 * LANES] = o[WINDOW:].astype(BF16)


def _attn_b(sinks, q, k, v, batch, seq):
    nb = seq // WINDOW
    ng = nb // SWA_BLOCKS
    kw = N_KV_B * HD_B
    prev = lambda b, g: (b * nb + jnp.maximum(SWA_BLOCKS * g - 1, 0), 0)
    cur = lambda b, g: (b * ng + g, 0)
    return pl.pallas_call(
        _attn_b_kernel,
        grid=(batch, ng),
        in_specs=[pl.BlockSpec(memory_space=pltpu.SMEM),
                  pl.BlockSpec((SWA_BLOCKS * WINDOW, H_B * HD_B), cur),
                  pl.BlockSpec((WINDOW, kw), prev), pl.BlockSpec((SWA_BLOCKS * WINDOW, kw), cur),
                  pl.BlockSpec((WINDOW, kw), prev), pl.BlockSpec((SWA_BLOCKS * WINDOW, kw), cur)],
        out_specs=pl.BlockSpec((SWA_BLOCKS * WINDOW, H_B * HD_B), cur),
        out_shape=jax.ShapeDtypeStruct((batch * seq, H_B * HD_B), BF16),
        compiler_params=pltpu.CompilerParams(dimension_semantics=("arbitrary", "arbitrary"),
                                             vmem_limit_bytes=VMEM_LIMIT),
        name="swa_prompt_attn",
    )(sinks, q, k, k, v, v)


SWA_Q_ROWS = 8


def _attn_b_sample_kernel(sink_ref, q_ref, k_ref, v_ref, o_ref, *, t, w_buf):
    keys = k_ref.shape[1]
    rows = 2 * SWA_Q_ROWS
    ti = lax.broadcasted_iota(jnp.int32, (rows, keys), 0) & (SWA_Q_ROWS - 1)
    col = lax.broadcasted_iota(jnp.int32, (rows, keys), 1)
    diff = jnp.where(col < w_buf, ti + w_buf - col, ti - (col - w_buf))
    valid = (diff >= 0) & (diff < WINDOW) & (col < w_buf + t) & (ti < t)
    jobs = []
    for b in range(q_ref.shape[0]):
        for kv in range(N_KV_B):
            ks = slice(LANES * (kv // 2), LANES * (kv // 2 + 1))
            jobs.append((q_ref[b, kv], _swa_halves(k_ref[b, :, ks], kv),
                         _swa_halves(v_ref[b, :, ks], kv, ones_lane=True), kv))
    outs = _swa_attend(jobs, lambda j: valid, lambda hh: sink_ref[hh])
    for j, o in enumerate(outs):
        o_ref[j // N_KV_B, j % N_KV_B] = o


def _attn_b_sample(sinks, q, k_all, v_all, t, w_buf):
    bd = q.shape[0]
    bs = SWA_SAMPLE_SEQS
    blk = lambda a: pl.BlockSpec((bs,) + a.shape[1:], lambda b: (b,) + (0,) * (a.ndim - 1))
    return pl.pallas_call(
        functools.partial(_attn_b_sample_kernel, t=t, w_buf=w_buf),
        grid=(bd // bs,),
        in_specs=[pl.BlockSpec(memory_space=pltpu.SMEM), blk(q), blk(k_all), blk(v_all)],
        out_specs=blk(q),
        out_shape=jax.ShapeDtypeStruct(q.shape, F32),
        compiler_params=pltpu.CompilerParams(dimension_semantics=("arbitrary",),
                                             vmem_limit_bytes=VMEM_LIMIT),
        name="swa_sample_attn",
    )(sinks, q, k_all, v_all)


def _rope_tables(pos, n_rot, period, lane_lo):
    half = n_rot // 2
    inv = ROPE_THETA ** (-np.arange(0, n_rot, 2, dtype=np.float64) / n_rot)
    ang = np.asarray(pos, np.float64)[:, None] * inv[None, :]
    cos, sin = np.cos(ang), np.sin(ang)
    rel = np.arange(LANES) % period - lane_lo
    in1 = (rel >= 0) & (rel < half)
    in2 = (rel >= half) & (rel < 2 * half)
    idx = np.where(in1, rel, np.where(in2, rel - half, 0))
    cg, sg = cos[:, idx], sin[:, idx]
    tables = np.stack([np.where(in1 | in2, cg, 1.0), np.where(in1, -sg, 0.0), np.where(in2, sg, 0.0)])
    return jnp.asarray(tables, F32)


def _rope_tables_dup(pos, n_rot, lane_lo):
    half = n_rot // 2
    inv = ROPE_THETA ** (-np.arange(0, n_rot, 2, dtype=np.float64) / n_rot)
    ang = np.asarray(pos, np.float64)[:, None] * inv[None, :]
    cos, sin = np.cos(ang), np.sin(ang)
    rel = np.arange(LANES) - lane_lo
    in1 = (rel >= 0) & (rel < half)
    in2 = (rel >= half) & (rel < 2 * half)
    idx = np.where(in1, rel, np.where(in2, rel - half, 0))
    cg, sg = cos[:, idx], sin[:, idx]
    tables = np.stack([np.where(in1 | in2, cg, np.where(rel < 0, 1.0, 0.0)),
                       np.where(in1, -sg, np.where(in2, sg, 0.0))])
    return jnp.asarray(tables, F32)


def _segments(width, seg_lanes, extra=None):
    seg = np.zeros((width, LANES), np.float32)
    spread = np.zeros((LANES, width), np.float32)
    cnt = np.zeros((1, LANES), np.float32)
    for s, (lo, hi) in enumerate(seg_lanes):
        seg[lo:hi, s] = 1.0
        spread[s, lo:hi + (extra[s] if extra else 0)] = 1.0
        cnt[0, s] = 1.0 / (hi - lo)
    expand = np.concatenate([spread, spread], axis=0)
    return jnp.asarray(seg, BF16), jnp.asarray(cnt, F32), jnp.asarray(expand, BF16)


def _slab_gain(parts):
    pieces, pos = [], 0
    for lo, vals in parts:
        pieces += [jnp.zeros((lo - pos,), F32), vals.astype(F32)]
        pos = lo + vals.shape[0]
    pieces.append(jnp.zeros((SLAB - pos,), F32))
    return jnp.concatenate(pieces)[None, :]


def _prep_a(norm_attn, w_a_in, g_qc, w_uq, g_ckv, w_uk, w_uv, g_qn, g_qr, g_kn, g_kr):
    half = D_ROPE // 2
    tail = SLAB - KPE_LANE - D_ROPE - half
    w_kpe = w_a_in[:, D_QC + D_C:]
    w_in = jnp.concatenate([w_a_in[:, :D_QC + D_C], jnp.zeros((D_MODEL, KPE_LANE), F32),
                            w_kpe, w_kpe[:, :half], jnp.zeros((D_MODEL, tail), F32)], axis=1)
    dqk = D_NOPE + D_ROPE
    w_uq3 = w_uq.reshape(D_QC, H_A, dqk)
    w_uq_pad = jnp.concatenate([w_uq3, w_uq3[:, :, D_NOPE:D_NOPE + half], jnp.zeros((D_QC, H_A, tail), F32)],
                               axis=2).reshape(D_QC, H_A * SLAB)
    w_uk3 = w_uk.reshape(D_C, H_A, D_NOPE)
    w_uk_pad = jnp.pad(w_uk3, ((0, 0), (0, 0), (0, SLAB - D_NOPE))).reshape(D_C, H_A * SLAB)
    w_uv3 = w_uv.reshape(D_C, H_A // 2, 2, D_V)
    even = jnp.pad(w_uv3[:, :, 0], ((0, 0), (0, 0), (0, SLAB - D_V)))
    odd = jnp.pad(w_uv3[:, :, 1], ((0, 0), (0, 0), (SLAB - D_V, 0)))
    w_uv_pad = jnp.stack([even, odd], axis=2).reshape(D_C, H_A * SLAB)
    v_ones = np.zeros((1, H_A * SLAB), np.float32)
    for h in range(H_A):
        v_ones[0, SLAB * h + _ones_lane(h % 2)] = 1.0
    q_segs = []
    for h in range(H_A):
        q_segs += [(SLAB * h, SLAB * h + D_NOPE), (SLAB * h + KPE_LANE, SLAB * h + KPE_LANE + D_ROPE)]
    segq, cntq, expq = _segments(H_A * SLAB, q_segs, extra=[0, half] * H_A)
    q_scale = SCALE_A * LOG2E
    gq = jnp.tile(_slab_gain([(0, g_qn * q_scale), (KPE_LANE, g_qr * q_scale),
                              (KPE_LANE + D_ROPE, g_qr[:half] * q_scale)]), (1, H_A))
    gk_slab = _slab_gain([(0, g_kn)])
    pe_mask = np.zeros((1, SLAB), np.float32)
    pe_mask[0, KPE_LANE:KPE_LANE + D_ROPE] = 1.0
    wukt = jnp.pad(jnp.transpose(w_uk3, (1, 2, 0)), ((0, 0), (0, SLAB - D_NOPE), (0, 0)))
    return dict(
        g_attn=norm_attn[None, :], w_in=w_in.astype(BF16), g_qc=g_qc[None, :], g_ckv=g_ckv[None, :],
        g_kpe=_slab_gain([(KPE_LANE, g_kr), (KPE_LANE + D_ROPE, g_kr[:half])]), pe_mask=jnp.asarray(pe_mask),
        w_uq=w_uq_pad.astype(BF16), segq=segq, cntq=cntq, expq=expq, gq=gq,
        w_uk=w_uk_pad.astype(BF16),
        w_uv=w_uv_pad.astype(BF16), v_ones=jnp.asarray(v_ones), gk_slab=gk_slab, wukt_pad=wukt.astype(BF16),
        wukt=jnp.transpose(w_uk3, (2, 1, 0)).reshape(H_A * D_NOPE, D_C).astype(BF16))


def _prep_b(g_kv, w_kv, g_k, norm_attn, w_q, g_q):
    kw = N_KV_B * HD_B
    segk, cntk, expk = _segments(kw, [(HD_B * h, HD_B * (h + 1)) for h in range(N_KV_B)])
    segq, cntq, expq = _segments(H_B * HD_B, [(HD_B * h, HD_B * (h + 1)) for h in range(H_B)])
    return dict(g_kv=g_kv[None, :], w_kv=w_kv.astype(BF16), segk=segk, cntk=cntk, expk=expk,
                gk=jnp.tile(g_k, N_KV_B)[None, :], g_attn=norm_attn[None, :], w_q=w_q.astype(BF16),
                segq=segq, cntq=cntq, expq=expq, gq=jnp.tile(g_q * (SCALE_B * LOG2E), H_B)[None, :])


def kernel(x_prompt, x_sample, cache_mla, state_win_k, state_win_v, page_table, norm_attn, norm_ffn, w_a_in, g_qc, w_uq, g_ckv, w_uk, w_uv, g_qn_a, g_qr_a, g_kn_a, g_kr_a, w_a_out, g_kv_shared, w_kv_shared, g_k_b, w_q_b, g_q_b, sinks, w_b_out, w_ffn_in, w_ffn_out):
    batch, seq, _ = x_prompt.shape
    bd, t_dec, _ = x_sample.shape
    past_len = page_table.shape[1] * PAGE_SIZE
    w_buf = state_win_k.shape[1]
    kw = N_KV_B * HD_B
    assert w_a_in.shape[0] == 1 and w_q_b.shape[0] == 1, "one MLA layer followed by one SWA layer"
    assert w_buf == WINDOW and seq % TM_POST == 0 and seq % TM_SWA_PROJ == 0 and (bd * t_dec) % 8 == 0
    assert seq >= WINDOW and bd * t_dec >= WINDOW, "the window tail is taken from the last 128 rows of a tile"
    assert t_dec <= SWA_Q_ROWS and bd % SWA_SAMPLE_SEQS == 0 and page_table.shape[1] % (2 * PAGES_PER_GROUP) == 0

    wa = _prep_a(norm_attn[0], w_a_in[0], g_qc[0], w_uq[0], g_ckv[0], w_uk[0], w_uv[0],
                 g_qn_a[0], g_qr_a[0], g_kn_a[0], g_kr_a[0])
    wb = _prep_b(g_kv_shared, w_kv_shared, g_k_b, norm_attn[1], w_q_b[0], g_q_b[0])
    w_a_out_b = w_a_out[0].astype(BF16)
    w_b_out_b = w_b_out[0].astype(BF16)
    ffn_in = [w_ffn_in[l].astype(BF16) for l in range(w_ffn_in.shape[0])]
    ffn_out = [w_ffn_out[l].astype(BF16) for l in range(w_ffn_out.shape[0])]
    g_ffn = norm_ffn[:, None, :]
    sink_b = sinks[0]

    pos_p = np.arange(seq)
    n_s = bd * t_dec
    pos_s = past_len + np.arange(n_s) % t_dec

    xp = x_prompt.reshape(batch * seq, D_MODEL)
    q, k, v, rows_p = _proj_a(xp, _rope_tables_dup(pos_p, D_ROPE, KPE_LANE), wa, TM_MLA_PROJ)
    o = _attn_a(q, k, v, batch, seq, TQ_MLA)
    h = _post(xp, o, w_a_out_b, g_ffn[0], ffn_in[0], ffn_out[0], TM_POST)
    k_p, v_p, q_b, kt_p, vt_p = _proj_b(h, _rope_tables(pos_p, ROT_B, HD_B, 0), wb, TM_SWA_PROJ)
    o = _attn_b(sink_b, q_b, k_p, v_p, batch, seq)
    y_prompt = _post(h, o, w_b_out_b, g_ffn[1], ffn_in[1], ffn_out[1], TM_POST)

    xs = x_sample.reshape(n_s, D_MODEL)
    q, _, _, rows_s = _proj_a(xs, _rope_tables_dup(pos_s, D_ROPE, KPE_LANE), wa, n_s)
    qabs = _qabs(q, wa["gk_slab"], wa["wukt_pad"]).reshape(bd, t_dec * H_A, D_C)
    qpe = q.reshape(n_s, H_A, SLAB)[:, :, KPE_LANE:KPE_LANE + D_ROPE].reshape(bd, t_dec * H_A, D_ROPE)
    new_pad = jnp.pad(jnp.swapaxes(rows_s.reshape(bd, t_dec, D_CKV), 1, 2), ((0, 0), (0, 0), (0, PAGE_SIZE - t_dec)))
    olat = _paged_attn(page_table, jnp.swapaxes(cache_mla, 2, 3), wa["wukt"], qabs, qpe, new_pad, t_dec)
    o = _latent_out(olat.reshape(n_s, H_A * D_C), wa["w_uv"])
    h = _post(xs, o, w_a_out_b, g_ffn[0], ffn_in[0], ffn_out[0], n_s)
    k_s, v_s, q_b, _, _ = _proj_b(h, _rope_tables(pos_s, ROT_B, HD_B, 0), wb, n_s)
    q4 = jnp.transpose(q_b.reshape(bd, t_dec, N_KV_B, 2, LANES), (0, 2, 3, 1, 4))
    q4 = jnp.pad(q4, ((0, 0), (0, 0), (0, 0), (0, SWA_Q_ROWS - t_dec), (0, 0))).reshape(bd, N_KV_B, 2 * SWA_Q_ROWS, LANES)
    key_pad = jnp.zeros((bd, 16 - t_dec, kw), F32)
    k_all = jnp.concatenate([state_win_k.reshape(bd, w_buf, kw), k_s.reshape(bd, t_dec, kw), key_pad], axis=1)
    v_all = jnp.concatenate([state_win_v.reshape(bd, w_buf, kw), v_s.reshape(bd, t_dec, kw), key_pad], axis=1)
    o4 = _attn_b_sample(sink_b, q4, k_all, v_all, t_dec, w_buf)
    o = jnp.transpose(o4.reshape(bd, N_KV_B, 2, SWA_Q_ROWS, LANES)[:, :, :, :t_dec], (0, 3, 1, 2, 4))
    y_sample = _post(h, o.reshape(n_s, H_B * HD_B).astype(BF16), w_b_out_b, g_ffn[1], ffn_in[1], ffn_out[1], n_s)

    win_k_p = jnp.transpose(kt_p.reshape(batch, N_KV_B, HD_B, WINDOW), (0, 3, 1, 2))
    win_v_p = jnp.transpose(vt_p.reshape(batch, N_KV_B, HD_B, WINDOW), (0, 3, 1, 2))
    win_k_s = jnp.concatenate([state_win_k, k_s.reshape(bd, t_dec, N_KV_B, HD_B)], axis=1)[:, -w_buf:]
    win_v_s = jnp.concatenate([state_win_v, v_s.reshape(bd, t_dec, N_KV_B, HD_B)], axis=1)[:, -w_buf:]
    return (y_prompt.reshape(batch, seq, D_MODEL), y_sample.reshape(bd, t_dec, D_MODEL),
            rows_p.reshape(1, batch, seq, D_CKV), rows_s.reshape(1, bd, t_dec, D_CKV),
            win_k_p, win_v_p, win_k_s, win_v_s)
```

```python
import functools

import numpy as np
import jax
import jax.numpy as jnp
from jax import lax
from jax.experimental import pallas as pl
from jax.experimental.pallas import tpu as pltpu

F32 = jnp.float32
BF16 = jnp.bfloat16

D_MODEL = 1024
PAGE_SIZE = 128
H_A = 16
D_NOPE = 64
D_ROPE = 32
D_V = 64
D_QC = 384
D_C = 256
D_CKV = D_C + D_ROPE
SCALE_A = (D_NOPE + D_ROPE) ** -0.5
H_B = 16
N_KV_B = 4
HD_B = 64
G_B = H_B // N_KV_B
WINDOW = 128
ROT_B = HD_B // 4
SCALE_B = HD_B ** -0.5
D_FF = 2816
ROPE_THETA = 500000.0
EPS = 1e-6
NEG = -1e30
LOG2E = 1.4426950408889634

LANES = 128
MXU_DIM = 256
VMEM_LIMIT = 56 * 1024 * 1024
SLAB = 128
KPE_LANE = 64

FF_CHUNK = MXU_DIM
TM_MLA_PROJ = 256
TM_SWA_PROJ = 512
TM_POST = 512
TQ_MLA = MXU_DIM
ATTN_LOOKAHEAD = 2
PAGES_PER_GROUP = 16
PAGES_PER_DOT = MXU_DIM // PAGE_SIZE
ACC_AFTER_DOTS = 4
SWA_SAMPLE_SEQS = 8
SWA_BLOCKS = 4
SWA_Q_ROWS = 8

_NT = (((1,), (1,)), ((), ()))


def _dot(a, b):
    return jnp.dot(a, b, preferred_element_type=F32)


def _dot_nt(a, b):
    return lax.dot_general(a, b, _NT, preferred_element_type=F32)


def _rms(x, g):
    ms = jnp.mean(x * x, axis=-1, keepdims=True)
    return x * lax.rsqrt(ms + EPS) * g


def _rope_slab(x, c, s1, s2, half):
    return x * c + pltpu.roll(x, LANES - half, 1) * s1 + pltpu.roll(x, half, 1) * s2


def _rope_slab_dup(x, c, s, half):
    return x * c + pltpu.roll(x, LANES - half, 1) * s


def _segment_scales(items):
    sums = [_dot((raw * raw).astype(BF16), seg_ref[...]) for raw, seg_ref, _, _ in items]
    scales = []
    for ss, (_, _, inv_cnt_ref, expand_ref) in zip(sums, items):
        rs = lax.rsqrt(ss * inv_cnt_ref[...] + EPS)
        hi = rs.astype(BF16)
        lo = (rs - hi.astype(F32)).astype(BF16)
        scales.append(_dot(jnp.concatenate([hi, lo], axis=1), expand_ref[...]))
    return scales


def _proj_a_kernel(x_ref, tab_ref, g_attn_ref, w_in_ref, g_qc_ref, g_ckv_ref, g_kpe_ref, pe_mask_ref,
                   w_uq_ref, segq_ref, cntq_ref, expq_ref, gq_ref,
                   w_uk_ref, gk_ref, w_uv_ref, v_ones_ref,
                   q_ref, k_ref, v_ref, rows_ref):
    c, s = tab_ref[0], tab_ref[1]
    hn = _rms(x_ref[...], g_attn_ref[...]).astype(BF16)
    a = _dot(hn, w_in_ref[...])
    cq = _rms(a[:, :D_QC], g_qc_ref[...]).astype(BF16)
    ckv = _rms(a[:, D_QC:D_QC + D_C], g_ckv_ref[...])
    kpe = a[:, D_QC + D_C:]
    ms = jnp.sum(kpe * kpe * pe_mask_ref[...], axis=-1, keepdims=True) * (1.0 / D_ROPE)
    kpe = _rope_slab_dup(kpe * lax.rsqrt(ms + EPS) * g_kpe_ref[...], c, s, D_ROPE // 2)

    ckv_b = ckv.astype(BF16)
    q_raw = _dot(cq, w_uq_ref[...])
    qn = q_raw * _segment_scales([(q_raw, segq_ref, cntq_ref, expq_ref)])[0] * gq_ref[...]
    k_raw = _dot(ckv_b, w_uk_ref[...])
    for h in range(H_A):
        sl = slice(SLAB * h, SLAB * (h + 1))
        q_ref[:, sl] = _rope_slab_dup(qn[:, sl], c, s, D_ROPE // 2).astype(BF16)
        k_h = k_raw[:, sl]
        ms_h = jnp.sum(k_h * k_h, axis=-1, keepdims=True) * (1.0 / D_NOPE)
        k_ref[:, sl] = (k_h * lax.rsqrt(ms_h + EPS) * gk_ref[...] + kpe).astype(BF16)
    v_ref[...] = (_dot(ckv_b, w_uv_ref[...]) + v_ones_ref[...]).astype(BF16)
    rows_ref[:, :D_C] = ckv
    rows_ref[:, D_C:] = kpe[:, KPE_LANE:KPE_LANE + D_ROPE]


def _const_spec(shape):
    zeros = (0,) * len(shape)
    return pl.BlockSpec(shape, lambda *_: zeros, pipeline_mode=pl.Buffered(1))


def _proj_a(x, tab, wa, tm):
    n = x.shape[0]
    n_tab = tab.shape[1] // tm
    weights = [wa["g_attn"], wa["w_in"], wa["g_qc"], wa["g_ckv"], wa["g_kpe"], wa["pe_mask"],
               wa["w_uq"], wa["segq"], wa["cntq"], wa["expq"], wa["gq"],
               wa["w_uk"], wa["gk_slab"], wa["w_uv"], wa["v_ones"]]
    wide = H_A * SLAB
    return pl.pallas_call(
        _proj_a_kernel,
        grid=(n // tm,),
        in_specs=[pl.BlockSpec((tm, D_MODEL), lambda i: (i, 0)),
                  pl.BlockSpec((tab.shape[0], tm, LANES), lambda i: (0, i % n_tab, 0))]
                 + [_const_spec(w.shape) for w in weights],
        out_specs=[pl.BlockSpec((tm, wide), lambda i: (i, 0)),
                   pl.BlockSpec((tm, wide), lambda i: (i, 0)),
                   pl.BlockSpec((tm, wide), lambda i: (i, 0)),
                   pl.BlockSpec((tm, D_CKV), lambda i: (i, 0))],
        out_shape=[jax.ShapeDtypeStruct((n, wide), BF16),
                   jax.ShapeDtypeStruct((n, wide), BF16),
                   jax.ShapeDtypeStruct((n, wide), BF16),
                   jax.ShapeDtypeStruct((n, D_CKV), F32)],
        compiler_params=pltpu.CompilerParams(dimension_semantics=("arbitrary",),
                                             vmem_limit_bytes=VMEM_LIMIT),
        name="mla_proj",
    )(x, tab, *weights)


def _ones_lane(parity):
    return D_V if parity == 0 else 0


def _attn_a_kernel(q_ref, k_ref, v_ref, o_ref, *, tq):
    seq = q_ref.shape[0]
    causal = (lax.broadcasted_iota(jnp.int32, (tq, tq), 1) <= lax.broadcasted_iota(jnp.int32, (tq, tq), 0))
    low_half = lax.broadcasted_iota(jnp.int32, (tq, SLAB), 1) < D_V
    jobs = [(c, e) for c in range(seq // tq) for e in range(2)]

    def windows(c, e):
        return slice(c * tq, (c + 1) * tq), slice(0, c * tq), slice(SLAB * e, SLAB * (e + 1))

    def score(c, e):
        rows, past, ls = windows(c, e)
        q = q_ref[rows, ls]
        s_d = _dot_nt(q, k_ref[rows, ls])
        return s_d, (_dot_nt(q, k_ref[past, ls]) if c else None)

    def attend(c, e, s_d, s_p):
        rows, past, ls = windows(c, e)
        s_d = jnp.where(causal, s_d, NEG)
        m = jnp.max(s_d, axis=-1, keepdims=True)
        if c:
            m = jnp.maximum(m, jnp.max(s_p, axis=-1, keepdims=True))
        acc = _dot(jnp.exp2(s_d - m).astype(BF16), v_ref[rows, ls])
        if c:
            acc = acc + _dot(jnp.exp2(s_p - m).astype(BF16), v_ref[past, ls])
        ones = _ones_lane(e)
        return acc / acc[:, ones:ones + 1]

    ahead = [score(*jobs[j]) for j in range(min(ATTN_LOOKAHEAD, len(jobs)))]
    out = None
    for j, (c, e) in enumerate(jobs):
        if j + ATTN_LOOKAHEAD < len(jobs):
            ahead.append(score(*jobs[j + ATTN_LOOKAHEAD]))
        o_e = attend(c, e, *ahead[j])
        ahead[j] = None
        if e == 0:
            out = o_e
        else:
            o_ref[c * tq:(c + 1) * tq, :] = jnp.where(low_half, out, o_e).astype(BF16)


def _attn_a(q, k, v, batch, seq, tq):
    pairs = H_A // 2
    return pl.pallas_call(
        functools.partial(_attn_a_kernel, tq=tq),
        grid=(batch, pairs),
        in_specs=[pl.BlockSpec((seq, 2 * SLAB), lambda b, j: (b, j)),
                  pl.BlockSpec((seq, 2 * SLAB), lambda b, j: (b, j)),
                  pl.BlockSpec((seq, 2 * SLAB), lambda b, j: (b, j))],
        out_specs=pl.BlockSpec((seq, SLAB), lambda b, j: (b, j)),
        out_shape=jax.ShapeDtypeStruct((batch * seq, H_A * D_V), BF16),
        compiler_params=pltpu.CompilerParams(
            dimension_semantics=("arbitrary", "arbitrary"),
            vmem_limit_bytes=VMEM_LIMIT),
        name="mla_prompt_attn",
    )(q, k, v)


def _qabs_kernel(q_ref, gk_ref, wukt_ref, o_ref):
    for h in range(H_A):
        qs = (q_ref[:, SLAB * h:SLAB * (h + 1)].astype(F32) * gk_ref[...]).astype(BF16)
        o_ref[:, D_C * h:D_C * (h + 1)] = _dot(qs, wukt_ref[h]).astype(BF16)


def _qabs(q, gk_slab, wukt):
    n = q.shape[0]
    return pl.pallas_call(
        _qabs_kernel,
        out_shape=jax.ShapeDtypeStruct((n, H_A * D_C), BF16),
        compiler_params=pltpu.CompilerParams(vmem_limit_bytes=VMEM_LIMIT),
        name="mla_absorb_q",
    )(q, gk_slab, wukt)


def _paged_kernel(pt_ref, cache_ref, wukt_ref, qabs_ref, qpe_ref, new_ref, o_ref, lhs_sc, pg_sc, sem, *, n_pages, t_new):
    seq = pl.program_id(0)
    group = pg_sc.shape[1]
    n_groups = n_pages // group
    rows_q = qabs_ref.shape[0]

    def page_copy(sq, g, u):
        slot = g % 2
        return pltpu.make_async_copy(cache_ref.at[0, pt_ref[sq, g * group + u]], pg_sc.at[slot, u], sem.at[slot])

    def start_group(sq, g):
        for u in range(group):
            page_copy(sq, g, u).start()

    def wait_group(sq, g):
        for u in range(group):
            page_copy(sq, g, u).wait()

    @pl.when(seq == 0)
    def _():
        start_group(seq, 0)

    lhs_sc[:H_A * D_NOPE, :] = wukt_ref[...]
    lhs_sc[H_A * D_NOPE:, :] = qabs_ref[...]

    def nope_scores(ct):
        keys = ct.shape[1]
        big = _dot(lhs_sc[...], ct)
        kt = big[:H_A * D_NOPE]
        ssq = jnp.sum((kt * kt).reshape(D_NOPE, H_A, keys), axis=0)
        rs = lax.rsqrt(ssq * (1.0 / D_NOPE) + EPS)
        rs_q = jnp.concatenate([rs] * (rows_q // H_A), axis=0)
        return big[H_A * D_NOPE:] * rs_q

    def rope_scores(kpets):
        return _dot(qpe_ref[...], jnp.concatenate(kpets, axis=1))

    def accumulate(state, s, ct):
        m_old, l, acc = state
        m_new = jnp.maximum(m_old, jnp.max(s, axis=-1, keepdims=True))
        corr = jnp.exp2(m_old - m_new)
        p = jnp.exp2(s - m_new)
        return (m_new, l * corr + jnp.sum(p, axis=-1, keepdims=True),
                acc * corr + _dot_nt(p.astype(BF16), ct))

    state = (jnp.full((rows_q, 1), NEG, F32), jnp.zeros((rows_q, 1), F32), jnp.zeros((rows_q, D_C), F32))
    pending = None
    for g in range(n_groups):
        wait_group(seq, g)
        if g + 1 < n_groups:
            start_group(seq, g + 1)
        else:
            @pl.when(seq + 1 < pl.num_programs(0))
            def _():
                start_group(seq + 1, 0)
        slot = g % 2
        cts, kpets = [], []
        for u in range(0, group, PAGES_PER_DOT):
            pages = range(u, u + PAGES_PER_DOT)
            cts.append(jnp.concatenate([pg_sc[slot, v, :D_C, :] for v in pages], axis=1).astype(BF16))
            kpets.append(jnp.concatenate([pg_sc[slot, v, D_C:, :] for v in pages], axis=1).astype(BF16))
        nope = []
        for i, ct in enumerate(cts):
            if i == ACC_AFTER_DOTS and pending is not None:
                state = accumulate(state, *pending)
            nope.append(nope_scores(ct))
        pending = (jnp.concatenate(nope, axis=1) + rope_scores(kpets), jnp.concatenate(cts, axis=1))
    state = accumulate(state, *pending)

    keys = new_ref.shape[1]
    t_row = lax.shift_right_logical(lax.broadcasted_iota(jnp.int32, (rows_q, keys), 0), H_A.bit_length() - 1)
    s_col = lax.broadcasted_iota(jnp.int32, (rows_q, keys), 1)
    ct = new_ref[:D_C, :].astype(BF16)
    s = nope_scores(ct) + rope_scores([new_ref[D_C:, :].astype(BF16)])
    _, l, acc = accumulate(state, jnp.where((s_col <= t_row) & (s_col < t_new), s, NEG), ct)
    o_ref[...] = acc / l


def _paged_attn(page_table, cache, wukt, qabs, qpe, new_pad, t_new):
    bd, n_pages = page_table.shape
    rows_q = qabs.shape[1]
    assert (n_pages // PAGES_PER_GROUP) % 2 == 0, "the slot of a page group must not depend on the sequence"
    grid_spec = pltpu.PrefetchScalarGridSpec(
        num_scalar_prefetch=1,
        grid=(bd,),
        in_specs=[pl.BlockSpec(memory_space=pl.ANY),
                  pl.BlockSpec(wukt.shape, lambda b, pt: (0, 0)),
                  pl.BlockSpec((None, rows_q, D_C), lambda b, pt: (b, 0, 0)),
                  pl.BlockSpec((None, rows_q, D_ROPE), lambda b, pt: (b, 0, 0)),
                  pl.BlockSpec((None,) + new_pad.shape[1:], lambda b, pt: (b, 0, 0))],
        out_specs=pl.BlockSpec((None, rows_q, D_C), lambda b, pt: (b, 0, 0)),
        scratch_shapes=[pltpu.VMEM((H_A * D_NOPE + rows_q, D_C), BF16),
                        pltpu.VMEM((2, PAGES_PER_GROUP, D_CKV, PAGE_SIZE), F32),
                        pltpu.SemaphoreType.DMA((2,))])
    return pl.pallas_call(
        functools.partial(_paged_kernel, n_pages=n_pages, t_new=t_new),
        grid_spec=grid_spec,
        out_shape=jax.ShapeDtypeStruct((bd, rows_q, D_C), F32),
        compiler_params=pltpu.CompilerParams(dimension_semantics=("arbitrary",),
                                             vmem_limit_bytes=VMEM_LIMIT),
        name="mla_paged_attn",
    )(page_table, cache, wukt, qabs, qpe, new_pad)


def _latent_out_kernel(olat_ref, w_uv_ref, o_ref):
    for j in range(H_A // 2):
        acc = None
        for e in range(2):
            h = 2 * j + e
            part = _dot(olat_ref[:, D_C * h:D_C * (h + 1)].astype(BF16),
                        w_uv_ref[:, SLAB * h:SLAB * (h + 1)])
            acc = part if acc is None else acc + part
        o_ref[:, SLAB * j:SLAB * (j + 1)] = acc.astype(BF16)


def _latent_out(olat, w_uv_pad):
    n = olat.shape[0]
    return pl.pallas_call(
        _latent_out_kernel,
        out_shape=jax.ShapeDtypeStruct((n, H_A * D_V), BF16),
        compiler_params=pltpu.CompilerParams(vmem_limit_bytes=VMEM_LIMIT),
        name="mla_latent_out",
    )(olat, w_uv_pad)


def _post_kernel(x_ref, o_ref, w_o_ref, g_ref, w_in_ref, w_out_ref, y_ref):
    h1 = x_ref[...] + _dot(o_ref[...], w_o_ref[...])
    hn = _rms(h1, g_ref[...]).astype(BF16)
    acc = h1
    for c in range(D_FF // FF_CHUNK):
        lo = c * FF_CHUNK
        a1 = _dot(hn, w_in_ref[:, lo:lo + FF_CHUNK])
        a2 = _dot(hn, w_in_ref[:, D_FF + lo:D_FF + lo + FF_CHUNK])
        gate = (a1 * jax.nn.sigmoid(a1)) * a2
        acc = acc + _dot(gate.astype(BF16), w_out_ref[lo:lo + FF_CHUNK, :])
    y_ref[...] = acc


def _layer_spec(stacked, layer):
    zeros = (0,) * (stacked.ndim - 1)
    return pl.BlockSpec((None,) + stacked.shape[1:], lambda *_: (layer,) + zeros, pipeline_mode=pl.Buffered(1))


def _post(x, o, w_o, g, w_in, w_out, layer, tm):
    n = x.shape[0]
    return pl.pallas_call(
        _post_kernel,
        grid=(n // tm,),
        in_specs=[pl.BlockSpec((tm, D_MODEL), lambda i: (i, 0)),
                  pl.BlockSpec((tm, o.shape[1]), lambda i: (i, 0)),
                  _const_spec(w_o.shape), _layer_spec(g, layer),
                  _layer_spec(w_in, layer), _layer_spec(w_out, layer)],
        out_specs=pl.BlockSpec((tm, D_MODEL), lambda i: (i, 0)),
        out_shape=jax.ShapeDtypeStruct((n, D_MODEL), F32),
        compiler_params=pltpu.CompilerParams(dimension_semantics=("arbitrary",),
                                             vmem_limit_bytes=VMEM_LIMIT),
        name="outproj_swiglu",
    )(x, o, w_o, g, w_in, w_out)


def _proj_b_kernel(h_ref, tab_ref, g_kv_ref, w_kv_ref, segk_ref, cntk_ref, expk_ref, gk_ref,
                   g_attn_ref, w_q_ref, segq_ref, cntq_ref, expq_ref, gq_ref,
                   k_ref, v_ref, q_ref, kt_ref, vt_ref):
    c, s1, s2 = tab_ref[0], tab_ref[1], tab_ref[2]
    h = h_ref[...]
    hr = h * lax.rsqrt(jnp.mean(h * h, axis=-1, keepdims=True) + EPS)
    kv = _dot((hr * g_kv_ref[...]).astype(BF16), w_kv_ref[...])
    q_raw = _dot((hr * g_attn_ref[...]).astype(BF16), w_q_ref[...])
    kw = N_KV_B * HD_B
    tail = slice(h.shape[0] - WINDOW, h.shape[0])
    k_raw = kv[:, :kw]
    v_ref[...] = kv[:, kw:]
    k_scale, q_scale = _segment_scales([(k_raw, segk_ref, cntk_ref, expk_ref),
                                        (q_raw, segq_ref, cntq_ref, expq_ref)])
    kn = k_raw * k_scale * gk_ref[...]
    for j in range(kw // LANES):
        sl = slice(LANES * j, LANES * (j + 1))
        k_slab = _rope_slab(kn[:, sl], c, s1, s2, ROT_B // 2)
        k_ref[:, sl] = k_slab
        kt_ref[sl, :] = k_slab[tail].T
        vt_ref[sl, :] = kv[tail, kw + LANES * j:kw + LANES * (j + 1)].T
    qn = q_raw * q_scale * gq_ref[...]
    for j in range(H_B * HD_B // LANES):
        sl = slice(LANES * j, LANES * (j + 1))
        q_ref[:, sl] = _rope_slab(qn[:, sl], c, s1, s2, ROT_B // 2).astype(BF16)


def _proj_b(h, tab, wb, tm):
    n = h.shape[0]
    n_tab = tab.shape[1] // tm
    weights = [wb["g_kv"], wb["w_kv"], wb["segk"], wb["cntk"], wb["expk"], wb["gk"],
               wb["g_attn"], wb["w_q"], wb["segq"], wb["cntq"], wb["expq"], wb["gq"]]
    kw = N_KV_B * HD_B
    return pl.pallas_call(
        _proj_b_kernel,
        grid=(n // tm,),
        in_specs=[pl.BlockSpec((tm, D_MODEL), lambda i: (i, 0)),
                  pl.BlockSpec((3, tm, LANES), lambda i: (0, i % n_tab, 0))]
                 + [_const_spec(w.shape) for w in weights],
        out_specs=[pl.BlockSpec((tm, kw), lambda i: (i, 0)),
                   pl.BlockSpec((tm, kw), lambda i: (i, 0)),
                   pl.BlockSpec((tm, H_B * HD_B), lambda i: (i, 0)),
                   pl.BlockSpec((None, kw, WINDOW), lambda i: (i // n_tab, 0, 0)),
                   pl.BlockSpec((None, kw, WINDOW), lambda i: (i // n_tab, 0, 0))],
        out_shape=[jax.ShapeDtypeStruct((n, kw), F32),
                   jax.ShapeDtypeStruct((n, kw), F32),
                   jax.ShapeDtypeStruct((n, H_B * HD_B), BF16),
                   jax.ShapeDtypeStruct((n // (n_tab * tm), kw, WINDOW), F32),
                   jax.ShapeDtypeStruct((n // (n_tab * tm), kw, WINDOW), F32)],
        compiler_params=pltpu.CompilerParams(dimension_semantics=("arbitrary",),
                                             vmem_limit_bytes=VMEM_LIMIT),
        name="swa_proj",
    )(h, tab, *weights)


def _swa_halves(slab, kv, ones_lane=False):
    lane = lax.broadcasted_iota(jnp.int32, slab.shape, 1)
    own = (lane >= HD_B) if kv % 2 else (lane < HD_B)
    halves = [None, None]
    halves[kv % 2] = jnp.where(own, slab, 0.0)
    halves[1 - kv % 2] = pltpu.roll(halves[kv % 2], HD_B, 1)
    if ones_lane:
        halves = [jnp.where(lane == _ones_lane(par), 1.0, h) for par, h in enumerate(halves)]
    return [h.astype(BF16) for h in halves]


def _swa_attend(jobs, valid_of, sink_of):
    scores = [[_dot_nt(q, kh[par]) for par in range(2)] for q, kh, _, _ in jobs]
    probs = []
    for j, (q, _, _, kv) in enumerate(jobs):
        m2 = q.shape[0]
        top = lax.broadcasted_iota(jnp.int32, (m2, 1), 0) < (m2 // 2)
        row = []
        for par in range(2):
            s = jnp.where(valid_of(j), scores[j][par], NEG)
            sink = jnp.where(top, sink_of(G_B * kv + par), sink_of(G_B * kv + par + 2)) * LOG2E
            m = jnp.maximum(jnp.max(s, axis=-1, keepdims=True), sink)
            row.append((jnp.exp2(s - m).astype(BF16), jnp.exp2(sink - m)))
        probs.append(row)
    outs = []
    for j, (q, _, vh, _) in enumerate(jobs):
        o = []
        for par in range(2):
            pv = _dot(probs[j][par][0], vh[par])
            ones = _ones_lane(par)
            o.append(pv / (pv[:, ones:ones + 1] + probs[j][par][1]))
        low_half = lax.broadcasted_iota(jnp.int32, o[0].shape, 1) < HD_B
        outs.append(jnp.where(low_half, o[0], o[1]))
    return outs


def _attn_b_kernel(sink_ref, q_ref, kp_ref, kc_ref, vp_ref, vc_ref, o_ref):
    g = pl.program_id(1)
    kcat = jnp.concatenate([kp_ref[...], kc_ref[...]], axis=0)
    vcat = jnp.concatenate([vp_ref[...], vc_ref[...]], axis=0)
    shape = (2 * WINDOW, 2 * WINDOW)
    qi = lax.broadcasted_iota(jnp.int32, shape, 0) & (WINDOW - 1)
    col = lax.broadcasted_iota(jnp.int32, shape, 1)
    band = (col > qi) & (col <= qi + WINDOW)
    band_first = band & ((col >= WINDOW) | (g > 0))
    n_blocks = q_ref.shape[0] // WINDOW
    for kv in range(N_KV_B):
        base = G_B * HD_B * kv
        ks = slice(LANES * (kv // 2), LANES * (kv // 2 + 1))
        k_half = _swa_halves(kcat[:, ks], kv)
        v_half = _swa_halves(vcat[:, ks], kv, ones_lane=True)
        jobs = []
        for r in range(n_blocks):
            rows = slice(WINDOW * r, WINDOW * (r + 1))
            win = slice(WINDOW * r, WINDOW * (r + 2))
            q_lhs = jnp.concatenate([q_ref[rows, base:base + LANES], q_ref[rows, base + LANES:base + 2 * LANES]], axis=0)
            jobs.append((q_lhs, [h[win] for h in k_half], [h[win] for h in v_half], kv))
        outs = _swa_attend(jobs, lambda r: band if r else band_first, lambda hh: sink_ref[hh])
        for r, o in enumerate(outs):
            rows = slice(WINDOW * r, WINDOW * (r + 1))
            o_ref[rows, base:base + LANES] = o[:WINDOW].astype(BF16)
            o_ref[rows, base + LANES:base + 2 * LANES] = o[WINDOW:].astype(BF16)


def _attn_b(sinks, q, k, v, batch, seq):
    nb = seq // WINDOW
    ng = nb // SWA_BLOCKS
    kw = N_KV_B * HD_B
    prev = lambda b, g: (b * nb + jnp.maximum(SWA_BLOCKS * g - 1, 0), 0)
    cur = lambda b, g: (b * ng + g, 0)
    return pl.pallas_call(
        _attn_b_kernel,
        grid=(batch, ng),
        in_specs=[pl.BlockSpec(memory_space=pltpu.SMEM),
                  pl.BlockSpec((SWA_BLOCKS * WINDOW, H_B * HD_B), cur),
                  pl.BlockSpec((WINDOW, kw), prev), pl.BlockSpec((SWA_BLOCKS * WINDOW, kw), cur),
                  pl.BlockSpec((WINDOW, kw), prev), pl.BlockSpec((SWA_BLOCKS * WINDOW, kw), cur)],
        out_specs=pl.BlockSpec((SWA_BLOCKS * WINDOW, H_B * HD_B), cur),
        out_shape=jax.ShapeDtypeStruct((batch * seq, H_B * HD_B), BF16),
        compiler_params=pltpu.CompilerParams(dimension_semantics=("arbitrary", "arbitrary"),
                                             vmem_limit_bytes=VMEM_LIMIT),
        name="swa_prompt_attn",
    )(sinks, q, k, k, v, v)


def _attn_b_sample_kernel(sink_ref, q_ref, k_ref, v_ref, o_ref, *, t, w_buf):
    keys = k_ref.shape[1]
    rows = 2 * SWA_Q_ROWS
    ti = lax.broadcasted_iota(jnp.int32, (rows, keys), 0) & (SWA_Q_ROWS - 1)
    col = lax.broadcasted_iota(jnp.int32, (rows, keys), 1)
    diff = jnp.where(col < w_buf, ti + w_buf - col, ti - (col - w_buf))
    valid = (diff >= 0) & (diff < WINDOW) & (col < w_buf + t) & (ti < t)
    jobs = []
    for b in range(q_ref.shape[0]):
        for kv in range(N_KV_B):
            ks = slice(LANES * (kv // 2), LANES * (kv // 2 + 1))
            jobs.append((q_ref[b, kv], _swa_halves(k_ref[b, :, ks], kv),
                         _swa_halves(v_ref[b, :, ks], kv, ones_lane=True), kv))
    outs = _swa_attend(jobs, lambda j: valid, lambda hh: sink_ref[hh])
    for j, o in enumerate(outs):
        o_ref[j // N_KV_B, j % N_KV_B] = o


def _attn_b_sample(sinks, q, k_all, v_all, t, w_buf):
    bd = q.shape[0]
    bs = SWA_SAMPLE_SEQS
    blk = lambda a: pl.BlockSpec((bs,) + a.shape[1:], lambda b: (b,) + (0,) * (a.ndim - 1))
    return pl.pallas_call(
        functools.partial(_attn_b_sample_kernel, t=t, w_buf=w_buf),
        grid=(bd // bs,),
        in_specs=[pl.BlockSpec(memory_space=pltpu.SMEM), blk(q), blk(k_all), blk(v_all)],
        out_specs=blk(q),
        out_shape=jax.ShapeDtypeStruct(q.shape, F32),
        compiler_params=pltpu.CompilerParams(dimension_semantics=("arbitrary",),
                                             vmem_limit_bytes=VMEM_LIMIT),
        name="swa_sample_attn",
    )(sinks, q, k_all, v_all)


def _rope_tables(pos, n_rot, period, lane_lo):
    half = n_rot // 2
    inv = ROPE_THETA ** (-np.arange(0, n_rot, 2, dtype=np.float64) / n_rot)
    ang = np.asarray(pos, np.float64)[:, None] * inv[None, :]
    cos, sin = np.cos(ang), np.sin(ang)
    rel = np.arange(LANES) % period - lane_lo
    in1 = (rel >= 0) & (rel < half)
    in2 = (rel >= half) & (rel < 2 * half)
    idx = np.where(in1, rel, np.where(in2, rel - half, 0))
    cg, sg = cos[:, idx], sin[:, idx]
    tables = np.stack([np.where(in1 | in2, cg, 1.0), np.where(in1, -sg, 0.0), np.where(in2, sg, 0.0)])
    return jnp.asarray(tables, F32)


def _rope_tables_dup(pos, n_rot, lane_lo):
    half = n_rot // 2
    inv = ROPE_THETA ** (-np.arange(0, n_rot, 2, dtype=np.float64) / n_rot)
    ang = np.asarray(pos, np.float64)[:, None] * inv[None, :]
    cos, sin = np.cos(ang), np.sin(ang)
    rel = np.arange(LANES) - lane_lo
    in1 = (rel >= 0) & (rel < half)
    in2 = (rel >= half) & (rel < 2 * half)
    idx = np.where(in1, rel, np.where(in2, rel - half, 0))
    cg, sg = cos[:, idx], sin[:, idx]
    tables = np.stack([np.where(in1 | in2, cg, np.where(rel < 0, 1.0, 0.0)),
                       np.where(in1, -sg, np.where(in2, sg, 0.0))])
    return jnp.asarray(tables, F32)


def _segments(width, seg_lanes, extra=None):
    seg = np.zeros((width, LANES), np.float32)
    spread = np.zeros((LANES, width), np.float32)
    cnt = np.zeros((1, LANES), np.float32)
    for s, (lo, hi) in enumerate(seg_lanes):
        seg[lo:hi, s] = 1.0
        spread[s, lo:hi + (extra[s] if extra else 0)] = 1.0
        cnt[0, s] = 1.0 / (hi - lo)
    expand = np.concatenate([spread, spread], axis=0)
    return jnp.asarray(seg, BF16), jnp.asarray(cnt, F32), jnp.asarray(expand, BF16)


def _slab_gain(parts):
    pieces, pos = [], 0
    for lo, vals in parts:
        pieces += [jnp.zeros((lo - pos,), F32), vals.astype(F32)]
        pos = lo + vals.shape[0]
    pieces.append(jnp.zeros((SLAB - pos,), F32))
    return jnp.concatenate(pieces)[None, :]


def _prep_a(norm_attn, w_a_in, g_qc, w_uq, g_ckv, w_uk, w_uv, g_qn, g_qr, g_kn, g_kr):
    half = D_ROPE // 2
    tail = SLAB - KPE_LANE - D_ROPE - half
    w_kpe = w_a_in[:, D_QC + D_C:]
    w_in = jnp.concatenate([w_a_in[:, :D_QC + D_C], jnp.zeros((D_MODEL, KPE_LANE), F32),
                            w_kpe, w_kpe[:, :half], jnp.zeros((D_MODEL, tail), F32)], axis=1)
    dqk = D_NOPE + D_ROPE
    w_uq3 = w_uq.reshape(D_QC, H_A, dqk)
    w_uq_pad = jnp.concatenate([w_uq3, w_uq3[:, :, D_NOPE:D_NOPE + half], jnp.zeros((D_QC, H_A, tail), F32)],
                               axis=2).reshape(D_QC, H_A * SLAB)
    w_uk3 = w_uk.reshape(D_C, H_A, D_NOPE)
    w_uk_pad = jnp.pad(w_uk3, ((0, 0), (0, 0), (0, SLAB - D_NOPE))).reshape(D_C, H_A * SLAB)
    w_uv3 = w_uv.reshape(D_C, H_A // 2, 2, D_V)
    even = jnp.pad(w_uv3[:, :, 0], ((0, 0), (0, 0), (0, SLAB - D_V)))
    odd = jnp.pad(w_uv3[:, :, 1], ((0, 0), (0, 0), (SLAB - D_V, 0)))
    w_uv_pad = jnp.stack([even, odd], axis=2).reshape(D_C, H_A * SLAB)
    v_ones = np.zeros((1, H_A * SLAB), np.float32)
    for h in range(H_A):
        v_ones[0, SLAB * h + _ones_lane(h % 2)] = 1.0
    q_segs = []
    for h in range(H_A):
        q_segs += [(SLAB * h, SLAB * h + D_NOPE), (SLAB * h + KPE_LANE, SLAB * h + KPE_LANE + D_ROPE)]
    segq, cntq, expq = _segments(H_A * SLAB, q_segs, extra=[0, half] * H_A)
    q_scale = SCALE_A * LOG2E
    gq = jnp.tile(_slab_gain([(0, g_qn * q_scale), (KPE_LANE, g_qr * q_scale),
                              (KPE_LANE + D_ROPE, g_qr[:half] * q_scale)]), (1, H_A))
    gk_slab = _slab_gain([(0, g_kn)])
    pe_mask = np.zeros((1, SLAB), np.float32)
    pe_mask[0, KPE_LANE:KPE_LANE + D_ROPE] = 1.0
    wukt = jnp.pad(jnp.transpose(w_uk3, (1, 2, 0)), ((0, 0), (0, SLAB - D_NOPE), (0, 0)))
    return dict(
        g_attn=norm_attn[None, :], w_in=w_in.astype(BF16), g_qc=g_qc[None, :], g_ckv=g_ckv[None, :],
        g_kpe=_slab_gain([(KPE_LANE, g_kr), (KPE_LANE + D_ROPE, g_kr[:half])]), pe_mask=jnp.asarray(pe_mask),
        w_uq=w_uq_pad.astype(BF16), segq=segq, cntq=cntq, expq=expq, gq=gq,
        w_uk=w_uk_pad.astype(BF16),
        w_uv=w_uv_pad.astype(BF16), v_ones=jnp.asarray(v_ones), gk_slab=gk_slab, wukt_pad=wukt.astype(BF16),
        wukt=jnp.transpose(w_uk3, (2, 1, 0)).reshape(H_A * D_NOPE, D_C).astype(BF16))


def _prep_b(g_kv, w_kv, g_k, norm_attn, w_q, g_q):
    kw = N_KV_B * HD_B
    segk, cntk, expk = _segments(kw, [(HD_B * h, HD_B * (h + 1)) for h in range(N_KV_B)])
    segq, cntq, expq = _segments(H_B * HD_B, [(HD_B * h, HD_B * (h + 1)) for h in range(H_B)])
    return dict(g_kv=g_kv[None, :], w_kv=w_kv.astype(BF16), segk=segk, cntk=cntk, expk=expk,
                gk=jnp.tile(g_k, N_KV_B)[None, :], g_attn=norm_attn[None, :], w_q=w_q.astype(BF16),
                segq=segq, cntq=cntq, expq=expq, gq=jnp.tile(g_q * (SCALE_B * LOG2E), H_B)[None, :])


def kernel(x_prompt, x_sample, cache_mla, state_win_k, state_win_v, page_table, norm_attn, norm_ffn, w_a_in, g_qc, w_uq, g_ckv, w_uk, w_uv, g_qn_a, g_qr_a, g_kn_a, g_kr_a, w_a_out, g_kv_shared, w_kv_shared, g_k_b, w_q_b, g_q_b, sinks, w_b_out, w_ffn_in, w_ffn_out):
    batch, seq, _ = x_prompt.shape
    bd, t_dec, _ = x_sample.shape
    past_len = page_table.shape[1] * PAGE_SIZE
    w_buf = state_win_k.shape[1]
    kw = N_KV_B * HD_B
    assert w_a_in.shape[0] == 1 and w_q_b.shape[0] == 1, "one MLA layer followed by one SWA layer"
    assert w_buf == WINDOW and seq % TM_POST == 0 and seq % TM_SWA_PROJ == 0 and (bd * t_dec) % 8 == 0
    assert seq >= WINDOW and bd * t_dec >= WINDOW, "the window tail is taken from the last 128 rows of a tile"
    assert t_dec <= SWA_Q_ROWS and bd % SWA_SAMPLE_SEQS == 0 and page_table.shape[1] % (2 * PAGES_PER_GROUP) == 0

    wa = _prep_a(norm_attn[0], w_a_in[0], g_qc[0], w_uq[0], g_ckv[0], w_uk[0], w_uv[0],
                 g_qn_a[0], g_qr_a[0], g_kn_a[0], g_kr_a[0])
    wb = _prep_b(g_kv_shared, w_kv_shared, g_k_b, norm_attn[1], w_q_b[0], g_q_b[0])
    w_a_out_b = w_a_out[0].astype(BF16)
    w_b_out_b = w_b_out[0].astype(BF16)
    ffn_in = w_ffn_in.astype(BF16)
    ffn_out = w_ffn_out.astype(BF16)
    g_ffn = norm_ffn[:, None, :]
    sink_b = sinks[0]

    pos_p = np.arange(seq)
    n_s = bd * t_dec
    pos_s = past_len + np.arange(n_s) % t_dec

    xp = x_prompt.reshape(batch * seq, D_MODEL)
    q, k, v, rows_p = _proj_a(xp, _rope_tables_dup(pos_p, D_ROPE, KPE_LANE), wa, TM_MLA_PROJ)
    o = _attn_a(q, k, v, batch, seq, TQ_MLA)
    h = _post(xp, o, w_a_out_b, g_ffn, ffn_in, ffn_out, 0, TM_POST)
    k_p, v_p, q_b, kt_p, vt_p = _proj_b(h, _rope_tables(pos_p, ROT_B, HD_B, 0), wb, TM_SWA_PROJ)
    o = _attn_b(sink_b, q_b, k_p, v_p, batch, seq)
    y_prompt = _post(h, o, w_b_out_b, g_ffn, ffn_in, ffn_out, 1, TM_POST)

    xs = x_sample.reshape(n_s, D_MODEL)
    q, _, _, rows_s = _proj_a(xs, _rope_tables_dup(pos_s, D_ROPE, KPE_LANE), wa, n_s)
    qabs = _qabs(q, wa["gk_slab"], wa["wukt_pad"]).reshape(bd, t_dec * H_A, D_C)
    qpe = q.reshape(n_s, H_A, SLAB)[:, :, KPE_LANE:KPE_LANE + D_ROPE].reshape(bd, t_dec * H_A, D_ROPE)
    new_pad = jnp.pad(jnp.swapaxes(rows_s.reshape(bd, t_dec, D_CKV), 1, 2), ((0, 0), (0, 0), (0, PAGE_SIZE - t_dec)))
    olat = _paged_attn(page_table, jnp.swapaxes(cache_mla, 2, 3), wa["wukt"], qabs, qpe, new_pad, t_dec)
    o = _latent_out(olat.reshape(n_s, H_A * D_C), wa["w_uv"])
    h = _post(xs, o, w_a_out_b, g_ffn, ffn_in, ffn_out, 0, n_s)
    k_s, v_s, q_b, _, _ = _proj_b(h, _rope_tables(pos_s, ROT_B, HD_B, 0), wb, n_s)
    q4 = jnp.transpose(q_b.reshape(bd, t_dec, N_KV_B, 2, LANES), (0, 2, 3, 1, 4))
    q4 = jnp.pad(q4, ((0, 0), (0, 0), (0, 0), (0, SWA_Q_ROWS - t_dec), (0, 0))).reshape(bd, N_KV_B, 2 * SWA_Q_ROWS, LANES)
    key_pad = jnp.zeros((bd, 2 * SWA_Q_ROWS - t_dec, kw), F32)
    k_all = jnp.concatenate([state_win_k.reshape(bd, w_buf, kw), k_s.reshape(bd, t_dec, kw), key_pad], axis=1)
    v_all = jnp.concatenate([state_win_v.reshape(bd, w_buf, kw), v_s.reshape(bd, t_dec, kw), key_pad], axis=1)
    o4 = _attn_b_sample(sink_b, q4, k_all, v_all, t_dec, w_buf)
    o = jnp.transpose(o4.reshape(bd, N_KV_B, 2, SWA_Q_ROWS, LANES)[:, :, :, :t_dec], (0, 3, 1, 2, 4))
    y_sample = _post(h, o.reshape(n_s, H_B * HD_B).astype(BF16), w_b_out_b, g_ffn, ffn_in, ffn_out, 1, n_s)

    win_k_p = jnp.transpose(kt_p.reshape(batch, N_KV_B, HD_B, WINDOW), (0, 3, 1, 2))
    win_v_p = jnp.transpose(vt_p.reshape(batch, N_KV_B, HD_B, WINDOW), (0, 3, 1, 2))
    win_k_s = jnp.concatenate([state_win_k, k_s.reshape(bd, t_dec, N_KV_B, HD_B)], axis=1)[:, -w_buf:]
    win_v_s = jnp.concatenate([state_win_v, v_s.reshape(bd, t_dec, N_KV_B, HD_B)], axis=1)[:, -w_buf:]
    return (y_prompt.reshape(batch, seq, D_MODEL), y_sample.reshape(bd, t_dec, D_MODEL),
            rows_p.reshape(1, batch, seq, D_CKV), rows_s.reshape(1, bd, t_dec, D_CKV),
            win_k_p, win_v_p, win_k_s, win_v_s)
```

```python
import functools

import numpy as np
import jax
import jax.numpy as jnp
from jax import lax
from jax.experimental import pallas as pl
from jax.experimental.pallas import tpu as pltpu

F32 = jnp.float32
BF16 = jnp.bfloat16

D_MODEL = 1024
PAGE_SIZE = 128
H_A = 16
D_NOPE = 64
D_ROPE = 32
D_V = 64
D_QC = 384
D_C = 256
D_CKV = D_C + D_ROPE
SCALE_A = (D_NOPE + D_ROPE) ** -0.5
H_B = 16
N_KV_B = 4
HD_B = 64
G_B = H_B // N_KV_B
WINDOW = 128
ROT_B = HD_B // 4
SCALE_B = HD_B ** -0.5
D_FF = 2816
ROPE_THETA = 500000.0
EPS = 1e-6
NEG = -1e30
LOG2E = 1.4426950408889634

LANES = 128
MXU_DIM = 256
VMEM_LIMIT = 56 * 1024 * 1024
SLAB = 128
KPE_LANE = 64

FF_CHUNK = MXU_DIM
TM_MLA_PROJ = 512
PROJ_SUB = 256
TM_SWA_PROJ = 1024
SWA_PROJ_SUB = 512
TM_POST = 512
TQ_MLA = MXU_DIM
ATTN_LOOKAHEAD = 2
PAGES_PER_GROUP = 16
PAGES_PER_DOT = MXU_DIM // PAGE_SIZE
ACC_AFTER_DOTS = 4
SWA_SAMPLE_SEQS = 8
SWA_BLOCKS = 4
SWA_Q_ROWS = 8

_NT = (((1,), (1,)), ((), ()))


def _dot(a, b):
    return jnp.dot(a, b, preferred_element_type=F32)


def _dot_nt(a, b):
    return lax.dot_general(a, b, _NT, preferred_element_type=F32)


def _rms(x, g):
    ms = jnp.mean(x * x, axis=-1, keepdims=True)
    return x * lax.rsqrt(ms + EPS) * g


def _rope_slab(x, c, s1, s2, half):
    return x * c + pltpu.roll(x, LANES - half, 1) * s1 + pltpu.roll(x, half, 1) * s2


def _rope_slab_dup(x, c, s, half):
    return x * c + pltpu.roll(x, LANES - half, 1) * s


def _segment_scales(items):
    sums = [_dot((raw * raw).astype(BF16), seg_ref[...]) for raw, seg_ref, _, _ in items]
    scales = []
    for ss, (_, _, inv_cnt_ref, expand_ref) in zip(sums, items):
        rs = lax.rsqrt(ss * inv_cnt_ref[...] + EPS)
        hi = rs.astype(BF16)
        lo = (rs - hi.astype(F32)).astype(BF16)
        scales.append(_dot(jnp.concatenate([hi, lo], axis=1), expand_ref[...]))
    return scales


def _proj_a_kernel(x_ref, tab_ref, g_attn_ref, w_in_ref, g_qc_ref, g_ckv_ref, g_kpe_ref, pe_mask_ref,
                   w_uq_ref, segq_ref, cntq_ref, expq_ref, gq_ref,
                   w_uk_ref, gk_ref, w_uv_ref, v_ones_ref,
                   q_ref, k_ref, v_ref, rows_ref):
    tm = x_ref.shape[0]
    blocks = [slice(r, r + min(PROJ_SUB, tm)) for r in range(0, tm, PROJ_SUB)]
    half = D_ROPE // 2

    def stage_in(rows):
        hn = _rms(x_ref[rows, :], g_attn_ref[...]).astype(BF16)
        return _dot(hn, w_in_ref[...])

    def stage_latents(rows, a):
        c, s = tab_ref[0, rows, :], tab_ref[1, rows, :]
        cq = _rms(a[:, :D_QC], g_qc_ref[...]).astype(BF16)
        ckv = _rms(a[:, D_QC:D_QC + D_C], g_ckv_ref[...])
        kpe = a[:, D_QC + D_C:]
        ms = jnp.sum(kpe * kpe * pe_mask_ref[...], axis=-1, keepdims=True) * (1.0 / D_ROPE)
        kpe = _rope_slab_dup(kpe * lax.rsqrt(ms + EPS) * g_kpe_ref[...], c, s, half)
        rows_ref[rows, :D_C] = ckv
        rows_ref[rows, D_C:] = kpe[:, KPE_LANE:KPE_LANE + D_ROPE]
        ckv_b = ckv.astype(BF16)
        q_raw = _dot(cq, w_uq_ref[...])
        k_raw = _dot(ckv_b, w_uk_ref[...])
        v_ref[rows, :] = (_dot(ckv_b, w_uv_ref[...]) + v_ones_ref[...]).astype(BF16)
        return q_raw, k_raw, kpe

    def stage_out(rows, q_raw, q_scale, k_raw, kpe):
        c, s = tab_ref[0, rows, :], tab_ref[1, rows, :]
        qn = q_raw * q_scale * gq_ref[...]
        for h in range(H_A):
            sl = slice(SLAB * h, SLAB * (h + 1))
            q_ref[rows, sl] = _rope_slab_dup(qn[:, sl], c, s, half).astype(BF16)
            k_h = k_raw[:, sl]
            ms_h = jnp.sum(k_h * k_h, axis=-1, keepdims=True) * (1.0 / D_NOPE)
            k_ref[rows, sl] = (k_h * lax.rsqrt(ms_h + EPS) * gk_ref[...] + kpe).astype(BF16)

    a_s = [stage_in(rows) for rows in blocks]
    mids = [stage_latents(rows, a) for rows, a in zip(blocks, a_s)]
    q_scales = _segment_scales([(q_raw, segq_ref, cntq_ref, expq_ref) for q_raw, _, _ in mids])
    for rows, (q_raw, k_raw, kpe), q_scale in zip(blocks, mids, q_scales):
        stage_out(rows, q_raw, q_scale, k_raw, kpe)


def _const_spec(shape):
    zeros = (0,) * len(shape)
    return pl.BlockSpec(shape, lambda *_: zeros, pipeline_mode=pl.Buffered(1))


def _proj_a(x, tab, wa, tm):
    n = x.shape[0]
    n_tab = tab.shape[1] // tm
    weights = [wa["g_attn"], wa["w_in"], wa["g_qc"], wa["g_ckv"], wa["g_kpe"], wa["pe_mask"],
               wa["w_uq"], wa["segq"], wa["cntq"], wa["expq"], wa["gq"],
               wa["w_uk"], wa["gk_slab"], wa["w_uv"], wa["v_ones"]]
    wide = H_A * SLAB
    return pl.pallas_call(
        _proj_a_kernel,
        grid=(n // tm,),
        in_specs=[pl.BlockSpec((tm, D_MODEL), lambda i: (i, 0)),
                  pl.BlockSpec((tab.shape[0], tm, LANES), lambda i: (0, i % n_tab, 0))]
                 + [_const_spec(w.shape) for w in weights],
        out_specs=[pl.BlockSpec((tm, wide), lambda i: (i, 0)),
                   pl.BlockSpec((tm, wide), lambda i: (i, 0)),
                   pl.BlockSpec((tm, wide), lambda i: (i, 0)),
                   pl.BlockSpec((tm, D_CKV), lambda i: (i, 0))],
        out_shape=[jax.ShapeDtypeStruct((n, wide), BF16),
                   jax.ShapeDtypeStruct((n, wide), BF16),
                   jax.ShapeDtypeStruct((n, wide), BF16),
                   jax.ShapeDtypeStruct((n, D_CKV), F32)],
        compiler_params=pltpu.CompilerParams(dimension_semantics=("arbitrary",),
                                             vmem_limit_bytes=VMEM_LIMIT),
        name="mla_proj",
    )(x, tab, *weights)


def _ones_lane(parity):
    return D_V if parity == 0 else 0


def _attn_a_kernel(q_ref, k_ref, v_ref, o_ref, *, tq):
    seq = q_ref.shape[0]
    causal = (lax.broadcasted_iota(jnp.int32, (tq, tq), 1) <= lax.broadcasted_iota(jnp.int32, (tq, tq), 0))
    low_half = lax.broadcasted_iota(jnp.int32, (tq, SLAB), 1) < D_V
    jobs = [(c, e) for c in range(seq // tq) for e in range(2)]

    def windows(c, e):
        return slice(c * tq, (c + 1) * tq), slice(0, c * tq), slice(SLAB * e, SLAB * (e + 1))

    def score(c, e):
        rows, past, ls = windows(c, e)
        q = q_ref[rows, ls]
        s_d = _dot_nt(q, k_ref[rows, ls])
        return s_d, (_dot_nt(q, k_ref[past, ls]) if c else None)

    def attend(c, e, s_d, s_p):
        rows, past, ls = windows(c, e)
        s_d = jnp.where(causal, s_d, NEG)
        m = jnp.max(s_d, axis=-1, keepdims=True)
        if c:
            m = jnp.maximum(m, jnp.max(s_p, axis=-1, keepdims=True))
        acc = _dot(jnp.exp2(s_d - m).astype(BF16), v_ref[rows, ls])
        if c:
            acc = acc + _dot(jnp.exp2(s_p - m).astype(BF16), v_ref[past, ls])
        ones = _ones_lane(e)
        return acc / acc[:, ones:ones + 1]

    ahead = [score(*jobs[j]) for j in range(min(ATTN_LOOKAHEAD, len(jobs)))]
    out = None
    for j, (c, e) in enumerate(jobs):
        if j + ATTN_LOOKAHEAD < len(jobs):
            ahead.append(score(*jobs[j + ATTN_LOOKAHEAD]))
        o_e = attend(c, e, *ahead[j])
        ahead[j] = None
        if e == 0:
            out = o_e
        else:
            o_ref[c * tq:(c + 1) * tq, :] = jnp.where(low_half, out, o_e).astype(BF16)


def _attn_a(q, k, v, batch, seq, tq):
    pairs = H_A // 2
    return pl.pallas_call(
        functools.partial(_attn_a_kernel, tq=tq),
        grid=(batch, pairs),
        in_specs=[pl.BlockSpec((seq, 2 * SLAB), lambda b, j: (b, j)),
                  pl.BlockSpec((seq, 2 * SLAB), lambda b, j: (b, j)),
                  pl.BlockSpec((seq, 2 * SLAB), lambda b, j: (b, j))],
        out_specs=pl.BlockSpec((seq, SLAB), lambda b, j: (b, j)),
        out_shape=jax.ShapeDtypeStruct((batch * seq, H_A * D_V), BF16),
        compiler_params=pltpu.CompilerParams(
            dimension_semantics=("arbitrary", "arbitrary"),
            vmem_limit_bytes=VMEM_LIMIT),
        name="mla_prompt_attn",
    )(q, k, v)


def _qabs_kernel(q_ref, gk_ref, wukt_ref, o_ref):
    for h in range(H_A):
        qs = (q_ref[:, SLAB * h:SLAB * (h + 1)].astype(F32) * gk_ref[...]).astype(BF16)
        o_ref[:, D_C * h:D_C * (h + 1)] = _dot(qs, wukt_ref[h]).astype(BF16)


def _qabs(q, gk_slab, wukt):
    n = q.shape[0]
    return pl.pallas_call(
        _qabs_kernel,
        out_shape=jax.ShapeDtypeStruct((n, H_A * D_C), BF16),
        compiler_params=pltpu.CompilerParams(vmem_limit_bytes=VMEM_LIMIT),
        name="mla_absorb_q",
    )(q, gk_slab, wukt)


def _paged_kernel(pt_ref, cache_ref, wukt_ref, qabs_ref, qpe_ref, new_ref, o_ref, lhs_sc, pg_sc, sem, *, n_pages, t_new):
    seq = pl.program_id(0)
    group = pg_sc.shape[1]
    n_groups = n_pages // group
    rows_q = qabs_ref.shape[0]

    def page_copy(sq, g, u):
        slot = g % 2
        return pltpu.make_async_copy(cache_ref.at[0, pt_ref[sq, g * group + u]], pg_sc.at[slot, u], sem.at[slot])

    def start_group(sq, g):
        for u in range(group):
            page_copy(sq, g, u).start()

    def wait_group(sq, g):
        for u in range(group):
            page_copy(sq, g, u).wait()

    @pl.when(seq == 0)
    def _():
        start_group(seq, 0)

    lhs_sc[:H_A * D_NOPE, :] = wukt_ref[...]
    lhs_sc[H_A * D_NOPE:, :] = qabs_ref[...]

    def nope_scores(ct):
        keys = ct.shape[1]
        big = _dot(lhs_sc[...], ct)
        kt = big[:H_A * D_NOPE]
        ssq = jnp.sum((kt * kt).reshape(D_NOPE, H_A, keys), axis=0)
        rs = lax.rsqrt(ssq * (1.0 / D_NOPE) + EPS)
        rs_q = jnp.concatenate([rs] * (rows_q // H_A), axis=0)
        return big[H_A * D_NOPE:] * rs_q

    def rope_scores(kpets):
        return _dot(qpe_ref[...], jnp.concatenate(kpets, axis=1))

    def accumulate(state, s, ct):
        m_old, l, acc = state
        m_new = jnp.maximum(m_old, jnp.max(s, axis=-1, keepdims=True))
        corr = jnp.exp2(m_old - m_new)
        p = jnp.exp2(s - m_new)
        return (m_new, l * corr + jnp.sum(p, axis=-1, keepdims=True),
                acc * corr + _dot_nt(p.astype(BF16), ct))

    state = (jnp.full((rows_q, 1), NEG, F32), jnp.zeros((rows_q, 1), F32), jnp.zeros((rows_q, D_C), F32))
    pending = None
    for g in range(n_groups):
        wait_group(seq, g)
        if g + 1 < n_groups:
            start_group(seq, g + 1)
        else:
            @pl.when(seq + 1 < pl.num_programs(0))
            def _():
                start_group(seq + 1, 0)
        slot = g % 2
        cts, kpets = [], []
        for u in range(0, group, PAGES_PER_DOT):
            pages = range(u, u + PAGES_PER_DOT)
            cts.append(jnp.concatenate([pg_sc[slot, v, :D_C, :] for v in pages], axis=1).astype(BF16))
            kpets.append(jnp.concatenate([pg_sc[slot, v, D_C:, :] for v in pages], axis=1).astype(BF16))
        nope = []
        for i, ct in enumerate(cts):
            if i == ACC_AFTER_DOTS and pending is not None:
                state = accumulate(state, *pending)
            nope.append(nope_scores(ct))
        pending = (jnp.concatenate(nope, axis=1) + rope_scores(kpets), jnp.concatenate(cts, axis=1))
    state = accumulate(state, *pending)

    keys = new_ref.shape[1]
    t_row = lax.shift_right_logical(lax.broadcasted_iota(jnp.int32, (rows_q, keys), 0), H_A.bit_length() - 1)
    s_col = lax.broadcasted_iota(jnp.int32, (rows_q, keys), 1)
    ct = new_ref[:D_C, :].astype(BF16)
    s = nope_scores(ct) + rope_scores([new_ref[D_C:, :].astype(BF16)])
    _, l, acc = accumulate(state, jnp.where((s_col <= t_row) & (s_col < t_new), s, NEG), ct)
    o_ref[...] = acc / l


def _paged_attn(page_table, cache, wukt, qabs, qpe, new_pad, t_new):
    bd, n_pages = page_table.shape
    rows_q = qabs.shape[1]
    assert (n_pages // PAGES_PER_GROUP) % 2 == 0, "the slot of a page group must not depend on the sequence"
    grid_spec = pltpu.PrefetchScalarGridSpec(
        num_scalar_prefetch=1,
        grid=(bd,),
        in_specs=[pl.BlockSpec(memory_space=pl.ANY),
                  pl.BlockSpec(wukt.shape, lambda b, pt: (0, 0)),
                  pl.BlockSpec((None, rows_q, D_C), lambda b, pt: (b, 0, 0)),
                  pl.BlockSpec((None, rows_q, D_ROPE), lambda b, pt: (b, 0, 0)),
                  pl.BlockSpec((None,) + new_pad.shape[1:], lambda b, pt: (b, 0, 0))],
        out_specs=pl.BlockSpec((None, rows_q, D_C), lambda b, pt: (b, 0, 0)),
        scratch_shapes=[pltpu.VMEM((H_A * D_NOPE + rows_q, D_C), BF16),
                        pltpu.VMEM((2, PAGES_PER_GROUP, D_CKV, PAGE_SIZE), F32),
                        pltpu.SemaphoreType.DMA((2,))])
    return pl.pallas_call(
        functools.partial(_paged_kernel, n_pages=n_pages, t_new=t_new),
        grid_spec=grid_spec,
        out_shape=jax.ShapeDtypeStruct((bd, rows_q, D_C), F32),
        compiler_params=pltpu.CompilerParams(dimension_semantics=("arbitrary",),
                                             vmem_limit_bytes=VMEM_LIMIT),
        name="mla_paged_attn",
    )(page_table, cache, wukt, qabs, qpe, new_pad)


def _latent_out_kernel(olat_ref, w_uv_ref, o_ref):
    for j in range(H_A // 2):
        acc = None
        for e in range(2):
            h = 2 * j + e
            part = _dot(olat_ref[:, D_C * h:D_C * (h + 1)].astype(BF16),
                        w_uv_ref[:, SLAB * h:SLAB * (h + 1)])
            acc = part if acc is None else acc + part
        o_ref[:, SLAB * j:SLAB * (j + 1)] = acc.astype(BF16)


def _latent_out(olat, w_uv_pad):
    n = olat.shape[0]
    return pl.pallas_call(
        _latent_out_kernel,
        out_shape=jax.ShapeDtypeStruct((n, H_A * D_V), BF16),
        compiler_params=pltpu.CompilerParams(vmem_limit_bytes=VMEM_LIMIT),
        name="mla_latent_out",
    )(olat, w_uv_pad)


def _post_kernel(x_ref, o_ref, w_o_ref, g_ref, w_in_ref, w_out_ref, y_ref):
    h1 = x_ref[...] + _dot(o_ref[...], w_o_ref[...])
    hn = _rms(h1, g_ref[...]).astype(BF16)
    acc = h1
    for c in range(D_FF // FF_CHUNK):
        lo = c * FF_CHUNK
        a1 = _dot(hn, w_in_ref[:, lo:lo + FF_CHUNK])
        a2 = _dot(hn, w_in_ref[:, D_FF + lo:D_FF + lo + FF_CHUNK])
        gate = (a1 * jax.nn.sigmoid(a1)) * a2
        acc = acc + _dot(gate.astype(BF16), w_out_ref[lo:lo + FF_CHUNK, :])
    y_ref[...] = acc


def _layer_spec(stacked, layer):
    zeros = (0,) * (stacked.ndim - 1)
    return pl.BlockSpec((None,) + stacked.shape[1:], lambda *_: (layer,) + zeros, pipeline_mode=pl.Buffered(1))


def _post(x, o, w_o, g, w_in, w_out, layer, tm):
    n = x.shape[0]
    return pl.pallas_call(
        _post_kernel,
        grid=(n // tm,),
        in_specs=[pl.BlockSpec((tm, D_MODEL), lambda i: (i, 0)),
                  pl.BlockSpec((tm, o.shape[1]), lambda i: (i, 0)),
                  _const_spec(w_o.shape), _layer_spec(g, layer),
                  _layer_spec(w_in, layer), _layer_spec(w_out, layer)],
        out_specs=pl.BlockSpec((tm, D_MODEL), lambda i: (i, 0)),
        out_shape=jax.ShapeDtypeStruct((n, D_MODEL), F32),
        compiler_params=pltpu.CompilerParams(dimension_semantics=("arbitrary",),
                                             vmem_limit_bytes=VMEM_LIMIT),
        name="outproj_swiglu",
    )(x, o, w_o, g, w_in, w_out)


def _proj_b_kernel(h_ref, tab_ref, g_kv_ref, w_kv_ref, segk_ref, cntk_ref, expk_ref, gk_ref,
                   g_attn_ref, w_q_ref, segq_ref, cntq_ref, expq_ref, gq_ref,
                   k_ref, v_ref, q_ref, kt_ref, vt_ref):
    tm = h_ref.shape[0]
    blocks = [slice(r, r + min(SWA_PROJ_SUB, tm)) for r in range(0, tm, SWA_PROJ_SUB)]
    kw = N_KV_B * HD_B
    half = ROT_B // 2

    def stage_in(rows):
        h = h_ref[rows, :]
        hr = h * lax.rsqrt(jnp.mean(h * h, axis=-1, keepdims=True) + EPS)
        kv = _dot((hr * g_kv_ref[...]).astype(BF16), w_kv_ref[...])
        q_raw = _dot((hr * g_attn_ref[...]).astype(BF16), w_q_ref[...])
        v_ref[rows, :] = kv[:, kw:]
        return kv, q_raw

    def stage_out(rows, kv, q_raw, k_scale, q_scale):
        c, s1, s2 = tab_ref[0, rows, :], tab_ref[1, rows, :], tab_ref[2, rows, :]
        last = rows.stop == tm
        n = rows.stop - rows.start
        kn = kv[:, :kw] * k_scale * gk_ref[...]
        for j in range(kw // LANES):
            sl = slice(LANES * j, LANES * (j + 1))
            k_slab = _rope_slab(kn[:, sl], c, s1, s2, half)
            k_ref[rows, sl] = k_slab
            if last:
                kt_ref[sl, :] = k_slab[n - WINDOW:].T
                vt_ref[sl, :] = kv[n - WINDOW:, kw + LANES * j:kw + LANES * (j + 1)].T
        qn = q_raw * q_scale * gq_ref[...]
        for j in range(H_B * HD_B // LANES):
            sl = slice(LANES * j, LANES * (j + 1))
            q_ref[rows, sl] = _rope_slab(qn[:, sl], c, s1, s2, half).astype(BF16)

    ins = [stage_in(rows) for rows in blocks]
    items = []
    for kv, q_raw in ins:
        items += [(kv[:, :kw], segk_ref, cntk_ref, expk_ref), (q_raw, segq_ref, cntq_ref, expq_ref)]
    scales = _segment_scales(items)
    for b, (rows, (kv, q_raw)) in enumerate(zip(blocks, ins)):
        stage_out(rows, kv, q_raw, scales[2 * b], scales[2 * b + 1])


def _proj_b(h, tab, wb, tm):
    n = h.shape[0]
    n_tab = tab.shape[1] // tm
    weights = [wb["g_kv"], wb["w_kv"], wb["segk"], wb["cntk"], wb["expk"], wb["gk"],
               wb["g_attn"], wb["w_q"], wb["segq"], wb["cntq"], wb["expq"], wb["gq"]]
    kw = N_KV_B * HD_B
    return pl.pallas_call(
        _proj_b_kernel,
        grid=(n // tm,),
        in_specs=[pl.BlockSpec((tm, D_MODEL), lambda i: (i, 0)),
                  pl.BlockSpec((3, tm, LANES), lambda i: (0, i % n_tab, 0))]
                 + [_const_spec(w.shape) for w in weights],
        out_specs=[pl.BlockSpec((tm, kw), lambda i: (i, 0)),
                   pl.BlockSpec((tm, kw), lambda i: (i, 0)),
                   pl.BlockSpec((tm, H_B * HD_B), lambda i: (i, 0)),
                   pl.BlockSpec((None, kw, WINDOW), lambda i: (i // n_tab, 0, 0)),
                   pl.BlockSpec((None, kw, WINDOW), lambda i: (i // n_tab, 0, 0))],
        out_shape=[jax.ShapeDtypeStruct((n, kw), F32),
                   jax.ShapeDtypeStruct((n, kw), F32),
                   jax.ShapeDtypeStruct((n, H_B * HD_B), BF16),
                   jax.ShapeDtypeStruct((n // (n_tab * tm), kw, WINDOW), F32),
                   jax.ShapeDtypeStruct((n // (n_tab * tm), kw, WINDOW), F32)],
        compiler_params=pltpu.CompilerParams(dimension_semantics=("arbitrary",),
                                             vmem_limit_bytes=VMEM_LIMIT),
        name="swa_proj",
    )(h, tab, *weights)


def _swa_halves(slab, kv, ones_lane=False):
    lane = lax.broadcasted_iota(jnp.int32, slab.shape, 1)
    own = (lane >= HD_B) if kv % 2 else (lane < HD_B)
    halves = [None, None]
    halves[kv % 2] = jnp.where(own, slab, 0.0)
    halves[1 - kv % 2] = pltpu.roll(halves[kv % 2], HD_B, 1)
    if ones_lane:
        halves = [jnp.where(lane == _ones_lane(par), 1.0, h) for par, h in enumerate(halves)]
    return [h.astype(BF16) for h in halves]


def _swa_attend(jobs, valid_of, sink_of):
    scores = [[_dot_nt(q, kh[par]) for par in range(2)] for q, kh, _, _ in jobs]
    probs = []
    for j, (q, _, _, kv) in enumerate(jobs):
        m2 = q.shape[0]
        top = lax.broadcasted_iota(jnp.int32, (m2, 1), 0) < (m2 // 2)
        row = []
        for par in range(2):
            s = jnp.where(valid_of(j), scores[j][par], NEG)
            sink = jnp.where(top, sink_of(G_B * kv + par), sink_of(G_B * kv + par + 2)) * LOG2E
            m = jnp.maximum(jnp.max(s, axis=-1, keepdims=True), sink)
            row.append((jnp.exp2(s - m).astype(BF16), jnp.exp2(sink - m)))
        probs.append(row)
    outs = []
    for j, (q, _, vh, _) in enumerate(jobs):
        o = []
        for par in range(2):
            pv = _dot(probs[j][par][0], vh[par])
            ones = _ones_lane(par)
            o.append(pv / (pv[:, ones:ones + 1] + probs[j][par][1]))
        low_half = lax.broadcasted_iota(jnp.int32, o[0].shape, 1) < HD_B
        outs.append(jnp.where(low_half, o[0], o[1]))
    return outs


def _attn_b_kernel(sink_ref, q_ref, kp_ref, kc_ref, vp_ref, vc_ref, o_ref):
    g = pl.program_id(1)
    kcat = jnp.concatenate([kp_ref[...], kc_ref[...]], axis=0)
    vcat = jnp.concatenate([vp_ref[...], vc_ref[...]], axis=0)
    shape = (2 * WINDOW, 2 * WINDOW)
    qi = lax.broadcasted_iota(jnp.int32, shape, 0) & (WINDOW - 1)
    col = lax.broadcasted_iota(jnp.int32, shape, 1)
    band = (col > qi) & (col <= qi + WINDOW)
    band_first = band & ((col >= WINDOW) | (g > 0))
    n_blocks = q_ref.shape[0] // WINDOW
    for kv in range(N_KV_B):
        base = G_B * HD_B * kv
        ks = slice(LANES * (kv // 2), LANES * (kv // 2 + 1))
        k_half = _swa_halves(kcat[:, ks], kv)
        v_half = _swa_halves(vcat[:, ks], kv, ones_lane=True)
        jobs = []
        for r in range(n_blocks):
            rows = slice(WINDOW * r, WINDOW * (r + 1))
            win = slice(WINDOW * r, WINDOW * (r + 2))
            q_lhs = jnp.concatenate([q_ref[rows, base:base + LANES], q_ref[rows, base + LANES:base + 2 * LANES]], axis=0)
            jobs.append((q_lhs, [h[win] for h in k_half], [h[win] for h in v_half], kv))
        outs = _swa_attend(jobs, lambda r: band if r else band_first, lambda hh: sink_ref[hh])
        for r, o in enumerate(outs):
            rows = slice(WINDOW * r, WINDOW * (r + 1))
            o_ref[rows, base:base + LANES] = o[:WINDOW].astype(BF16)
            o_ref[rows, base + LANES:base + 2 * LANES] = o[WINDOW:].astype(BF16)


def _attn_b(sinks, q, k, v, batch, seq):
    nb = seq // WINDOW
    ng = nb // SWA_BLOCKS
    kw = N_KV_B * HD_B
    prev = lambda b, g: (b * nb + jnp.maximum(SWA_BLOCKS * g - 1, 0), 0)
    cur = lambda b, g: (b * ng + g, 0)
    return pl.pallas_call(
        _attn_b_kernel,
        grid=(batch, ng),
        in_specs=[pl.BlockSpec(memory_space=pltpu.SMEM),
                  pl.BlockSpec((SWA_BLOCKS * WINDOW, H_B * HD_B), cur),
                  pl.BlockSpec((WINDOW, kw), prev), pl.BlockSpec((SWA_BLOCKS * WINDOW, kw), cur),
                  pl.BlockSpec((WINDOW, kw), prev), pl.BlockSpec((SWA_BLOCKS * WINDOW, kw), cur)],
        out_specs=pl.BlockSpec((SWA_BLOCKS * WINDOW, H_B * HD_B), cur),
        out_shape=jax.ShapeDtypeStruct((batch * seq, H_B * HD_B), BF16),
        compiler_params=pltpu.CompilerParams(dimension_semantics=("arbitrary", "arbitrary"),
                                             vmem_limit_bytes=VMEM_LIMIT),
        name="swa_prompt_attn",
    )(sinks, q, k, k, v, v)


def _attn_b_sample_kernel(sink_ref, q_ref, k_ref, v_ref, o_ref, *, t, w_buf):
    keys = k_ref.shape[1]
    rows = 2 * SWA_Q_ROWS
    ti = lax.broadcasted_iota(jnp.int32, (rows, keys), 0) & (SWA_Q_ROWS - 1)
    col = lax.broadcasted_iota(jnp.int32, (rows, keys), 1)
    diff = jnp.where(col < w_buf, ti + w_buf - col, ti - (col - w_buf))
    valid = (diff >= 0) & (diff < WINDOW) & (col < w_buf + t) & (ti < t)
    jobs = []
    for b in range(q_ref.shape[0]):
        for kv in range(N_KV_B):
            ks = slice(LANES * (kv // 2), LANES * (kv // 2 + 1))
            jobs.append((q_ref[b, kv], _swa_halves(k_ref[b, :, ks], kv),
                         _swa_halves(v_ref[b, :, ks], kv, ones_lane=True), kv))
    outs = _swa_attend(jobs, lambda j: valid, lambda hh: sink_ref[hh])
    for j, o in enumerate(outs):
        o_ref[j // N_KV_B, j % N_KV_B] = o


def _attn_b_sample(sinks, q, k_all, v_all, t, w_buf):
    bd = q.shape[0]
    bs = SWA_SAMPLE_SEQS
    blk = lambda a: pl.BlockSpec((bs,) + a.shape[1:], lambda b: (b,) + (0,) * (a.ndim - 1))
    return pl.pallas_call(
        functools.partial(_attn_b_sample_kernel, t=t, w_buf=w_buf),
        grid=(bd // bs,),
        in_specs=[pl.BlockSpec(memory_space=pltpu.SMEM), blk(q), blk(k_all), blk(v_all)],
        out_specs=blk(q),
        out_shape=jax.ShapeDtypeStruct(q.shape, F32),
        compiler_params=pltpu.CompilerParams(dimension_semantics=("arbitrary",),
                                             vmem_limit_bytes=VMEM_LIMIT),
        name="swa_sample_attn",
    )(sinks, q, k_all, v_all)


def _rope_tables(pos, n_rot, period, lane_lo):
    half = n_rot // 2
    inv = ROPE_THETA ** (-np.arange(0, n_rot, 2, dtype=np.float64) / n_rot)
    ang = np.asarray(pos, np.float64)[:, None] * inv[None, :]
    cos, sin = np.cos(ang), np.sin(ang)
    rel = np.arange(LANES) % period - lane_lo
    in1 = (rel >= 0) & (rel < half)
    in2 = (rel >= half) & (rel < 2 * half)
    idx = np.where(in1, rel, np.where(in2, rel - half, 0))
    cg, sg = cos[:, idx], sin[:, idx]
    tables = np.stack([np.where(in1 | in2, cg, 1.0), np.where(in1, -sg, 0.0), np.where(in2, sg, 0.0)])
    return jnp.asarray(tables, F32)


def _rope_tables_dup(pos, n_rot, lane_lo):
    half = n_rot // 2
    inv = ROPE_THETA ** (-np.arange(0, n_rot, 2, dtype=np.float64) / n_rot)
    ang = np.asarray(pos, np.float64)[:, None] * inv[None, :]
    cos, sin = np.cos(ang), np.sin(ang)
    rel = np.arange(LANES) - lane_lo
    in1 = (rel >= 0) & (rel < half)
    in2 = (rel >= half) & (rel < 2 * half)
    idx = np.where(in1, rel, np.where(in2, rel - half, 0))
    cg, sg = cos[:, idx], sin[:, idx]
    tables = np.stack([np.where(in1 | in2, cg, np.where(rel < 0, 1.0, 0.0)),
                       np.where(in1, -sg, np.where(in2, sg, 0.0))])
    return jnp.asarray(tables, F32)


def _segments(width, seg_lanes, extra=None):
    seg = np.zeros((width, LANES), np.float32)
    spread = np.zeros((LANES, width), np.float32)
    cnt = np.zeros((1, LANES), np.float32)
    for s, (lo, hi) in enumerate(seg_lanes):
        seg[lo:hi, s] = 1.0
        spread[s, lo:hi + (extra[s] if extra else 0)] = 1.0
        cnt[0, s] = 1.0 / (hi - lo)
    expand = np.concatenate([spread, spread], axis=0)
    return jnp.asarray(seg, BF16), jnp.asarray(cnt, F32), jnp.asarray(expand, BF16)


def _slab_gain(parts):
    pieces, pos = [], 0
    for lo, vals in parts:
        pieces += [jnp.zeros((lo - pos,), F32), vals.astype(F32)]
        pos = lo + vals.shape[0]
    pieces.append(jnp.zeros((SLAB - pos,), F32))
    return jnp.concatenate(pieces)[None, :]


def _prep_a(norm_attn, w_a_in, g_qc, w_uq, g_ckv, w_uk, w_uv, g_qn, g_qr, g_kn, g_kr):
    half = D_ROPE // 2
    tail = SLAB - KPE_LANE - D_ROPE - half
    w_kpe = w_a_in[:, D_QC + D_C:]
    w_in = jnp.concatenate([w_a_in[:, :D_QC + D_C], jnp.zeros((D_MODEL, KPE_LANE), F32),
                            w_kpe, w_kpe[:, :half], jnp.zeros((D_MODEL, tail), F32)], axis=1)
    dqk = D_NOPE + D_ROPE
    w_uq3 = w_uq.reshape(D_QC, H_A, dqk)
    w_uq_pad = jnp.concatenate([w_uq3, w_uq3[:, :, D_NOPE:D_NOPE + half], jnp.zeros((D_QC, H_A, tail), F32)],
                               axis=2).reshape(D_QC, H_A * SLAB)
    w_uk3 = w_uk.reshape(D_C, H_A, D_NOPE)
    w_uk_pad = jnp.pad(w_uk3, ((0, 0), (0, 0), (0, SLAB - D_NOPE))).reshape(D_C, H_A * SLAB)
    w_uv3 = w_uv.reshape(D_C, H_A // 2, 2, D_V)
    even = jnp.pad(w_uv3[:, :, 0], ((0, 0), (0, 0), (0, SLAB - D_V)))
    odd = jnp.pad(w_uv3[:, :, 1], ((0, 0), (0, 0), (SLAB - D_V, 0)))
    w_uv_pad = jnp.stack([even, odd], axis=2).reshape(D_C, H_A * SLAB)
    v_ones = np.zeros((1, H_A * SLAB), np.float32)
    for h in range(H_A):
        v_ones[0, SLAB * h + _ones_lane(h % 2)] = 1.0
    q_segs = []
    for h in range(H_A):
        q_segs += [(SLAB * h, SLAB * h + D_NOPE), (SLAB * h + KPE_LANE, SLAB * h + KPE_LANE + D_ROPE)]
    segq, cntq, expq = _segments(H_A * SLAB, q_segs, extra=[0, half] * H_A)
    q_scale = SCALE_A * LOG2E
    gq = jnp.tile(_slab_gain([(0, g_qn * q_scale), (KPE_LANE, g_qr * q_scale),
                              (KPE_LANE + D_ROPE, g_qr[:half] * q_scale)]), (1, H_A))
    gk_slab = _slab_gain([(0, g_kn)])
    pe_mask = np.zeros((1, SLAB), np.float32)
    pe_mask[0, KPE_LANE:KPE_LANE + D_ROPE] = 1.0
    wukt = jnp.pad(jnp.transpose(w_uk3, (1, 2, 0)), ((0, 0), (0, SLAB - D_NOPE), (0, 0)))
    return dict(
        g_attn=norm_attn[None, :], w_in=w_in.astype(BF16), g_qc=g_qc[None, :], g_ckv=g_ckv[None, :],
        g_kpe=_slab_gain([(KPE_LANE, g_kr), (KPE_LANE + D_ROPE, g_kr[:half])]), pe_mask=jnp.asarray(pe_mask),
        w_uq=w_uq_pad.astype(BF16), segq=segq, cntq=cntq, expq=expq, gq=gq,
        w_uk=w_uk_pad.astype(BF16),
        w_uv=w_uv_pad.astype(BF16), v_ones=jnp.asarray(v_ones), gk_slab=gk_slab, wukt_pad=wukt.astype(BF16),
        wukt=jnp.transpose(w_uk3, (2, 1, 0)).reshape(H_A * D_NOPE, D_C).astype(BF16))


def _prep_b(g_kv, w_kv, g_k, norm_attn, w_q, g_q):
    kw = N_KV_B * HD_B
    segk, cntk, expk = _segments(kw, [(HD_B * h, HD_B * (h + 1)) for h in range(N_KV_B)])
    segq, cntq, expq = _segments(H_B * HD_B, [(HD_B * h, HD_B * (h + 1)) for h in range(H_B)])
    return dict(g_kv=g_kv[None, :], w_kv=w_kv.astype(BF16), segk=segk, cntk=cntk, expk=expk,
                gk=jnp.tile(g_k, N_KV_B)[None, :], g_attn=norm_attn[None, :], w_q=w_q.astype(BF16),
                segq=segq, cntq=cntq, expq=expq, gq=jnp.tile(g_q * (SCALE_B * LOG2E), H_B)[None, :])


def kernel(x_prompt, x_sample, cache_mla, state_win_k, state_win_v, page_table, norm_attn, norm_ffn, w_a_in, g_qc, w_uq, g_ckv, w_uk, w_uv, g_qn_a, g_qr_a, g_kn_a, g_kr_a, w_a_out, g_kv_shared, w_kv_shared, g_k_b, w_q_b, g_q_b, sinks, w_b_out, w_ffn_in, w_ffn_out):
    batch, seq, _ = x_prompt.shape
    bd, t_dec, _ = x_sample.shape
    past_len = page_table.shape[1] * PAGE_SIZE
    w_buf = state_win_k.shape[1]
    kw = N_KV_B * HD_B
    assert w_a_in.shape[0] == 1 and w_q_b.shape[0] == 1, "one MLA layer followed by one SWA layer"
    assert w_buf == WINDOW and seq % TM_POST == 0 and seq % TM_SWA_PROJ == 0 and (bd * t_dec) % 8 == 0
    assert seq >= WINDOW and bd * t_dec >= WINDOW, "the window tail is taken from the last 128 rows of a tile"
    assert t_dec <= SWA_Q_ROWS and bd % SWA_SAMPLE_SEQS == 0 and page_table.shape[1] % (2 * PAGES_PER_GROUP) == 0

    wa = _prep_a(norm_attn[0], w_a_in[0], g_qc[0], w_uq[0], g_ckv[0], w_uk[0], w_uv[0],
                 g_qn_a[0], g_qr_a[0], g_kn_a[0], g_kr_a[0])
    wb = _prep_b(g_kv_shared, w_kv_shared, g_k_b, norm_attn[1], w_q_b[0], g_q_b[0])
    w_a_out_b = w_a_out[0].astype(BF16)
    w_b_out_b = w_b_out[0].astype(BF16)
    ffn_in = w_ffn_in.astype(BF16)
    ffn_out = w_ffn_out.astype(BF16)
    g_ffn = norm_ffn[:, None, :]
    sink_b = sinks[0]

    pos_p = np.arange(seq)
    n_s = bd * t_dec
    pos_s = past_len + np.arange(n_s) % t_dec

    xp = x_prompt.reshape(batch * seq, D_MODEL)
    q, k, v, rows_p = _proj_a(xp, _rope_tables_dup(pos_p, D_ROPE, KPE_LANE), wa, TM_MLA_PROJ)
    o = _attn_a(q, k, v, batch, seq, TQ_MLA)
    h = _post(xp, o, w_a_out_b, g_ffn, ffn_in, ffn_out, 0, TM_POST)
    k_p, v_p, q_b, kt_p, vt_p = _proj_b(h, _rope_tables(pos_p, ROT_B, HD_B, 0), wb, TM_SWA_PROJ)
    o = _attn_b(sink_b, q_b, k_p, v_p, batch, seq)
    y_prompt = _post(h, o, w_b_out_b, g_ffn, ffn_in, ffn_out, 1, TM_POST)

    xs = x_sample.reshape(n_s, D_MODEL)
    q, _, _, rows_s = _proj_a(xs, _rope_tables_dup(pos_s, D_ROPE, KPE_LANE), wa, n_s)
    qabs = _qabs(q, wa["gk_slab"], wa["wukt_pad"]).reshape(bd, t_dec * H_A, D_C)
    qpe = q.reshape(n_s, H_A, SLAB)[:, :, KPE_LANE:KPE_LANE + D_ROPE].reshape(bd, t_dec * H_A, D_ROPE)
    new_pad = jnp.pad(jnp.swapaxes(rows_s.reshape(bd, t_dec, D_CKV), 1, 2), ((0, 0), (0, 0), (0, PAGE_SIZE - t_dec)))
    olat = _paged_attn(page_table, jnp.swapaxes(cache_mla, 2, 3), wa["wukt"], qabs, qpe, new_pad, t_dec)
    o = _latent_out(olat.reshape(n_s, H_A * D_C), wa["w_uv"])
    h = _post(xs, o, w_a_out_b, g_ffn, ffn_in, ffn_out, 0, n_s)
    k_s, v_s, q_b, _, _ = _proj_b(h, _rope_tables(pos_s, ROT_B, HD_B, 0), wb, n_s)
    q4 = jnp.transpose(q_b.reshape(bd, t_dec, N_KV_B, 2, LANES), (0, 2, 3, 1, 4))
    q4 = jnp.pad(q4, ((0, 0), (0, 0), (0, 0), (0, SWA_Q_ROWS - t_dec), (0, 0))).reshape(bd, N_KV_B, 2 * SWA_Q_ROWS, LANES)
    key_pad = jnp.zeros((bd, 2 * SWA_Q_ROWS - t_dec, kw), F32)
    k_all = jnp.concatenate([state_win_k.reshape(bd, w_buf, kw), k_s.reshape(bd, t_dec, kw), key_pad], axis=1)
    v_all = jnp.concatenate([state_win_v.reshape(bd, w_buf, kw), v_s.reshape(bd, t_dec, kw), key_pad], axis=1)
    o4 = _attn_b_sample(sink_b, q4, k_all, v_all, t_dec, w_buf)
    o = jnp.transpose(o4.reshape(bd, N_KV_B, 2, SWA_Q_ROWS, LANES)[:, :, :, :t_dec], (0, 3, 1, 2, 4))
    y_sample = _post(h, o.reshape(n_s, H_B * HD_B).astype(BF16), w_b_out_b, g_ffn, ffn_in, ffn_out, 1, n_s)

    win_k_p = jnp.transpose(kt_p.reshape(batch, N_KV_B, HD_B, WINDOW), (0, 3, 1, 2))
    win_v_p = jnp.transpose(vt_p.reshape(batch, N_KV_B, HD_B, WINDOW), (0, 3, 1, 2))
    win_k_s = jnp.concatenate([state_win_k, k_s.reshape(bd, t_dec, N_KV_B, HD_B)], axis=1)[:, -w_buf:]
    win_v_s = jnp.concatenate([state_win_v, v_s.reshape(bd, t_dec, N_KV_B, HD_B)], axis=1)[:, -w_buf:]
    return (y_prompt.reshape(batch, seq, D_MODEL), y_sample.reshape(bd, t_dec, D_MODEL),
            rows_p.reshape(1, batch, seq, D_CKV), rows_s.reshape(1, bd, t_dec, D_CKV),
            win_k_p, win_v_p, win_k_s, win_v_s)
```

```python
import functools

import numpy as np
import jax
import jax.numpy as jnp
from jax import lax
from jax.experimental import pallas as pl
from jax.experimental.pallas import tpu as pltpu

F32 = jnp.float32
BF16 = jnp.bfloat16

D_MODEL = 1024
PAGE_SIZE = 128
H_A = 16
D_NOPE = 64
D_ROPE = 32
D_V = 64
D_QC = 384
D_C = 256
D_CKV = D_C + D_ROPE
SCALE_A = (D_NOPE + D_ROPE) ** -0.5
H_B = 16
N_KV_B = 4
HD_B = 64
G_B = H_B // N_KV_B
WINDOW = 128
ROT_B = HD_B // 4
SCALE_B = HD_B ** -0.5
D_FF = 2816
ROPE_THETA = 500000.0
EPS = 1e-6
NEG = -1e30
LOG2E = 1.4426950408889634

LANES = 128
MXU_DIM = 256
VMEM_LIMIT = 56 * 1024 * 1024
SLAB = 128
KPE_LANE = 64

FF_CHUNK = MXU_DIM
TM_MLA_PROJ = 512
PROJ_SUB = 256
TM_SWA_PROJ = 1024
SWA_PROJ_SUB = 512
TM_POST = 512
TQ_MLA = MXU_DIM
ATTN_LOOKAHEAD = 3
PAGES_PER_GROUP = 16
PAGES_PER_DOT = MXU_DIM // PAGE_SIZE
ACC_AFTER_DOTS = 4
SWA_SAMPLE_SEQS = 8
SWA_BLOCKS = 8
SWA_Q_ROWS = 8

_NT = (((1,), (1,)), ((), ()))


def _dot(a, b):
    return jnp.dot(a, b, preferred_element_type=F32)


def _dot_nt(a, b):
    return lax.dot_general(a, b, _NT, preferred_element_type=F32)


def _rms(x, g):
    ms = jnp.mean(x * x, axis=-1, keepdims=True)
    return x * lax.rsqrt(ms + EPS) * g


def _rope_slab(x, c, s1, s2, half):
    return x * c + pltpu.roll(x, LANES - half, 1) * s1 + pltpu.roll(x, half, 1) * s2


def _rope_slab_dup(x, c, s, half):
    return x * c + pltpu.roll(x, LANES - half, 1) * s


def _segment_scales(items):
    sums = [_dot((raw * raw).astype(BF16), seg_ref[...]) for raw, seg_ref, _, _ in items]
    scales = []
    for ss, (_, _, inv_cnt_ref, expand_ref) in zip(sums, items):
        rs = lax.rsqrt(ss * inv_cnt_ref[...] + EPS)
        hi = rs.astype(BF16)
        lo = (rs - hi.astype(F32)).astype(BF16)
        scales.append(_dot(jnp.concatenate([hi, lo], axis=1), expand_ref[...]))
    return scales


def _proj_a_kernel(x_ref, tab_ref, g_attn_ref, w_in_ref, g_qc_ref, g_ckv_ref, g_kpe_ref, pe_mask_ref,
                   w_uq_ref, segq_ref, cntq_ref, expq_ref, gq_ref,
                   w_uk_ref, gk_ref, w_uv_ref, v_ones_ref,
                   q_ref, k_ref, v_ref, rows_ref):
    tm = x_ref.shape[0]
    blocks = [slice(r, r + min(PROJ_SUB, tm)) for r in range(0, tm, PROJ_SUB)]
    half = D_ROPE // 2

    def stage_in(rows):
        hn = _rms(x_ref[rows, :], g_attn_ref[...]).astype(BF16)
        return _dot(hn, w_in_ref[...])

    def stage_latents(rows, a):
        c, s = tab_ref[0, rows, :], tab_ref[1, rows, :]
        cq = _rms(a[:, :D_QC], g_qc_ref[...]).astype(BF16)
        ckv = _rms(a[:, D_QC:D_QC + D_C], g_ckv_ref[...])
        kpe = a[:, D_QC + D_C:]
        ms = jnp.sum(kpe * kpe * pe_mask_ref[...], axis=-1, keepdims=True) * (1.0 / D_ROPE)
        kpe = _rope_slab_dup(kpe * lax.rsqrt(ms + EPS) * g_kpe_ref[...], c, s, half)
        rows_ref[rows, :D_C] = ckv
        rows_ref[rows, D_C:] = kpe[:, KPE_LANE:KPE_LANE + D_ROPE]
        ckv_b = ckv.astype(BF16)
        q_raw = _dot(cq, w_uq_ref[...])
        k_raw = _dot(ckv_b, w_uk_ref[...])
        v_ref[rows, :] = (_dot(ckv_b, w_uv_ref[...]) + v_ones_ref[...]).astype(BF16)
        return q_raw, k_raw, kpe

    def stage_out(rows, q_raw, q_scale, k_raw, kpe):
        c, s = tab_ref[0, rows, :], tab_ref[1, rows, :]
        qn = q_raw * q_scale * gq_ref[...]
        for h in range(H_A):
            sl = slice(SLAB * h, SLAB * (h + 1))
            q_ref[rows, sl] = _rope_slab_dup(qn[:, sl], c, s, half).astype(BF16)
            k_h = k_raw[:, sl]
            ms_h = jnp.sum(k_h * k_h, axis=-1, keepdims=True) * (1.0 / D_NOPE)
            k_ref[rows, sl] = (k_h * lax.rsqrt(ms_h + EPS) * gk_ref[...] + kpe).astype(BF16)

    a_s = [stage_in(rows) for rows in blocks]
    mids = [stage_latents(rows, a) for rows, a in zip(blocks, a_s)]
    q_scales = _segment_scales([(q_raw, segq_ref, cntq_ref, expq_ref) for q_raw, _, _ in mids])
    for rows, (q_raw, k_raw, kpe), q_scale in zip(blocks, mids, q_scales):
        stage_out(rows, q_raw, q_scale, k_raw, kpe)


def _const_spec(shape):
    zeros = (0,) * len(shape)
    return pl.BlockSpec(shape, lambda *_: zeros, pipeline_mode=pl.Buffered(1))


def _proj_a(x, tab, wa, tm):
    n = x.shape[0]
    n_tab = tab.shape[1] // tm
    weights = [wa["g_attn"], wa["w_in"], wa["g_qc"], wa["g_ckv"], wa["g_kpe"], wa["pe_mask"],
               wa["w_uq"], wa["segq"], wa["cntq"], wa["expq"], wa["gq"],
               wa["w_uk"], wa["gk_slab"], wa["w_uv"], wa["v_ones"]]
    wide = H_A * SLAB
    return pl.pallas_call(
        _proj_a_kernel,
        grid=(n // tm,),
        in_specs=[pl.BlockSpec((tm, D_MODEL), lambda i: (i, 0)),
                  pl.BlockSpec((tab.shape[0], tm, LANES), lambda i: (0, i % n_tab, 0))]
                 + [_const_spec(w.shape) for w in weights],
        out_specs=[pl.BlockSpec((tm, wide), lambda i: (i, 0)),
                   pl.BlockSpec((tm, wide), lambda i: (i, 0)),
                   pl.BlockSpec((tm, wide), lambda i: (i, 0)),
                   pl.BlockSpec((tm, D_CKV), lambda i: (i, 0))],
        out_shape=[jax.ShapeDtypeStruct((n, wide), BF16),
                   jax.ShapeDtypeStruct((n, wide), BF16),
                   jax.ShapeDtypeStruct((n, wide), BF16),
                   jax.ShapeDtypeStruct((n, D_CKV), F32)],
        compiler_params=pltpu.CompilerParams(dimension_semantics=("arbitrary",),
                                             vmem_limit_bytes=VMEM_LIMIT),
        name="mla_proj",
    )(x, tab, *weights)


def _ones_lane(parity):
    return D_V if parity == 0 else 0


def _attn_a_kernel(q_ref, k_ref, v_ref, o_ref, *, tq):
    seq = q_ref.shape[0]
    causal = (lax.broadcasted_iota(jnp.int32, (tq, tq), 1) <= lax.broadcasted_iota(jnp.int32, (tq, tq), 0))
    low_half = lax.broadcasted_iota(jnp.int32, (tq, SLAB), 1) < D_V
    jobs = [(c, e) for c in range(seq // tq) for e in range(2)]

    def windows(c, e):
        return slice(c * tq, (c + 1) * tq), slice(0, c * tq), slice(SLAB * e, SLAB * (e + 1))

    def score(c, e):
        rows, past, ls = windows(c, e)
        q = q_ref[rows, ls]
        s_d = _dot_nt(q, k_ref[rows, ls])
        return s_d, (_dot_nt(q, k_ref[past, ls]) if c else None)

    def attend(c, e, s_d, s_p):
        rows, past, ls = windows(c, e)
        s_d = jnp.where(causal, s_d, NEG)
        m = jnp.max(s_d, axis=-1, keepdims=True)
        if c:
            m = jnp.maximum(m, jnp.max(s_p, axis=-1, keepdims=True))
        acc = _dot(jnp.exp2(s_d - m).astype(BF16), v_ref[rows, ls])
        if c:
            acc = acc + _dot(jnp.exp2(s_p - m).astype(BF16), v_ref[past, ls])
        ones = _ones_lane(e)
        return acc / acc[:, ones:ones + 1]

    ahead = [score(*jobs[j]) for j in range(min(ATTN_LOOKAHEAD, len(jobs)))]
    out = None
    for j, (c, e) in enumerate(jobs):
        if j + ATTN_LOOKAHEAD < len(jobs):
            ahead.append(score(*jobs[j + ATTN_LOOKAHEAD]))
        o_e = attend(c, e, *ahead[j])
        ahead[j] = None
        if e == 0:
            out = o_e
        else:
            o_ref[c * tq:(c + 1) * tq, :] = jnp.where(low_half, out, o_e).astype(BF16)


def _attn_a(q, k, v, batch, seq, tq):
    pairs = H_A // 2
    return pl.pallas_call(
        functools.partial(_attn_a_kernel, tq=tq),
        grid=(batch, pairs),
        in_specs=[pl.BlockSpec((seq, 2 * SLAB), lambda b, j: (b, j)),
                  pl.BlockSpec((seq, 2 * SLAB), lambda b, j: (b, j)),
                  pl.BlockSpec((seq, 2 * SLAB), lambda b, j: (b, j))],
        out_specs=pl.BlockSpec((seq, SLAB), lambda b, j: (b, j)),
        out_shape=jax.ShapeDtypeStruct((batch * seq, H_A * D_V), BF16),
        compiler_params=pltpu.CompilerParams(
            dimension_semantics=("arbitrary", "arbitrary"),
            vmem_limit_bytes=VMEM_LIMIT),
        name="mla_prompt_attn",
    )(q, k, v)


def _qabs_kernel(q_ref, gk_ref, wukt_ref, o_ref):
    for h in range(H_A):
        qs = (q_ref[:, SLAB * h:SLAB * (h + 1)].astype(F32) * gk_ref[...]).astype(BF16)
        o_ref[:, D_C * h:D_C * (h + 1)] = _dot(qs, wukt_ref[h]).astype(BF16)


def _qabs(q, gk_slab, wukt):
    n = q.shape[0]
    return pl.pallas_call(
        _qabs_kernel,
        out_shape=jax.ShapeDtypeStruct((n, H_A * D_C), BF16),
        compiler_params=pltpu.CompilerParams(vmem_limit_bytes=VMEM_LIMIT),
        name="mla_absorb_q",
    )(q, gk_slab, wukt)


def _paged_kernel(pt_ref, cache_ref, wukt_ref, qabs_ref, qpe_ref, new_ref, o_ref, lhs_sc, pg_sc, sem, *, n_pages, t_new):
    seq = pl.program_id(0)
    group = pg_sc.shape[1]
    n_groups = n_pages // group
    rows_q = qabs_ref.shape[0]

    def page_copy(sq, g, u):
        slot = g % 2
        return pltpu.make_async_copy(cache_ref.at[0, pt_ref[sq, g * group + u]], pg_sc.at[slot, u], sem.at[slot])

    def start_group(sq, g):
        for u in range(group):
            page_copy(sq, g, u).start()

    def wait_group(sq, g):
        for u in range(group):
            page_copy(sq, g, u).wait()

    @pl.when(seq == 0)
    def _():
        start_group(seq, 0)

    lhs_sc[:H_A * D_NOPE, :] = wukt_ref[...]
    lhs_sc[H_A * D_NOPE:, :] = qabs_ref[...]

    def nope_scores(ct):
        keys = ct.shape[1]
        big = _dot(lhs_sc[...], ct)
        kt = big[:H_A * D_NOPE]
        ssq = jnp.sum((kt * kt).reshape(D_NOPE, H_A, keys), axis=0)
        rs = lax.rsqrt(ssq * (1.0 / D_NOPE) + EPS)
        rs_q = jnp.concatenate([rs] * (rows_q // H_A), axis=0)
        return big[H_A * D_NOPE:] * rs_q

    def rope_scores(kpets):
        return _dot(qpe_ref[...], jnp.concatenate(kpets, axis=1))

    def accumulate(state, s, ct):
        m_old, l, acc = state
        m_new = jnp.maximum(m_old, jnp.max(s, axis=-1, keepdims=True))
        corr = jnp.exp2(m_old - m_new)
        p = jnp.exp2(s - m_new)
        return (m_new, l * corr + jnp.sum(p, axis=-1, keepdims=True),
                acc * corr + _dot_nt(p.astype(BF16), ct))

    state = (jnp.full((rows_q, 1), NEG, F32), jnp.zeros((rows_q, 1), F32), jnp.zeros((rows_q, D_C), F32))
    pending = None
    for g in range(n_groups):
        wait_group(seq, g)
        if g + 1 < n_groups:
            start_group(seq, g + 1)
        else:
            @pl.when(seq + 1 < pl.num_programs(0))
            def _():
                start_group(seq + 1, 0)
        slot = g % 2
        cts, kpets = [], []
        for u in range(0, group, PAGES_PER_DOT):
            pages = range(u, u + PAGES_PER_DOT)
            cts.append(jnp.concatenate([pg_sc[slot, v, :D_C, :] for v in pages], axis=1).astype(BF16))
            kpets.append(jnp.concatenate([pg_sc[slot, v, D_C:, :] for v in pages], axis=1).astype(BF16))
        nope = []
        for i, ct in enumerate(cts):
            if i == ACC_AFTER_DOTS and pending is not None:
                state = accumulate(state, *pending)
            nope.append(nope_scores(ct))
        pending = (jnp.concatenate(nope, axis=1) + rope_scores(kpets), jnp.concatenate(cts, axis=1))
    state = accumulate(state, *pending)

    keys = new_ref.shape[1]
    t_row = lax.shift_right_logical(lax.broadcasted_iota(jnp.int32, (rows_q, keys), 0), H_A.bit_length() - 1)
    s_col = lax.broadcasted_iota(jnp.int32, (rows_q, keys), 1)
    ct = new_ref[:D_C, :].astype(BF16)
    s = nope_scores(ct) + rope_scores([new_ref[D_C:, :].astype(BF16)])
    _, l, acc = accumulate(state, jnp.where((s_col <= t_row) & (s_col < t_new), s, NEG), ct)
    o_ref[...] = acc / l


def _paged_attn(page_table, cache, wukt, qabs, qpe, new_pad, t_new):
    bd, n_pages = page_table.shape
    rows_q = qabs.shape[1]
    assert (n_pages // PAGES_PER_GROUP) % 2 == 0, "the slot of a page group must not depend on the sequence"
    grid_spec = pltpu.PrefetchScalarGridSpec(
        num_scalar_prefetch=1,
        grid=(bd,),
        in_specs=[pl.BlockSpec(memory_space=pl.ANY),
                  pl.BlockSpec(wukt.shape, lambda b, pt: (0, 0)),
                  pl.BlockSpec((None, rows_q, D_C), lambda b, pt: (b, 0, 0)),
                  pl.BlockSpec((None, rows_q, D_ROPE), lambda b, pt: (b, 0, 0)),
                  pl.BlockSpec((None,) + new_pad.shape[1:], lambda b, pt: (b, 0, 0))],
        out_specs=pl.BlockSpec((None, rows_q, D_C), lambda b, pt: (b, 0, 0)),
        scratch_shapes=[pltpu.VMEM((H_A * D_NOPE + rows_q, D_C), BF16),
                        pltpu.VMEM((2, PAGES_PER_GROUP, D_CKV, PAGE_SIZE), F32),
                        pltpu.SemaphoreType.DMA((2,))])
    return pl.pallas_call(
        functools.partial(_paged_kernel, n_pages=n_pages, t_new=t_new),
        grid_spec=grid_spec,
        out_shape=jax.ShapeDtypeStruct((bd, rows_q, D_C), F32),
        compiler_params=pltpu.CompilerParams(dimension_semantics=("arbitrary",),
                                             vmem_limit_bytes=VMEM_LIMIT),
        name="mla_paged_attn",
    )(page_table, cache, wukt, qabs, qpe, new_pad)


def _latent_out_kernel(olat_ref, w_uv_ref, o_ref):
    for j in range(H_A // 2):
        acc = None
        for e in range(2):
            h = 2 * j + e
            part = _dot(olat_ref[:, D_C * h:D_C * (h + 1)].astype(BF16),
                        w_uv_ref[:, SLAB * h:SLAB * (h + 1)])
            acc = part if acc is None else acc + part
        o_ref[:, SLAB * j:SLAB * (j + 1)] = acc.astype(BF16)


def _latent_out(olat, w_uv_pad):
    n = olat.shape[0]
    return pl.pallas_call(
        _latent_out_kernel,
        out_shape=jax.ShapeDtypeStruct((n, H_A * D_V), BF16),
        compiler_params=pltpu.CompilerParams(vmem_limit_bytes=VMEM_LIMIT),
        name="mla_latent_out",
    )(olat, w_uv_pad)


def _post_kernel(x_ref, o_ref, w_o_ref, g_ref, w_in_ref, w_out_ref, y_ref):
    h1 = x_ref[...] + _dot(o_ref[...], w_o_ref[...])
    hn = _rms(h1, g_ref[...]).astype(BF16)
    acc = h1
    for c in range(D_FF // FF_CHUNK):
        lo = c * FF_CHUNK
        a1 = _dot(hn, w_in_ref[:, lo:lo + FF_CHUNK])
        a2 = _dot(hn, w_in_ref[:, D_FF + lo:D_FF + lo + FF_CHUNK])
        gate = (a1 * jax.nn.sigmoid(a1)) * a2
        acc = acc + _dot(gate.astype(BF16), w_out_ref[lo:lo + FF_CHUNK, :])
    y_ref[...] = acc


def _layer_spec(stacked, layer):
    zeros = (0,) * (stacked.ndim - 1)
    return pl.BlockSpec((None,) + stacked.shape[1:], lambda *_: (layer,) + zeros, pipeline_mode=pl.Buffered(1))


def _post(x, o, w_o, g, w_in, w_out, layer, tm):
    n = x.shape[0]
    return pl.pallas_call(
        _post_kernel,
        grid=(n // tm,),
        in_specs=[pl.BlockSpec((tm, D_MODEL), lambda i: (i, 0)),
                  pl.BlockSpec((tm, o.shape[1]), lambda i: (i, 0)),
                  _const_spec(w_o.shape), _layer_spec(g, layer),
                  _layer_spec(w_in, layer), _layer_spec(w_out, layer)],
        out_specs=pl.BlockSpec((tm, D_MODEL), lambda i: (i, 0)),
        out_shape=jax.ShapeDtypeStruct((n, D_MODEL), F32),
        compiler_params=pltpu.CompilerParams(dimension_semantics=("arbitrary",),
                                             vmem_limit_bytes=VMEM_LIMIT),
        name="outproj_swiglu",
    )(x, o, w_o, g, w_in, w_out)


def _proj_b_kernel(h_ref, tab_ref, g_kv_ref, w_kv_ref, segk_ref, cntk_ref, expk_ref, gk_ref,
                   g_attn_ref, w_q_ref, segq_ref, cntq_ref, expq_ref, gq_ref,
                   k_ref, v_ref, q_ref, kt_ref, vt_ref):
    tm = h_ref.shape[0]
    blocks = [slice(r, r + min(SWA_PROJ_SUB, tm)) for r in range(0, tm, SWA_PROJ_SUB)]
    kw = N_KV_B * HD_B
    half = ROT_B // 2

    def stage_in(rows):
        h = h_ref[rows, :]
        hr = h * lax.rsqrt(jnp.mean(h * h, axis=-1, keepdims=True) + EPS)
        kv = _dot((hr * g_kv_ref[...]).astype(BF16), w_kv_ref[...])
        q_raw = _dot((hr * g_attn_ref[...]).astype(BF16), w_q_ref[...])
        v_ref[rows, :] = kv[:, kw:]
        return kv, q_raw

    def stage_out(rows, kv, q_raw, k_scale, q_scale):
        c, s1, s2 = tab_ref[0, rows, :], tab_ref[1, rows, :], tab_ref[2, rows, :]
        last = rows.stop == tm
        n = rows.stop - rows.start
        kn = kv[:, :kw] * k_scale * gk_ref[...]
        for j in range(kw // LANES):
            sl = slice(LANES * j, LANES * (j + 1))
            k_slab = _rope_slab(kn[:, sl], c, s1, s2, half)
            k_ref[rows, sl] = k_slab
            if last:
                kt_ref[sl, :] = k_slab[n - WINDOW:].T
                vt_ref[sl, :] = kv[n - WINDOW:, kw + LANES * j:kw + LANES * (j + 1)].T
        qn = q_raw * q_scale * gq_ref[...]
        for j in range(H_B * HD_B // LANES):
            sl = slice(LANES * j, LANES * (j + 1))
            q_ref[rows, sl] = _rope_slab(qn[:, sl], c, s1, s2, half).astype(BF16)

    ins = [stage_in(rows) for rows in blocks]
    items = []
    for kv, q_raw in ins:
        items += [(kv[:, :kw], segk_ref, cntk_ref, expk_ref), (q_raw, segq_ref, cntq_ref, expq_ref)]
    scales = _segment_scales(items)
    for b, (rows, (kv, q_raw)) in enumerate(zip(blocks, ins)):
        stage_out(rows, kv, q_raw, scales[2 * b], scales[2 * b + 1])


def _proj_b(h, tab, wb, tm):
    n = h.shape[0]
    n_tab = tab.shape[1] // tm
    weights = [wb["g_kv"], wb["w_kv"], wb["segk"], wb["cntk"], wb["expk"], wb["gk"],
               wb["g_attn"], wb["w_q"], wb["segq"], wb["cntq"], wb["expq"], wb["gq"]]
    kw = N_KV_B * HD_B
    return pl.pallas_call(
        _proj_b_kernel,
        grid=(n // tm,),
        in_specs=[pl.BlockSpec((tm, D_MODEL), lambda i: (i, 0)),
                  pl.BlockSpec((3, tm, LANES), lambda i: (0, i % n_tab, 0))]
                 + [_const_spec(w.shape) for w in weights],
        out_specs=[pl.BlockSpec((tm, kw), lambda i: (i, 0)),
                   pl.BlockSpec((tm, kw), lambda i: (i, 0)),
                   pl.BlockSpec((tm, H_B * HD_B), lambda i: (i, 0)),
                   pl.BlockSpec((None, kw, WINDOW), lambda i: (i // n_tab, 0, 0)),
                   pl.BlockSpec((None, kw, WINDOW), lambda i: (i // n_tab, 0, 0))],
        out_shape=[jax.ShapeDtypeStruct((n, kw), F32),
                   jax.ShapeDtypeStruct((n, kw), F32),
                   jax.ShapeDtypeStruct((n, H_B * HD_B), BF16),
                   jax.ShapeDtypeStruct((n // (n_tab * tm), kw, WINDOW), F32),
                   jax.ShapeDtypeStruct((n // (n_tab * tm), kw, WINDOW), F32)],
        compiler_params=pltpu.CompilerParams(dimension_semantics=("arbitrary",),
                                             vmem_limit_bytes=VMEM_LIMIT),
        name="swa_proj",
    )(h, tab, *weights)


def _swa_halves(slab, kv, ones_lane=False):
    lane = lax.broadcasted_iota(jnp.int32, slab.shape, 1)
    own = (lane >= HD_B) if kv % 2 else (lane < HD_B)
    halves = [None, None]
    halves[kv % 2] = jnp.where(own, slab, 0.0)
    halves[1 - kv % 2] = pltpu.roll(halves[kv % 2], HD_B, 1)
    if ones_lane:
        halves = [jnp.where(lane == _ones_lane(par), 1.0, h) for par, h in enumerate(halves)]
    return [h.astype(BF16) for h in halves]


def _swa_attend(jobs, valid_of, sink_of):
    scores = [[_dot_nt(q, kh[par]) for par in range(2)] for q, kh, _, _ in jobs]
    probs = []
    for j, (q, _, _, kv) in enumerate(jobs):
        m2 = q.shape[0]
        top = lax.broadcasted_iota(jnp.int32, (m2, 1), 0) < (m2 // 2)
        row = []
        for par in range(2):
            s = jnp.where(valid_of(j), scores[j][par], NEG)
            sink = jnp.where(top, sink_of(G_B * kv + par), sink_of(G_B * kv + par + 2)) * LOG2E
            m = jnp.maximum(jnp.max(s, axis=-1, keepdims=True), sink)
            row.append((jnp.exp2(s - m).astype(BF16), jnp.exp2(sink - m)))
        probs.append(row)
    outs = []
    for j, (q, _, vh, _) in enumerate(jobs):
        o = []
        for par in range(2):
            pv = _dot(probs[j][par][0], vh[par])
            ones = _ones_lane(par)
            o.append(pv / (pv[:, ones:ones + 1] + probs[j][par][1]))
        low_half = lax.broadcasted_iota(jnp.int32, o[0].shape, 1) < HD_B
        outs.append(jnp.where(low_half, o[0], o[1]))
    return outs


def _attn_b_kernel(sink_ref, q_ref, kp_ref, kc_ref, vp_ref, vc_ref, o_ref):
    g = pl.program_id(1)
    kcat = jnp.concatenate([kp_ref[...], kc_ref[...]], axis=0)
    vcat = jnp.concatenate([vp_ref[...], vc_ref[...]], axis=0)
    shape = (2 * WINDOW, 2 * WINDOW)
    qi = lax.broadcasted_iota(jnp.int32, shape, 0) & (WINDOW - 1)
    col = lax.broadcasted_iota(jnp.int32, shape, 1)
    band = (col > qi) & (col <= qi + WINDOW)
    band_first = band & ((col >= WINDOW) | (g > 0))
    n_blocks = q_ref.shape[0] // WINDOW
    for kv in range(N_KV_B):
        base = G_B * HD_B * kv
        ks = slice(LANES * (kv // 2), LANES * (kv // 2 + 1))
        k_half = _swa_halves(kcat[:, ks], kv)
        v_half = _swa_halves(vcat[:, ks], kv, ones_lane=True)
        jobs = []
        for r in range(n_blocks):
            rows = slice(WINDOW * r, WINDOW * (r + 1))
            win = slice(WINDOW * r, WINDOW * (r + 2))
            q_lhs = jnp.concatenate([q_ref[rows, base:base + LANES], q_ref[rows, base + LANES:base + 2 * LANES]], axis=0)
            jobs.append((q_lhs, [h[win] for h in k_half], [h[win] for h in v_half], kv))
        outs = _swa_attend(jobs, lambda r: band if r else band_first, lambda hh: sink_ref[hh])
        for r, o in enumerate(outs):
            rows = slice(WINDOW * r, WINDOW * (r + 1))
            o_ref[rows, base:base + LANES] = o[:WINDOW].astype(BF16)
            o_ref[rows, base + LANES:base + 2 * LANES] = o[WINDOW:].astype(BF16)


def _attn_b(sinks, q, k, v, batch, seq):
    nb = seq // WINDOW
    ng = nb // SWA_BLOCKS
    kw = N_KV_B * HD_B
    prev = lambda b, g: (b * nb + jnp.maximum(SWA_BLOCKS * g - 1, 0), 0)
    cur = lambda b, g: (b * ng + g, 0)
    return pl.pallas_call(
        _attn_b_kernel,
        grid=(batch, ng),
        in_specs=[pl.BlockSpec(memory_space=pltpu.SMEM),
                  pl.BlockSpec((SWA_BLOCKS * WINDOW, H_B * HD_B), cur),
                  pl.BlockSpec((WINDOW, kw), prev), pl.BlockSpec((SWA_BLOCKS * WINDOW, kw), cur),
                  pl.BlockSpec((WINDOW, kw), prev), pl.BlockSpec((SWA_BLOCKS * WINDOW, kw), cur)],
        out_specs=pl.BlockSpec((SWA_BLOCKS * WINDOW, H_B * HD_B), cur),
        out_shape=jax.ShapeDtypeStruct((batch * seq, H_B * HD_B), BF16),
        compiler_params=pltpu.CompilerParams(dimension_semantics=("arbitrary", "arbitrary"),
                                             vmem_limit_bytes=VMEM_LIMIT),
        name="swa_prompt_attn",
    )(sinks, q, k, k, v, v)


def _attn_b_sample_kernel(sink_ref, q_ref, k_ref, v_ref, o_ref, *, t, w_buf):
    keys = k_ref.shape[1]
    rows = 2 * SWA_Q_ROWS
    ti = lax.broadcasted_iota(jnp.int32, (rows, keys), 0) & (SWA_Q_ROWS - 1)
    col = lax.broadcasted_iota(jnp.int32, (rows, keys), 1)
    diff = jnp.where(col < w_buf, ti + w_buf - col, ti - (col - w_buf))
    valid = (diff >= 0) & (diff < WINDOW) & (col < w_buf + t) & (ti < t)
    jobs = []
    for b in range(q_ref.shape[0]):
        for kv in range(N_KV_B):
            ks = slice(LANES * (kv // 2), LANES * (kv // 2 + 1))
            jobs.append((q_ref[b, kv], _swa_halves(k_ref[b, :, ks], kv),
                         _swa_halves(v_ref[b, :, ks], kv, ones_lane=True), kv))
    outs = _swa_attend(jobs, lambda j: valid, lambda hh: sink_ref[hh])
    for j, o in enumerate(outs):
        o_ref[j // N_KV_B, j % N_KV_B] = o


def _attn_b_sample(sinks, q, k_all, v_all, t, w_buf):
    bd = q.shape[0]
    bs = SWA_SAMPLE_SEQS
    blk = lambda a: pl.BlockSpec((bs,) + a.shape[1:], lambda b: (b,) + (0,) * (a.ndim - 1))
    return pl.pallas_call(
        functools.partial(_attn_b_sample_kernel, t=t, w_buf=w_buf),
        grid=(bd // bs,),
        in_specs=[pl.BlockSpec(memory_space=pltpu.SMEM), blk(q), blk(k_all), blk(v_all)],
        out_specs=blk(q),
        out_shape=jax.ShapeDtypeStruct(q.shape, F32),
        compiler_params=pltpu.CompilerParams(dimension_semantics=("arbitrary",),
                                             vmem_limit_bytes=VMEM_LIMIT),
        name="swa_sample_attn",
    )(sinks, q, k_all, v_all)


def _rope_tables(pos, n_rot, period, lane_lo):
    half = n_rot // 2
    inv = ROPE_THETA ** (-np.arange(0, n_rot, 2, dtype=np.float64) / n_rot)
    ang = np.asarray(pos, np.float64)[:, None] * inv[None, :]
    cos, sin = np.cos(ang), np.sin(ang)
    rel = np.arange(LANES) % period - lane_lo
    in1 = (rel >= 0) & (rel < half)
    in2 = (rel >= half) & (rel < 2 * half)
    idx = np.where(in1, rel, np.where(in2, rel - half, 0))
    cg, sg = cos[:, idx], sin[:, idx]
    tables = np.stack([np.where(in1 | in2, cg, 1.0), np.where(in1, -sg, 0.0), np.where(in2, sg, 0.0)])
    return jnp.asarray(tables, F32)


def _rope_tables_dup(pos, n_rot, lane_lo):
    half = n_rot // 2
    inv = ROPE_THETA ** (-np.arange(0, n_rot, 2, dtype=np.float64) / n_rot)
    ang = np.asarray(pos, np.float64)[:, None] * inv[None, :]
    cos, sin = np.cos(ang), np.sin(ang)
    rel = np.arange(LANES) - lane_lo
    in1 = (rel >= 0) & (rel < half)
    in2 = (rel >= half) & (rel < 2 * half)
    idx = np.where(in1, rel, np.where(in2, rel - half, 0))
    cg, sg = cos[:, idx], sin[:, idx]
    tables = np.stack([np.where(in1 | in2, cg, np.where(rel < 0, 1.0, 0.0)),
                       np.where(in1, -sg, np.where(in2, sg, 0.0))])
    return jnp.asarray(tables, F32)


def _segments(width, seg_lanes, extra=None):
    seg = np.zeros((width, LANES), np.float32)
    spread = np.zeros((LANES, width), np.float32)
    cnt = np.zeros((1, LANES), np.float32)
    for s, (lo, hi) in enumerate(seg_lanes):
        seg[lo:hi, s] = 1.0
        spread[s, lo:hi + (extra[s] if extra else 0)] = 1.0
        cnt[0, s] = 1.0 / (hi - lo)
    expand = np.concatenate([spread, spread], axis=0)
    return jnp.asarray(seg, BF16), jnp.asarray(cnt, F32), jnp.asarray(expand, BF16)


def _slab_gain(parts):
    pieces, pos = [], 0
    for lo, vals in parts:
        pieces += [jnp.zeros((lo - pos,), F32), vals.astype(F32)]
        pos = lo + vals.shape[0]
    pieces.append(jnp.zeros((SLAB - pos,), F32))
    return jnp.concatenate(pieces)[None, :]


def _prep_a(norm_attn, w_a_in, g_qc, w_uq, g_ckv, w_uk, w_uv, g_qn, g_qr, g_kn, g_kr):
    half = D_ROPE // 2
    tail = SLAB - KPE_LANE - D_ROPE - half
    w_kpe = w_a_in[:, D_QC + D_C:]
    w_in = jnp.concatenate([w_a_in[:, :D_QC + D_C], jnp.zeros((D_MODEL, KPE_LANE), F32),
                            w_kpe, w_kpe[:, :half], jnp.zeros((D_MODEL, tail), F32)], axis=1)
    dqk = D_NOPE + D_ROPE
    w_uq3 = w_uq.reshape(D_QC, H_A, dqk)
    w_uq_pad = jnp.concatenate([w_uq3, w_uq3[:, :, D_NOPE:D_NOPE + half], jnp.zeros((D_QC, H_A, tail), F32)],
                               axis=2).reshape(D_QC, H_A * SLAB)
    w_uk3 = w_uk.reshape(D_C, H_A, D_NOPE)
    w_uk_pad = jnp.pad(w_uk3, ((0, 0), (0, 0), (0, SLAB - D_NOPE))).reshape(D_C, H_A * SLAB)
    w_uv3 = w_uv.reshape(D_C, H_A // 2, 2, D_V)
    even = jnp.pad(w_uv3[:, :, 0], ((0, 0), (0, 0), (0, SLAB - D_V)))
    odd = jnp.pad(w_uv3[:, :, 1], ((0, 0), (0, 0), (SLAB - D_V, 0)))
    w_uv_pad = jnp.stack([even, odd], axis=2).reshape(D_C, H_A * SLAB)
    v_ones = np.zeros((1, H_A * SLAB), np.float32)
    for h in range(H_A):
        v_ones[0, SLAB * h + _ones_lane(h % 2)] = 1.0
    q_segs = []
    for h in range(H_A):
        q_segs += [(SLAB * h, SLAB * h + D_NOPE), (SLAB * h + KPE_LANE, SLAB * h + KPE_LANE + D_ROPE)]
    segq, cntq, expq = _segments(H_A * SLAB, q_segs, extra=[0, half] * H_A)
    q_scale = SCALE_A * LOG2E
    gq = jnp.tile(_slab_gain([(0, g_qn * q_scale), (KPE_LANE, g_qr * q_scale),
                              (KPE_LANE + D_ROPE, g_qr[:half] * q_scale)]), (1, H_A))
    gk_slab = _slab_gain([(0, g_kn)])
    pe_mask = np.zeros((1, SLAB), np.float32)
    pe_mask[0, KPE_LANE:KPE_LANE + D_ROPE] = 1.0
    wukt = jnp.pad(jnp.transpose(w_uk3, (1, 2, 0)), ((0, 0), (0, SLAB - D_NOPE), (0, 0)))
    return dict(
        g_attn=norm_attn[None, :], w_in=w_in.astype(BF16), g_qc=g_qc[None, :], g_ckv=g_ckv[None, :],
        g_kpe=_slab_gain([(KPE_LANE, g_kr), (KPE_LANE + D_ROPE, g_kr[:half])]), pe_mask=jnp.asarray(pe_mask),
        w_uq=w_uq_pad.astype(BF16), segq=segq, cntq=cntq, expq=expq, gq=gq,
        w_uk=w_uk_pad.astype(BF16),
        w_uv=w_uv_pad.astype(BF16), v_ones=jnp.asarray(v_ones), gk_slab=gk_slab, wukt_pad=wukt.astype(BF16),
        wukt=jnp.transpose(w_uk3, (2, 1, 0)).reshape(H_A * D_NOPE, D_C).astype(BF16))


def _prep_b(g_kv, w_kv, g_k, norm_attn, w_q, g_q):
    kw = N_KV_B * HD_B
    segk, cntk, expk = _segments(kw, [(HD_B * h, HD_B * (h + 1)) for h in range(N_KV_B)])
    segq, cntq, expq = _segments(H_B * HD_B, [(HD_B * h, HD_B * (h + 1)) for h in range(H_B)])
    return dict(g_kv=g_kv[None, :], w_kv=w_kv.astype(BF16), segk=segk, cntk=cntk, expk=expk,
                gk=jnp.tile(g_k, N_KV_B)[None, :], g_attn=norm_attn[None, :], w_q=w_q.astype(BF16),
                segq=segq, cntq=cntq, expq=expq, gq=jnp.tile(g_q * (SCALE_B * LOG2E), H_B)[None, :])


def kernel(x_prompt, x_sample, cache_mla, state_win_k, state_win_v, page_table, norm_attn, norm_ffn, w_a_in, g_qc, w_uq, g_ckv, w_uk, w_uv, g_qn_a, g_qr_a, g_kn_a, g_kr_a, w_a_out, g_kv_shared, w_kv_shared, g_k_b, w_q_b, g_q_b, sinks, w_b_out, w_ffn_in, w_ffn_out):
    batch, seq, _ = x_prompt.shape
    bd, t_dec, _ = x_sample.shape
    past_len = page_table.shape[1] * PAGE_SIZE
    w_buf = state_win_k.shape[1]
    kw = N_KV_B * HD_B
    assert w_a_in.shape[0] == 1 and w_q_b.shape[0] == 1, "one MLA layer followed by one SWA layer"
    assert w_buf == WINDOW and (bd * t_dec) % 8 == 0
    assert all(seq % t == 0 for t in (TM_MLA_PROJ, TM_POST, TM_SWA_PROJ, TQ_MLA, SWA_BLOCKS * WINDOW))
    assert seq >= WINDOW and bd * t_dec >= WINDOW, "the window tail is taken from the last 128 rows of a tile"
    assert t_dec <= SWA_Q_ROWS and bd % SWA_SAMPLE_SEQS == 0 and page_table.shape[1] % (2 * PAGES_PER_GROUP) == 0

    wa = _prep_a(norm_attn[0], w_a_in[0], g_qc[0], w_uq[0], g_ckv[0], w_uk[0], w_uv[0],
                 g_qn_a[0], g_qr_a[0], g_kn_a[0], g_kr_a[0])
    wb = _prep_b(g_kv_shared, w_kv_shared, g_k_b, norm_attn[1], w_q_b[0], g_q_b[0])
    w_a_out_b = w_a_out[0].astype(BF16)
    w_b_out_b = w_b_out[0].astype(BF16)
    ffn_in = w_ffn_in.astype(BF16)
    ffn_out = w_ffn_out.astype(BF16)
    g_ffn = norm_ffn[:, None, :]
    sink_b = sinks[0]

    pos_p = np.arange(seq)
    n_s = bd * t_dec
    pos_s = past_len + np.arange(n_s) % t_dec

    xp = x_prompt.reshape(batch * seq, D_MODEL)
    q, k, v, rows_p = _proj_a(xp, _rope_tables_dup(pos_p, D_ROPE, KPE_LANE), wa, TM_MLA_PROJ)
    o = _attn_a(q, k, v, batch, seq, TQ_MLA)
    h = _post(xp, o, w_a_out_b, g_ffn, ffn_in, ffn_out, 0, TM_POST)
    k_p, v_p, q_b, kt_p, vt_p = _proj_b(h, _rope_tables(pos_p, ROT_B, HD_B, 0), wb, TM_SWA_PROJ)
    o = _attn_b(sink_b, q_b, k_p, v_p, batch, seq)
    y_prompt = _post(h, o, w_b_out_b, g_ffn, ffn_in, ffn_out, 1, TM_POST)

    xs = x_sample.reshape(n_s, D_MODEL)
    q, _, _, rows_s = _proj_a(xs, _rope_tables_dup(pos_s, D_ROPE, KPE_LANE), wa, n_s)
    qabs = _qabs(q, wa["gk_slab"], wa["wukt_pad"]).reshape(bd, t_dec * H_A, D_C)
    qpe = q.reshape(n_s, H_A, SLAB)[:, :, KPE_LANE:KPE_LANE + D_ROPE].reshape(bd, t_dec * H_A, D_ROPE)
    new_pad = jnp.pad(jnp.swapaxes(rows_s.reshape(bd, t_dec, D_CKV), 1, 2), ((0, 0), (0, 0), (0, PAGE_SIZE - t_dec)))
    olat = _paged_attn(page_table, jnp.swapaxes(cache_mla, 2, 3), wa["wukt"], qabs, qpe, new_pad, t_dec)
    o = _latent_out(olat.reshape(n_s, H_A * D_C), wa["w_uv"])
    h = _post(xs, o, w_a_out_b, g_ffn, ffn_in, ffn_out, 0, n_s)
    k_s, v_s, q_b, _, _ = _proj_b(h, _rope_tables(pos_s, ROT_B, HD_B, 0), wb, n_s)
    q4 = jnp.transpose(q_b.reshape(bd, t_dec, N_KV_B, 2, LANES), (0, 2, 3, 1, 4))
    q4 = jnp.pad(q4, ((0, 0), (0, 0), (0, 0), (0, SWA_Q_ROWS - t_dec), (0, 0))).reshape(bd, N_KV_B, 2 * SWA_Q_ROWS, LANES)
    key_pad = jnp.zeros((bd, 2 * SWA_Q_ROWS - t_dec, kw), F32)
    k_all = jnp.concatenate([state_win_k.reshape(bd, w_buf, kw), k_s.reshape(bd, t_dec, kw), key_pad], axis=1)
    v_all = jnp.concatenate([state_win_v.reshape(bd, w_buf, kw), v_s.reshape(bd, t_dec, kw), key_pad], axis=1)
    o4 = _attn_b_sample(sink_b, q4, k_all, v_all, t_dec, w_buf)
    o = jnp.transpose(o4.reshape(bd, N_KV_B, 2, SWA_Q_ROWS, LANES)[:, :, :, :t_dec], (0, 3, 1, 2, 4))
    y_sample = _post(h, o.reshape(n_s, H_B * HD_B).astype(BF16), w_b_out_b, g_ffn, ffn_in, ffn_out, 1, n_s)

    win_k_p = jnp.transpose(kt_p.reshape(batch, N_KV_B, HD_B, WINDOW), (0, 3, 1, 2))
    win_v_p = jnp.transpose(vt_p.reshape(batch, N_KV_B, HD_B, WINDOW), (0, 3, 1, 2))
    win_k_s = jnp.concatenate([state_win_k, k_s.reshape(bd, t_dec, N_KV_B, HD_B)], axis=1)[:, -w_buf:]
    win_v_s = jnp.concatenate([state_win_v, v_s.reshape(bd, t_dec, N_KV_B, HD_B)], axis=1)[:, -w_buf:]
    return (y_prompt.reshape(batch, seq, D_MODEL), y_sample.reshape(bd, t_dec, D_MODEL),
            rows_p.reshape(1, batch, seq, D_CKV), rows_s.reshape(1, bd, t_dec, D_CKV),
            win_k_p, win_v_p, win_k_s, win_v_s)
```

```python
import functools

import numpy as np
import jax
import jax.numpy as jnp
from jax import lax
from jax.experimental import pallas as pl
from jax.experimental.pallas import tpu as pltpu

F32 = jnp.float32
BF16 = jnp.bfloat16

D_MODEL = 1024
PAGE_SIZE = 128
H_A = 16
D_NOPE = 64
D_ROPE = 32
D_V = 64
D_QC = 384
D_C = 256
D_CKV = D_C + D_ROPE
SCALE_A = (D_NOPE + D_ROPE) ** -0.5
H_B = 16
N_KV_B = 4
HD_B = 64
G_B = H_B // N_KV_B
WINDOW = 128
ROT_B = HD_B // 4
SCALE_B = HD_B ** -0.5
D_FF = 2816
ROPE_THETA = 500000.0
EPS = 1e-6
NEG = -1e30
LOG2E = 1.4426950408889634

LANES = 128
MXU_DIM = 256
VMEM_LIMIT = 56 * 1024 * 1024
SLAB = 128
KPE_LANE = 64

FF_CHUNK = MXU_DIM
TM_MLA_PROJ = 512
PROJ_SUB = 256
TM_SWA_PROJ = 1024
SWA_PROJ_SUB = 512
TM_POST = 512
TQ_MLA = MXU_DIM
ATTN_LOOKAHEAD = 3
PAGES_PER_GROUP = 16
PAGES_PER_DOT = MXU_DIM // PAGE_SIZE
PAGED_SEQS = 4
ACC_AFTER_DOTS = 4
SWA_SAMPLE_SEQS = 8
SWA_BLOCKS = 8
SWA_Q_ROWS = 8

_NT = (((1,), (1,)), ((), ()))


def _dot(a, b):
    return jnp.dot(a, b, preferred_element_type=F32)


def _dot_nt(a, b):
    return lax.dot_general(a, b, _NT, preferred_element_type=F32)


def _rms(x, g):
    ms = jnp.mean(x * x, axis=-1, keepdims=True)
    return x * lax.rsqrt(ms + EPS) * g


def _rope_slab(x, c, s1, s2, half):
    return x * c + pltpu.roll(x, LANES - half, 1) * s1 + pltpu.roll(x, half, 1) * s2


def _rope_slab_dup(x, c, s, half):
    return x * c + pltpu.roll(x, LANES - half, 1) * s


def _segment_scales(items):
    sums = [_dot((raw * raw).astype(BF16), seg_ref[...]) for raw, seg_ref, _, _ in items]
    scales = []
    for ss, (_, _, inv_cnt_ref, expand_ref) in zip(sums, items):
        rs = lax.rsqrt(ss * inv_cnt_ref[...] + EPS)
        hi = rs.astype(BF16)
        lo = (rs - hi.astype(F32)).astype(BF16)
        scales.append(_dot(jnp.concatenate([hi, lo], axis=1), expand_ref[...]))
    return scales


def _proj_a_kernel(x_ref, tab_ref, g_attn_ref, w_in_ref, g_qc_ref, g_ckv_ref, g_kpe_ref, pe_mask_ref,
                   w_uq_ref, segq_ref, cntq_ref, expq_ref, gq_ref,
                   w_uk_ref, gk_ref, w_uv_ref, v_ones_ref,
                   q_ref, k_ref, v_ref, rows_ref):
    tm = x_ref.shape[0]
    blocks = [slice(r, r + min(PROJ_SUB, tm)) for r in range(0, tm, PROJ_SUB)]
    half = D_ROPE // 2

    def stage_in(rows):
        hn = _rms(x_ref[rows, :], g_attn_ref[...]).astype(BF16)
        return _dot(hn, w_in_ref[...])

    def stage_latents(rows, a):
        c, s = tab_ref[0, rows, :], tab_ref[1, rows, :]
        cq = _rms(a[:, :D_QC], g_qc_ref[...]).astype(BF16)
        ckv = _rms(a[:, D_QC:D_QC + D_C], g_ckv_ref[...])
        kpe = a[:, D_QC + D_C:]
        ms = jnp.sum(kpe * kpe * pe_mask_ref[...], axis=-1, keepdims=True) * (1.0 / D_ROPE)
        kpe = _rope_slab_dup(kpe * lax.rsqrt(ms + EPS) * g_kpe_ref[...], c, s, half)
        rows_ref[rows, :D_C] = ckv
        rows_ref[rows, D_C:] = kpe[:, KPE_LANE:KPE_LANE + D_ROPE]
        ckv_b = ckv.astype(BF16)
        q_raw = _dot(cq, w_uq_ref[...])
        k_raw = _dot(ckv_b, w_uk_ref[...])
        v_ref[rows, :] = (_dot(ckv_b, w_uv_ref[...]) + v_ones_ref[...]).astype(BF16)
        return q_raw, k_raw, kpe

    def stage_out(rows, q_raw, q_scale, k_raw, kpe):
        c, s = tab_ref[0, rows, :], tab_ref[1, rows, :]
        qn = q_raw * q_scale * gq_ref[...]
        for h in range(H_A):
            sl = slice(SLAB * h, SLAB * (h + 1))
            q_ref[rows, sl] = _rope_slab_dup(qn[:, sl], c, s, half).astype(BF16)
            k_h = k_raw[:, sl]
            ms_h = jnp.sum(k_h * k_h, axis=-1, keepdims=True) * (1.0 / D_NOPE)
            k_ref[rows, sl] = (k_h * lax.rsqrt(ms_h + EPS) * gk_ref[...] + kpe).astype(BF16)

    a_s = [stage_in(rows) for rows in blocks]
    mids = [stage_latents(rows, a) for rows, a in zip(blocks, a_s)]
    q_scales = _segment_scales([(q_raw, segq_ref, cntq_ref, expq_ref) for q_raw, _, _ in mids])
    for rows, (q_raw, k_raw, kpe), q_scale in zip(blocks, mids, q_scales):
        stage_out(rows, q_raw, q_scale, k_raw, kpe)


def _const_spec(shape):
    zeros = (0,) * len(shape)
    return pl.BlockSpec(shape, lambda *_: zeros, pipeline_mode=pl.Buffered(1))


def _proj_a(x, tab, wa, tm):
    n = x.shape[0]
    n_tab = tab.shape[1] // tm
    weights = [wa["g_attn"], wa["w_in"], wa["g_qc"], wa["g_ckv"], wa["g_kpe"], wa["pe_mask"],
               wa["w_uq"], wa["segq"], wa["cntq"], wa["expq"], wa["gq"],
               wa["w_uk"], wa["gk_slab"], wa["w_uv"], wa["v_ones"]]
    wide = H_A * SLAB
    return pl.pallas_call(
        _proj_a_kernel,
        grid=(n // tm,),
        in_specs=[pl.BlockSpec((tm, D_MODEL), lambda i: (i, 0)),
                  pl.BlockSpec((tab.shape[0], tm, LANES), lambda i: (0, i % n_tab, 0))]
                 + [_const_spec(w.shape) for w in weights],
        out_specs=[pl.BlockSpec((tm, wide), lambda i: (i, 0)),
                   pl.BlockSpec((tm, wide), lambda i: (i, 0)),
                   pl.BlockSpec((tm, wide), lambda i: (i, 0)),
                   pl.BlockSpec((tm, D_CKV), lambda i: (i, 0))],
        out_shape=[jax.ShapeDtypeStruct((n, wide), BF16),
                   jax.ShapeDtypeStruct((n, wide), BF16),
                   jax.ShapeDtypeStruct((n, wide), BF16),
                   jax.ShapeDtypeStruct((n, D_CKV), F32)],
        compiler_params=pltpu.CompilerParams(dimension_semantics=("arbitrary",),
                                             vmem_limit_bytes=VMEM_LIMIT),
        name="mla_proj",
    )(x, tab, *weights)


def _ones_lane(parity):
    return D_V if parity == 0 else 0


def _attn_a_kernel(q_ref, k_ref, v_ref, o_ref, *, tq):
    seq = q_ref.shape[0]
    causal = (lax.broadcasted_iota(jnp.int32, (tq, tq), 1) <= lax.broadcasted_iota(jnp.int32, (tq, tq), 0))
    low_half = lax.broadcasted_iota(jnp.int32, (tq, SLAB), 1) < D_V
    jobs = [(c, e) for c in range(seq // tq) for e in range(2)]

    def windows(c, e):
        return slice(c * tq, (c + 1) * tq), slice(0, c * tq), slice(SLAB * e, SLAB * (e + 1))

    def score(c, e):
        rows, past, ls = windows(c, e)
        q = q_ref[rows, ls]
        s_d = _dot_nt(q, k_ref[rows, ls])
        return s_d, (_dot_nt(q, k_ref[past, ls]) if c else None)

    def attend(c, e, s_d, s_p):
        rows, past, ls = windows(c, e)
        s_d = jnp.where(causal, s_d, NEG)
        m = jnp.max(s_d, axis=-1, keepdims=True)
        if c:
            m = jnp.maximum(m, jnp.max(s_p, axis=-1, keepdims=True))
        acc = _dot(jnp.exp2(s_d - m).astype(BF16), v_ref[rows, ls])
        if c:
            acc = acc + _dot(jnp.exp2(s_p - m).astype(BF16), v_ref[past, ls])
        ones = _ones_lane(e)
        return acc / acc[:, ones:ones + 1]

    ahead = [score(*jobs[j]) for j in range(min(ATTN_LOOKAHEAD, len(jobs)))]
    out = None
    for j, (c, e) in enumerate(jobs):
        if j + ATTN_LOOKAHEAD < len(jobs):
            ahead.append(score(*jobs[j + ATTN_LOOKAHEAD]))
        o_e = attend(c, e, *ahead[j])
        ahead[j] = None
        if e == 0:
            out = o_e
        else:
            o_ref[c * tq:(c + 1) * tq, :] = jnp.where(low_half, out, o_e).astype(BF16)


def _attn_a(q, k, v, batch, seq, tq):
    pairs = H_A // 2
    return pl.pallas_call(
        functools.partial(_attn_a_kernel, tq=tq),
        grid=(batch, pairs),
        in_specs=[pl.BlockSpec((seq, 2 * SLAB), lambda b, j: (b, j)),
                  pl.BlockSpec((seq, 2 * SLAB), lambda b, j: (b, j)),
                  pl.BlockSpec((seq, 2 * SLAB), lambda b, j: (b, j))],
        out_specs=pl.BlockSpec((seq, SLAB), lambda b, j: (b, j)),
        out_shape=jax.ShapeDtypeStruct((batch * seq, H_A * D_V), BF16),
        compiler_params=pltpu.CompilerParams(
            dimension_semantics=("arbitrary", "arbitrary"),
            vmem_limit_bytes=VMEM_LIMIT),
        name="mla_prompt_attn",
    )(q, k, v)


def _qabs_kernel(q_ref, gk_ref, wukt_ref, o_ref):
    for h in range(H_A):
        qs = (q_ref[:, SLAB * h:SLAB * (h + 1)].astype(F32) * gk_ref[...]).astype(BF16)
        o_ref[:, D_C * h:D_C * (h + 1)] = _dot(qs, wukt_ref[h]).astype(BF16)


def _qabs(q, gk_slab, wukt):
    n = q.shape[0]
    return pl.pallas_call(
        _qabs_kernel,
        out_shape=jax.ShapeDtypeStruct((n, H_A * D_C), BF16),
        compiler_params=pltpu.CompilerParams(vmem_limit_bytes=VMEM_LIMIT),
        name="mla_absorb_q",
    )(q, gk_slab, wukt)


def _paged_kernel(pt_ref, cache_ref, wukt_ref, qabs_ref, qpe_ref, new_ref, o_ref, lhs_sc, pg_sc, sem, *, n_pages, t_new):
    step = pl.program_id(0)
    n_seq, rows_q = qabs_ref.shape[0], qabs_ref.shape[1]
    group = pg_sc.shape[1]
    n_groups = n_pages // group
    units = [(j, g) for j in range(n_seq) for g in range(n_groups)]

    def page_copy(j, g, u):
        slot = g % 2
        page = pt_ref[step * n_seq + j, g * group + u]
        return pltpu.make_async_copy(cache_ref.at[0, page], pg_sc.at[slot, u], sem.at[slot])

    def start_group(j, g):
        for u in range(group):
            page_copy(j, g, u).start()

    def wait_group(j, g):
        for u in range(group):
            page_copy(j, g, u).wait()

    @pl.when(step == 0)
    def _():
        start_group(0, 0)

    for j in range(n_seq):
        lhs_sc[j, :H_A * D_NOPE, :] = wukt_ref[...]
        lhs_sc[j, H_A * D_NOPE:, :] = qabs_ref[j]

    def nope_scores(j, ct):
        keys = ct.shape[1]
        big = _dot(lhs_sc[j], ct)
        kt = big[:H_A * D_NOPE]
        ssq = jnp.sum((kt * kt).reshape(D_NOPE, H_A, keys), axis=0)
        rs = lax.rsqrt(ssq * (1.0 / D_NOPE) + EPS)
        rs_q = jnp.concatenate([rs] * (rows_q // H_A), axis=0)
        return big[H_A * D_NOPE:] * rs_q

    def rope_scores(j, kpets):
        return _dot(qpe_ref[j], jnp.concatenate(kpets, axis=1))

    def accumulate(state, s, ct):
        m_old, l, acc = state
        m_new = jnp.maximum(m_old, jnp.max(s, axis=-1, keepdims=True))
        corr = jnp.exp2(m_old - m_new)
        p = jnp.exp2(s - m_new)
        return (m_new, l * corr + jnp.sum(p, axis=-1, keepdims=True),
                acc * corr + _dot_nt(p.astype(BF16), ct))

    def finish(j, state):
        keys = new_ref.shape[2]
        t_row = lax.shift_right_logical(lax.broadcasted_iota(jnp.int32, (rows_q, keys), 0), H_A.bit_length() - 1)
        s_col = lax.broadcasted_iota(jnp.int32, (rows_q, keys), 1)
        ct = new_ref[j, :D_C, :].astype(BF16)
        s = nope_scores(j, ct) + rope_scores(j, [new_ref[j, D_C:, :].astype(BF16)])
        _, l, acc = accumulate(state, jnp.where((s_col <= t_row) & (s_col < t_new), s, NEG), ct)
        o_ref[j] = acc / l

    def flush(pending, states):
        pj, last, s, ct = pending
        states[pj] = accumulate(states[pj], s, ct)
        if last:
            finish(pj, states[pj])

    states = [(jnp.full((rows_q, 1), NEG, F32), jnp.zeros((rows_q, 1), F32), jnp.zeros((rows_q, D_C), F32))
              for _ in range(n_seq)]
    pending = None
    for idx, (j, g) in enumerate(units):
        wait_group(j, g)
        if idx + 1 < len(units):
            start_group(*units[idx + 1])
        else:
            @pl.when(step + 1 < pl.num_programs(0))
            def _():
                start_group(n_seq, 0)
        slot = g % 2
        cts, kpets = [], []
        for u in range(0, group, PAGES_PER_DOT):
            pages = range(u, u + PAGES_PER_DOT)
            cts.append(jnp.concatenate([pg_sc[slot, v, :D_C, :] for v in pages], axis=1).astype(BF16))
            kpets.append(jnp.concatenate([pg_sc[slot, v, D_C:, :] for v in pages], axis=1).astype(BF16))
        nope = []
        for i, ct in enumerate(cts):
            if i == ACC_AFTER_DOTS and pending is not None:
                flush(pending, states)
            nope.append(nope_scores(j, ct))
        pending = (j, g == n_groups - 1, jnp.concatenate(nope, axis=1) + rope_scores(j, kpets),
                   jnp.concatenate(cts, axis=1))
    flush(pending, states)


def _paged_attn(page_table, cache, wukt, qabs, qpe, new_pad, t_new):
    bd, n_pages = page_table.shape
    rows_q = qabs.shape[1]
    n_seq = PAGED_SEQS
    assert (n_pages // PAGES_PER_GROUP) % 2 == 0, "the slot of a page group must not depend on the sequence"
    assert ACC_AFTER_DOTS < PAGES_PER_GROUP // PAGES_PER_DOT and bd % n_seq == 0
    per_seq = lambda a: pl.BlockSpec((n_seq,) + a.shape[1:], lambda b, pt: (b, 0, 0))
    grid_spec = pltpu.PrefetchScalarGridSpec(
        num_scalar_prefetch=1,
        grid=(bd // n_seq,),
        in_specs=[pl.BlockSpec(memory_space=pl.ANY),
                  pl.BlockSpec(wukt.shape, lambda b, pt: (0, 0)),
                  per_seq(qabs), per_seq(qpe), per_seq(new_pad)],
        out_specs=pl.BlockSpec((n_seq, rows_q, D_C), lambda b, pt: (b, 0, 0)),
        scratch_shapes=[pltpu.VMEM((n_seq, H_A * D_NOPE + rows_q, D_C), BF16),
                        pltpu.VMEM((2, PAGES_PER_GROUP, D_CKV, PAGE_SIZE), F32),
                        pltpu.SemaphoreType.DMA((2,))])
    return pl.pallas_call(
        functools.partial(_paged_kernel, n_pages=n_pages, t_new=t_new),
        grid_spec=grid_spec,
        out_shape=jax.ShapeDtypeStruct((bd, rows_q, D_C), F32),
        compiler_params=pltpu.CompilerParams(dimension_semantics=("arbitrary",),
                                             vmem_limit_bytes=VMEM_LIMIT),
        name="mla_paged_attn",
    )(page_table, cache, wukt, qabs, qpe, new_pad)


def _latent_out_kernel(olat_ref, w_uv_ref, o_ref):
    for j in range(H_A // 2):
        acc = None
        for e in range(2):
            h = 2 * j + e
            part = _dot(olat_ref[:, D_C * h:D_C * (h + 1)].astype(BF16),
                        w_uv_ref[:, SLAB * h:SLAB * (h + 1)])
            acc = part if acc is None else acc + part
        o_ref[:, SLAB * j:SLAB * (j + 1)] = acc.astype(BF16)


def _latent_out(olat, w_uv_pad):
    n = olat.shape[0]
    return pl.pallas_call(
        _latent_out_kernel,
        out_shape=jax.ShapeDtypeStruct((n, H_A * D_V), BF16),
        compiler_params=pltpu.CompilerParams(vmem_limit_bytes=VMEM_LIMIT),
        name="mla_latent_out",
    )(olat, w_uv_pad)


def _post_kernel(x_ref, o_ref, w_o_ref, g_ref, w_in_ref, w_out_ref, y_ref):
    h1 = x_ref[...] + _dot(o_ref[...], w_o_ref[...])
    hn = _rms(h1, g_ref[...]).astype(BF16)
    acc = h1
    for c in range(D_FF // FF_CHUNK):
        lo = c * FF_CHUNK
        a1 = _dot(hn, w_in_ref[:, lo:lo + FF_CHUNK])
        a2 = _dot(hn, w_in_ref[:, D_FF + lo:D_FF + lo + FF_CHUNK])
        gate = (a1 * jax.nn.sigmoid(a1)) * a2
        acc = acc + _dot(gate.astype(BF16), w_out_ref[lo:lo + FF_CHUNK, :])
    y_ref[...] = acc


def _layer_spec(stacked, layer):
    zeros = (0,) * (stacked.ndim - 1)
    return pl.BlockSpec((None,) + stacked.shape[1:], lambda *_: (layer,) + zeros, pipeline_mode=pl.Buffered(1))


def _post(x, o, w_o, g, w_in, w_out, layer, tm):
    n = x.shape[0]
    return pl.pallas_call(
        _post_kernel,
        grid=(n // tm,),
        in_specs=[pl.BlockSpec((tm, D_MODEL), lambda i: (i, 0)),
                  pl.BlockSpec((tm, o.shape[1]), lambda i: (i, 0)),
                  _const_spec(w_o.shape), _layer_spec(g, layer),
                  _layer_spec(w_in, layer), _layer_spec(w_out, layer)],
        out_specs=pl.BlockSpec((tm, D_MODEL), lambda i: (i, 0)),
        out_shape=jax.ShapeDtypeStruct((n, D_MODEL), F32),
        compiler_params=pltpu.CompilerParams(dimension_semantics=("arbitrary",),
                                             vmem_limit_bytes=VMEM_LIMIT),
        name="outproj_swiglu",
    )(x, o, w_o, g, w_in, w_out)


def _proj_b_kernel(h_ref, tab_ref, g_kv_ref, w_kv_ref, segk_ref, cntk_ref, expk_ref, gk_ref,
                   g_attn_ref, w_q_ref, segq_ref, cntq_ref, expq_ref, gq_ref,
                   k_ref, v_ref, q_ref, kt_ref, vt_ref):
    tm = h_ref.shape[0]
    blocks = [slice(r, r + min(SWA_PROJ_SUB, tm)) for r in range(0, tm, SWA_PROJ_SUB)]
    kw = N_KV_B * HD_B
    half = ROT_B // 2

    def stage_in(rows):
        h = h_ref[rows, :]
        hr = h * lax.rsqrt(jnp.mean(h * h, axis=-1, keepdims=True) + EPS)
        kv = _dot((hr * g_kv_ref[...]).astype(BF16), w_kv_ref[...])
        q_raw = _dot((hr * g_attn_ref[...]).astype(BF16), w_q_ref[...])
        v_ref[rows, :] = kv[:, kw:]
        return kv, q_raw

    def stage_out(rows, kv, q_raw, k_scale, q_scale):
        c, s1, s2 = tab_ref[0, rows, :], tab_ref[1, rows, :], tab_ref[2, rows, :]
        last = rows.stop == tm
        n = rows.stop - rows.start
        kn = kv[:, :kw] * k_scale * gk_ref[...]
        for j in range(kw // LANES):
            sl = slice(LANES * j, LANES * (j + 1))
            k_slab = _rope_slab(kn[:, sl], c, s1, s2, half)
            k_ref[rows, sl] = k_slab
            if last:
                kt_ref[sl, :] = k_slab[n - WINDOW:].T
                vt_ref[sl, :] = kv[n - WINDOW:, kw + LANES * j:kw + LANES * (j + 1)].T
        qn = q_raw * q_scale * gq_ref[...]
        for j in range(H_B * HD_B // LANES):
            sl = slice(LANES * j, LANES * (j + 1))
            q_ref[rows, sl] = _rope_slab(qn[:, sl], c, s1, s2, half).astype(BF16)

    ins = [stage_in(rows) for rows in blocks]
    items = []
    for kv, q_raw in ins:
        items += [(kv[:, :kw], segk_ref, cntk_ref, expk_ref), (q_raw, segq_ref, cntq_ref, expq_ref)]
    scales = _segment_scales(items)
    for b, (rows, (kv, q_raw)) in enumerate(zip(blocks, ins)):
        stage_out(rows, kv, q_raw, scales[2 * b], scales[2 * b + 1])


def _proj_b(h, tab, wb, tm):
    n = h.shape[0]
    n_tab = tab.shape[1] // tm
    weights = [wb["g_kv"], wb["w_kv"], wb["segk"], wb["cntk"], wb["expk"], wb["gk"],
               wb["g_attn"], wb["w_q"], wb["segq"], wb["cntq"], wb["expq"], wb["gq"]]
    kw = N_KV_B * HD_B
    return pl.pallas_call(
        _proj_b_kernel,
        grid=(n // tm,),
        in_specs=[pl.BlockSpec((tm, D_MODEL), lambda i: (i, 0)),
                  pl.BlockSpec((3, tm, LANES), lambda i: (0, i % n_tab, 0))]
                 + [_const_spec(w.shape) for w in weights],
        out_specs=[pl.BlockSpec((tm, kw), lambda i: (i, 0)),
                   pl.BlockSpec((tm, kw), lambda i: (i, 0)),
                   pl.BlockSpec((tm, H_B * HD_B), lambda i: (i, 0)),
                   pl.BlockSpec((None, kw, WINDOW), lambda i: (i // n_tab, 0, 0)),
                   pl.BlockSpec((None, kw, WINDOW), lambda i: (i // n_tab, 0, 0))],
        out_shape=[jax.ShapeDtypeStruct((n, kw), F32),
                   jax.ShapeDtypeStruct((n, kw), F32),
                   jax.ShapeDtypeStruct((n, H_B * HD_B), BF16),
                   jax.ShapeDtypeStruct((n // (n_tab * tm), kw, WINDOW), F32),
                   jax.ShapeDtypeStruct((n // (n_tab * tm), kw, WINDOW), F32)],
        compiler_params=pltpu.CompilerParams(dimension_semantics=("arbitrary",),
                                             vmem_limit_bytes=VMEM_LIMIT),
        name="swa_proj",
    )(h, tab, *weights)


def _swa_halves(slab, kv, ones_lane=False):
    lane = lax.broadcasted_iota(jnp.int32, slab.shape, 1)
    own = (lane >= HD_B) if kv % 2 else (lane < HD_B)
    halves = [None, None]
    halves[kv % 2] = jnp.where(own, slab, 0.0)
    halves[1 - kv % 2] = pltpu.roll(halves[kv % 2], HD_B, 1)
    if ones_lane:
        halves = [jnp.where(lane == _ones_lane(par), 1.0, h) for par, h in enumerate(halves)]
    return [h.astype(BF16) for h in halves]


def _swa_attend(jobs, valid_of, sink_of):
    scores = [[_dot_nt(q, kh[par]) for par in range(2)] for q, kh, _, _ in jobs]
    probs = []
    for j, (q, _, _, kv) in enumerate(jobs):
        m2 = q.shape[0]
        top = lax.broadcasted_iota(jnp.int32, (m2, 1), 0) < (m2 // 2)
        row = []
        for par in range(2):
            s = jnp.where(valid_of(j), scores[j][par], NEG)
            sink = jnp.where(top, sink_of(G_B * kv + par), sink_of(G_B * kv + par + 2)) * LOG2E
            m = jnp.maximum(jnp.max(s, axis=-1, keepdims=True), sink)
            row.append((jnp.exp2(s - m).astype(BF16), jnp.exp2(sink - m)))
        probs.append(row)
    outs = []
    for j, (q, _, vh, _) in enumerate(jobs):
        o = []
        for par in range(2):
            pv = _dot(probs[j][par][0], vh[par])
            ones = _ones_lane(par)
            o.append(pv / (pv[:, ones:ones + 1] + probs[j][par][1]))
        low_half = lax.broadcasted_iota(jnp.int32, o[0].shape, 1) < HD_B
        outs.append(jnp.where(low_half, o[0], o[1]))
    return outs


def _attn_b_kernel(sink_ref, q_ref, kp_ref, kc_ref, vp_ref, vc_ref, o_ref):
    g = pl.program_id(1)
    kcat = jnp.concatenate([kp_ref[...], kc_ref[...]], axis=0)
    vcat = jnp.concatenate([vp_ref[...], vc_ref[...]], axis=0)
    shape = (2 * WINDOW, 2 * WINDOW)
    qi = lax.broadcasted_iota(jnp.int32, shape, 0) & (WINDOW - 1)
    col = lax.broadcasted_iota(jnp.int32, shape, 1)
    band = (col > qi) & (col <= qi + WINDOW)
    band_first = band & ((col >= WINDOW) | (g > 0))
    n_blocks = q_ref.shape[0] // WINDOW
    for kv in range(N_KV_B):
        base = G_B * HD_B * kv
        ks = slice(LANES * (kv // 2), LANES * (kv // 2 + 1))
        k_half = _swa_halves(kcat[:, ks], kv)
        v_half = _swa_halves(vcat[:, ks], kv, ones_lane=True)
        jobs = []
        for r in range(n_blocks):
            rows = slice(WINDOW * r, WINDOW * (r + 1))
            win = slice(WINDOW * r, WINDOW * (r + 2))
            q_lhs = jnp.concatenate([q_ref[rows, base:base + LANES], q_ref[rows, base + LANES:base + 2 * LANES]], axis=0)
            jobs.append((q_lhs, [h[win] for h in k_half], [h[win] for h in v_half], kv))
        outs = _swa_attend(jobs, lambda r: band if r else band_first, lambda hh: sink_ref[hh])
        for r, o in enumerate(outs):
            rows = slice(WINDOW * r, WINDOW * (r + 1))
            o_ref[rows, base:base + LANES] = o[:WINDOW].astype(BF16)
            o_ref[rows, base + LANES:base + 2 * LANES] = o[WINDOW:].astype(BF16)


def _attn_b(sinks, q, k, v, batch, seq):
    nb = seq // WINDOW
    ng = nb // SWA_BLOCKS
    kw = N_KV_B * HD_B
    prev = lambda b, g: (b * nb + jnp.maximum(SWA_BLOCKS * g - 1, 0), 0)
    cur = lambda b, g: (b * ng + g, 0)
    return pl.pallas_call(
        _attn_b_kernel,
        grid=(batch, ng),
        in_specs=[pl.BlockSpec(memory_space=pltpu.SMEM),
                  pl.BlockSpec((SWA_BLOCKS * WINDOW, H_B * HD_B), cur),
                  pl.BlockSpec((WINDOW, kw), prev), pl.BlockSpec((SWA_BLOCKS * WINDOW, kw), cur),
                  pl.BlockSpec((WINDOW, kw), prev), pl.BlockSpec((SWA_BLOCKS * WINDOW, kw), cur)],
        out_specs=pl.BlockSpec((SWA_BLOCKS * WINDOW, H_B * HD_B), cur),
        out_shape=jax.ShapeDtypeStruct((batch * seq, H_B * HD_B), BF16),
        compiler_params=pltpu.CompilerParams(dimension_semantics=("arbitrary", "arbitrary"),
                                             vmem_limit_bytes=VMEM_LIMIT),
        name="swa_prompt_attn",
    )(sinks, q, k, k, v, v)


def _attn_b_sample_kernel(sink_ref, q_ref, k_ref, v_ref, o_ref, *, t, w_buf):
    keys = k_ref.shape[1]
    rows = 2 * SWA_Q_ROWS
    ti = lax.broadcasted_iota(jnp.int32, (rows, keys), 0) & (SWA_Q_ROWS - 1)
    col = lax.broadcasted_iota(jnp.int32, (rows, keys), 1)
    diff = jnp.where(col < w_buf, ti + w_buf - col, ti - (col - w_buf))
    valid = (diff >= 0) & (diff < WINDOW) & (col < w_buf + t) & (ti < t)
    jobs = []
    for b in range(q_ref.shape[0]):
        for kv in range(N_KV_B):
            ks = slice(LANES * (kv // 2), LANES * (kv // 2 + 1))
            jobs.append((q_ref[b, kv], _swa_halves(k_ref[b, :, ks], kv),
                         _swa_halves(v_ref[b, :, ks], kv, ones_lane=True), kv))
    outs = _swa_attend(jobs, lambda j: valid, lambda hh: sink_ref[hh])
    for j, o in enumerate(outs):
        o_ref[j // N_KV_B, j % N_KV_B] = o


def _attn_b_sample(sinks, q, k_all, v_all, t, w_buf):
    bd = q.shape[0]
    bs = SWA_SAMPLE_SEQS
    blk = lambda a: pl.BlockSpec((bs,) + a.shape[1:], lambda b: (b,) + (0,) * (a.ndim - 1))
    return pl.pallas_call(
        functools.partial(_attn_b_sample_kernel, t=t, w_buf=w_buf),
        grid=(bd // bs,),
        in_specs=[pl.BlockSpec(memory_space=pltpu.SMEM), blk(q), blk(k_all), blk(v_all)],
        out_specs=blk(q),
        out_shape=jax.ShapeDtypeStruct(q.shape, F32),
        compiler_params=pltpu.CompilerParams(dimension_semantics=("arbitrary",),
                                             vmem_limit_bytes=VMEM_LIMIT),
        name="swa_sample_attn",
    )(sinks, q, k_all, v_all)


def _rope_tables(pos, n_rot, period, lane_lo):
    half = n_rot // 2
    inv = ROPE_THETA ** (-np.arange(0, n_rot, 2, dtype=np.float64) / n_rot)
    ang = np.asarray(pos, np.float64)[:, None] * inv[None, :]
    cos, sin = np.cos(ang), np.sin(ang)
    rel = np.arange(LANES) % period - lane_lo
    in1 = (rel >= 0) & (rel < half)
    in2 = (rel >= half) & (rel < 2 * half)
    idx = np.where(in1, rel, np.where(in2, rel - half, 0))
    cg, sg = cos[:, idx], sin[:, idx]
    tables = np.stack([np.where(in1 | in2, cg, 1.0), np.where(in1, -sg, 0.0), np.where(in2, sg, 0.0)])
    return jnp.asarray(tables, F32)


def _rope_tables_dup(pos, n_rot, lane_lo):
    half = n_rot // 2
    inv = ROPE_THETA ** (-np.arange(0, n_rot, 2, dtype=np.float64) / n_rot)
    ang = np.asarray(pos, np.float64)[:, None] * inv[None, :]
    cos, sin = np.cos(ang), np.sin(ang)
    rel = np.arange(LANES) - lane_lo
    in1 = (rel >= 0) & (rel < half)
    in2 = (rel >= half) & (rel < 2 * half)
    idx = np.where(in1, rel, np.where(in2, rel - half, 0))
    cg, sg = cos[:, idx], sin[:, idx]
    tables = np.stack([np.where(in1 | in2, cg, np.where(rel < 0, 1.0, 0.0)),
                       np.where(in1, -sg, np.where(in2, sg, 0.0))])
    return jnp.asarray(tables, F32)


def _segments(width, seg_lanes, extra=None):
    seg = np.zeros((width, LANES), np.float32)
    spread = np.zeros((LANES, width), np.float32)
    cnt = np.zeros((1, LANES), np.float32)
    for s, (lo, hi) in enumerate(seg_lanes):
        seg[lo:hi, s] = 1.0
        spread[s, lo:hi + (extra[s] if extra else 0)] = 1.0
        cnt[0, s] = 1.0 / (hi - lo)
    expand = np.concatenate([spread, spread], axis=0)
    return jnp.asarray(seg, BF16), jnp.asarray(cnt, F32), jnp.asarray(expand, BF16)


def _slab_gain(parts):
    pieces, pos = [], 0
    for lo, vals in parts:
        pieces += [jnp.zeros((lo - pos,), F32), vals.astype(F32)]
        pos = lo + vals.shape[0]
    pieces.append(jnp.zeros((SLAB - pos,), F32))
    return jnp.concatenate(pieces)[None, :]


def _prep_a(norm_attn, w_a_in, g_qc, w_uq, g_ckv, w_uk, w_uv, g_qn, g_qr, g_kn, g_kr):
    half = D_ROPE // 2
    tail = SLAB - KPE_LANE - D_ROPE - half
    w_kpe = w_a_in[:, D_QC + D_C:]
    w_in = jnp.concatenate([w_a_in[:, :D_QC + D_C], jnp.zeros((D_MODEL, KPE_LANE), F32),
                            w_kpe, w_kpe[:, :half], jnp.zeros((D_MODEL, tail), F32)], axis=1)
    dqk = D_NOPE + D_ROPE
    w_uq3 = w_uq.reshape(D_QC, H_A, dqk)
    w_uq_pad = jnp.concatenate([w_uq3, w_uq3[:, :, D_NOPE:D_NOPE + half], jnp.zeros((D_QC, H_A, tail), F32)],
                               axis=2).reshape(D_QC, H_A * SLAB)
    w_uk3 = w_uk.reshape(D_C, H_A, D_NOPE)
    w_uk_pad = jnp.pad(w_uk3, ((0, 0), (0, 0), (0, SLAB - D_NOPE))).reshape(D_C, H_A * SLAB)
    w_uv3 = w_uv.reshape(D_C, H_A // 2, 2, D_V)
    even = jnp.pad(w_uv3[:, :, 0], ((0, 0), (0, 0), (0, SLAB - D_V)))
    odd = jnp.pad(w_uv3[:, :, 1], ((0, 0), (0, 0), (SLAB - D_V, 0)))
    w_uv_pad = jnp.stack([even, odd], axis=2).reshape(D_C, H_A * SLAB)
    v_ones = np.zeros((1, H_A * SLAB), np.float32)
    for h in range(H_A):
        v_ones[0, SLAB * h + _ones_lane(h % 2)] = 1.0
    q_segs = []
    for h in range(H_A):
        q_segs += [(SLAB * h, SLAB * h + D_NOPE), (SLAB * h + KPE_LANE, SLAB * h + KPE_LANE + D_ROPE)]
    segq, cntq, expq = _segments(H_A * SLAB, q_segs, extra=[0, half] * H_A)
    q_scale = SCALE_A * LOG2E
    gq = jnp.tile(_slab_gain([(0, g_qn * q_scale), (KPE_LANE, g_qr * q_scale),
                              (KPE_LANE + D_ROPE, g_qr[:half] * q_scale)]), (1, H_A))
    gk_slab = _slab_gain([(0, g_kn)])
    pe_mask = np.zeros((1, SLAB), np.float32)
    pe_mask[0, KPE_LANE:KPE_LANE + D_ROPE] = 1.0
    wukt = jnp.pad(jnp.transpose(w_uk3, (1, 2, 0)), ((0, 0), (0, SLAB - D_NOPE), (0, 0)))
    return dict(
        g_attn=norm_attn[None, :], w_in=w_in.astype(BF16), g_qc=g_qc[None, :], g_ckv=g_ckv[None, :],
        g_kpe=_slab_gain([(KPE_LANE, g_kr), (KPE_LANE + D_ROPE, g_kr[:half])]), pe_mask=jnp.asarray(pe_mask),
        w_uq=w_uq_pad.astype(BF16), segq=segq, cntq=cntq, expq=expq, gq=gq,
        w_uk=w_uk_pad.astype(BF16),
        w_uv=w_uv_pad.astype(BF16), v_ones=jnp.asarray(v_ones), gk_slab=gk_slab, wukt_pad=wukt.astype(BF16),
        wukt=jnp.transpose(w_uk3, (2, 1, 0)).reshape(H_A * D_NOPE, D_C).astype(BF16))


def _prep_b(g_kv, w_kv, g_k, norm_attn, w_q, g_q):
    kw = N_KV_B * HD_B
    segk, cntk, expk = _segments(kw, [(HD_B * h, HD_B * (h + 1)) for h in range(N_KV_B)])
    segq, cntq, expq = _segments(H_B * HD_B, [(HD_B * h, HD_B * (h + 1)) for h in range(H_B)])
    return dict(g_kv=g_kv[None, :], w_kv=w_kv.astype(BF16), segk=segk, cntk=cntk, expk=expk,
                gk=jnp.tile(g_k, N_KV_B)[None, :], g_attn=norm_attn[None, :], w_q=w_q.astype(BF16),
                segq=segq, cntq=cntq, expq=expq, gq=jnp.tile(g_q * (SCALE_B * LOG2E), H_B)[None, :])


def kernel(x_prompt, x_sample, cache_mla, state_win_k, state_win_v, page_table, norm_attn, norm_ffn, w_a_in, g_qc, w_uq, g_ckv, w_uk, w_uv, g_qn_a, g_qr_a, g_kn_a, g_kr_a, w_a_out, g_kv_shared, w_kv_shared, g_k_b, w_q_b, g_q_b, sinks, w_b_out, w_ffn_in, w_ffn_out):
    batch, seq, _ = x_prompt.shape
    bd, t_dec, _ = x_sample.shape
    past_len = page_table.shape[1] * PAGE_SIZE
    w_buf = state_win_k.shape[1]
    kw = N_KV_B * HD_B
    assert w_a_in.shape[0] == 1 and w_q_b.shape[0] == 1, "one MLA layer followed by one SWA layer"
    assert w_buf == WINDOW and (bd * t_dec) % 8 == 0
    assert all(seq % t == 0 for t in (TM_MLA_PROJ, TM_POST, TM_SWA_PROJ, TQ_MLA, SWA_BLOCKS * WINDOW))
    assert seq >= WINDOW and bd * t_dec >= WINDOW, "the window tail is taken from the last 128 rows of a tile"
    assert t_dec <= SWA_Q_ROWS and bd % SWA_SAMPLE_SEQS == 0 and page_table.shape[1] % (2 * PAGES_PER_GROUP) == 0

    wa = _prep_a(norm_attn[0], w_a_in[0], g_qc[0], w_uq[0], g_ckv[0], w_uk[0], w_uv[0],
                 g_qn_a[0], g_qr_a[0], g_kn_a[0], g_kr_a[0])
    wb = _prep_b(g_kv_shared, w_kv_shared, g_k_b, norm_attn[1], w_q_b[0], g_q_b[0])
    w_a_out_b = w_a_out[0].astype(BF16)
    w_b_out_b = w_b_out[0].astype(BF16)
    ffn_in = w_ffn_in.astype(BF16)
    ffn_out = w_ffn_out.astype(BF16)
    g_ffn = norm_ffn[:, None, :]
    sink_b = sinks[0]

    pos_p = np.arange(seq)
    n_s = bd * t_dec
    pos_s = past_len + np.arange(n_s) % t_dec

    xp = x_prompt.reshape(batch * seq, D_MODEL)
    q, k, v, rows_p = _proj_a(xp, _rope_tables_dup(pos_p, D_ROPE, KPE_LANE), wa, TM_MLA_PROJ)
    o = _attn_a(q, k, v, batch, seq, TQ_MLA)
    h = _post(xp, o, w_a_out_b, g_ffn, ffn_in, ffn_out, 0, TM_POST)
    k_p, v_p, q_b, kt_p, vt_p = _proj_b(h, _rope_tables(pos_p, ROT_B, HD_B, 0), wb, TM_SWA_PROJ)
    o = _attn_b(sink_b, q_b, k_p, v_p, batch, seq)
    y_prompt = _post(h, o, w_b_out_b, g_ffn, ffn_in, ffn_out, 1, TM_POST)

    xs = x_sample.reshape(n_s, D_MODEL)
    q, _, _, rows_s = _proj_a(xs, _rope_tables_dup(pos_s, D_ROPE, KPE_LANE), wa, n_s)
    qabs = _qabs(q, wa["gk_slab"], wa["wukt_pad"]).reshape(bd, t_dec * H_A, D_C)
    qpe = q.reshape(n_s, H_A, SLAB)[:, :, KPE_LANE:KPE_LANE + D_ROPE].reshape(bd, t_dec * H_A, D_ROPE)
    new_pad = jnp.pad(jnp.swapaxes(rows_s.reshape(bd, t_dec, D_CKV), 1, 2), ((0, 0), (0, 0), (0, PAGE_SIZE - t_dec)))
    olat = _paged_attn(page_table, jnp.swapaxes(cache_mla, 2, 3), wa["wukt"], qabs, qpe, new_pad, t_dec)
    o = _latent_out(olat.reshape(n_s, H_A * D_C), wa["w_uv"])
    h = _post(xs, o, w_a_out_b, g_ffn, ffn_in, ffn_out, 0, n_s)
    k_s, v_s, q_b, _, _ = _proj_b(h, _rope_tables(pos_s, ROT_B, HD_B, 0), wb, n_s)
    q4 = jnp.transpose(q_b.reshape(bd, t_dec, N_KV_B, 2, LANES), (0, 2, 3, 1, 4))
    q4 = jnp.pad(q4, ((0, 0), (0, 0), (0, 0), (0, SWA_Q_ROWS - t_dec), (0, 0))).reshape(bd, N_KV_B, 2 * SWA_Q_ROWS, LANES)
    key_pad = jnp.zeros((bd, 2 * SWA_Q_ROWS - t_dec, kw), F32)
    k_all = jnp.concatenate([state_win_k.reshape(bd, w_buf, kw), k_s.reshape(bd, t_dec, kw), key_pad], axis=1)
    v_all = jnp.concatenate([state_win_v.reshape(bd, w_buf, kw), v_s.reshape(bd, t_dec, kw), key_pad], axis=1)
    o4 = _attn_b_sample(sink_b, q4, k_all, v_all, t_dec, w_buf)
    o = jnp.transpose(o4.reshape(bd, N_KV_B, 2, SWA_Q_ROWS, LANES)[:, :, :, :t_dec], (0, 3, 1, 2, 4))
    y_sample = _post(h, o.reshape(n_s, H_B * HD_B).astype(BF16), w_b_out_b, g_ffn, ffn_in, ffn_out, 1, n_s)

    win_k_p = jnp.transpose(kt_p.reshape(batch, N_KV_B, HD_B, WINDOW), (0, 3, 1, 2))
    win_v_p = jnp.transpose(vt_p.reshape(batch, N_KV_B, HD_B, WINDOW), (0, 3, 1, 2))
    win_k_s = jnp.concatenate([state_win_k, k_s.reshape(bd, t_dec, N_KV_B, HD_B)], axis=1)[:, -w_buf:]
    win_v_s = jnp.concatenate([state_win_v, v_s.reshape(bd, t_dec, N_KV_B, HD_B)], axis=1)[:, -w_buf:]
    return (y_prompt.reshape(batch, seq, D_MODEL), y_sample.reshape(bd, t_dec, D_MODEL),
            rows_p.reshape(1, batch, seq, D_CKV), rows_s.reshape(1, bd, t_dec, D_CKV),
            win_k_p, win_v_p, win_k_s, win_v_s)
```

```python
import functools

import numpy as np
import jax
import jax.numpy as jnp
from jax import lax
from jax.experimental import pallas as pl
from jax.experimental.pallas import tpu as pltpu

F32 = jnp.float32
BF16 = jnp.bfloat16

D_MODEL = 1024
PAGE_SIZE = 128
H_A = 16
D_NOPE = 64
D_ROPE = 32
D_V = 64
D_QC = 384
D_C = 256
D_CKV = D_C + D_ROPE
SCALE_A = (D_NOPE + D_ROPE) ** -0.5
H_B = 16
N_KV_B = 4
HD_B = 64
G_B = H_B // N_KV_B
WINDOW = 128
ROT_B = HD_B // 4
SCALE_B = HD_B ** -0.5
D_FF = 2816
ROPE_THETA = 500000.0
EPS = 1e-6
NEG = -1e30
LOG2E = 1.4426950408889634

LANES = 128
MXU_DIM = 256
VMEM_LIMIT = 56 * 1024 * 1024
SLAB = 128
KPE_LANE = 64

FF_CHUNK = MXU_DIM
TM_MLA_PROJ = 512
PROJ_SUB = 256
TM_SWA_PROJ = 1024
SWA_PROJ_SUB = 512
TM_POST = 512
TQ_MLA = MXU_DIM
ATTN_LOOKAHEAD = 3
PAGES_PER_GROUP = 16
PAGES_PER_DOT = MXU_DIM // PAGE_SIZE
GROUPS_PER_SPAN = 4
ACC_AFTER_DOTS = 4
SWA_SAMPLE_SEQS = 8
SWA_BLOCKS = 8
SWA_Q_ROWS = 8

_NT = (((1,), (1,)), ((), ()))


def _dot(a, b):
    return jnp.dot(a, b, preferred_element_type=F32)


def _dot_nt(a, b):
    return lax.dot_general(a, b, _NT, preferred_element_type=F32)


def _rms(x, g):
    ms = jnp.mean(x * x, axis=-1, keepdims=True)
    return x * lax.rsqrt(ms + EPS) * g


def _rope_slab(x, c, s1, s2, half):
    return x * c + pltpu.roll(x, LANES - half, 1) * s1 + pltpu.roll(x, half, 1) * s2


def _rope_slab_dup(x, c, s, half):
    return x * c + pltpu.roll(x, LANES - half, 1) * s


def _segment_scales(items):
    sums = [_dot((raw * raw).astype(BF16), seg_ref[...]) for raw, seg_ref, _, _ in items]
    scales = []
    for ss, (_, _, inv_cnt_ref, expand_ref) in zip(sums, items):
        rs = lax.rsqrt(ss * inv_cnt_ref[...] + EPS)
        hi = rs.astype(BF16)
        lo = (rs - hi.astype(F32)).astype(BF16)
        scales.append(_dot(jnp.concatenate([hi, lo], axis=1), expand_ref[...]))
    return scales


def _proj_a_kernel(x_ref, tab_ref, g_attn_ref, w_in_ref, g_qc_ref, g_ckv_ref, g_kpe_ref, pe_mask_ref,
                   w_uq_ref, segq_ref, cntq_ref, expq_ref, gq_ref,
                   w_uk_ref, gk_ref, w_uv_ref, v_ones_ref,
                   q_ref, k_ref, v_ref, rows_ref):
    tm = x_ref.shape[0]
    blocks = [slice(r, r + min(PROJ_SUB, tm)) for r in range(0, tm, PROJ_SUB)]
    half = D_ROPE // 2

    def stage_in(rows):
        hn = _rms(x_ref[rows, :], g_attn_ref[...]).astype(BF16)
        return _dot(hn, w_in_ref[...])

    def stage_latents(rows, a):
        c, s = tab_ref[0, rows, :], tab_ref[1, rows, :]
        cq = _rms(a[:, :D_QC], g_qc_ref[...]).astype(BF16)
        ckv = _rms(a[:, D_QC:D_QC + D_C], g_ckv_ref[...])
        kpe = a[:, D_QC + D_C:]
        ms = jnp.sum(kpe * kpe * pe_mask_ref[...], axis=-1, keepdims=True) * (1.0 / D_ROPE)
        kpe = _rope_slab_dup(kpe * lax.rsqrt(ms + EPS) * g_kpe_ref[...], c, s, half)
        rows_ref[rows, :D_C] = ckv
        rows_ref[rows, D_C:] = kpe[:, KPE_LANE:KPE_LANE + D_ROPE]
        ckv_b = ckv.astype(BF16)
        q_raw = _dot(cq, w_uq_ref[...])
        k_raw = _dot(ckv_b, w_uk_ref[...])
        v_ref[rows, :] = (_dot(ckv_b, w_uv_ref[...]) + v_ones_ref[...]).astype(BF16)
        return q_raw, k_raw, kpe

    def stage_out(rows, q_raw, q_scale, k_raw, kpe):
        c, s = tab_ref[0, rows, :], tab_ref[1, rows, :]
        qn = q_raw * q_scale * gq_ref[...]
        for h in range(H_A):
            sl = slice(SLAB * h, SLAB * (h + 1))
            q_ref[rows, sl] = _rope_slab_dup(qn[:, sl], c, s, half).astype(BF16)
            k_h = k_raw[:, sl]
            ms_h = jnp.sum(k_h * k_h, axis=-1, keepdims=True) * (1.0 / D_NOPE)
            k_ref[rows, sl] = (k_h * lax.rsqrt(ms_h + EPS) * gk_ref[...] + kpe).astype(BF16)

    a_s = [stage_in(rows) for rows in blocks]
    mids = [stage_latents(rows, a) for rows, a in zip(blocks, a_s)]
    q_scales = _segment_scales([(q_raw, segq_ref, cntq_ref, expq_ref) for q_raw, _, _ in mids])
    for rows, (q_raw, k_raw, kpe), q_scale in zip(blocks, mids, q_scales):
        stage_out(rows, q_raw, q_scale, k_raw, kpe)


def _const_spec(shape):
    zeros = (0,) * len(shape)
    return pl.BlockSpec(shape, lambda *_: zeros, pipeline_mode=pl.Buffered(1))


def _proj_a(x, tab, wa, tm):
    n = x.shape[0]
    n_tab = tab.shape[1] // tm
    weights = [wa["g_attn"], wa["w_in"], wa["g_qc"], wa["g_ckv"], wa["g_kpe"], wa["pe_mask"],
               wa["w_uq"], wa["segq"], wa["cntq"], wa["expq"], wa["gq"],
               wa["w_uk"], wa["gk_slab"], wa["w_uv"], wa["v_ones"]]
    wide = H_A * SLAB
    return pl.pallas_call(
        _proj_a_kernel,
        grid=(n // tm,),
        in_specs=[pl.BlockSpec((tm, D_MODEL), lambda i: (i, 0)),
                  pl.BlockSpec((tab.shape[0], tm, LANES), lambda i: (0, i % n_tab, 0))]
                 + [_const_spec(w.shape) for w in weights],
        out_specs=[pl.BlockSpec((tm, wide), lambda i: (i, 0)),
                   pl.BlockSpec((tm, wide), lambda i: (i, 0)),
                   pl.BlockSpec((tm, wide), lambda i: (i, 0)),
                   pl.BlockSpec((tm, D_CKV), lambda i: (i, 0))],
        out_shape=[jax.ShapeDtypeStruct((n, wide), BF16),
                   jax.ShapeDtypeStruct((n, wide), BF16),
                   jax.ShapeDtypeStruct((n, wide), BF16),
                   jax.ShapeDtypeStruct((n, D_CKV), F32)],
        compiler_params=pltpu.CompilerParams(dimension_semantics=("arbitrary",),
                                             vmem_limit_bytes=VMEM_LIMIT),
        name="mla_proj",
    )(x, tab, *weights)


def _ones_lane(parity):
    return D_V if parity == 0 else 0


def _attn_a_kernel(q_ref, k_ref, v_ref, o_ref, *, tq):
    seq = q_ref.shape[0]
    causal = (lax.broadcasted_iota(jnp.int32, (tq, tq), 1) <= lax.broadcasted_iota(jnp.int32, (tq, tq), 0))
    low_half = lax.broadcasted_iota(jnp.int32, (tq, SLAB), 1) < D_V
    jobs = [(c, e) for c in range(seq // tq) for e in range(2)]

    def windows(c, e):
        return slice(c * tq, (c + 1) * tq), slice(0, c * tq), slice(SLAB * e, SLAB * (e + 1))

    def score(c, e):
        rows, past, ls = windows(c, e)
        q = q_ref[rows, ls]
        s_d = _dot_nt(q, k_ref[rows, ls])
        return s_d, (_dot_nt(q, k_ref[past, ls]) if c else None)

    def attend(c, e, s_d, s_p):
        rows, past, ls = windows(c, e)
        s_d = jnp.where(causal, s_d, NEG)
        m = jnp.max(s_d, axis=-1, keepdims=True)
        if c:
            m = jnp.maximum(m, jnp.max(s_p, axis=-1, keepdims=True))
        acc = _dot(jnp.exp2(s_d - m).astype(BF16), v_ref[rows, ls])
        if c:
            acc = acc + _dot(jnp.exp2(s_p - m).astype(BF16), v_ref[past, ls])
        ones = _ones_lane(e)
        return acc / acc[:, ones:ones + 1]

    ahead = [score(*jobs[j]) for j in range(min(ATTN_LOOKAHEAD, len(jobs)))]
    out = None
    for j, (c, e) in enumerate(jobs):
        if j + ATTN_LOOKAHEAD < len(jobs):
            ahead.append(score(*jobs[j + ATTN_LOOKAHEAD]))
        o_e = attend(c, e, *ahead[j])
        ahead[j] = None
        if e == 0:
            out = o_e
        else:
            o_ref[c * tq:(c + 1) * tq, :] = jnp.where(low_half, out, o_e).astype(BF16)


def _attn_a(q, k, v, batch, seq, tq):
    pairs = H_A // 2
    return pl.pallas_call(
        functools.partial(_attn_a_kernel, tq=tq),
        grid=(batch, pairs),
        in_specs=[pl.BlockSpec((seq, 2 * SLAB), lambda b, j: (b, j)),
                  pl.BlockSpec((seq, 2 * SLAB), lambda b, j: (b, j)),
                  pl.BlockSpec((seq, 2 * SLAB), lambda b, j: (b, j))],
        out_specs=pl.BlockSpec((seq, SLAB), lambda b, j: (b, j)),
        out_shape=jax.ShapeDtypeStruct((batch * seq, H_A * D_V), BF16),
        compiler_params=pltpu.CompilerParams(
            dimension_semantics=("arbitrary", "arbitrary"),
            vmem_limit_bytes=VMEM_LIMIT),
        name="mla_prompt_attn",
    )(q, k, v)


def _qabs_kernel(q_ref, gk_ref, wukt_ref, o_ref):
    for h in range(H_A):
        qs = (q_ref[:, SLAB * h:SLAB * (h + 1)].astype(F32) * gk_ref[...]).astype(BF16)
        o_ref[:, D_C * h:D_C * (h + 1)] = _dot(qs, wukt_ref[h]).astype(BF16)


def _qabs(q, gk_slab, wukt):
    n = q.shape[0]
    return pl.pallas_call(
        _qabs_kernel,
        out_shape=jax.ShapeDtypeStruct((n, H_A * D_C), BF16),
        compiler_params=pltpu.CompilerParams(vmem_limit_bytes=VMEM_LIMIT),
        name="mla_absorb_q",
    )(q, gk_slab, wukt)


def _paged_kernel(pt_ref, cache_ref, wukt_ref, qabs_ref, qpe_ref, new_ref, o_ref,
                  lhs_sc, pg_sc, sem, m_sc, l_sc, acc_sc, s_sc, ct_sc, *, n_pages, t_new):
    seq = pl.program_id(0)
    group = pg_sc.shape[1]
    n_groups = n_pages // group
    rows_q = qabs_ref.shape[0]

    def page_copy(sq, g, u, slot):
        return pltpu.make_async_copy(cache_ref.at[0, pt_ref[sq, g * group + u]], pg_sc.at[slot, u], sem.at[slot])

    def start_group(sq, g, slot):
        for u in range(group):
            page_copy(sq, g, u, slot).start()

    def wait_group(sq, g, slot):
        for u in range(group):
            page_copy(sq, g, u, slot).wait()

    @pl.when(seq == 0)
    def _():
        start_group(seq, 0, 0)

    lhs_sc[:H_A * D_NOPE, :] = wukt_ref[...]
    lhs_sc[H_A * D_NOPE:, :] = qabs_ref[...]
    m_sc[...] = jnp.full(m_sc.shape, NEG, F32)
    l_sc[...] = jnp.zeros(l_sc.shape, F32)
    acc_sc[...] = jnp.zeros(acc_sc.shape, F32)
    s_sc[...] = jnp.full(s_sc.shape, -jnp.inf, F32)
    ct_sc[...] = jnp.zeros(ct_sc.shape, BF16)

    def nope_scores(ct):
        keys = ct.shape[1]
        big = _dot(lhs_sc[...], ct)
        kt = big[:H_A * D_NOPE]
        ssq = jnp.sum((kt * kt).reshape(D_NOPE, H_A, keys), axis=0)
        rs = lax.rsqrt(ssq * (1.0 / D_NOPE) + EPS)
        rs_q = jnp.concatenate([rs] * (rows_q // H_A), axis=0)
        return big[H_A * D_NOPE:] * rs_q

    def rope_scores(kpets):
        return _dot(qpe_ref[...], jnp.concatenate(kpets, axis=1))

    def accumulate(s, ct):
        m_old = m_sc[...]
        m_new = jnp.maximum(m_old, jnp.max(s, axis=-1, keepdims=True))
        corr = jnp.exp2(m_old - m_new)
        p = jnp.exp2(s - m_new)
        l_sc[...] = l_sc[...] * corr + jnp.sum(p, axis=-1, keepdims=True)
        acc_sc[...] = acc_sc[...] * corr + _dot_nt(p.astype(BF16), ct)
        m_sc[...] = m_new

    def span_step(h, carry):
        pending = (s_sc[...], ct_sc[...])
        for gl in range(GROUPS_PER_SPAN):
            g = h * GROUPS_PER_SPAN + gl
            slot = gl % 2
            wait_group(seq, g, slot)
            if gl + 1 < GROUPS_PER_SPAN:
                start_group(seq, g + 1, 1 - slot)
            else:
                @pl.when(g + 1 < n_groups)
                def _():
                    start_group(seq, g + 1, 0)

                @pl.when((g + 1 == n_groups) & (seq + 1 < pl.num_programs(0)))
                def _():
                    start_group(seq + 1, 0, 0)
            cts, kpets = [], []
            for u in range(0, group, PAGES_PER_DOT):
                pages = range(u, u + PAGES_PER_DOT)
                cts.append(jnp.concatenate([pg_sc[slot, v, :D_C, :] for v in pages], axis=1).astype(BF16))
                kpets.append(jnp.concatenate([pg_sc[slot, v, D_C:, :] for v in pages], axis=1).astype(BF16))
            nope = []
            for i, ct in enumerate(cts):
                if i == ACC_AFTER_DOTS:
                    accumulate(*pending)
                nope.append(nope_scores(ct))
            pending = (jnp.concatenate(nope, axis=1) + rope_scores(kpets), jnp.concatenate(cts, axis=1))
        s_sc[...], ct_sc[...] = pending
        return carry

    lax.fori_loop(0, n_groups // GROUPS_PER_SPAN, span_step, 0)
    accumulate(s_sc[...], ct_sc[...])

    keys = new_ref.shape[1]
    t_row = lax.shift_right_logical(lax.broadcasted_iota(jnp.int32, (rows_q, keys), 0), H_A.bit_length() - 1)
    s_col = lax.broadcasted_iota(jnp.int32, (rows_q, keys), 1)
    ct = new_ref[:D_C, :].astype(BF16)
    s = nope_scores(ct) + rope_scores([new_ref[D_C:, :].astype(BF16)])
    accumulate(jnp.where((s_col <= t_row) & (s_col < t_new), s, NEG), ct)
    o_ref[...] = acc_sc[...] / l_sc[...]


def _paged_attn(page_table, cache, wukt, qabs, qpe, new_pad, t_new):
    bd, n_pages = page_table.shape
    rows_q = qabs.shape[1]
    assert (n_pages // PAGES_PER_GROUP) % 2 == 0, "the slot of a page group must not depend on the sequence"
    assert ACC_AFTER_DOTS < PAGES_PER_GROUP // PAGES_PER_DOT
    assert GROUPS_PER_SPAN % 2 == 0 and (n_pages // PAGES_PER_GROUP) % GROUPS_PER_SPAN == 0
    per_seq = lambda a: pl.BlockSpec((None,) + a.shape[1:], lambda b, pt: (b, 0, 0))
    group_keys = PAGES_PER_GROUP * PAGE_SIZE
    grid_spec = pltpu.PrefetchScalarGridSpec(
        num_scalar_prefetch=1,
        grid=(bd,),
        in_specs=[pl.BlockSpec(memory_space=pl.ANY),
                  pl.BlockSpec(wukt.shape, lambda b, pt: (0, 0)),
                  per_seq(qabs), per_seq(qpe), per_seq(new_pad)],
        out_specs=pl.BlockSpec((None, rows_q, D_C), lambda b, pt: (b, 0, 0)),
        scratch_shapes=[pltpu.VMEM((H_A * D_NOPE + rows_q, D_C), BF16),
                        pltpu.VMEM((2, PAGES_PER_GROUP, D_CKV, PAGE_SIZE), F32),
                        pltpu.SemaphoreType.DMA((2,)),
                        pltpu.VMEM((rows_q, 1), F32), pltpu.VMEM((rows_q, 1), F32), pltpu.VMEM((rows_q, D_C), F32),
                        pltpu.VMEM((rows_q, group_keys), F32), pltpu.VMEM((D_C, group_keys), BF16)])
    return pl.pallas_call(
        functools.partial(_paged_kernel, n_pages=n_pages, t_new=t_new),
        grid_spec=grid_spec,
        out_shape=jax.ShapeDtypeStruct((bd, rows_q, D_C), F32),
        compiler_params=pltpu.CompilerParams(dimension_semantics=("arbitrary",),
                                             vmem_limit_bytes=VMEM_LIMIT),
        name="mla_paged_attn",
    )(page_table, cache, wukt, qabs, qpe, new_pad)


def _latent_out_kernel(olat_ref, w_uv_ref, o_ref):
    for j in range(H_A // 2):
        acc = None
        for e in range(2):
            h = 2 * j + e
            part = _dot(olat_ref[:, D_C * h:D_C * (h + 1)].astype(BF16),
                        w_uv_ref[:, SLAB * h:SLAB * (h + 1)])
            acc = part if acc is None else acc + part
        o_ref[:, SLAB * j:SLAB * (j + 1)] = acc.astype(BF16)


def _latent_out(olat, w_uv_pad):
    n = olat.shape[0]
    return pl.pallas_call(
        _latent_out_kernel,
        out_shape=jax.ShapeDtypeStruct((n, H_A * D_V), BF16),
        compiler_params=pltpu.CompilerParams(vmem_limit_bytes=VMEM_LIMIT),
        name="mla_latent_out",
    )(olat, w_uv_pad)


def _post_kernel(x_ref, o_ref, w_o_ref, g_ref, w_in_ref, w_out_ref, y_ref):
    h1 = x_ref[...] + _dot(o_ref[...], w_o_ref[...])
    hn = _rms(h1, g_ref[...]).astype(BF16)
    acc = h1
    for c in range(D_FF // FF_CHUNK):
        lo = c * FF_CHUNK
        a1 = _dot(hn, w_in_ref[:, lo:lo + FF_CHUNK])
        a2 = _dot(hn, w_in_ref[:, D_FF + lo:D_FF + lo + FF_CHUNK])
        gate = (a1 * jax.nn.sigmoid(a1)) * a2
        acc = acc + _dot(gate.astype(BF16), w_out_ref[lo:lo + FF_CHUNK, :])
    y_ref[...] = acc


def _layer_spec(stacked, layer):
    zeros = (0,) * (stacked.ndim - 1)
    return pl.BlockSpec((None,) + stacked.shape[1:], lambda *_: (layer,) + zeros, pipeline_mode=pl.Buffered(1))


def _post(x, o, w_o, g, w_in, w_out, layer, tm):
    n = x.shape[0]
    return pl.pallas_call(
        _post_kernel,
        grid=(n // tm,),
        in_specs=[pl.BlockSpec((tm, D_MODEL), lambda i: (i, 0)),
                  pl.BlockSpec((tm, o.shape[1]), lambda i: (i, 0)),
                  _const_spec(w_o.shape), _layer_spec(g, layer),
                  _layer_spec(w_in, layer), _layer_spec(w_out, layer)],
        out_specs=pl.BlockSpec((tm, D_MODEL), lambda i: (i, 0)),
        out_shape=jax.ShapeDtypeStruct((n, D_MODEL), F32),
        compiler_params=pltpu.CompilerParams(dimension_semantics=("arbitrary",),
                                             vmem_limit_bytes=VMEM_LIMIT),
        name="outproj_swiglu",
    )(x, o, w_o, g, w_in, w_out)


def _proj_b_kernel(h_ref, tab_ref, g_kv_ref, w_kv_ref, segk_ref, cntk_ref, expk_ref, gk_ref,
                   g_attn_ref, w_q_ref, segq_ref, cntq_ref, expq_ref, gq_ref,
                   k_ref, v_ref, q_ref, kt_ref, vt_ref):
    tm = h_ref.shape[0]
    blocks = [slice(r, r + min(SWA_PROJ_SUB, tm)) for r in range(0, tm, SWA_PROJ_SUB)]
    kw = N_KV_B * HD_B
    half = ROT_B // 2

    def stage_in(rows):
        h = h_ref[rows, :]
        hr = h * lax.rsqrt(jnp.mean(h * h, axis=-1, keepdims=True) + EPS)
        kv = _dot((hr * g_kv_ref[...]).astype(BF16), w_kv_ref[...])
        q_raw = _dot((hr * g_attn_ref[...]).astype(BF16), w_q_ref[...])
        v_ref[rows, :] = kv[:, kw:]
        return kv, q_raw

    def stage_out(rows, kv, q_raw, k_scale, q_scale):
        c, s1, s2 = tab_ref[0, rows, :], tab_ref[1, rows, :], tab_ref[2, rows, :]
        last = rows.stop == tm
        n = rows.stop - rows.start
        kn = kv[:, :kw] * k_scale * gk_ref[...]
        for j in range(kw // LANES):
            sl = slice(LANES * j, LANES * (j + 1))
            k_slab = _rope_slab(kn[:, sl], c, s1, s2, half)
            k_ref[rows, sl] = k_slab
            if last:
                kt_ref[sl, :] = k_slab[n - WINDOW:].T
                vt_ref[sl, :] = kv[n - WINDOW:, kw + LANES * j:kw + LANES * (j + 1)].T
        qn = q_raw * q_scale * gq_ref[...]
        for j in range(H_B * HD_B // LANES):
            sl = slice(LANES * j, LANES * (j + 1))
            q_ref[rows, sl] = _rope_slab(qn[:, sl], c, s1, s2, half).astype(BF16)

    ins = [stage_in(rows) for rows in blocks]
    items = []
    for kv, q_raw in ins:
        items += [(kv[:, :kw], segk_ref, cntk_ref, expk_ref), (q_raw, segq_ref, cntq_ref, expq_ref)]
    scales = _segment_scales(items)
    for b, (rows, (kv, q_raw)) in enumerate(zip(blocks, ins)):
        stage_out(rows, kv, q_raw, scales[2 * b], scales[2 * b + 1])


def _proj_b(h, tab, wb, tm):
    n = h.shape[0]
    n_tab = tab.shape[1] // tm
    weights = [wb["g_kv"], wb["w_kv"], wb["segk"], wb["cntk"], wb["expk"], wb["gk"],
               wb["g_attn"], wb["w_q"], wb["segq"], wb["cntq"], wb["expq"], wb["gq"]]
    kw = N_KV_B * HD_B
    return pl.pallas_call(
        _proj_b_kernel,
        grid=(n // tm,),
        in_specs=[pl.BlockSpec((tm, D_MODEL), lambda i: (i, 0)),
                  pl.BlockSpec((3, tm, LANES), lambda i: (0, i % n_tab, 0))]
                 + [_const_spec(w.shape) for w in weights],
        out_specs=[pl.BlockSpec((tm, kw), lambda i: (i, 0)),
                   pl.BlockSpec((tm, kw), lambda i: (i, 0)),
                   pl.BlockSpec((tm, H_B * HD_B), lambda i: (i, 0)),
                   pl.BlockSpec((None, kw, WINDOW), lambda i: (i // n_tab, 0, 0)),
                   pl.BlockSpec((None, kw, WINDOW), lambda i: (i // n_tab, 0, 0))],
        out_shape=[jax.ShapeDtypeStruct((n, kw), F32),
                   jax.ShapeDtypeStruct((n, kw), F32),
                   jax.ShapeDtypeStruct((n, H_B * HD_B), BF16),
                   jax.ShapeDtypeStruct((n // (n_tab * tm), kw, WINDOW), F32),
                   jax.ShapeDtypeStruct((n // (n_tab * tm), kw, WINDOW), F32)],
        compiler_params=pltpu.CompilerParams(dimension_semantics=("arbitrary",),
                                             vmem_limit_bytes=VMEM_LIMIT),
        name="swa_proj",
    )(h, tab, *weights)


def _swa_halves(slab, kv, ones_lane=False):
    lane = lax.broadcasted_iota(jnp.int32, slab.shape, 1)
    own = (lane >= HD_B) if kv % 2 else (lane < HD_B)
    halves = [None, None]
    halves[kv % 2] = jnp.where(own, slab, 0.0)
    halves[1 - kv % 2] = pltpu.roll(halves[kv % 2], HD_B, 1)
    if ones_lane:
        halves = [jnp.where(lane == _ones_lane(par), 1.0, h) for par, h in enumerate(halves)]
    return [h.astype(BF16) for h in halves]


def _swa_attend(jobs, valid_of, sink_of):
    scores = [[_dot_nt(q, kh[par]) for par in range(2)] for q, kh, _, _ in jobs]
    probs = []
    for j, (q, _, _, kv) in enumerate(jobs):
        m2 = q.shape[0]
        top = lax.broadcasted_iota(jnp.int32, (m2, 1), 0) < (m2 // 2)
        row = []
        for par in range(2):
            s = jnp.where(valid_of(j), scores[j][par], NEG)
            sink = jnp.where(top, sink_of(G_B * kv + par), sink_of(G_B * kv + par + 2)) * LOG2E
            m = jnp.maximum(jnp.max(s, axis=-1, keepdims=True), sink)
            row.append((jnp.exp2(s - m).astype(BF16), jnp.exp2(sink - m)))
        probs.append(row)
    outs = []
    for j, (q, _, vh, _) in enumerate(jobs):
        o = []
        for par in range(2):
            pv = _dot(probs[j][par][0], vh[par])
            ones = _ones_lane(par)
            o.append(pv / (pv[:, ones:ones + 1] + probs[j][par][1]))
        low_half = lax.broadcasted_iota(jnp.int32, o[0].shape, 1) < HD_B
        outs.append(jnp.where(low_half, o[0], o[1]))
    return outs


def _attn_b_kernel(sink_ref, q_ref, kp_ref, kc_ref, vp_ref, vc_ref, o_ref):
    g = pl.program_id(1)
    kcat = jnp.concatenate([kp_ref[...], kc_ref[...]], axis=0)
    vcat = jnp.concatenate([vp_ref[...], vc_ref[...]], axis=0)
    shape = (2 * WINDOW, 2 * WINDOW)
    qi = lax.broadcasted_iota(jnp.int32, shape, 0) & (WINDOW - 1)
    col = lax.broadcasted_iota(jnp.int32, shape, 1)
    band = (col > qi) & (col <= qi + WINDOW)
    band_first = band & ((col >= WINDOW) | (g > 0))
    n_blocks = q_ref.shape[0] // WINDOW
    for kv in range(N_KV_B):
        base = G_B * HD_B * kv
        ks = slice(LANES * (kv // 2), LANES * (kv // 2 + 1))
        k_half = _swa_halves(kcat[:, ks], kv)
        v_half = _swa_halves(vcat[:, ks], kv, ones_lane=True)
        jobs = []
        for r in range(n_blocks):
            rows = slice(WINDOW * r, WINDOW * (r + 1))
            win = slice(WINDOW * r, WINDOW * (r + 2))
            q_lhs = jnp.concatenate([q_ref[rows, base:base + LANES], q_ref[rows, base + LANES:base + 2 * LANES]], axis=0)
            jobs.append((q_lhs, [h[win] for h in k_half], [h[win] for h in v_half], kv))
        outs = _swa_attend(jobs, lambda r: band if r else band_first, lambda hh: sink_ref[hh])
        for r, o in enumerate(outs):
            rows = slice(WINDOW * r, WINDOW * (r + 1))
            o_ref[rows, base:base + LANES] = o[:WINDOW].astype(BF16)
            o_ref[rows, base + LANES:base + 2 * LANES] = o[WINDOW:].astype(BF16)


def _attn_b(sinks, q, k, v, batch, seq):
    nb = seq // WINDOW
    ng = nb // SWA_BLOCKS
    kw = N_KV_B * HD_B
    prev = lambda b, g: (b * nb + jnp.maximum(SWA_BLOCKS * g - 1, 0), 0)
    cur = lambda b, g: (b * ng + g, 0)
    return pl.pallas_call(
        _attn_b_kernel,
        grid=(batch, ng),
        in_specs=[pl.BlockSpec(memory_space=pltpu.SMEM),
                  pl.BlockSpec((SWA_BLOCKS * WINDOW, H_B * HD_B), cur),
                  pl.BlockSpec((WINDOW, kw), prev), pl.BlockSpec((SWA_BLOCKS * WINDOW, kw), cur),
                  pl.BlockSpec((WINDOW, kw), prev), pl.BlockSpec((SWA_BLOCKS * WINDOW, kw), cur)],
        out_specs=pl.BlockSpec((SWA_BLOCKS * WINDOW, H_B * HD_B), cur),
        out_shape=jax.ShapeDtypeStruct((batch * seq, H_B * HD_B), BF16),
        compiler_params=pltpu.CompilerParams(dimension_semantics=("arbitrary", "arbitrary"),
                                             vmem_limit_bytes=VMEM_LIMIT),
        name="swa_prompt_attn",
    )(sinks, q, k, k, v, v)


def _attn_b_sample_kernel(sink_ref, q_ref, k_ref, v_ref, o_ref, *, t, w_buf):
    keys = k_ref.shape[1]
    rows = 2 * SWA_Q_ROWS
    ti = lax.broadcasted_iota(jnp.int32, (rows, keys), 0) & (SWA_Q_ROWS - 1)
    col = lax.broadcasted_iota(jnp.int32, (rows, keys), 1)
    diff = jnp.where(col < w_buf, ti + w_buf - col, ti - (col - w_buf))
    valid = (diff >= 0) & (diff < WINDOW) & (col < w_buf + t) & (ti < t)
    jobs = []
    for b in range(q_ref.shape[0]):
        for kv in range(N_KV_B):
            ks = slice(LANES * (kv // 2), LANES * (kv // 2 + 1))
            jobs.append((q_ref[b, kv], _swa_halves(k_ref[b, :, ks], kv),
                         _swa_halves(v_ref[b, :, ks], kv, ones_lane=True), kv))
    outs = _swa_attend(jobs, lambda j: valid, lambda hh: sink_ref[hh])
    for j, o in enumerate(outs):
        o_ref[j // N_KV_B, j % N_KV_B] = o


def _attn_b_sample(sinks, q, k_all, v_all, t, w_buf):
    bd = q.shape[0]
    bs = SWA_SAMPLE_SEQS
    blk = lambda a: pl.BlockSpec((bs,) + a.shape[1:], lambda b: (b,) + (0,) * (a.ndim - 1))
    return pl.pallas_call(
        functools.partial(_attn_b_sample_kernel, t=t, w_buf=w_buf),
        grid=(bd // bs,),
        in_specs=[pl.BlockSpec(memory_space=pltpu.SMEM), blk(q), blk(k_all), blk(v_all)],
        out_specs=blk(q),
        out_shape=jax.ShapeDtypeStruct(q.shape, F32),
        compiler_params=pltpu.CompilerParams(dimension_semantics=("arbitrary",),
                                             vmem_limit_bytes=VMEM_LIMIT),
        name="swa_sample_attn",
    )(sinks, q, k_all, v_all)


def _rope_tables(pos, n_rot, period, lane_lo):
    half = n_rot // 2
    inv = ROPE_THETA ** (-np.arange(0, n_rot, 2, dtype=np.float64) / n_rot)
    ang = np.asarray(pos, np.float64)[:, None] * inv[None, :]
    cos, sin = np.cos(ang), np.sin(ang)
    rel = np.arange(LANES) % period - lane_lo
    in1 = (rel >= 0) & (rel < half)
    in2 = (rel >= half) & (rel < 2 * half)
    idx = np.where(in1, rel, np.where(in2, rel - half, 0))
    cg, sg = cos[:, idx], sin[:, idx]
    tables = np.stack([np.where(in1 | in2, cg, 1.0), np.where(in1, -sg, 0.0), np.where(in2, sg, 0.0)])
    return jnp.asarray(tables, F32)


def _rope_tables_dup(pos, n_rot, lane_lo):
    half = n_rot // 2
    inv = ROPE_THETA ** (-np.arange(0, n_rot, 2, dtype=np.float64) / n_rot)
    ang = np.asarray(pos, np.float64)[:, None] * inv[None, :]
    cos, sin = np.cos(ang), np.sin(ang)
    rel = np.arange(LANES) - lane_lo
    in1 = (rel >= 0) & (rel < half)
    in2 = (rel >= half) & (rel < 2 * half)
    idx = np.where(in1, rel, np.where(in2, rel - half, 0))
    cg, sg = cos[:, idx], sin[:, idx]
    tables = np.stack([np.where(in1 | in2, cg, np.where(rel < 0, 1.0, 0.0)),
                       np.where(in1, -sg, np.where(in2, sg, 0.0))])
    return jnp.asarray(tables, F32)


def _segments(width, seg_lanes, extra=None):
    seg = np.zeros((width, LANES), np.float32)
    spread = np.zeros((LANES, width), np.float32)
    cnt = np.zeros((1, LANES), np.float32)
    for s, (lo, hi) in enumerate(seg_lanes):
        seg[lo:hi, s] = 1.0
        spread[s, lo:hi + (extra[s] if extra else 0)] = 1.0
        cnt[0, s] = 1.0 / (hi - lo)
    expand = np.concatenate([spread, spread], axis=0)
    return jnp.asarray(seg, BF16), jnp.asarray(cnt, F32), jnp.asarray(expand, BF16)


def _slab_gain(parts):
    pieces, pos = [], 0
    for lo, vals in parts:
        pieces += [jnp.zeros((lo - pos,), F32), vals.astype(F32)]
        pos = lo + vals.shape[0]
    pieces.append(jnp.zeros((SLAB - pos,), F32))
    return jnp.concatenate(pieces)[None, :]


def _prep_a(norm_attn, w_a_in, g_qc, w_uq, g_ckv, w_uk, w_uv, g_qn, g_qr, g_kn, g_kr):
    half = D_ROPE // 2
    tail = SLAB - KPE_LANE - D_ROPE - half
    w_kpe = w_a_in[:, D_QC + D_C:]
    w_in = jnp.concatenate([w_a_in[:, :D_QC + D_C], jnp.zeros((D_MODEL, KPE_LANE), F32),
                            w_kpe, w_kpe[:, :half], jnp.zeros((D_MODEL, tail), F32)], axis=1)
    dqk = D_NOPE + D_ROPE
    w_uq3 = w_uq.reshape(D_QC, H_A, dqk)
    w_uq_pad = jnp.concatenate([w_uq3, w_uq3[:, :, D_NOPE:D_NOPE + half], jnp.zeros((D_QC, H_A, tail), F32)],
                               axis=2).reshape(D_QC, H_A * SLAB)
    w_uk3 = w_uk.reshape(D_C, H_A, D_NOPE)
    w_uk_pad = jnp.pad(w_uk3, ((0, 0), (0, 0), (0, SLAB - D_NOPE))).reshape(D_C, H_A * SLAB)
    w_uv3 = w_uv.reshape(D_C, H_A // 2, 2, D_V)
    even = jnp.pad(w_uv3[:, :, 0], ((0, 0), (0, 0), (0, SLAB - D_V)))
    odd = jnp.pad(w_uv3[:, :, 1], ((0, 0), (0, 0), (SLAB - D_V, 0)))
    w_uv_pad = jnp.stack([even, odd], axis=2).reshape(D_C, H_A * SLAB)
    v_ones = np.zeros((1, H_A * SLAB), np.float32)
    for h in range(H_A):
        v_ones[0, SLAB * h + _ones_lane(h % 2)] = 1.0
    q_segs = []
    for h in range(H_A):
        q_segs += [(SLAB * h, SLAB * h + D_NOPE), (SLAB * h + KPE_LANE, SLAB * h + KPE_LANE + D_ROPE)]
    segq, cntq, expq = _segments(H_A * SLAB, q_segs, extra=[0, half] * H_A)
    q_scale = SCALE_A * LOG2E
    gq = jnp.tile(_slab_gain([(0, g_qn * q_scale), (KPE_LANE, g_qr * q_scale),
                              (KPE_LANE + D_ROPE, g_qr[:half] * q_scale)]), (1, H_A))
    gk_slab = _slab_gain([(0, g_kn)])
    pe_mask = np.zeros((1, SLAB), np.float32)
    pe_mask[0, KPE_LANE:KPE_LANE + D_ROPE] = 1.0
    wukt = jnp.pad(jnp.transpose(w_uk3, (1, 2, 0)), ((0, 0), (0, SLAB - D_NOPE), (0, 0)))
    return dict(
        g_attn=norm_attn[None, :], w_in=w_in.astype(BF16), g_qc=g_qc[None, :], g_ckv=g_ckv[None, :],
        g_kpe=_slab_gain([(KPE_LANE, g_kr), (KPE_LANE + D_ROPE, g_kr[:half])]), pe_mask=jnp.asarray(pe_mask),
        w_uq=w_uq_pad.astype(BF16), segq=segq, cntq=cntq, expq=expq, gq=gq,
        w_uk=w_uk_pad.astype(BF16),
        w_uv=w_uv_pad.astype(BF16), v_ones=jnp.asarray(v_ones), gk_slab=gk_slab, wukt_pad=wukt.astype(BF16),
        wukt=jnp.transpose(w_uk3, (2, 1, 0)).reshape(H_A * D_NOPE, D_C).astype(BF16))


def _prep_b(g_kv, w_kv, g_k, norm_attn, w_q, g_q):
    kw = N_KV_B * HD_B
    segk, cntk, expk = _segments(kw, [(HD_B * h, HD_B * (h + 1)) for h in range(N_KV_B)])
    segq, cntq, expq = _segments(H_B * HD_B, [(HD_B * h, HD_B * (h + 1)) for h in range(H_B)])
    return dict(g_kv=g_kv[None, :], w_kv=w_kv.astype(BF16), segk=segk, cntk=cntk, expk=expk,
                gk=jnp.tile(g_k, N_KV_B)[None, :], g_attn=norm_attn[None, :], w_q=w_q.astype(BF16),
                segq=segq, cntq=cntq, expq=expq, gq=jnp.tile(g_q * (SCALE_B * LOG2E), H_B)[None, :])


def kernel(x_prompt, x_sample, cache_mla, state_win_k, state_win_v, page_table, norm_attn, norm_ffn, w_a_in, g_qc, w_uq, g_ckv, w_uk, w_uv, g_qn_a, g_qr_a, g_kn_a, g_kr_a, w_a_out, g_kv_shared, w_kv_shared, g_k_b, w_q_b, g_q_b, sinks, w_b_out, w_ffn_in, w_ffn_out):
    batch, seq, _ = x_prompt.shape
    bd, t_dec, _ = x_sample.shape
    past_len = page_table.shape[1] * PAGE_SIZE
    w_buf = state_win_k.shape[1]
    kw = N_KV_B * HD_B
    assert w_a_in.shape[0] == 1 and w_q_b.shape[0] == 1, "one MLA layer followed by one SWA layer"
    assert w_buf == WINDOW and (bd * t_dec) % 8 == 0
    assert all(seq % t == 0 for t in (TM_MLA_PROJ, TM_POST, TM_SWA_PROJ, TQ_MLA, SWA_BLOCKS * WINDOW))
    assert seq >= WINDOW and bd * t_dec >= WINDOW, "the window tail is taken from the last 128 rows of a tile"
    assert t_dec <= SWA_Q_ROWS and bd % SWA_SAMPLE_SEQS == 0 and page_table.shape[1] % (2 * PAGES_PER_GROUP) == 0

    wa = _prep_a(norm_attn[0], w_a_in[0], g_qc[0], w_uq[0], g_ckv[0], w_uk[0], w_uv[0],
                 g_qn_a[0], g_qr_a[0], g_kn_a[0], g_kr_a[0])
    wb = _prep_b(g_kv_shared, w_kv_shared, g_k_b, norm_attn[1], w_q_b[0], g_q_b[0])
    w_a_out_b = w_a_out[0].astype(BF16)
    w_b_out_b = w_b_out[0].astype(BF16)
    ffn_in = w_ffn_in.astype(BF16)
    ffn_out = w_ffn_out.astype(BF16)
    g_ffn = norm_ffn[:, None, :]
    sink_b = sinks[0]

    pos_p = np.arange(seq)
    n_s = bd * t_dec
    pos_s = past_len + np.arange(n_s) % t_dec

    xp = x_prompt.reshape(batch * seq, D_MODEL)
    q, k, v, rows_p = _proj_a(xp, _rope_tables_dup(pos_p, D_ROPE, KPE_LANE), wa, TM_MLA_PROJ)
    o = _attn_a(q, k, v, batch, seq, TQ_MLA)
    h = _post(xp, o, w_a_out_b, g_ffn, ffn_in, ffn_out, 0, TM_POST)
    k_p, v_p, q_b, kt_p, vt_p = _proj_b(h, _rope_tables(pos_p, ROT_B, HD_B, 0), wb, TM_SWA_PROJ)
    o = _attn_b(sink_b, q_b, k_p, v_p, batch, seq)
    y_prompt = _post(h, o, w_b_out_b, g_ffn, ffn_in, ffn_out, 1, TM_POST)

    xs = x_sample.reshape(n_s, D_MODEL)
    q, _, _, rows_s = _proj_a(xs, _rope_tables_dup(pos_s, D_ROPE, KPE_LANE), wa, n_s)
    qabs = _qabs(q, wa["gk_slab"], wa["wukt_pad"]).reshape(bd, t_dec * H_A, D_C)
    qpe = q.reshape(n_s, H_A, SLAB)[:, :, KPE_LANE:KPE_LANE + D_ROPE].reshape(bd, t_dec * H_A, D_ROPE)
    new_pad = jnp.pad(jnp.swapaxes(rows_s.reshape(bd, t_dec, D_CKV), 1, 2), ((0, 0), (0, 0), (0, PAGE_SIZE - t_dec)))
    olat = _paged_attn(page_table, jnp.swapaxes(cache_mla, 2, 3), wa["wukt"], qabs, qpe, new_pad, t_dec)
    o = _latent_out(olat.reshape(n_s, H_A * D_C), wa["w_uv"])
    h = _post(xs, o, w_a_out_b, g_ffn, ffn_in, ffn_out, 0, n_s)
    k_s, v_s, q_b, _, _ = _proj_b(h, _rope_tables(pos_s, ROT_B, HD_B, 0), wb, n_s)
    q4 = jnp.transpose(q_b.reshape(bd, t_dec, N_KV_B, 2, LANES), (0, 2, 3, 1, 4))
    q4 = jnp.pad(q4, ((0, 0), (0, 0), (0, 0), (0, SWA_Q_ROWS - t_dec), (0, 0))).reshape(bd, N_KV_B, 2 * SWA_Q_ROWS, LANES)
    key_pad = jnp.zeros((bd, 2 * SWA_Q_ROWS - t_dec, kw), F32)
    k_all = jnp.concatenate([state_win_k.reshape(bd, w_buf, kw), k_s.reshape(bd, t_dec, kw), key_pad], axis=1)
    v_all = jnp.concatenate([state_win_v.reshape(bd, w_buf, kw), v_s.reshape(bd, t_dec, kw), key_pad], axis=1)
    o4 = _attn_b_sample(sink_b, q4, k_all, v_all, t_dec, w_buf)
    o = jnp.transpose(o4.reshape(bd, N_KV_B, 2, SWA_Q_ROWS, LANES)[:, :, :, :t_dec], (0, 3, 1, 2, 4))
    y_sample = _post(h, o.reshape(n_s, H_B * HD_B).astype(BF16), w_b_out_b, g_ffn, ffn_in, ffn_out, 1, n_s)

    win_k_p = jnp.transpose(kt_p.reshape(batch, N_KV_B, HD_B, WINDOW), (0, 3, 1, 2))
    win_v_p = jnp.transpose(vt_p.reshape(batch, N_KV_B, HD_B, WINDOW), (0, 3, 1, 2))
    win_k_s = jnp.concatenate([state_win_k, k_s.reshape(bd, t_dec, N_KV_B, HD_B)], axis=1)[:, -w_buf:]
    win_v_s = jnp.concatenate([state_win_v, v_s.reshape(bd, t_dec, N_KV_B, HD_B)], axis=1)[:, -w_buf:]
    return (y_prompt.reshape(batch, seq, D_MODEL), y_sample.reshape(bd, t_dec, D_MODEL),
            rows_p.reshape(1, batch, seq, D_CKV), rows_s.reshape(1, bd, t_dec, D_CKV),
            win_k_p, win_v_p, win_k_s, win_v_s)
```

```python
import functools

import numpy as np
import jax
import jax.numpy as jnp
from jax import lax
from jax.experimental import pallas as pl
from jax.experimental.pallas import tpu as pltpu

F32 = jnp.float32
BF16 = jnp.bfloat16

D_MODEL = 1024
PAGE_SIZE = 128
H_A = 16
D_NOPE = 64
D_ROPE = 32
D_V = 64
D_QC = 384
D_C = 256
D_CKV = D_C + D_ROPE
SCALE_A = (D_NOPE + D_ROPE) ** -0.5
H_B = 16
N_KV_B = 4
HD_B = 64
G_B = H_B // N_KV_B
WINDOW = 128
ROT_B = HD_B // 4
SCALE_B = HD_B ** -0.5
D_FF = 2816
ROPE_THETA = 500000.0
EPS = 1e-6
NEG = -1e30
LOG2E = 1.4426950408889634

LANES = 128
MXU_DIM = 256
VMEM_LIMIT = 56 * 1024 * 1024
SLAB = 128
KPE_LANE = 64

FF_CHUNK = MXU_DIM
TM_MLA_PROJ = 512
PROJ_SUB = 256
TM_SWA_PROJ = 1024
SWA_PROJ_SUB = 256
TM_POST = 512
TQ_MLA = MXU_DIM
ATTN_LOOKAHEAD = 3
PAGES_PER_GROUP = 16
PAGES_PER_DOT = MXU_DIM // PAGE_SIZE
ACC_AFTER_DOTS = 4
SWA_SAMPLE_SEQS = 8
SWA_BLOCKS = 8
SWA_Q_ROWS = 8

_NT = (((1,), (1,)), ((), ()))


def _dot(a, b):
    return jnp.dot(a, b, preferred_element_type=F32)


def _dot_nt(a, b):
    return lax.dot_general(a, b, _NT, preferred_element_type=F32)


def _rms(x, g):
    ms = jnp.mean(x * x, axis=-1, keepdims=True)
    return x * lax.rsqrt(ms + EPS) * g


def _rope_slab(x, c, s1, s2, half):
    return x * c + pltpu.roll(x, LANES - half, 1) * s1 + pltpu.roll(x, half, 1) * s2


def _rope_slab_dup(x, c, s, half):
    return x * c + pltpu.roll(x, LANES - half, 1) * s


def _segment_scales(items):
    sums = [_dot((raw * raw).astype(BF16), seg_ref[...]) for raw, seg_ref, _, _ in items]
    scales = []
    for ss, (_, _, inv_cnt_ref, expand_ref) in zip(sums, items):
        rs = lax.rsqrt(ss * inv_cnt_ref[...] + EPS)
        hi = rs.astype(BF16)
        lo = (rs - hi.astype(F32)).astype(BF16)
        scales.append(_dot(jnp.concatenate([hi, lo], axis=1), expand_ref[...]))
    return scales


def _proj_a_kernel(x_ref, tab_ref, g_attn_ref, w_in_ref, g_qc_ref, g_ckv_ref, g_kpe_ref, pe_mask_ref,
                   w_uq_ref, segq_ref, cntq_ref, expq_ref, gq_ref,
                   w_uk_ref, gk_ref, w_uv_ref, v_ones_ref,
                   q_ref, k_ref, v_ref, rows_ref):
    tm = x_ref.shape[0]
    blocks = [slice(r, r + min(PROJ_SUB, tm)) for r in range(0, tm, PROJ_SUB)]
    half = D_ROPE // 2

    def stage_in(rows):
        hn = _rms(x_ref[rows, :], g_attn_ref[...]).astype(BF16)
        return _dot(hn, w_in_ref[...])

    def stage_latents(rows, a):
        c, s = tab_ref[0, rows, :], tab_ref[1, rows, :]
        cq = _rms(a[:, :D_QC], g_qc_ref[...]).astype(BF16)
        ckv = _rms(a[:, D_QC:D_QC + D_C], g_ckv_ref[...])
        kpe = a[:, D_QC + D_C:]
        ms = jnp.sum(kpe * kpe * pe_mask_ref[...], axis=-1, keepdims=True) * (1.0 / D_ROPE)
        kpe = _rope_slab_dup(kpe * lax.rsqrt(ms + EPS) * g_kpe_ref[...], c, s, half)
        rows_ref[rows, :D_C] = ckv
        rows_ref[rows, D_C:] = kpe[:, KPE_LANE:KPE_LANE + D_ROPE]
        ckv_b = ckv.astype(BF16)
        q_raw = _dot(cq, w_uq_ref[...])
        k_raw = _dot(ckv_b, w_uk_ref[...])
        v_ref[rows, :] = (_dot(ckv_b, w_uv_ref[...]) + v_ones_ref[...]).astype(BF16)
        return q_raw, k_raw, kpe

    def stage_out(rows, q_raw, q_scale, k_raw, kpe):
        c, s = tab_ref[0, rows, :], tab_ref[1, rows, :]
        qn = q_raw * q_scale * gq_ref[...]
        for h in range(H_A):
            sl = slice(SLAB * h, SLAB * (h + 1))
            q_ref[rows, sl] = _rope_slab_dup(qn[:, sl], c, s, half).astype(BF16)
            k_h = k_raw[:, sl]
            ms_h = jnp.sum(k_h * k_h, axis=-1, keepdims=True) * (1.0 / D_NOPE)
            k_ref[rows, sl] = (k_h * lax.rsqrt(ms_h + EPS) * gk_ref[...] + kpe).astype(BF16)

    a_s = [stage_in(rows) for rows in blocks]
    mids = [stage_latents(rows, a) for rows, a in zip(blocks, a_s)]
    q_scales = _segment_scales([(q_raw, segq_ref, cntq_ref, expq_ref) for q_raw, _, _ in mids])
    for rows, (q_raw, k_raw, kpe), q_scale in zip(blocks, mids, q_scales):
        stage_out(rows, q_raw, q_scale, k_raw, kpe)


def _const_spec(shape):
    zeros = (0,) * len(shape)
    return pl.BlockSpec(shape, lambda *_: zeros, pipeline_mode=pl.Buffered(1))


def _proj_a(x, tab, wa, tm):
    n = x.shape[0]
    n_tab = tab.shape[1] // tm
    weights = [wa["g_attn"], wa["w_in"], wa["g_qc"], wa["g_ckv"], wa["g_kpe"], wa["pe_mask"],
               wa["w_uq"], wa["segq"], wa["cntq"], wa["expq"], wa["gq"],
               wa["w_uk"], wa["gk_slab"], wa["w_uv"], wa["v_ones"]]
    wide = H_A * SLAB
    return pl.pallas_call(
        _proj_a_kernel,
        grid=(n // tm,),
        in_specs=[pl.BlockSpec((tm, D_MODEL), lambda i: (i, 0)),
                  pl.BlockSpec((tab.shape[0], tm, LANES), lambda i: (0, i % n_tab, 0))]
                 + [_const_spec(w.shape) for w in weights],
        out_specs=[pl.BlockSpec((tm, wide), lambda i: (i, 0)),
                   pl.BlockSpec((tm, wide), lambda i: (i, 0)),
                   pl.BlockSpec((tm, wide), lambda i: (i, 0)),
                   pl.BlockSpec((tm, D_CKV), lambda i: (i, 0))],
        out_shape=[jax.ShapeDtypeStruct((n, wide), BF16),
                   jax.ShapeDtypeStruct((n, wide), BF16),
                   jax.ShapeDtypeStruct((n, wide), BF16),
                   jax.ShapeDtypeStruct((n, D_CKV), F32)],
        compiler_params=pltpu.CompilerParams(dimension_semantics=("arbitrary",),
                                             vmem_limit_bytes=VMEM_LIMIT),
        name="mla_proj",
    )(x, tab, *weights)


def _ones_lane(parity):
    return D_V if parity == 0 else 0


def _attn_a_kernel(q_ref, k_ref, v_ref, o_ref, *, tq):
    seq = q_ref.shape[0]
    causal = (lax.broadcasted_iota(jnp.int32, (tq, tq), 1) <= lax.broadcasted_iota(jnp.int32, (tq, tq), 0))
    low_half = lax.broadcasted_iota(jnp.int32, (tq, SLAB), 1) < D_V
    jobs = [(c, e) for c in range(seq // tq) for e in range(2)]

    def windows(c, e):
        return slice(c * tq, (c + 1) * tq), slice(0, c * tq), slice(SLAB * e, SLAB * (e + 1))

    def score(c, e):
        rows, past, ls = windows(c, e)
        q = q_ref[rows, ls]
        s_d = _dot_nt(q, k_ref[rows, ls])
        return s_d, (_dot_nt(q, k_ref[past, ls]) if c else None)

    def attend(c, e, s_d, s_p):
        rows, past, ls = windows(c, e)
        s_d = jnp.where(causal, s_d, NEG)
        m = jnp.max(s_d, axis=-1, keepdims=True)
        if c:
            m = jnp.maximum(m, jnp.max(s_p, axis=-1, keepdims=True))
        acc = _dot(jnp.exp2(s_d - m).astype(BF16), v_ref[rows, ls])
        if c:
            acc = acc + _dot(jnp.exp2(s_p - m).astype(BF16), v_ref[past, ls])
        ones = _ones_lane(e)
        return acc / acc[:, ones:ones + 1]

    ahead = [score(*jobs[j]) for j in range(min(ATTN_LOOKAHEAD, len(jobs)))]
    out = None
    for j, (c, e) in enumerate(jobs):
        if j + ATTN_LOOKAHEAD < len(jobs):
            ahead.append(score(*jobs[j + ATTN_LOOKAHEAD]))
        o_e = attend(c, e, *ahead[j])
        ahead[j] = None
        if e == 0:
            out = o_e
        else:
            o_ref[c * tq:(c + 1) * tq, :] = jnp.where(low_half, out, o_e).astype(BF16)


def _attn_a(q, k, v, batch, seq, tq):
    pairs = H_A // 2
    return pl.pallas_call(
        functools.partial(_attn_a_kernel, tq=tq),
        grid=(batch, pairs),
        in_specs=[pl.BlockSpec((seq, 2 * SLAB), lambda b, j: (b, j)),
                  pl.BlockSpec((seq, 2 * SLAB), lambda b, j: (b, j)),
                  pl.BlockSpec((seq, 2 * SLAB), lambda b, j: (b, j))],
        out_specs=pl.BlockSpec((seq, SLAB), lambda b, j: (b, j)),
        out_shape=jax.ShapeDtypeStruct((batch * seq, H_A * D_V), BF16),
        compiler_params=pltpu.CompilerParams(
            dimension_semantics=("arbitrary", "arbitrary"),
            vmem_limit_bytes=VMEM_LIMIT),
        name="mla_prompt_attn",
    )(q, k, v)


def _qabs_kernel(q_ref, gk_ref, wukt_ref, o_ref):
    for h in range(H_A):
        qs = (q_ref[:, SLAB * h:SLAB * (h + 1)].astype(F32) * gk_ref[...]).astype(BF16)
        o_ref[:, D_C * h:D_C * (h + 1)] = _dot(qs, wukt_ref[h]).astype(BF16)


def _qabs(q, gk_slab, wukt):
    n = q.shape[0]
    return pl.pallas_call(
        _qabs_kernel,
        out_shape=jax.ShapeDtypeStruct((n, H_A * D_C), BF16),
        compiler_params=pltpu.CompilerParams(vmem_limit_bytes=VMEM_LIMIT),
        name="mla_absorb_q",
    )(q, gk_slab, wukt)


def _paged_kernel(pt_ref, cache_ref, wukt_ref, qabs_ref, qpe_ref, new_ref, o_ref, lhs_sc, pg_sc, sem, *, n_pages, t_new):
    seq = pl.program_id(0)
    group = pg_sc.shape[1]
    n_groups = n_pages // group
    rows_q = qabs_ref.shape[0]

    def page_copy(sq, g, u):
        slot = g % 2
        return pltpu.make_async_copy(cache_ref.at[0, pt_ref[sq, g * group + u]], pg_sc.at[slot, u], sem.at[slot])

    def start_group(sq, g):
        for u in range(group):
            page_copy(sq, g, u).start()

    def wait_group(sq, g):
        for u in range(group):
            page_copy(sq, g, u).wait()

    @pl.when(seq == 0)
    def _():
        start_group(seq, 0)

    lhs_sc[:H_A * D_NOPE, :] = wukt_ref[...]
    lhs_sc[H_A * D_NOPE:, :] = qabs_ref[...]

    def nope_scores(ct):
        keys = ct.shape[1]
        big = _dot(lhs_sc[...], ct)
        kt = big[:H_A * D_NOPE]
        ssq = jnp.sum((kt * kt).reshape(D_NOPE, H_A, keys), axis=0)
        rs = lax.rsqrt(ssq * (1.0 / D_NOPE) + EPS)
        rs_q = jnp.concatenate([rs] * (rows_q // H_A), axis=0)
        return big[H_A * D_NOPE:] * rs_q

    def rope_scores(kpets):
        return _dot(qpe_ref[...], jnp.concatenate(kpets, axis=1))

    def accumulate(state, s, ct):
        m_old, l, acc = state
        m_new = jnp.maximum(m_old, jnp.max(s, axis=-1, keepdims=True))
        corr = jnp.exp2(m_old - m_new)
        p = jnp.exp2(s - m_new)
        return (m_new, l * corr + jnp.sum(p, axis=-1, keepdims=True),
                acc * corr + _dot_nt(p.astype(BF16), ct))

    state = (jnp.full((rows_q, 1), NEG, F32), jnp.zeros((rows_q, 1), F32), jnp.zeros((rows_q, D_C), F32))
    pending = None
    for g in range(n_groups):
        wait_group(seq, g)
        if g + 1 < n_groups:
            start_group(seq, g + 1)
        else:
            @pl.when(seq + 1 < pl.num_programs(0))
            def _():
                start_group(seq + 1, 0)
        slot = g % 2
        cts, kpets = [], []
        for u in range(0, group, PAGES_PER_DOT):
            pages = range(u, u + PAGES_PER_DOT)
            cts.append(jnp.concatenate([pg_sc[slot, v, :D_C, :] for v in pages], axis=1).astype(BF16))
            kpets.append(jnp.concatenate([pg_sc[slot, v, D_C:, :] for v in pages], axis=1).astype(BF16))
        nope = []
        for i, ct in enumerate(cts):
            if i == ACC_AFTER_DOTS and pending is not None:
                state = accumulate(state, *pending)
            nope.append(nope_scores(ct))
        pending = (jnp.concatenate(nope, axis=1) + rope_scores(kpets), jnp.concatenate(cts, axis=1))
    state = accumulate(state, *pending)

    keys = new_ref.shape[1]
    t_row = lax.shift_right_logical(lax.broadcasted_iota(jnp.int32, (rows_q, keys), 0), H_A.bit_length() - 1)
    s_col = lax.broadcasted_iota(jnp.int32, (rows_q, keys), 1)
    ct = new_ref[:D_C, :].astype(BF16)
    s = nope_scores(ct) + rope_scores([new_ref[D_C:, :].astype(BF16)])
    _, l, acc = accumulate(state, jnp.where((s_col <= t_row) & (s_col < t_new), s, NEG), ct)
    o_ref[...] = acc / l


def _paged_attn(page_table, cache, wukt, qabs, qpe, new_pad, t_new):
    bd, n_pages = page_table.shape
    rows_q = qabs.shape[1]
    assert (n_pages // PAGES_PER_GROUP) % 2 == 0, "the slot of a page group must not depend on the sequence"
    grid_spec = pltpu.PrefetchScalarGridSpec(
        num_scalar_prefetch=1,
        grid=(bd,),
        in_specs=[pl.BlockSpec(memory_space=pl.ANY),
                  pl.BlockSpec(wukt.shape, lambda b, pt: (0, 0)),
                  pl.BlockSpec((None, rows_q, D_C), lambda b, pt: (b, 0, 0)),
                  pl.BlockSpec((None, rows_q, D_ROPE), lambda b, pt: (b, 0, 0)),
                  pl.BlockSpec((None,) + new_pad.shape[1:], lambda b, pt: (b, 0, 0))],
        out_specs=pl.BlockSpec((None, rows_q, D_C), lambda b, pt: (b, 0, 0)),
        scratch_shapes=[pltpu.VMEM((H_A * D_NOPE + rows_q, D_C), BF16),
                        pltpu.VMEM((2, PAGES_PER_GROUP, D_CKV, PAGE_SIZE), F32),
                        pltpu.SemaphoreType.DMA((2,))])
    return pl.pallas_call(
        functools.partial(_paged_kernel, n_pages=n_pages, t_new=t_new),
        grid_spec=grid_spec,
        out_shape=jax.ShapeDtypeStruct((bd, rows_q, D_C), F32),
        compiler_params=pltpu.CompilerParams(dimension_semantics=("arbitrary",),
                                             vmem_limit_bytes=VMEM_LIMIT),
        name="mla_paged_attn",
    )(page_table, cache, wukt, qabs, qpe, new_pad)


def _latent_out_kernel(olat_ref, w_uv_ref, o_ref):
    for j in range(H_A // 2):
        acc = None
        for e in range(2):
            h = 2 * j + e
            part = _dot(olat_ref[:, D_C * h:D_C * (h + 1)].astype(BF16),
                        w_uv_ref[:, SLAB * h:SLAB * (h + 1)])
            acc = part if acc is None else acc + part
        o_ref[:, SLAB * j:SLAB * (j + 1)] = acc.astype(BF16)


def _latent_out(olat, w_uv_pad):
    n = olat.shape[0]
    return pl.pallas_call(
        _latent_out_kernel,
        out_shape=jax.ShapeDtypeStruct((n, H_A * D_V), BF16),
        compiler_params=pltpu.CompilerParams(vmem_limit_bytes=VMEM_LIMIT),
        name="mla_latent_out",
    )(olat, w_uv_pad)


def _post_kernel(x_ref, o_ref, w_o_ref, g_ref, w_in_ref, w_out_ref, y_ref):
    h1 = x_ref[...] + _dot(o_ref[...], w_o_ref[...])
    hn = _rms(h1, g_ref[...]).astype(BF16)
    acc = h1
    for c in range(D_FF // FF_CHUNK):
        lo = c * FF_CHUNK
        a1 = _dot(hn, w_in_ref[:, lo:lo + FF_CHUNK])
        a2 = _dot(hn, w_in_ref[:, D_FF + lo:D_FF + lo + FF_CHUNK])
        gate = (a1 * jax.nn.sigmoid(a1)) * a2
        acc = acc + _dot(gate.astype(BF16), w_out_ref[lo:lo + FF_CHUNK, :])
    y_ref[...] = acc


def _layer_spec(stacked, layer):
    zeros = (0,) * (stacked.ndim - 1)
    return pl.BlockSpec((None,) + stacked.shape[1:], lambda *_: (layer,) + zeros, pipeline_mode=pl.Buffered(1))


def _post(x, o, w_o, g, w_in, w_out, layer, tm):
    n = x.shape[0]
    return pl.pallas_call(
        _post_kernel,
        grid=(n // tm,),
        in_specs=[pl.BlockSpec((tm, D_MODEL), lambda i: (i, 0)),
                  pl.BlockSpec((tm, o.shape[1]), lambda i: (i, 0)),
                  _const_spec(w_o.shape), _layer_spec(g, layer),
                  _layer_spec(w_in, layer), _layer_spec(w_out, layer)],
        out_specs=pl.BlockSpec((tm, D_MODEL), lambda i: (i, 0)),
        out_shape=jax.ShapeDtypeStruct((n, D_MODEL), F32),
        compiler_params=pltpu.CompilerParams(dimension_semantics=("arbitrary",),
                                             vmem_limit_bytes=VMEM_LIMIT),
        name="outproj_swiglu",
    )(x, o, w_o, g, w_in, w_out)


def _proj_b_kernel(h_ref, tab_ref, g_kv_ref, w_kv_ref, segk_ref, cntk_ref, expk_ref, gk_ref,
                   g_attn_ref, w_q_ref, segq_ref, cntq_ref, expq_ref, gq_ref,
                   k_ref, v_ref, q_ref, kt_ref, vt_ref):
    tm = h_ref.shape[0]
    blocks = [slice(r, r + min(SWA_PROJ_SUB, tm)) for r in range(0, tm, SWA_PROJ_SUB)]
    kw = N_KV_B * HD_B
    half = ROT_B // 2

    def stage_in(rows):
        h = h_ref[rows, :]
        hr = h * lax.rsqrt(jnp.mean(h * h, axis=-1, keepdims=True) + EPS)
        kv = _dot((hr * g_kv_ref[...]).astype(BF16), w_kv_ref[...])
        q_raw = _dot((hr * g_attn_ref[...]).astype(BF16), w_q_ref[...])
        v_ref[rows, :] = kv[:, kw:]
        return kv, q_raw

    def stage_out(rows, kv, q_raw, k_scale, q_scale):
        c, s1, s2 = tab_ref[0, rows, :], tab_ref[1, rows, :], tab_ref[2, rows, :]
        last = rows.stop == tm
        n = rows.stop - rows.start
        kn = kv[:, :kw] * k_scale * gk_ref[...]
        for j in range(kw // LANES):
            sl = slice(LANES * j, LANES * (j + 1))
            k_slab = _rope_slab(kn[:, sl], c, s1, s2, half)
            k_ref[rows, sl] = k_slab
            if last:
                kt_ref[sl, :] = k_slab[n - WINDOW:].T
                vt_ref[sl, :] = kv[n - WINDOW:, kw + LANES * j:kw + LANES * (j + 1)].T
        qn = q_raw * q_scale * gq_ref[...]
        for j in range(H_B * HD_B // LANES):
            sl = slice(LANES * j, LANES * (j + 1))
            q_ref[rows, sl] = _rope_slab(qn[:, sl], c, s1, s2, half).astype(BF16)

    ins, scales = [], []
    for rows in blocks:
        kv, q_raw = stage_in(rows)
        ins.append((kv, q_raw))
        scales += _segment_scales([(kv[:, :kw], segk_ref, cntk_ref, expk_ref), (q_raw, segq_ref, cntq_ref, expq_ref)])
    for b, (rows, (kv, q_raw)) in enumerate(zip(blocks, ins)):
        stage_out(rows, kv, q_raw, scales[2 * b], scales[2 * b + 1])


def _proj_b(h, tab, wb, tm):
    n = h.shape[0]
    n_tab = tab.shape[1] // tm
    weights = [wb["g_kv"], wb["w_kv"], wb["segk"], wb["cntk"], wb["expk"], wb["gk"],
               wb["g_attn"], wb["w_q"], wb["segq"], wb["cntq"], wb["expq"], wb["gq"]]
    kw = N_KV_B * HD_B
    return pl.pallas_call(
        _proj_b_kernel,
        grid=(n // tm,),
        in_specs=[pl.BlockSpec((tm, D_MODEL), lambda i: (i, 0)),
                  pl.BlockSpec((3, tm, LANES), lambda i: (0, i % n_tab, 0))]
                 + [_const_spec(w.shape) for w in weights],
        out_specs=[pl.BlockSpec((tm, kw), lambda i: (i, 0)),
                   pl.BlockSpec((tm, kw), lambda i: (i, 0)),
                   pl.BlockSpec((tm, H_B * HD_B), lambda i: (i, 0)),
                   pl.BlockSpec((None, kw, WINDOW), lambda i: (i // n_tab, 0, 0)),
                   pl.BlockSpec((None, kw, WINDOW), lambda i: (i // n_tab, 0, 0))],
        out_shape=[jax.ShapeDtypeStruct((n, kw), F32),
                   jax.ShapeDtypeStruct((n, kw), F32),
                   jax.ShapeDtypeStruct((n, H_B * HD_B), BF16),
                   jax.ShapeDtypeStruct((n // (n_tab * tm), kw, WINDOW), F32),
                   jax.ShapeDtypeStruct((n // (n_tab * tm), kw, WINDOW), F32)],
        compiler_params=pltpu.CompilerParams(dimension_semantics=("arbitrary",),
                                             vmem_limit_bytes=VMEM_LIMIT),
        name="swa_proj",
    )(h, tab, *weights)


def _swa_halves(slab, kv, ones_lane=False):
    lane = lax.broadcasted_iota(jnp.int32, slab.shape, 1)
    own = (lane >= HD_B) if kv % 2 else (lane < HD_B)
    halves = [None, None]
    halves[kv % 2] = jnp.where(own, slab, 0.0)
    halves[1 - kv % 2] = pltpu.roll(halves[kv % 2], HD_B, 1)
    if ones_lane:
        halves = [jnp.where(lane == _ones_lane(par), 1.0, h) for par, h in enumerate(halves)]
    return [h.astype(BF16) for h in halves]


def _swa_attend(jobs, valid_of, sink_of):
    scores = [[_dot_nt(q, kh[par]) for par in range(2)] for q, kh, _, _ in jobs]
    probs = []
    for j, (q, _, _, kv) in enumerate(jobs):
        m2 = q.shape[0]
        top = lax.broadcasted_iota(jnp.int32, (m2, 1), 0) < (m2 // 2)
        row = []
        for par in range(2):
            s = jnp.where(valid_of(j), scores[j][par], NEG)
            sink = jnp.where(top, sink_of(G_B * kv + par), sink_of(G_B * kv + par + 2)) * LOG2E
            m = jnp.maximum(jnp.max(s, axis=-1, keepdims=True), sink)
            row.append((jnp.exp2(s - m).astype(BF16), jnp.exp2(sink - m)))
        probs.append(row)
    outs = []
    for j, (q, _, vh, _) in enumerate(jobs):
        o = []
        for par in range(2):
            pv = _dot(probs[j][par][0], vh[par])
            ones = _ones_lane(par)
            o.append(pv / (pv[:, ones:ones + 1] + probs[j][par][1]))
        low_half = lax.broadcasted_iota(jnp.int32, o[0].shape, 1) < HD_B
        outs.append(jnp.where(low_half, o[0], o[1]))
    return outs


def _attn_b_kernel(sink_ref, q_ref, kp_ref, kc_ref, vp_ref, vc_ref, o_ref):
    g = pl.program_id(1)
    kcat = jnp.concatenate([kp_ref[...], kc_ref[...]], axis=0)
    vcat = jnp.concatenate([vp_ref[...], vc_ref[...]], axis=0)
    shape = (2 * WINDOW, 2 * WINDOW)
    qi = lax.broadcasted_iota(jnp.int32, shape, 0) & (WINDOW - 1)
    col = lax.broadcasted_iota(jnp.int32, shape, 1)
    band = (col > qi) & (col <= qi + WINDOW)
    band_first = band & ((col >= WINDOW) | (g > 0))
    n_blocks = q_ref.shape[0] // WINDOW
    for kv in range(N_KV_B):
        base = G_B * HD_B * kv
        ks = slice(LANES * (kv // 2), LANES * (kv // 2 + 1))
        k_half = _swa_halves(kcat[:, ks], kv)
        v_half = _swa_halves(vcat[:, ks], kv, ones_lane=True)
        jobs = []
        for r in range(n_blocks):
            rows = slice(WINDOW * r, WINDOW * (r + 1))
            win = slice(WINDOW * r, WINDOW * (r + 2))
            q_lhs = jnp.concatenate([q_ref[rows, base:base + LANES], q_ref[rows, base + LANES:base + 2 * LANES]], axis=0)
            jobs.append((q_lhs, [h[win] for h in k_half], [h[win] for h in v_half], kv))
        outs = _swa_attend(jobs, lambda r: band if r else band_first, lambda hh: sink_ref[hh])
        for r, o in enumerate(outs):
            rows = slice(WINDOW * r, WINDOW * (r + 1))
            o_ref[rows, base:base + LANES] = o[:WINDOW].astype(BF16)
            o_ref[rows, base + LANES:base + 2 * LANES] = o[WINDOW:].astype(BF16)


def _attn_b(sinks, q, k, v, batch, seq):
    nb = seq // WINDOW
    ng = nb // SWA_BLOCKS
    kw = N_KV_B * HD_B
    prev = lambda b, g: (b * nb + jnp.maximum(SWA_BLOCKS * g - 1, 0), 0)
    cur = lambda b, g: (b * ng + g, 0)
    return pl.pallas_call(
        _attn_b_kernel,
        grid=(batch, ng),
        in_specs=[pl.BlockSpec(memory_space=pltpu.SMEM),
                  pl.BlockSpec((SWA_BLOCKS * WINDOW, H_B * HD_B), cur),
                  pl.BlockSpec((WINDOW, kw), prev), pl.BlockSpec((SWA_BLOCKS * WINDOW, kw), cur),
                  pl.BlockSpec((WINDOW, kw), prev), pl.BlockSpec((SWA_BLOCKS * WINDOW, kw), cur)],
        out_specs=pl.BlockSpec((SWA_BLOCKS * WINDOW, H_B * HD_B), cur),
        out_shape=jax.ShapeDtypeStruct((batch * seq, H_B * HD_B), BF16),
        compiler_params=pltpu.CompilerParams(dimension_semantics=("arbitrary", "arbitrary"),
                                             vmem_limit_bytes=VMEM_LIMIT),
        name="swa_prompt_attn",
    )(sinks, q, k, k, v, v)


def _attn_b_sample_kernel(sink_ref, q_ref, k_ref, v_ref, o_ref, *, t, w_buf):
    keys = k_ref.shape[1]
    rows = 2 * SWA_Q_ROWS
    ti = lax.broadcasted_iota(jnp.int32, (rows, keys), 0) & (SWA_Q_ROWS - 1)
    col = lax.broadcasted_iota(jnp.int32, (rows, keys), 1)
    diff = jnp.where(col < w_buf, ti + w_buf - col, ti - (col - w_buf))
    valid = (diff >= 0) & (diff < WINDOW) & (col < w_buf + t) & (ti < t)
    jobs = []
    for b in range(q_ref.shape[0]):
        for kv in range(N_KV_B):
            ks = slice(LANES * (kv // 2), LANES * (kv // 2 + 1))
            jobs.append((q_ref[b, kv], _swa_halves(k_ref[b, :, ks], kv),
                         _swa_halves(v_ref[b, :, ks], kv, ones_lane=True), kv))
    outs = _swa_attend(jobs, lambda j: valid, lambda hh: sink_ref[hh])
    for j, o in enumerate(outs):
        o_ref[j // N_KV_B, j % N_KV_B] = o


def _attn_b_sample(sinks, q, k_all, v_all, t, w_buf):
    bd = q.shape[0]
    bs = SWA_SAMPLE_SEQS
    blk = lambda a: pl.BlockSpec((bs,) + a.shape[1:], lambda b: (b,) + (0,) * (a.ndim - 1))
    return pl.pallas_call(
        functools.partial(_attn_b_sample_kernel, t=t, w_buf=w_buf),
        grid=(bd // bs,),
        in_specs=[pl.BlockSpec(memory_space=pltpu.SMEM), blk(q), blk(k_all), blk(v_all)],
        out_specs=blk(q),
        out_shape=jax.ShapeDtypeStruct(q.shape, F32),
        compiler_params=pltpu.CompilerParams(dimension_semantics=("arbitrary",),
                                             vmem_limit_bytes=VMEM_LIMIT),
        name="swa_sample_attn",
    )(sinks, q, k_all, v_all)


def _rope_tables(pos, n_rot, period, lane_lo):
    half = n_rot // 2
    inv = ROPE_THETA ** (-np.arange(0, n_rot, 2, dtype=np.float64) / n_rot)
    ang = np.asarray(pos, np.float64)[:, None] * inv[None, :]
    cos, sin = np.cos(ang), np.sin(ang)
    rel = np.arange(LANES) % period - lane_lo
    in1 = (rel >= 0) & (rel < half)
    in2 = (rel >= half) & (rel < 2 * half)
    idx = np.where(in1, rel, np.where(in2, rel - half, 0))
    cg, sg = cos[:, idx], sin[:, idx]
    tables = np.stack([np.where(in1 | in2, cg, 1.0), np.where(in1, -sg, 0.0), np.where(in2, sg, 0.0)])
    return jnp.asarray(tables, F32)


def _rope_tables_dup(pos, n_rot, lane_lo):
    half = n_rot // 2
    inv = ROPE_THETA ** (-np.arange(0, n_rot, 2, dtype=np.float64) / n_rot)
    ang = np.asarray(pos, np.float64)[:, None] * inv[None, :]
    cos, sin = np.cos(ang), np.sin(ang)
    rel = np.arange(LANES) - lane_lo
    in1 = (rel >= 0) & (rel < half)
    in2 = (rel >= half) & (rel < 2 * half)
    idx = np.where(in1, rel, np.where(in2, rel - half, 0))
    cg, sg = cos[:, idx], sin[:, idx]
    tables = np.stack([np.where(in1 | in2, cg, np.where(rel < 0, 1.0, 0.0)),
                       np.where(in1, -sg, np.where(in2, sg, 0.0))])
    return jnp.asarray(tables, F32)


def _segments(width, seg_lanes, extra=None):
    seg = np.zeros((width, LANES), np.float32)
    spread = np.zeros((LANES, width), np.float32)
    cnt = np.zeros((1, LANES), np.float32)
    for s, (lo, hi) in enumerate(seg_lanes):
        seg[lo:hi, s] = 1.0
        spread[s, lo:hi + (extra[s] if extra else 0)] = 1.0
        cnt[0, s] = 1.0 / (hi - lo)
    expand = np.concatenate([spread, spread], axis=0)
    return jnp.asarray(seg, BF16), jnp.asarray(cnt, F32), jnp.asarray(expand, BF16)


def _slab_gain(parts):
    pieces, pos = [], 0
    for lo, vals in parts:
        pieces += [jnp.zeros((lo - pos,), F32), vals.astype(F32)]
        pos = lo + vals.shape[0]
    pieces.append(jnp.zeros((SLAB - pos,), F32))
    return jnp.concatenate(pieces)[None, :]


def _prep_a(norm_attn, w_a_in, g_qc, w_uq, g_ckv, w_uk, w_uv, g_qn, g_qr, g_kn, g_kr):
    half = D_ROPE // 2
    tail = SLAB - KPE_LANE - D_ROPE - half
    w_kpe = w_a_in[:, D_QC + D_C:]
    w_in = jnp.concatenate([w_a_in[:, :D_QC + D_C], jnp.zeros((D_MODEL, KPE_LANE), F32),
                            w_kpe, w_kpe[:, :half], jnp.zeros((D_MODEL, tail), F32)], axis=1)
    dqk = D_NOPE + D_ROPE
    w_uq3 = w_uq.reshape(D_QC, H_A, dqk)
    w_uq_pad = jnp.concatenate([w_uq3, w_uq3[:, :, D_NOPE:D_NOPE + half], jnp.zeros((D_QC, H_A, tail), F32)],
                               axis=2).reshape(D_QC, H_A * SLAB)
    w_uk3 = w_uk.reshape(D_C, H_A, D_NOPE)
    w_uk_pad = jnp.pad(w_uk3, ((0, 0), (0, 0), (0, SLAB - D_NOPE))).reshape(D_C, H_A * SLAB)
    w_uv3 = w_uv.reshape(D_C, H_A // 2, 2, D_V)
    even = jnp.pad(w_uv3[:, :, 0], ((0, 0), (0, 0), (0, SLAB - D_V)))
    odd = jnp.pad(w_uv3[:, :, 1], ((0, 0), (0, 0), (SLAB - D_V, 0)))
    w_uv_pad = jnp.stack([even, odd], axis=2).reshape(D_C, H_A * SLAB)
    v_ones = np.zeros((1, H_A * SLAB), np.float32)
    for h in range(H_A):
        v_ones[0, SLAB * h + _ones_lane(h % 2)] = 1.0
    q_segs = []
    for h in range(H_A):
        q_segs += [(SLAB * h, SLAB * h + D_NOPE), (SLAB * h + KPE_LANE, SLAB * h + KPE_LANE + D_ROPE)]
    segq, cntq, expq = _segments(H_A * SLAB, q_segs, extra=[0, half] * H_A)
    q_scale = SCALE_A * LOG2E
    gq = jnp.tile(_slab_gain([(0, g_qn * q_scale), (KPE_LANE, g_qr * q_scale),
                              (KPE_LANE + D_ROPE, g_qr[:half] * q_scale)]), (1, H_A))
    gk_slab = _slab_gain([(0, g_kn)])
    pe_mask = np.zeros((1, SLAB), np.float32)
    pe_mask[0, KPE_LANE:KPE_LANE + D_ROPE] = 1.0
    wukt = jnp.pad(jnp.transpose(w_uk3, (1, 2, 0)), ((0, 0), (0, SLAB - D_NOPE), (0, 0)))
    return dict(
        g_attn=norm_attn[None, :], w_in=w_in.astype(BF16), g_qc=g_qc[None, :], g_ckv=g_ckv[None, :],
        g_kpe=_slab_gain([(KPE_LANE, g_kr), (KPE_LANE + D_ROPE, g_kr[:half])]), pe_mask=jnp.asarray(pe_mask),
        w_uq=w_uq_pad.astype(BF16), segq=segq, cntq=cntq, expq=expq, gq=gq,
        w_uk=w_uk_pad.astype(BF16),
        w_uv=w_uv_pad.astype(BF16), v_ones=jnp.asarray(v_ones), gk_slab=gk_slab, wukt_pad=wukt.astype(BF16),
        wukt=jnp.transpose(w_uk3, (2, 1, 0)).reshape(H_A * D_NOPE, D_C).astype(BF16))


def _prep_b(g_kv, w_kv, g_k, norm_attn, w_q, g_q):
    kw = N_KV_B * HD_B
    segk, cntk, expk = _segments(kw, [(HD_B * h, HD_B * (h + 1)) for h in range(N_KV_B)])
    segq, cntq, expq = _segments(H_B * HD_B, [(HD_B * h, HD_B * (h + 1)) for h in range(H_B)])
    return dict(g_kv=g_kv[None, :], w_kv=w_kv.astype(BF16), segk=segk, cntk=cntk, expk=expk,
                gk=jnp.tile(g_k, N_KV_B)[None, :], g_attn=norm_attn[None, :], w_q=w_q.astype(BF16),
                segq=segq, cntq=cntq, expq=expq, gq=jnp.tile(g_q * (SCALE_B * LOG2E), H_B)[None, :])


def kernel(x_prompt, x_sample, cache_mla, state_win_k, state_win_v, page_table, norm_attn, norm_ffn, w_a_in, g_qc, w_uq, g_ckv, w_uk, w_uv, g_qn_a, g_qr_a, g_kn_a, g_kr_a, w_a_out, g_kv_shared, w_kv_shared, g_k_b, w_q_b, g_q_b, sinks, w_b_out, w_ffn_in, w_ffn_out):
    batch, seq, _ = x_prompt.shape
    bd, t_dec, _ = x_sample.shape
    past_len = page_table.shape[1] * PAGE_SIZE
    w_buf = state_win_k.shape[1]
    kw = N_KV_B * HD_B
    assert w_a_in.shape[0] == 1 and w_q_b.shape[0] == 1, "one MLA layer followed by one SWA layer"
    assert w_buf == WINDOW and (bd * t_dec) % 8 == 0
    assert all(seq % t == 0 for t in (TM_MLA_PROJ, TM_POST, TM_SWA_PROJ, TQ_MLA, SWA_BLOCKS * WINDOW))
    assert seq >= WINDOW and bd * t_dec >= WINDOW, "the window tail is taken from the last 128 rows of a tile"
    assert t_dec <= SWA_Q_ROWS and bd % SWA_SAMPLE_SEQS == 0 and page_table.shape[1] % (2 * PAGES_PER_GROUP) == 0

    wa = _prep_a(norm_attn[0], w_a_in[0], g_qc[0], w_uq[0], g_ckv[0], w_uk[0], w_uv[0],
                 g_qn_a[0], g_qr_a[0], g_kn_a[0], g_kr_a[0])
    wb = _prep_b(g_kv_shared, w_kv_shared, g_k_b, norm_attn[1], w_q_b[0], g_q_b[0])
    w_a_out_b = w_a_out[0].astype(BF16)
    w_b_out_b = w_b_out[0].astype(BF16)
    ffn_in = w_ffn_in.astype(BF16)
    ffn_out = w_ffn_out.astype(BF16)
    g_ffn = norm_ffn[:, None, :]
    sink_b = sinks[0]

    pos_p = np.arange(seq)
    n_s = bd * t_dec
    pos_s = past_len + np.arange(n_s) % t_dec

    xp = x_prompt.reshape(batch * seq, D_MODEL)
    q, k, v, rows_p = _proj_a(xp, _rope_tables_dup(pos_p, D_ROPE, KPE_LANE), wa, TM_MLA_PROJ)
    o = _attn_a(q, k, v, batch, seq, TQ_MLA)
    h = _post(xp, o, w_a_out_b, g_ffn, ffn_in, ffn_out, 0, TM_POST)
    k_p, v_p, q_b, kt_p, vt_p = _proj_b(h, _rope_tables(pos_p, ROT_B, HD_B, 0), wb, TM_SWA_PROJ)
    o = _attn_b(sink_b, q_b, k_p, v_p, batch, seq)
    y_prompt = _post(h, o, w_b_out_b, g_ffn, ffn_in, ffn_out, 1, TM_POST)

    xs = x_sample.reshape(n_s, D_MODEL)
    q, _, _, rows_s = _proj_a(xs, _rope_tables_dup(pos_s, D_ROPE, KPE_LANE), wa, n_s)
    qabs = _qabs(q, wa["gk_slab"], wa["wukt_pad"]).reshape(bd, t_dec * H_A, D_C)
    qpe = q.reshape(n_s, H_A, SLAB)[:, :, KPE_LANE:KPE_LANE + D_ROPE].reshape(bd, t_dec * H_A, D_ROPE)
    new_pad = jnp.pad(jnp.swapaxes(rows_s.reshape(bd, t_dec, D_CKV), 1, 2), ((0, 0), (0, 0), (0, PAGE_SIZE - t_dec)))
    olat = _paged_attn(page_table, jnp.swapaxes(cache_mla, 2, 3), wa["wukt"], qabs, qpe, new_pad, t_dec)
    o = _latent_out(olat.reshape(n_s, H_A * D_C), wa["w_uv"])
    h = _post(xs, o, w_a_out_b, g_ffn, ffn_in, ffn_out, 0, n_s)
    k_s, v_s, q_b, _, _ = _proj_b(h, _rope_tables(pos_s, ROT_B, HD_B, 0), wb, n_s)
    q4 = jnp.transpose(q_b.reshape(bd, t_dec, N_KV_B, 2, LANES), (0, 2, 3, 1, 4))
    q4 = jnp.pad(q4, ((0, 0), (0, 0), (0, 0), (0, SWA_Q_ROWS - t_dec), (0, 0))).reshape(bd, N_KV_B, 2 * SWA_Q_ROWS, LANES)
    key_pad = jnp.zeros((bd, 2 * SWA_Q_ROWS - t_dec, kw), F32)
    k_all = jnp.concatenate([state_win_k.reshape(bd, w_buf, kw), k_s.reshape(bd, t_dec, kw), key_pad], axis=1)
    v_all = jnp.concatenate([state_win_v.reshape(bd, w_buf, kw), v_s.reshape(bd, t_dec, kw), key_pad], axis=1)
    o4 = _attn_b_sample(sink_b, q4, k_all, v_all, t_dec, w_buf)
    o = jnp.transpose(o4.reshape(bd, N_KV_B, 2, SWA_Q_ROWS, LANES)[:, :, :, :t_dec], (0, 3, 1, 2, 4))
    y_sample = _post(h, o.reshape(n_s, H_B * HD_B).astype(BF16), w_b_out_b, g_ffn, ffn_in, ffn_out, 1, n_s)

    win_k_p = jnp.transpose(kt_p.reshape(batch, N_KV_B, HD_B, WINDOW), (0, 3, 1, 2))
    win_v_p = jnp.transpose(vt_p.reshape(batch, N_KV_B, HD_B, WINDOW), (0, 3, 1, 2))
    win_k_s = jnp.concatenate([state_win_k, k_s.reshape(bd, t_dec, N_KV_B, HD_B)], axis=1)[:, -w_buf:]
    win_v_s = jnp.concatenate([state_win_v, v_s.reshape(bd, t_dec, N_KV_B, HD_B)], axis=1)[:, -w_buf:]
    return (y_prompt.reshape(batch, seq, D_MODEL), y_sample.reshape(bd, t_dec, D_MODEL),
            rows_p.reshape(1, batch, seq, D_CKV), rows_s.reshape(1, bd, t_dec, D_CKV),
            win_k_p, win_v_p, win_k_s, win_v_s)
```

```python
import functools

import numpy as np
import jax
import jax.numpy as jnp
from jax import lax
from jax.experimental import pallas as pl
from jax.experimental.pallas import tpu as pltpu

F32 = jnp.float32
BF16 = jnp.bfloat16

D_MODEL = 1024
PAGE_SIZE = 128
H_A = 16
D_NOPE = 64
D_ROPE = 32
D_V = 64
D_QC = 384
D_C = 256
D_CKV = D_C + D_ROPE
SCALE_A = (D_NOPE + D_ROPE) ** -0.5
H_B = 16
N_KV_B = 4
HD_B = 64
G_B = H_B // N_KV_B
WINDOW = 128
ROT_B = HD_B // 4
SCALE_B = HD_B ** -0.5
D_FF = 2816
ROPE_THETA = 500000.0
EPS = 1e-6
NEG = -1e30
LOG2E = 1.4426950408889634

LANES = 128
MXU_DIM = 256
VMEM_LIMIT = 56 * 1024 * 1024
SLAB = 128
KPE_LANE = 64

FF_CHUNK = MXU_DIM
TM_MLA_PROJ = 512
PROJ_SUB = 256
TM_SWA_PROJ = 1024
SWA_PROJ_SUB = 256
TM_POST = 512
TQ_MLA = MXU_DIM
ATTN_LOOKAHEAD = 3
PAGES_PER_GROUP = 16
PAGES_PER_DOT = MXU_DIM // PAGE_SIZE
ACC_AFTER_DOTS = 4
SWA_SAMPLE_SEQS = 8
SWA_BLOCKS = 8
SWA_Q_ROWS = 8

_NT = (((1,), (1,)), ((), ()))


def _dot(a, b):
    return jnp.dot(a, b, preferred_element_type=F32)


def _dot_nt(a, b):
    return lax.dot_general(a, b, _NT, preferred_element_type=F32)


def _rms(x, g):
    ms = jnp.mean(x * x, axis=-1, keepdims=True)
    return x * lax.rsqrt(ms + EPS) * g


def _rope_slab(x, c, s1, s2, half):
    return x * c + pltpu.roll(x, LANES - half, 1) * s1 + pltpu.roll(x, half, 1) * s2


def _rope_slab_dup(x, c, s, half):
    return x * c + pltpu.roll(x, LANES - half, 1) * s


def _segment_scales(items):
    sums = [_dot((raw * raw).astype(BF16), seg_ref[...]) for raw, seg_ref, _, _ in items]
    scales = []
    for ss, (_, _, inv_cnt_ref, expand_ref) in zip(sums, items):
        rs = lax.rsqrt(ss * inv_cnt_ref[...] + EPS)
        hi = rs.astype(BF16)
        lo = (rs - hi.astype(F32)).astype(BF16)
        scales.append(_dot(jnp.concatenate([hi, lo], axis=1), expand_ref[...]))
    return scales


def _proj_a_kernel(x_ref, tab_ref, g_attn_ref, w_in_ref, g_qc_ref, g_ckv_ref, g_kpe_ref, pe_mask_ref,
                   w_uq_ref, segq_ref, cntq_ref, expq_ref, gq_ref,
                   w_uk_ref, gk_ref, w_uv_ref, v_ones_ref,
                   q_ref, k_ref, v_ref, rows_ref):
    tm = x_ref.shape[0]
    blocks = [slice(r, r + min(PROJ_SUB, tm)) for r in range(0, tm, PROJ_SUB)]
    half = D_ROPE // 2

    def stage_in(rows):
        hn = _rms(x_ref[rows, :], g_attn_ref[...]).astype(BF16)
        return _dot(hn, w_in_ref[...])

    def stage_latents(rows, a):
        c, s = tab_ref[0, rows, :], tab_ref[1, rows, :]
        cq = _rms(a[:, :D_QC], g_qc_ref[...]).astype(BF16)
        ckv = _rms(a[:, D_QC:D_QC + D_C], g_ckv_ref[...])
        kpe = a[:, D_QC + D_C:]
        ms = jnp.sum(kpe * kpe * pe_mask_ref[...], axis=-1, keepdims=True) * (1.0 / D_ROPE)
        kpe = _rope_slab_dup(kpe * lax.rsqrt(ms + EPS) * g_kpe_ref[...], c, s, half)
        rows_ref[rows, :D_C] = ckv
        rows_ref[rows, D_C:] = kpe[:, KPE_LANE:KPE_LANE + D_ROPE]
        ckv_b = ckv.astype(BF16)
        q_raw = _dot(cq, w_uq_ref[...])
        k_raw = _dot(ckv_b, w_uk_ref[...])
        v_ref[rows, :] = (_dot(ckv_b, w_uv_ref[...]) + v_ones_ref[...]).astype(BF16)
        return q_raw, k_raw, kpe

    def stage_out(rows, q_raw, q_scale, k_raw, kpe):
        c, s = tab_ref[0, rows, :], tab_ref[1, rows, :]
        qn = q_raw * q_scale * gq_ref[...]
        for h in range(H_A):
            sl = slice(SLAB * h, SLAB * (h + 1))
            q_ref[rows, sl] = _rope_slab_dup(qn[:, sl], c, s, half).astype(BF16)
            k_h = k_raw[:, sl]
            ms_h = jnp.sum(k_h * k_h, axis=-1, keepdims=True) * (1.0 / D_NOPE)
            k_ref[rows, sl] = (k_h * lax.rsqrt(ms_h + EPS) * gk_ref[...] + kpe).astype(BF16)

    a_s = [stage_in(rows) for rows in blocks]
    mids = [stage_latents(rows, a) for rows, a in zip(blocks, a_s)]
    q_scales = _segment_scales([(q_raw, segq_ref, cntq_ref, expq_ref) for q_raw, _, _ in mids])
    for rows, (q_raw, k_raw, kpe), q_scale in zip(blocks, mids, q_scales):
        stage_out(rows, q_raw, q_scale, k_raw, kpe)


def _const_spec(shape):
    zeros = (0,) * len(shape)
    return pl.BlockSpec(shape, lambda *_: zeros, pipeline_mode=pl.Buffered(1))


def _proj_a(x, tab, wa, tm):
    n = x.shape[0]
    n_tab = tab.shape[1] // tm
    weights = [wa["g_attn"], wa["w_in"], wa["g_qc"], wa["g_ckv"], wa["g_kpe"], wa["pe_mask"],
               wa["w_uq"], wa["segq"], wa["cntq"], wa["expq"], wa["gq"],
               wa["w_uk"], wa["gk_slab"], wa["w_uv"], wa["v_ones"]]
    wide = H_A * SLAB
    return pl.pallas_call(
        _proj_a_kernel,
        grid=(n // tm,),
        in_specs=[pl.BlockSpec((tm, D_MODEL), lambda i: (i, 0)),
                  pl.BlockSpec((tab.shape[0], tm, LANES), lambda i: (0, i % n_tab, 0))]
                 + [_const_spec(w.shape) for w in weights],
        out_specs=[pl.BlockSpec((tm, wide), lambda i: (i, 0)),
                   pl.BlockSpec((tm, wide), lambda i: (i, 0)),
                   pl.BlockSpec((tm, wide), lambda i: (i, 0)),
                   pl.BlockSpec((tm, D_CKV), lambda i: (i, 0))],
        out_shape=[jax.ShapeDtypeStruct((n, wide), BF16),
                   jax.ShapeDtypeStruct((n, wide), BF16),
                   jax.ShapeDtypeStruct((n, wide), BF16),
                   jax.ShapeDtypeStruct((n, D_CKV), F32)],
        compiler_params=pltpu.CompilerParams(dimension_semantics=("arbitrary",),
                                             vmem_limit_bytes=VMEM_LIMIT),
        name="mla_proj",
    )(x, tab, *weights)


def _ones_lane(parity):
    return D_V if parity == 0 else 0


def _attn_a_kernel(q_ref, k_ref, v_ref, o_ref, *, tq):
    seq = q_ref.shape[0]
    causal = (lax.broadcasted_iota(jnp.int32, (tq, tq), 1) <= lax.broadcasted_iota(jnp.int32, (tq, tq), 0))
    low_half = lax.broadcasted_iota(jnp.int32, (tq, SLAB), 1) < D_V
    jobs = [(c, e) for c in range(seq // tq) for e in range(2)]

    def windows(c, e):
        return slice(c * tq, (c + 1) * tq), slice(0, c * tq), slice(SLAB * e, SLAB * (e + 1))

    def score(c, e):
        rows, past, ls = windows(c, e)
        q = q_ref[rows, ls]
        s_d = _dot_nt(q, k_ref[rows, ls])
        return s_d, (_dot_nt(q, k_ref[past, ls]) if c else None)

    def attend(c, e, s_d, s_p):
        rows, past, ls = windows(c, e)
        s_d = jnp.where(causal, s_d, NEG)
        m = jnp.max(s_d, axis=-1, keepdims=True)
        if c:
            m = jnp.maximum(m, jnp.max(s_p, axis=-1, keepdims=True))
        acc = _dot(jnp.exp2(s_d - m).astype(BF16), v_ref[rows, ls])
        if c:
            acc = acc + _dot(jnp.exp2(s_p - m).astype(BF16), v_ref[past, ls])
        ones = _ones_lane(e)
        return acc / acc[:, ones:ones + 1]

    ahead = [score(*jobs[j]) for j in range(min(ATTN_LOOKAHEAD, len(jobs)))]
    out = None
    for j, (c, e) in enumerate(jobs):
        if j + ATTN_LOOKAHEAD < len(jobs):
            ahead.append(score(*jobs[j + ATTN_LOOKAHEAD]))
        o_e = attend(c, e, *ahead[j])
        ahead[j] = None
        if e == 0:
            out = o_e
        else:
            o_ref[c * tq:(c + 1) * tq, :] = jnp.where(low_half, out, o_e).astype(BF16)


def _attn_a(q, k, v, batch, seq, tq):
    pairs = H_A // 2
    return pl.pallas_call(
        functools.partial(_attn_a_kernel, tq=tq),
        grid=(batch, pairs),
        in_specs=[pl.BlockSpec((seq, 2 * SLAB), lambda b, j: (b, j)),
                  pl.BlockSpec((seq, 2 * SLAB), lambda b, j: (b, j)),
                  pl.BlockSpec((seq, 2 * SLAB), lambda b, j: (b, j))],
        out_specs=pl.BlockSpec((seq, SLAB), lambda b, j: (b, j)),
        out_shape=jax.ShapeDtypeStruct((batch * seq, H_A * D_V), BF16),
        compiler_params=pltpu.CompilerParams(
            dimension_semantics=("arbitrary", "arbitrary"),
            vmem_limit_bytes=VMEM_LIMIT),
        name="mla_prompt_attn",
    )(q, k, v)


def _qabs_kernel(q_ref, gk_ref, wukt_ref, o_ref):
    for h in range(H_A):
        qs = (q_ref[:, SLAB * h:SLAB * (h + 1)].astype(F32) * gk_ref[...]).astype(BF16)
        o_ref[:, D_C * h:D_C * (h + 1)] = _dot(qs, wukt_ref[h]).astype(BF16)


def _qabs(q, gk_slab, wukt):
    n = q.shape[0]
    return pl.pallas_call(
        _qabs_kernel,
        out_shape=jax.ShapeDtypeStruct((n, H_A * D_C), BF16),
        compiler_params=pltpu.CompilerParams(vmem_limit_bytes=VMEM_LIMIT),
        name="mla_absorb_q",
    )(q, gk_slab, wukt)


def _paged_kernel(pt_ref, cache_ref, wukt_ref, qabs_ref, qpe_ref, new_ref, o_ref, lhs_sc, pg_sc, sem, *, n_pages, t_new):
    seq = pl.program_id(0)
    group = pg_sc.shape[1]
    n_groups = n_pages // group
    rows_q = qabs_ref.shape[0]

    def page_copy(sq, g, u):
        slot = g % 2
        return pltpu.make_async_copy(cache_ref.at[0, pt_ref[sq, g * group + u]], pg_sc.at[slot, u], sem.at[slot])

    def start_group(sq, g):
        for u in range(group):
            page_copy(sq, g, u).start(priority=u % 2)

    def wait_group(sq, g):
        for u in range(group):
            page_copy(sq, g, u).wait()

    @pl.when(seq == 0)
    def _():
        start_group(seq, 0)

    lhs_sc[:H_A * D_NOPE, :] = wukt_ref[...]
    lhs_sc[H_A * D_NOPE:, :] = qabs_ref[...]

    def nope_scores(ct):
        keys = ct.shape[1]
        big = _dot(lhs_sc[...], ct)
        kt = big[:H_A * D_NOPE]
        ssq = jnp.sum((kt * kt).reshape(D_NOPE, H_A, keys), axis=0)
        rs = lax.rsqrt(ssq * (1.0 / D_NOPE) + EPS)
        rs_q = jnp.concatenate([rs] * (rows_q // H_A), axis=0)
        return big[H_A * D_NOPE:] * rs_q

    def rope_scores(kpets):
        return _dot(qpe_ref[...], jnp.concatenate(kpets, axis=1))

    def accumulate(state, s, ct):
        m_old, l, acc = state
        m_new = jnp.maximum(m_old, jnp.max(s, axis=-1, keepdims=True))
        corr = jnp.exp2(m_old - m_new)
        p = jnp.exp2(s - m_new)
        return (m_new, l * corr + jnp.sum(p, axis=-1, keepdims=True),
                acc * corr + _dot_nt(p.astype(BF16), ct))

    state = (jnp.full((rows_q, 1), NEG, F32), jnp.zeros((rows_q, 1), F32), jnp.zeros((rows_q, D_C), F32))
    pending = None
    for g in range(n_groups):
        wait_group(seq, g)
        if g + 1 < n_groups:
            start_group(seq, g + 1)
        else:
            @pl.when(seq + 1 < pl.num_programs(0))
            def _():
                start_group(seq + 1, 0)
        slot = g % 2
        cts, kpets = [], []
        for u in range(0, group, PAGES_PER_DOT):
            pages = range(u, u + PAGES_PER_DOT)
            cts.append(jnp.concatenate([pg_sc[slot, v, :D_C, :] for v in pages], axis=1).astype(BF16))
            kpets.append(jnp.concatenate([pg_sc[slot, v, D_C:, :] for v in pages], axis=1).astype(BF16))
        nope = []
        for i, ct in enumerate(cts):
            if i == ACC_AFTER_DOTS and pending is not None:
                state = accumulate(state, *pending)
            nope.append(nope_scores(ct))
        pending = (jnp.concatenate(nope, axis=1) + rope_scores(kpets), jnp.concatenate(cts, axis=1))
    state = accumulate(state, *pending)

    keys = new_ref.shape[1]
    t_row = lax.shift_right_logical(lax.broadcasted_iota(jnp.int32, (rows_q, keys), 0), H_A.bit_length() - 1)
    s_col = lax.broadcasted_iota(jnp.int32, (rows_q, keys), 1)
    ct = new_ref[:D_C, :].astype(BF16)
    s = nope_scores(ct) + rope_scores([new_ref[D_C:, :].astype(BF16)])
    _, l, acc = accumulate(state, jnp.where((s_col <= t_row) & (s_col < t_new), s, NEG), ct)
    o_ref[...] = acc / l


def _paged_attn(page_table, cache, wukt, qabs, qpe, new_pad, t_new):
    bd, n_pages = page_table.shape
    rows_q = qabs.shape[1]
    assert (n_pages // PAGES_PER_GROUP) % 2 == 0, "the slot of a page group must not depend on the sequence"
    grid_spec = pltpu.PrefetchScalarGridSpec(
        num_scalar_prefetch=1,
        grid=(bd,),
        in_specs=[pl.BlockSpec(memory_space=pl.ANY),
                  pl.BlockSpec(wukt.shape, lambda b, pt: (0, 0)),
                  pl.BlockSpec((None, rows_q, D_C), lambda b, pt: (b, 0, 0)),
                  pl.BlockSpec((None, rows_q, D_ROPE), lambda b, pt: (b, 0, 0)),
                  pl.BlockSpec((None,) + new_pad.shape[1:], lambda b, pt: (b, 0, 0))],
        out_specs=pl.BlockSpec((None, rows_q, D_C), lambda b, pt: (b, 0, 0)),
        scratch_shapes=[pltpu.VMEM((H_A * D_NOPE + rows_q, D_C), BF16),
                        pltpu.VMEM((2, PAGES_PER_GROUP, D_CKV, PAGE_SIZE), F32),
                        pltpu.SemaphoreType.DMA((2,))])
    return pl.pallas_call(
        functools.partial(_paged_kernel, n_pages=n_pages, t_new=t_new),
        grid_spec=grid_spec,
        out_shape=jax.ShapeDtypeStruct((bd, rows_q, D_C), F32),
        compiler_params=pltpu.CompilerParams(dimension_semantics=("arbitrary",),
                                             vmem_limit_bytes=VMEM_LIMIT),
        name="mla_paged_attn",
    )(page_table, cache, wukt, qabs, qpe, new_pad)


def _latent_out_kernel(olat_ref, w_uv_ref, o_ref):
    for j in range(H_A // 2):
        acc = None
        for e in range(2):
            h = 2 * j + e
            part = _dot(olat_ref[:, D_C * h:D_C * (h + 1)].astype(BF16),
                        w_uv_ref[:, SLAB * h:SLAB * (h + 1)])
            acc = part if acc is None else acc + part
        o_ref[:, SLAB * j:SLAB * (j + 1)] = acc.astype(BF16)


def _latent_out(olat, w_uv_pad):
    n = olat.shape[0]
    return pl.pallas_call(
        _latent_out_kernel,
        out_shape=jax.ShapeDtypeStruct((n, H_A * D_V), BF16),
        compiler_params=pltpu.CompilerParams(vmem_limit_bytes=VMEM_LIMIT),
        name="mla_latent_out",
    )(olat, w_uv_pad)


def _post_kernel(x_ref, o_ref, w_o_ref, g_ref, w_in_ref, w_out_ref, y_ref):
    h1 = x_ref[...] + _dot(o_ref[...], w_o_ref[...])
    hn = _rms(h1, g_ref[...]).astype(BF16)
    acc = h1
    for c in range(D_FF // FF_CHUNK):
        lo = c * FF_CHUNK
        a1 = _dot(hn, w_in_ref[:, lo:lo + FF_CHUNK])
        a2 = _dot(hn, w_in_ref[:, D_FF + lo:D_FF + lo + FF_CHUNK])
        gate = (a1 * jax.nn.sigmoid(a1)) * a2
        acc = acc + _dot(gate.astype(BF16), w_out_ref[lo:lo + FF_CHUNK, :])
    y_ref[...] = acc


def _layer_spec(stacked, layer):
    zeros = (0,) * (stacked.ndim - 1)
    return pl.BlockSpec((None,) + stacked.shape[1:], lambda *_: (layer,) + zeros, pipeline_mode=pl.Buffered(1))


def _post(x, o, w_o, g, w_in, w_out, layer, tm):
    n = x.shape[0]
    return pl.pallas_call(
        _post_kernel,
        grid=(n // tm,),
        in_specs=[pl.BlockSpec((tm, D_MODEL), lambda i: (i, 0)),
                  pl.BlockSpec((tm, o.shape[1]), lambda i: (i, 0)),
                  _const_spec(w_o.shape), _layer_spec(g, layer),
                  _layer_spec(w_in, layer), _layer_spec(w_out, layer)],
        out_specs=pl.BlockSpec((tm, D_MODEL), lambda i: (i, 0)),
        out_shape=jax.ShapeDtypeStruct((n, D_MODEL), F32),
        compiler_params=pltpu.CompilerParams(dimension_semantics=("arbitrary",),
                                             vmem_limit_bytes=VMEM_LIMIT),
        name="outproj_swiglu",
    )(x, o, w_o, g, w_in, w_out)


def _proj_b_kernel(h_ref, tab_ref, g_kv_ref, w_kv_ref, segk_ref, cntk_ref, expk_ref, gk_ref,
                   g_attn_ref, w_q_ref, segq_ref, cntq_ref, expq_ref, gq_ref,
                   k_ref, v_ref, q_ref, kt_ref, vt_ref):
    tm = h_ref.shape[0]
    blocks = [slice(r, r + min(SWA_PROJ_SUB, tm)) for r in range(0, tm, SWA_PROJ_SUB)]
    kw = N_KV_B * HD_B
    half = ROT_B // 2

    def stage_in(rows):
        h = h_ref[rows, :]
        hr = h * lax.rsqrt(jnp.mean(h * h, axis=-1, keepdims=True) + EPS)
        kv = _dot((hr * g_kv_ref[...]).astype(BF16), w_kv_ref[...])
        q_raw = _dot((hr * g_attn_ref[...]).astype(BF16), w_q_ref[...])
        v_ref[rows, :] = kv[:, kw:]
        return kv, q_raw

    def stage_out(rows, kv, q_raw, k_scale, q_scale):
        c, s1, s2 = tab_ref[0, rows, :], tab_ref[1, rows, :], tab_ref[2, rows, :]
        last = rows.stop == tm
        n = rows.stop - rows.start
        kn = kv[:, :kw] * k_scale * gk_ref[...]
        for j in range(kw // LANES):
            sl = slice(LANES * j, LANES * (j + 1))
            k_slab = _rope_slab(kn[:, sl], c, s1, s2, half)
            k_ref[rows, sl] = k_slab
            if last:
                kt_ref[sl, :] = k_slab[n - WINDOW:].T
                vt_ref[sl, :] = kv[n - WINDOW:, kw + LANES * j:kw + LANES * (j + 1)].T
        qn = q_raw * q_scale * gq_ref[...]
        for j in range(H_B * HD_B // LANES):
            sl = slice(LANES * j, LANES * (j + 1))
            q_ref[rows, sl] = _rope_slab(qn[:, sl], c, s1, s2, half).astype(BF16)

    ins, scales = [], []
    for rows in blocks:
        kv, q_raw = stage_in(rows)
        ins.append((kv, q_raw))
        scales += _segment_scales([(kv[:, :kw], segk_ref, cntk_ref, expk_ref), (q_raw, segq_ref, cntq_ref, expq_ref)])
    for b, (rows, (kv, q_raw)) in enumerate(zip(blocks, ins)):
        stage_out(rows, kv, q_raw, scales[2 * b], scales[2 * b + 1])


def _proj_b(h, tab, wb, tm):
    n = h.shape[0]
    n_tab = tab.shape[1] // tm
    weights = [wb["g_kv"], wb["w_kv"], wb["segk"], wb["cntk"], wb["expk"], wb["gk"],
               wb["g_attn"], wb["w_q"], wb["segq"], wb["cntq"], wb["expq"], wb["gq"]]
    kw = N_KV_B * HD_B
    return pl.pallas_call(
        _proj_b_kernel,
        grid=(n // tm,),
        in_specs=[pl.BlockSpec((tm, D_MODEL), lambda i: (i, 0)),
                  pl.BlockSpec((3, tm, LANES), lambda i: (0, i % n_tab, 0))]
                 + [_const_spec(w.shape) for w in weights],
        out_specs=[pl.BlockSpec((tm, kw), lambda i: (i, 0)),
                   pl.BlockSpec((tm, kw), lambda i: (i, 0)),
                   pl.BlockSpec((tm, H_B * HD_B), lambda i: (i, 0)),
                   pl.BlockSpec((None, kw, WINDOW), lambda i: (i // n_tab, 0, 0)),
                   pl.BlockSpec((None, kw, WINDOW), lambda i: (i // n_tab, 0, 0))],
        out_shape=[jax.ShapeDtypeStruct((n, kw), F32),
                   jax.ShapeDtypeStruct((n, kw), F32),
                   jax.ShapeDtypeStruct((n, H_B * HD_B), BF16),
                   jax.ShapeDtypeStruct((n // (n_tab * tm), kw, WINDOW), F32),
                   jax.ShapeDtypeStruct((n // (n_tab * tm), kw, WINDOW), F32)],
        compiler_params=pltpu.CompilerParams(dimension_semantics=("arbitrary",),
                                             vmem_limit_bytes=VMEM_LIMIT),
        name="swa_proj",
    )(h, tab, *weights)


def _swa_halves(slab, kv, ones_lane=False):
    lane = lax.broadcasted_iota(jnp.int32, slab.shape, 1)
    own = (lane >= HD_B) if kv % 2 else (lane < HD_B)
    halves = [None, None]
    halves[kv % 2] = jnp.where(own, slab, 0.0)
    halves[1 - kv % 2] = pltpu.roll(halves[kv % 2], HD_B, 1)
    if ones_lane:
        halves = [jnp.where(lane == _ones_lane(par), 1.0, h) for par, h in enumerate(halves)]
    return [h.astype(BF16) for h in halves]


def _swa_attend(jobs, valid_of, sink_of):
    scores = [[_dot_nt(q, kh[par]) for par in range(2)] for q, kh, _, _ in jobs]
    probs = []
    for j, (q, _, _, kv) in enumerate(jobs):
        m2 = q.shape[0]
        top = lax.broadcasted_iota(jnp.int32, (m2, 1), 0) < (m2 // 2)
        row = []
        for par in range(2):
            s = jnp.where(valid_of(j), scores[j][par], NEG)
            sink = jnp.where(top, sink_of(G_B * kv + par), sink_of(G_B * kv + par + 2)) * LOG2E
            m = jnp.maximum(jnp.max(s, axis=-1, keepdims=True), sink)
            row.append((jnp.exp2(s - m).astype(BF16), jnp.exp2(sink - m)))
        probs.append(row)
    outs = []
    for j, (q, _, vh, _) in enumerate(jobs):
        o = []
        for par in range(2):
            pv = _dot(probs[j][par][0], vh[par])
            ones = _ones_lane(par)
            o.append(pv / (pv[:, ones:ones + 1] + probs[j][par][1]))
        low_half = lax.broadcasted_iota(jnp.int32, o[0].shape, 1) < HD_B
        outs.append(jnp.where(low_half, o[0], o[1]))
    return outs


def _attn_b_kernel(sink_ref, q_ref, kp_ref, kc_ref, vp_ref, vc_ref, o_ref):
    g = pl.program_id(1)
    kcat = jnp.concatenate([kp_ref[...], kc_ref[...]], axis=0)
    vcat = jnp.concatenate([vp_ref[...], vc_ref[...]], axis=0)
    shape = (2 * WINDOW, 2 * WINDOW)
    qi = lax.broadcasted_iota(jnp.int32, shape, 0) & (WINDOW - 1)
    col = lax.broadcasted_iota(jnp.int32, shape, 1)
    band = (col > qi) & (col <= qi + WINDOW)
    band_first = band & ((col >= WINDOW) | (g > 0))
    n_blocks = q_ref.shape[0] // WINDOW
    for kv in range(N_KV_B):
        base = G_B * HD_B * kv
        ks = slice(LANES * (kv // 2), LANES * (kv // 2 + 1))
        k_half = _swa_halves(kcat[:, ks], kv)
        v_half = _swa_halves(vcat[:, ks], kv, ones_lane=True)
        jobs = []
        for r in range(n_blocks):
            rows = slice(WINDOW * r, WINDOW * (r + 1))
            win = slice(WINDOW * r, WINDOW * (r + 2))
            q_lhs = jnp.concatenate([q_ref[rows, base:base + LANES], q_ref[rows, base + LANES:base + 2 * LANES]], axis=0)
            jobs.append((q_lhs, [h[win] for h in k_half], [h[win] for h in v_half], kv))
        outs = _swa_attend(jobs, lambda r: band if r else band_first, lambda hh: sink_ref[hh])
        for r, o in enumerate(outs):
            rows = slice(WINDOW * r, WINDOW * (r + 1))
            o_ref[rows, base:base + LANES] = o[:WINDOW].astype(BF16)
            o_ref[rows, base + LANES:base + 2 * LANES] = o[WINDOW:].astype(BF16)


def _attn_b(sinks, q, k, v, batch, seq):
    nb = seq // WINDOW
    ng = nb // SWA_BLOCKS
    kw = N_KV_B * HD_B
    prev = lambda b, g: (b * nb + jnp.maximum(SWA_BLOCKS * g - 1, 0), 0)
    cur = lambda b, g: (b * ng + g, 0)
    return pl.pallas_call(
        _attn_b_kernel,
        grid=(batch, ng),
        in_specs=[pl.BlockSpec(memory_space=pltpu.SMEM),
                  pl.BlockSpec((SWA_BLOCKS * WINDOW, H_B * HD_B), cur),
                  pl.BlockSpec((WINDOW, kw), prev), pl.BlockSpec((SWA_BLOCKS * WINDOW, kw), cur),
                  pl.BlockSpec((WINDOW, kw), prev), pl.BlockSpec((SWA_BLOCKS * WINDOW, kw), cur)],
        out_specs=pl.BlockSpec((SWA_BLOCKS * WINDOW, H_B * HD_B), cur),
        out_shape=jax.ShapeDtypeStruct((batch * seq, H_B * HD_B), BF16),
        compiler_params=pltpu.CompilerParams(dimension_semantics=("arbitrary", "arbitrary"),
                                             vmem_limit_bytes=VMEM_LIMIT),
        name="swa_prompt_attn",
    )(sinks, q, k, k, v, v)


def _attn_b_sample_kernel(sink_ref, q_ref, k_ref, v_ref, o_ref, *, t, w_buf):
    keys = k_ref.shape[1]
    rows = 2 * SWA_Q_ROWS
    ti = lax.broadcasted_iota(jnp.int32, (rows, keys), 0) & (SWA_Q_ROWS - 1)
    col = lax.broadcasted_iota(jnp.int32, (rows, keys), 1)
    diff = jnp.where(col < w_buf, ti + w_buf - col, ti - (col - w_buf))
    valid = (diff >= 0) & (diff < WINDOW) & (col < w_buf + t) & (ti < t)
    jobs = []
    for b in range(q_ref.shape[0]):
        for kv in range(N_KV_B):
            ks = slice(LANES * (kv // 2), LANES * (kv // 2 + 1))
            jobs.append((q_ref[b, kv], _swa_halves(k_ref[b, :, ks], kv),
                         _swa_halves(v_ref[b, :, ks], kv, ones_lane=True), kv))
    outs = _swa_attend(jobs, lambda j: valid, lambda hh: sink_ref[hh])
    for j, o in enumerate(outs):
        o_ref[j // N_KV_B, j % N_KV_B] = o


def _attn_b_sample(sinks, q, k_all, v_all, t, w_buf):
    bd = q.shape[0]
    bs = SWA_SAMPLE_SEQS
    blk = lambda a: pl.BlockSpec((bs,) + a.shape[1:], lambda b: (b,) + (0,) * (a.ndim - 1))
    return pl.pallas_call(
        functools.partial(_attn_b_sample_kernel, t=t, w_buf=w_buf),
        grid=(bd // bs,),
        in_specs=[pl.BlockSpec(memory_space=pltpu.SMEM), blk(q), blk(k_all), blk(v_all)],
        out_specs=blk(q),
        out_shape=jax.ShapeDtypeStruct(q.shape, F32),
        compiler_params=pltpu.CompilerParams(dimension_semantics=("arbitrary",),
                                             vmem_limit_bytes=VMEM_LIMIT),
        name="swa_sample_attn",
    )(sinks, q, k_all, v_all)


def _rope_tables(pos, n_rot, period, lane_lo):
    half = n_rot // 2
    inv = ROPE_THETA ** (-np.arange(0, n_rot, 2, dtype=np.float64) / n_rot)
    ang = np.asarray(pos, np.float64)[:, None] * inv[None, :]
    cos, sin = np.cos(ang), np.sin(ang)
    rel = np.arange(LANES) % period - lane_lo
    in1 = (rel >= 0) & (rel < half)
    in2 = (rel >= half) & (rel < 2 * half)
    idx = np.where(in1, rel, np.where(in2, rel - half, 0))
    cg, sg = cos[:, idx], sin[:, idx]
    tables = np.stack([np.where(in1 | in2, cg, 1.0), np.where(in1, -sg, 0.0), np.where(in2, sg, 0.0)])
    return jnp.asarray(tables, F32)


def _rope_tables_dup(pos, n_rot, lane_lo):
    half = n_rot // 2
    inv = ROPE_THETA ** (-np.arange(0, n_rot, 2, dtype=np.float64) / n_rot)
    ang = np.asarray(pos, np.float64)[:, None] * inv[None, :]
    cos, sin = np.cos(ang), np.sin(ang)
    rel = np.arange(LANES) - lane_lo
    in1 = (rel >= 0) & (rel < half)
    in2 = (rel >= half) & (rel < 2 * half)
    idx = np.where(in1, rel, np.where(in2, rel - half, 0))
    cg, sg = cos[:, idx], sin[:, idx]
    tables = np.stack([np.where(in1 | in2, cg, np.where(rel < 0, 1.0, 0.0)),
                       np.where(in1, -sg, np.where(in2, sg, 0.0))])
    return jnp.asarray(tables, F32)


def _segments(width, seg_lanes, extra=None):
    seg = np.zeros((width, LANES), np.float32)
    spread = np.zeros((LANES, width), np.float32)
    cnt = np.zeros((1, LANES), np.float32)
    for s, (lo, hi) in enumerate(seg_lanes):
        seg[lo:hi, s] = 1.0
        spread[s, lo:hi + (extra[s] if extra else 0)] = 1.0
        cnt[0, s] = 1.0 / (hi - lo)
    expand = np.concatenate([spread, spread], axis=0)
    return jnp.asarray(seg, BF16), jnp.asarray(cnt, F32), jnp.asarray(expand, BF16)


def _slab_gain(parts):
    pieces, pos = [], 0
    for lo, vals in parts:
        pieces += [jnp.zeros((lo - pos,), F32), vals.astype(F32)]
        pos = lo + vals.shape[0]
    pieces.append(jnp.zeros((SLAB - pos,), F32))
    return jnp.concatenate(pieces)[None, :]


def _prep_a(norm_attn, w_a_in, g_qc, w_uq, g_ckv, w_uk, w_uv, g_qn, g_qr, g_kn, g_kr):
    half = D_ROPE // 2
    tail = SLAB - KPE_LANE - D_ROPE - half
    w_kpe = w_a_in[:, D_QC + D_C:]
    w_in = jnp.concatenate([w_a_in[:, :D_QC + D_C], jnp.zeros((D_MODEL, KPE_LANE), F32),
                            w_kpe, w_kpe[:, :half], jnp.zeros((D_MODEL, tail), F32)], axis=1)
    dqk = D_NOPE + D_ROPE
    w_uq3 = w_uq.reshape(D_QC, H_A, dqk)
    w_uq_pad = jnp.concatenate([w_uq3, w_uq3[:, :, D_NOPE:D_NOPE + half], jnp.zeros((D_QC, H_A, tail), F32)],
                               axis=2).reshape(D_QC, H_A * SLAB)
    w_uk3 = w_uk.reshape(D_C, H_A, D_NOPE)
    w_uk_pad = jnp.pad(w_uk3, ((0, 0), (0, 0), (0, SLAB - D_NOPE))).reshape(D_C, H_A * SLAB)
    w_uv3 = w_uv.reshape(D_C, H_A // 2, 2, D_V)
    even = jnp.pad(w_uv3[:, :, 0], ((0, 0), (0, 0), (0, SLAB - D_V)))
    odd = jnp.pad(w_uv3[:, :, 1], ((0, 0), (0, 0), (SLAB - D_V, 0)))
    w_uv_pad = jnp.stack([even, odd], axis=2).reshape(D_C, H_A * SLAB)
    v_ones = np.zeros((1, H_A * SLAB), np.float32)
    for h in range(H_A):
        v_ones[0, SLAB * h + _ones_lane(h % 2)] = 1.0
    q_segs = []
    for h in range(H_A):
        q_segs += [(SLAB * h, SLAB * h + D_NOPE), (SLAB * h + KPE_LANE, SLAB * h + KPE_LANE + D_ROPE)]
    segq, cntq, expq = _segments(H_A * SLAB, q_segs, extra=[0, half] * H_A)
    q_scale = SCALE_A * LOG2E
    gq = jnp.tile(_slab_gain([(0, g_qn * q_scale), (KPE_LANE, g_qr * q_scale),
                              (KPE_LANE + D_ROPE, g_qr[:half] * q_scale)]), (1, H_A))
    gk_slab = _slab_gain([(0, g_kn)])
    pe_mask = np.zeros((1, SLAB), np.float32)
    pe_mask[0, KPE_LANE:KPE_LANE + D_ROPE] = 1.0
    wukt = jnp.pad(jnp.transpose(w_uk3, (1, 2, 0)), ((0, 0), (0, SLAB - D_NOPE), (0, 0)))
    return dict(
        g_attn=norm_attn[None, :], w_in=w_in.astype(BF16), g_qc=g_qc[None, :], g_ckv=g_ckv[None, :],
        g_kpe=_slab_gain([(KPE_LANE, g_kr), (KPE_LANE + D_ROPE, g_kr[:half])]), pe_mask=jnp.asarray(pe_mask),
        w_uq=w_uq_pad.astype(BF16), segq=segq, cntq=cntq, expq=expq, gq=gq,
        w_uk=w_uk_pad.astype(BF16),
        w_uv=w_uv_pad.astype(BF16), v_ones=jnp.asarray(v_ones), gk_slab=gk_slab, wukt_pad=wukt.astype(BF16),
        wukt=jnp.transpose(w_uk3, (2, 1, 0)).reshape(H_A * D_NOPE, D_C).astype(BF16))


def _prep_b(g_kv, w_kv, g_k, norm_attn, w_q, g_q):
    kw = N_KV_B * HD_B
    segk, cntk, expk = _segments(kw, [(HD_B * h, HD_B * (h + 1)) for h in range(N_KV_B)])
    segq, cntq, expq = _segments(H_B * HD_B, [(HD_B * h, HD_B * (h + 1)) for h in range(H_B)])
    return dict(g_kv=g_kv[None, :], w_kv=w_kv.astype(BF16), segk=segk, cntk=cntk, expk=expk,
                gk=jnp.tile(g_k, N_KV_B)[None, :], g_attn=norm_attn[None, :], w_q=w_q.astype(BF16),
                segq=segq, cntq=cntq, expq=expq, gq=jnp.tile(g_q * (SCALE_B * LOG2E), H_B)[None, :])


def kernel(x_prompt, x_sample, cache_mla, state_win_k, state_win_v, page_table, norm_attn, norm_ffn, w_a_in, g_qc, w_uq, g_ckv, w_uk, w_uv, g_qn_a, g_qr_a, g_kn_a, g_kr_a, w_a_out, g_kv_shared, w_kv_shared, g_k_b, w_q_b, g_q_b, sinks, w_b_out, w_ffn_in, w_ffn_out):
    batch, seq, _ = x_prompt.shape
    bd, t_dec, _ = x_sample.shape
    past_len = page_table.shape[1] * PAGE_SIZE
    w_buf = state_win_k.shape[1]
    kw = N_KV_B * HD_B
    assert w_a_in.shape[0] == 1 and w_q_b.shape[0] == 1, "one MLA layer followed by one SWA layer"
    assert w_buf == WINDOW and (bd * t_dec) % 8 == 0
    assert all(seq % t == 0 for t in (TM_MLA_PROJ, TM_POST, TM_SWA_PROJ, TQ_MLA, SWA_BLOCKS * WINDOW))
    assert seq >= WINDOW and bd * t_dec >= WINDOW, "the window tail is taken from the last 128 rows of a tile"
    assert t_dec <= SWA_Q_ROWS and bd % SWA_SAMPLE_SEQS == 0 and page_table.shape[1] % (2 * PAGES_PER_GROUP) == 0

    wa = _prep_a(norm_attn[0], w_a_in[0], g_qc[0], w_uq[0], g_ckv[0], w_uk[0], w_uv[0],
                 g_qn_a[0], g_qr_a[0], g_kn_a[0], g_kr_a[0])
    wb = _prep_b(g_kv_shared, w_kv_shared, g_k_b, norm_attn[1], w_q_b[0], g_q_b[0])
    w_a_out_b = w_a_out[0].astype(BF16)
    w_b_out_b = w_b_out[0].astype(BF16)
    ffn_in = w_ffn_in.astype(BF16)
    ffn_out = w_ffn_out.astype(BF16)
    g_ffn = norm_ffn[:, None, :]
    sink_b = sinks[0]

    pos_p = np.arange(seq)
    n_s = bd * t_dec
    pos_s = past_len + np.arange(n_s) % t_dec

    xp = x_prompt.reshape(batch * seq, D_MODEL)
    q, k, v, rows_p = _proj_a(xp, _rope_tables_dup(pos_p, D_ROPE, KPE_LANE), wa, TM_MLA_PROJ)
    o = _attn_a(q, k, v, batch, seq, TQ_MLA)
    h = _post(xp, o, w_a_out_b, g_ffn, ffn_in, ffn_out, 0, TM_POST)
    k_p, v_p, q_b, kt_p, vt_p = _proj_b(h, _rope_tables(pos_p, ROT_B, HD_B, 0), wb, TM_SWA_PROJ)
    o = _attn_b(sink_b, q_b, k_p, v_p, batch, seq)
    y_prompt = _post(h, o, w_b_out_b, g_ffn, ffn_in, ffn_out, 1, TM_POST)

    xs = x_sample.reshape(n_s, D_MODEL)
    q, _, _, rows_s = _proj_a(xs, _rope_tables_dup(pos_s, D_ROPE, KPE_LANE), wa, n_s)
    qabs = _qabs(q, wa["gk_slab"], wa["wukt_pad"]).reshape(bd, t_dec * H_A, D_C)
    qpe = q.reshape(n_s, H_A, SLAB)[:, :, KPE_LANE:KPE_LANE + D_ROPE].reshape(bd, t_dec * H_A, D_ROPE)
    new_pad = jnp.pad(jnp.swapaxes(rows_s.reshape(bd, t_dec, D_CKV), 1, 2), ((0, 0), (0, 0), (0, PAGE_SIZE - t_dec)))
    olat = _paged_attn(page_table, jnp.swapaxes(cache_mla, 2, 3), wa["wukt"], qabs, qpe, new_pad, t_dec)
    o = _latent_out(olat.reshape(n_s, H_A * D_C), wa["w_uv"])
    h = _post(xs, o, w_a_out_b, g_ffn, ffn_in, ffn_out, 0, n_s)
    k_s, v_s, q_b, _, _ = _proj_b(h, _rope_tables(pos_s, ROT_B, HD_B, 0), wb, n_s)
    q4 = jnp.transpose(q_b.reshape(bd, t_dec, N_KV_B, 2, LANES), (0, 2, 3, 1, 4))
    q4 = jnp.pad(q4, ((0, 0), (0, 0), (0, 0), (0, SWA_Q_ROWS - t_dec), (0, 0))).reshape(bd, N_KV_B, 2 * SWA_Q_ROWS, LANES)
    key_pad = jnp.zeros((bd, 2 * SWA_Q_ROWS - t_dec, kw), F32)
    k_all = jnp.concatenate([state_win_k.reshape(bd, w_buf, kw), k_s.reshape(bd, t_dec, kw), key_pad], axis=1)
    v_all = jnp.concatenate([state_win_v.reshape(bd, w_buf, kw), v_s.reshape(bd, t_dec, kw), key_pad], axis=1)
    o4 = _attn_b_sample(sink_b, q4, k_all, v_all, t_dec, w_buf)
    o = jnp.transpose(o4.reshape(bd, N_KV_B, 2, SWA_Q_ROWS, LANES)[:, :, :, :t_dec], (0, 3, 1, 2, 4))
    y_sample = _post(h, o.reshape(n_s, H_B * HD_B).astype(BF16), w_b_out_b, g_ffn, ffn_in, ffn_out, 1, n_s)

    win_k_p = jnp.transpose(kt_p.reshape(batch, N_KV_B, HD_B, WINDOW), (0, 3, 1, 2))
    win_v_p = jnp.transpose(vt_p.reshape(batch, N_KV_B, HD_B, WINDOW), (0, 3, 1, 2))
    win_k_s = jnp.concatenate([state_win_k, k_s.reshape(bd, t_dec, N_KV_B, HD_B)], axis=1)[:, -w_buf:]
    win_v_s = jnp.concatenate([state_win_v, v_s.reshape(bd, t_dec, N_KV_B, HD_B)], axis=1)[:, -w_buf:]
    return (y_prompt.reshape(batch, seq, D_MODEL), y_sample.reshape(bd, t_dec, D_MODEL),
            rows_p.reshape(1, batch, seq, D_CKV), rows_s.reshape(1, bd, t_dec, D_CKV),
            win_k_p, win_v_p, win_k_s, win_v_s)
```
